```python
import math
import jax
import jax.numpy as jnp
from jax import lax
import numpy as np

D_MODEL = 1024
BATCH = 8
SEQ = 4096
DEPTH = 1
DEC_BATCH = 8
DEC_SEQ = 64
PAST_LEN = 2048

CHUNK = 64
MIX_WIDTH = D_MODEL
SSM_WIDTH = MIX_WIDTH // 2
SSM_GROUP = 16
SSM_GROUPS = SSM_WIDTH // SSM_GROUP
SSM_STATE = 64
FOX_WIDTH = MIX_WIDTH - SSM_WIDTH
FOX_HEAD_DIM = 64
FOX_HEADS = FOX_WIDTH // FOX_HEAD_DIM
Q_BLOCK = 128
MEM_LEN = 256
MEM_HEADS = 4
MEM_HEAD_DIM = D_MODEL // MEM_HEADS
N_EXPERTS = 64
TOP_K = 8
N_EXPERT_GROUPS = 8
TOPK_GROUPS = 4
D_EXPERT = 256
D_SHARED = 256
ROUTED_SCALE = 2.5
DT_MIN = 1e-3
DT_MAX = 1e-1
RMS_EPS = 1e-6
IN_COLS = SSM_WIDTH + 3 * FOX_WIDTH + FOX_HEADS

kernel_name = 'hymba_s5_fox_moe_stream_step'

F32 = jnp.float32


def rmsnorm(x, g):
    xf = x.astype(F32)
    y = xf * lax.rsqrt(jnp.mean(xf * xf, axis=-1, keepdims=True) + RMS_EPS) * g.astype(F32)
    return y.astype(x.dtype)


def _ssm_combine(left, right):
    a_l, b_l = left
    a_r, b_r = right
    return a_l * a_r, a_r * b_l + b_r


def s5_mixer(u, h0, lam_re, lam_im, log_dt, b_re, b_im, c_re, c_im, d_skip, w_glu, b_glu):
    bsz, L, _ = u.shape
    lam = lax.complex(lam_re.astype(F32), lam_im.astype(F32))
    dt = jnp.exp(log_dt.astype(F32))[:, None]
    a_bar = jnp.exp(lam * dt)
    b = lax.complex(b_re.astype(F32), b_im.astype(F32))
    b_bar = ((a_bar - 1.0) / lam)[..., None] * b
    ug = u.astype(F32).reshape(bsz, L, SSM_GROUPS, SSM_GROUP)
    bu = jnp.einsum('blgh,gph->blgp', ug.astype(jnp.complex64), b_bar)
    if h0 is not None:
        bu = bu.at[:, 0].add(a_bar * h0)

    def scan_one(bu_seq):
        a_seq = jnp.broadcast_to(a_bar, bu_seq.shape)
        return lax.associative_scan(_ssm_combine, (a_seq, bu_seq), axis=0)[1]

    h = lax.map(scan_one, bu)
    c = lax.complex(c_re.astype(F32), c_im.astype(F32))
    y = jnp.einsum('ghp,blgp->blgh', c, h).real + d_skip.astype(F32).reshape(SSM_GROUPS, SSM_GROUP) * ug
    y = jax.nn.gelu(y.reshape(bsz, L, SSM_WIDTH))
    y = y * jax.nn.sigmoid(y @ w_glu.astype(F32) + b_glu.astype(F32))
    return y.astype(u.dtype), h[:, -1]


def fox_block(q_blk, dq_blk, qpos, k, v, dk, kpos):
    s = jnp.einsum('bqhd,bkhd->bhqk', q_blk, k).astype(F32) * (FOX_HEAD_DIM ** -0.5)
    s = s + jnp.swapaxes(dq_blk, 1, 2)[:, :, :, None] - jnp.swapaxes(dk, 1, 2)[:, :, None, :]
    s = jnp.where(kpos[None, :] <= qpos[:, None], s, -jnp.inf)
    p = jax.nn.softmax(s, axis=-1)
    return jnp.einsum('bhqk,bkhd->bqhd', p.astype(v.dtype), v)


def fox_mixer(q, k, v, logf, past_k, past_v, past_logf):
    bsz, L = q.shape[0], q.shape[1]
    if past_k is None:
        k_all, v_all, lf_all, offset = k, v, logf, 0
    else:
        offset = past_k.shape[1]
        k_all = jnp.concatenate([past_k.astype(k.dtype), k], axis=1)
        v_all = jnp.concatenate([past_v.astype(v.dtype), v], axis=1)
        lf_all = jnp.concatenate([past_logf.astype(F32), logf], axis=1)
    dcum = jnp.cumsum(lf_all, axis=1)
    kpos = jnp.arange(k_all.shape[1])
    dq = dcum[:, offset:]
    qpos = offset + jnp.arange(L)
    if L % Q_BLOCK == 0:
        nb = L // Q_BLOCK
        qs = jnp.moveaxis(q.reshape(bsz, nb, Q_BLOCK, FOX_HEADS, FOX_HEAD_DIM), 1, 0)
        dqs = jnp.moveaxis(dq.reshape(bsz, nb, Q_BLOCK, FOX_HEADS), 1, 0)
        ps = qpos.reshape(nb, Q_BLOCK)
        out = lax.map(lambda blk: fox_block(blk[0], blk[1], blk[2], k_all, v_all, dcum, kpos), (qs, dqs, ps))
        out = jnp.moveaxis(out, 0, 1)
    else:
        out = fox_block(q, dq, qpos, k_all, v_all, dcum, kpos)
    return out.reshape(bsz, L, FOX_WIDTH)


def mem_kv(mem, g_mem_kv, w_mk, w_mv):
    bsz = mem.shape[0]
    mn = rmsnorm(mem, g_mem_kv)
    mk = (mn @ w_mk).reshape(bsz, MEM_LEN, MEM_HEADS, MEM_HEAD_DIM)
    mv = (mn @ w_mv).reshape(bsz, MEM_LEN, MEM_HEADS, MEM_HEAD_DIM)
    return mk, mv


def mem_attend(xn, mk, mv, w_mq, w_mo):
    bsz, L, _ = xn.shape
    q = (xn @ w_mq).reshape(bsz, L, MEM_HEADS, MEM_HEAD_DIM)
    s = jnp.einsum('bqhd,bkhd->bhqk', q, mk.astype(q.dtype)).astype(F32) * (MEM_HEAD_DIM ** -0.5)
    p = jax.nn.softmax(s, axis=-1)
    o = jnp.einsum('bhqk,bkhd->bqhd', p.astype(q.dtype), mv.astype(q.dtype)).reshape(bsz, L, D_MODEL)
    return o @ w_mo


def route(xs, w_router, e_bias):
    score = jax.nn.sigmoid((xs @ w_router).astype(F32))
    sel = score + e_bias.astype(F32)
    per_group = N_EXPERTS // N_EXPERT_GROUPS
    gscore = lax.top_k(sel.reshape(-1, N_EXPERT_GROUPS, per_group), 2)[0].sum(-1)
    _, gidx = lax.top_k(gscore, TOPK_GROUPS)
    gmask = jnp.any(gidx[..., None] == jnp.arange(N_EXPERT_GROUPS), axis=-2)
    emask = jnp.repeat(gmask, per_group, axis=-1)
    _, eidx = lax.top_k(jnp.where(emask, sel, -jnp.inf), TOP_K)
    w = jnp.take_along_axis(score, eidx, axis=-1)
    w = w / jnp.sum(w, axis=-1, keepdims=True) * ROUTED_SCALE
    return jnp.einsum('lk,lke->le', w, jax.nn.one_hot(eidx, N_EXPERTS, dtype=F32))


def moe_ffn(xn, w_router, e_bias, w1, w3, w2, ws1, ws3, ws2):
    def per_seq(xs):
        gates = route(xs, w_router, e_bias).astype(xs.dtype)
        h = jax.nn.silu(jnp.einsum('ld,edf->lef', xs, w1)) * jnp.einsum('ld,edf->lef', xs, w3)
        routed = jnp.einsum('lef,efd->ld', h * gates[..., None], w2)
        shared = (jax.nn.silu(xs @ ws1) * (xs @ ws3)) @ ws2
        return routed + shared
    return lax.map(per_seq, xn)


def trunk_layer(x, h0, past_k, past_v, past_logf, mk, mv, p):
    bsz, L, _ = x.shape
    xn = rmsnorm(x, p['g_mix'])
    z = xn @ p['w_in']
    o = SSM_WIDTH
    u = z[..., :o]
    q = z[..., o:o + FOX_WIDTH].reshape(bsz, L, FOX_HEADS, FOX_HEAD_DIM)
    k = z[..., o + FOX_WIDTH:o + 2 * FOX_WIDTH].reshape(bsz, L, FOX_HEADS, FOX_HEAD_DIM)
    v = z[..., o + 2 * FOX_WIDTH:o + 3 * FOX_WIDTH].reshape(bsz, L, FOX_HEADS, FOX_HEAD_DIM)
    logf = jax.nn.log_sigmoid(z[..., o + 3 * FOX_WIDTH:].astype(F32) + p['b_f'].astype(F32))
    y_ssm, h_last = s5_mixer(u, h0, p['lam_re'], p['lam_im'], p['log_dt'], p['b_re'], p['b_im'],
                             p['c_re'], p['c_im'], p['d_skip'], p['w_glu'], p['b_glu'])
    y_fox = fox_mixer(q, k, v, logf, past_k, past_v, past_logf)
    mix = jnp.concatenate([rmsnorm(y_ssm, p['g_ssm_out']), rmsnorm(y_fox, p['g_fox_out'])], axis=-1)
    x = x + mix @ p['w_out']
    x = x + mem_attend(rmsnorm(x, p['g_mem_q']), mk, mv, p['w_mq'], p['w_mo'])
    x = x + moe_ffn(rmsnorm(x, p['g_ffn']), p['w_router'], p['e_bias'], p['w1'], p['w3'], p['w2'],
                    p['ws1'], p['ws3'], p['ws2'])
    return x, h_last, k, v, logf


def setup_inputs(seed: int = 0) -> dict:
    key = jax.random.key(seed)
    ks = jax.random.split(key, 42)

    def nrm(i, shape, scale):
        return scale * jax.random.normal(ks[i], shape, F32)

    def gain(i, shape):
        return 1.0 + 0.02 * jax.random.normal(ks[i], shape, F32)

    G, P, H = SSM_GROUPS, SSM_STATE, SSM_GROUP
    E, F, FS = N_EXPERTS, D_EXPERT, D_SHARED
    return {
        'x_prompt': nrm(0, (BATCH, SEQ, D_MODEL), 1.0),
        'x_sample': nrm(1, (DEC_BATCH, DEC_SEQ, D_MODEL), 1.0),
        'state_ssm_re': nrm(2, (DEPTH, DEC_BATCH, G, P), 0.05),
        'state_ssm_im': nrm(3, (DEPTH, DEC_BATCH, G, P), 0.05),
        'cache_fox_k': nrm(4, (DEPTH, DEC_BATCH, PAST_LEN, FOX_HEADS, FOX_HEAD_DIM), 1.0),
        'cache_fox_v': nrm(5, (DEPTH, DEC_BATCH, PAST_LEN, FOX_HEADS, FOX_HEAD_DIM), 1.0),
        'cache_fox_logf': jax.nn.log_sigmoid(nrm(6, (DEPTH, DEC_BATCH, PAST_LEN, FOX_HEADS), 1.0) + 2.5),
        'cache_mem_k': nrm(7, (DEPTH, DEC_BATCH, MEM_LEN, MEM_HEADS, MEM_HEAD_DIM), 1.0),
        'cache_mem_v': nrm(8, (DEPTH, DEC_BATCH, MEM_LEN, MEM_HEADS, MEM_HEAD_DIM), 1.0),
        'mem_prompt': nrm(9, (BATCH, MEM_LEN, D_MODEL), 1.0),
        'g_mix': gain(10, (DEPTH, D_MODEL)),
        'w_in': nrm(11, (DEPTH, D_MODEL, IN_COLS), D_MODEL ** -0.5),
        'b_f': jnp.linspace(1.0, 4.0, FOX_HEADS, dtype=F32) + nrm(12, (DEPTH, FOX_HEADS), 0.1),
        'lam_re': -0.5 + nrm(13, (DEPTH, G, P), 0.01),
        'lam_im': jnp.pi * jnp.arange(P, dtype=F32) + nrm(14, (DEPTH, G, P), 0.01),
        'log_dt': jax.random.uniform(ks[15], (DEPTH, G), F32, math.log(DT_MIN), math.log(DT_MAX)),
        'b_re': nrm(16, (DEPTH, G, P, H), (2 * H) ** -0.5),
        'b_im': nrm(17, (DEPTH, G, P, H), (2 * H) ** -0.5),
        'c_re': nrm(18, (DEPTH, G, H, P), 0.5),
        'c_im': nrm(19, (DEPTH, G, H, P), 0.5),
        'd_skip': nrm(20, (DEPTH, SSM_WIDTH), 1.0),
        'w_glu': nrm(21, (DEPTH, SSM_WIDTH, SSM_WIDTH), SSM_WIDTH ** -0.5),
        'b_glu': nrm(22, (DEPTH, SSM_WIDTH), 0.01),
        'g_ssm_out': gain(23, (DEPTH, SSM_WIDTH)),
        'g_fox_out': gain(24, (DEPTH, FOX_WIDTH)),
        'w_out': nrm(25, (DEPTH, MIX_WIDTH, D_MODEL), MIX_WIDTH ** -0.5),
        'g_mem_q': gain(26, (DEPTH, D_MODEL)),
        'g_mem_kv': gain(27, (DEPTH, D_MODEL)),
        'w_mq': nrm(28, (DEPTH, D_MODEL, MEM_HEADS * MEM_HEAD_DIM), D_MODEL ** -0.5),
        'w_mk': nrm(29, (DEPTH, D_MODEL, MEM_HEADS * MEM_HEAD_DIM), D_MODEL ** -0.5),
        'w_mv': nrm(30, (DEPTH, D_MODEL, MEM_HEADS * MEM_HEAD_DIM), D_MODEL ** -0.5),
        'w_mo': nrm(31, (DEPTH, MEM_HEADS * MEM_HEAD_DIM, D_MODEL), D_MODEL ** -0.5),
        'g_ffn': gain(32, (DEPTH, D_MODEL)),
        'w_router': nrm(33, (DEPTH, D_MODEL, E), D_MODEL ** -0.5),
        'e_bias': nrm(34, (DEPTH, E), 0.01),
        'w1': nrm(35, (DEPTH, E, D_MODEL, F), D_MODEL ** -0.5),
        'w3': nrm(36, (DEPTH, E, D_MODEL, F), D_MODEL ** -0.5),
        'w2': nrm(37, (DEPTH, E, F, D_MODEL), F ** -0.5),
        'ws1': nrm(38, (DEPTH, D_MODEL, FS), D_MODEL ** -0.5),
        'ws3': nrm(39, (DEPTH, D_MODEL, FS), D_MODEL ** -0.5),
        'ws2': nrm(40, (DEPTH, FS, D_MODEL), FS ** -0.5),
        'g_final': gain(41, (D_MODEL,)),
    }


def reference(x_prompt, x_sample, state_ssm_re, state_ssm_im, cache_fox_k, cache_fox_v, cache_fox_logf,
              cache_mem_k, cache_mem_v, mem_prompt, g_mix, w_in, b_f, lam_re, lam_im, log_dt, b_re, b_im,
              c_re, c_im, d_skip, w_glu, b_glu, g_ssm_out, g_fox_out, w_out, g_mem_q, g_mem_kv, w_mq, w_mk,
              w_mv, w_mo, g_ffn, w_router, e_bias, w1, w3, w2, ws1, ws3, ws2, g_final):
    assert x_sample.shape[1] <= CHUNK
    hp, hs = x_prompt, x_sample
    p_re, p_im, p_k, p_v, p_lf, p_mk, p_mv = [], [], [], [], [], [], []
    s_re, s_im, s_k, s_v, s_lf = [], [], [], [], []
    for l in range(DEPTH):
        p = dict(g_mix=g_mix[l], w_in=w_in[l], b_f=b_f[l], lam_re=lam_re[l], lam_im=lam_im[l],
                 log_dt=log_dt[l], b_re=b_re[l], b_im=b_im[l], c_re=c_re[l], c_im=c_im[l],
                 d_skip=d_skip[l], w_glu=w_glu[l], b_glu=b_glu[l], g_ssm_out=g_ssm_out[l],
                 g_fox_out=g_fox_out[l], w_out=w_out[l], g_mem_q=g_mem_q[l], w_mq=w_mq[l], w_mo=w_mo[l],
                 g_ffn=g_ffn[l], w_router=w_router[l], e_bias=e_bias[l], w1=w1[l], w3=w3[l], w2=w2[l],
                 ws1=ws1[l], ws3=ws3[l], ws2=ws2[l])
        mk_p, mv_p = mem_kv(mem_prompt, g_mem_kv[l], w_mk[l], w_mv[l])
        hp, hl_p, k_p, v_p, lf_p = trunk_layer(hp, None, None, None, None, mk_p, mv_p, p)
        h0 = lax.complex(state_ssm_re[l].astype(F32), state_ssm_im[l].astype(F32))
        hs, hl_s, k_s, v_s, lf_s = trunk_layer(hs, h0, cache_fox_k[l], cache_fox_v[l], cache_fox_logf[l],
                                               cache_mem_k[l], cache_mem_v[l], p)
        p_re.append(hl_p.real); p_im.append(hl_p.imag); p_k.append(k_p); p_v.append(v_p); p_lf.append(lf_p)
        p_mk.append(mk_p); p_mv.append(mv_p)
        s_re.append(hl_s.real); s_im.append(hl_s.imag); s_k.append(k_s); s_v.append(v_s); s_lf.append(lf_s)
    y_prompt = rmsnorm(hp, g_final)
    y_sample = rmsnorm(hs, g_final)
    p_ssm_re = jnp.stack(p_re)
    p_ssm_im = jnp.stack(p_im)
    p_fox_k = jnp.stack(p_k)
    p_fox_v = jnp.stack(p_v)
    p_fox_logf = jnp.stack(p_lf)
    p_mem_k = jnp.stack(p_mk)
    p_mem_v = jnp.stack(p_mv)
    s_ssm_re = jnp.stack(s_re)
    s_ssm_im = jnp.stack(s_im)
    s_fox_k = jnp.stack(s_k)
    s_fox_v = jnp.stack(s_v)
    s_fox_logf = jnp.stack(s_lf)
    return (y_prompt, y_sample, p_ssm_re, p_ssm_im, p_fox_k, p_fox_v, p_fox_logf, p_mem_k, p_mem_v,
            s_ssm_re, s_ssm_im, s_fox_k, s_fox_v, s_fox_logf)
```

```python
import functools
import math

import jax
import jax.numpy as jnp
from jax import lax
from jax.experimental import pallas as pl
from jax.experimental.pallas import tpu as pltpu

F32 = jnp.float32
BF16 = jnp.bfloat16

SSM_GROUP = 16
SSM_STATE = 64
FOX_HEAD_DIM = 64
MEM_HEADS = 4
TOP_K = 8
N_EXPERT_GROUPS = 8
TOPK_GROUPS = 4
ROUTED_SCALE = 2.5
RMS_EPS = 1e-6

LANES = 128
SUBLANES = 8
MXU_DIM = 256
VMEM_LIMIT_BYTES = 56 * 1024 * 1024

_HIGHEST = lax.Precision.HIGHEST
_NT = (((1,), (1,)), ((), ()))


def _params(*sem):
    return pltpu.CompilerParams(dimension_semantics=sem, vmem_limit_bytes=VMEM_LIMIT_BYTES)


def _rms(x, g):
    return x * lax.rsqrt(jnp.mean(x * x, axis=-1, keepdims=True) + RMS_EPS) * g


def _sigmoid(x):
    return 1.0 / (1.0 + jnp.exp(-x))


def _silu(x):
    return x * _sigmoid(x)


def _gelu_tanh(x):
    return x * (0.5 * (1.0 + jnp.tanh(math.sqrt(2.0 / math.pi) * (x + 0.044715 * (x * x * x)))))


def _log_sigmoid(x):
    return jnp.minimum(x, 0.0) - jnp.log1p(jnp.exp(-jnp.abs(x)))


def _dot(a, b):
    return jnp.dot(a, b, preferred_element_type=F32)


def _const_spec(shape):
    nd = len(shape)
    return pl.BlockSpec(shape, lambda *_: (0,) * nd)


def _memkv_kernel(m_ref, g_ref, wk_ref, wv_ref, k_ref, v_ref, kb_ref, vb_ref):
    mn = _rms(m_ref[0], g_ref[...]).astype(BF16)
    k = _dot(mn, wk_ref[...])
    v = _dot(mn, wv_ref[...])
    k_ref[0] = k
    v_ref[0] = v
    kb_ref[0] = k.astype(BF16)
    vb_ref[0] = v.astype(BF16)


def _mem_kv(mem, g, wk, wv):
    bsz, n, d = mem.shape
    blk = pl.BlockSpec((1, n, d), lambda b: (b, 0, 0))
    return pl.pallas_call(
        _memkv_kernel,
        grid=(bsz,),
        in_specs=[blk, _const_spec((1, d)), _const_spec((d, d)), _const_spec((d, d))],
        out_specs=[blk, blk, blk, blk],
        out_shape=[jax.ShapeDtypeStruct((bsz, n, d), F32)] * 2 + [jax.ShapeDtypeStruct((bsz, n, d), BF16)] * 2,
        compiler_params=_params("parallel"),
        name="mem_kv",
    )(mem, g, wk, wv)


def _proj_in_kernel(x_ref, g_ref, w_ref, wf_ref, bf_ref, u_ref, q_ref, k_ref, v_ref, kb_ref, vb_ref, lf_ref,
                    *, n_ssm, n_fox, n_heads):
    xb = _rms(x_ref[0], g_ref[...]).astype(BF16)
    z = _dot(xb, w_ref[...])
    u_ref[0] = z[:, :n_ssm]
    o = n_ssm
    q_ref[0] = (z[:, o:o + n_fox] * (FOX_HEAD_DIM ** -0.5)).astype(BF16)
    k = z[:, o + n_fox:o + 2 * n_fox]
    v = z[:, o + 2 * n_fox:o + 3 * n_fox]
    k_ref[0] = k
    v_ref[0] = v
    kb_ref[0] = k.astype(BF16)
    vb_ref[0] = v.astype(BF16)
    zf = _dot(xb, wf_ref[...])
    lf_ref[0] = _log_sigmoid(zf + bf_ref[...])[:, :n_heads]


def _proj_in(x, g, w_main, w_f, b_f, tm, n_ssm, n_fox, n_heads):
    bsz, L, d = x.shape
    nmain = w_main.shape[1]
    row = lambda n: pl.BlockSpec((1, tm, n), lambda b, i: (b, i, 0))
    outs = [jax.ShapeDtypeStruct((bsz, L, n_ssm), F32), jax.ShapeDtypeStruct((bsz, L, n_fox), BF16),
            jax.ShapeDtypeStruct((bsz, L, n_fox), F32), jax.ShapeDtypeStruct((bsz, L, n_fox), F32),
            jax.ShapeDtypeStruct((bsz, L, n_fox), BF16), jax.ShapeDtypeStruct((bsz, L, n_fox), BF16),
            jax.ShapeDtypeStruct((bsz, L, n_heads), F32)]
    return pl.pallas_call(
        functools.partial(_proj_in_kernel, n_ssm=n_ssm, n_fox=n_fox, n_heads=n_heads),
        grid=(bsz, L // tm),
        in_specs=[row(d), _const_spec((1, d)), _const_spec((d, nmain)), _const_spec((d, LANES)),
                  _const_spec((1, LANES))],
        out_specs=[row(n_ssm), row(n_fox), row(n_fox), row(n_fox), row(n_fox), row(n_fox), row(n_heads)],
        out_shape=outs,
        compiler_params=_params("parallel", "parallel"),
        name="proj_in",
    )(x, g, w_main, w_f, b_f)


def _decay_kernel(lft_ref, c0_ref, dt_ref, car_ref, *, tl):
    @pl.when(pl.program_id(1) == 0)
    def _():
        car_ref[...] = c0_ref[0]
    r = lax.broadcasted_iota(jnp.int32, (tl, tl), 0)
    c = lax.broadcasted_iota(jnp.int32, (tl, tl), 1)
    tri = (r <= c).astype(F32)
    d = jnp.dot(lft_ref[0], tri, preferred_element_type=F32, precision=_HIGHEST) + car_ref[...]
    dt_ref[0] = d
    car_ref[...] = d[:, tl - 1:tl]


def _decay_cumsum(lft, c0, tl):
    bsz, nh, L = lft.shape
    return pl.pallas_call(
        functools.partial(_decay_kernel, tl=tl),
        grid=(bsz, L // tl),
        in_specs=[pl.BlockSpec((1, nh, tl), lambda b, i: (b, 0, i)), pl.BlockSpec((1, nh, 1), lambda b, i: (b, 0, 0))],
        out_specs=pl.BlockSpec((1, nh, tl), lambda b, i: (b, 0, i)),
        out_shape=jax.ShapeDtypeStruct((bsz, nh, L), F32),
        scratch_shapes=[pltpu.VMEM((nh, 1), F32)],
        compiler_params=_params("parallel", "arbitrary"),
        name="decay_cumsum",
    )(lft, c0)


def _s5_kernel(u_ref, h0_ref, ar_ref, ai_ref, wb_ref, wc_ref, dsk_ref, wglu_ref, bglu_ref, gout_ref,
               y_ref, hl_ref, hs_ref, hst_ref, *, t_chunk, bsz, n_slab):
    @pl.when(pl.program_id(0) == 0)
    def _():
        hst_ref[...] = h0_ref[...]

    u = u_ref[...]
    ub = u.astype(BF16)
    slab_per_k = MXU_DIM // (2 * SSM_GROUP)
    for j in range(n_slab):
        kt = j // slab_per_k
        hs_ref[:, MXU_DIM * j:MXU_DIM * (j + 1)] = _dot(ub[:, MXU_DIM * kt:MXU_DIM * (kt + 1)], wb_ref[j])

    def step(t, h):
        r0 = pl.multiple_of(t * bsz, bsz)
        bu = hs_ref[pl.ds(r0, bsz), :]
        parts = []
        for j in range(n_slab):
            lo, mid, hi = MXU_DIM * j, MXU_DIM * j + LANES, MXU_DIM * (j + 1)
            re, im = h[:, lo:mid], h[:, mid:hi]
            ar = ar_ref[:, LANES * j:LANES * (j + 1)]
            ai = ai_ref[:, LANES * j:LANES * (j + 1)]
            parts.append(ar * re - ai * im + bu[:, lo:mid])
            parts.append(ar * im + ai * re + bu[:, mid:hi])
        hn = jnp.concatenate(parts, axis=1)
        hs_ref[pl.ds(r0, bsz), :] = hn
        return hn

    h = lax.fori_loop(0, t_chunk, step, hst_ref[...])
    hst_ref[...] = h
    hl_ref[...] = h

    halves = []
    for hf in range(n_slab // slab_per_k):
        acc = None
        for jj in range(slab_per_k):
            j = hf * slab_per_k + jj
            d = _dot(hs_ref[:, MXU_DIM * j:MXU_DIM * (j + 1)].astype(BF16), wc_ref[j])
            acc = d if acc is None else acc + d
        halves.append(acc)
    y = jnp.concatenate(halves, axis=1) + dsk_ref[...] * u
    y = _gelu_tanh(y)
    y = y * _sigmoid(_dot(y.astype(BF16), wglu_ref[...]) + bglu_ref[...])
    y_ref[...] = _rms(y, gout_ref[...]).astype(BF16)


def _s5(u_tb, h0, ar, ai, wb, wc, dsk, wglu, bglu, gout, t_chunk, bsz):
    rows, n_ssm = u_tb.shape
    n_state = h0.shape[1]
    n_slab = n_state // MXU_DIM
    r = t_chunk * bsz
    return pl.pallas_call(
        functools.partial(_s5_kernel, t_chunk=t_chunk, bsz=bsz, n_slab=n_slab),
        grid=(rows // r,),
        in_specs=[pl.BlockSpec((r, n_ssm), lambda c: (c, 0)), _const_spec((bsz, n_state)),
                  _const_spec((bsz, n_state // 2)), _const_spec((bsz, n_state // 2)),
                  _const_spec((n_slab, MXU_DIM, MXU_DIM)), _const_spec((n_slab, MXU_DIM, MXU_DIM)),
                  _const_spec((1, n_ssm)), _const_spec((n_ssm, n_ssm)), _const_spec((1, n_ssm)),
                  _const_spec((1, n_ssm))],
        out_specs=[pl.BlockSpec((r, n_ssm), lambda c: (c, 0)), _const_spec((bsz, n_state))],
        out_shape=[jax.ShapeDtypeStruct((rows, n_ssm), BF16), jax.ShapeDtypeStruct((bsz, n_state), F32)],
        scratch_shapes=[pltpu.VMEM((r, n_state), F32), pltpu.VMEM((bsz, n_state), F32)],
        compiler_params=_params("arbitrary"),
        name="s5_mixer",
    )(u_tb, h0, ar, ai, wb, wc, dsk, wglu, bglu, gout)


def _s5_tables(lam_re, lam_im, log_dt, b_re, b_im, c_re, c_im, bsz):
    G, P = lam_re.shape
    H = b_re.shape[-1]
    lam = lax.complex(lam_re.astype(F32), lam_im.astype(F32))
    dt = jnp.exp(log_dt.astype(F32))[:, None]
    a_bar = jnp.exp(lam * dt)
    b_bar = ((a_bar - 1.0) / lam)[..., None] * lax.complex(b_re.astype(F32), b_im.astype(F32))
    n_pair = G // 2
    per_k = MXU_DIM // (2 * H)
    eye2 = jnp.eye(2, dtype=F32)
    place = jax.nn.one_hot(jnp.arange(n_pair) % per_k, per_k, dtype=F32)

    ar = jnp.real(a_bar).reshape(n_pair, 2 * P).reshape(1, -1)
    ai = jnp.imag(a_bar).reshape(n_pair, 2 * P).reshape(1, -1)
    ar = jnp.broadcast_to(ar, (bsz, ar.shape[1]))
    ai = jnp.broadcast_to(ai, (bsz, ai.shape[1]))

    bb = jnp.stack([jnp.real(b_bar), jnp.imag(b_bar)]).reshape(2, n_pair, 2, P, H)
    wpair = jnp.einsum("ajgph,gk->jghakp", bb, eye2).reshape(n_pair, 2 * H, MXU_DIM)
    wb = jnp.einsum("jrc,jk->jkrc", wpair, place).reshape(n_pair, MXU_DIM, MXU_DIM)

    cc = jnp.stack([c_re.astype(F32), -c_im.astype(F32)]).reshape(2, n_pair, 2, H, P)
    cpair = jnp.einsum("ajghp,gk->jagpkh", cc, eye2).reshape(n_pair, MXU_DIM, 2 * H)
    wc = jnp.einsum("jnc,jk->jnkc", cpair, place).reshape(n_pair, MXU_DIM, MXU_DIM)
    return ar, ai, wb.astype(BF16), wc.astype(BF16)


def _state_to_lanes(re, im):
    bsz, G, P = re.shape
    s = jnp.stack([re, im], axis=1).reshape(bsz, 2, G // 2, 2, P)
    return s.transpose(0, 2, 1, 3, 4).reshape(bsz, 2 * G * P)


def _lanes_to_state(h, G, P):
    bsz = h.shape[0]
    s = h.reshape(bsz, G // 2, 2, 2, P).transpose(0, 2, 1, 3, 4).reshape(bsz, 2, G, P)
    return s[:, 0], s[:, 1]


def _fox_kernel(q_ref, k_ref, v_ref, dq_ref, dk_ref, o_ref, m_ref, l_ref, acc_ref, *, tq, tk, past):
    qi = pl.program_id(2)
    q = q_ref[0]
    lane = lax.broadcasted_iota(jnp.int32, (tq, LANES), 1)
    first = lane < FOX_HEAD_DIM
    zero = jnp.zeros_like(q)
    qs = (jnp.where(first, q, zero), jnp.where(first, zero, q))
    m_ref[...] = jnp.full(m_ref.shape, -1e30, F32)
    l_ref[...] = jnp.zeros(l_ref.shape, F32)
    acc_ref[...] = jnp.zeros(acc_ref.shape, F32)
    q_start = past + qi * tq
    n_full = (q_start + 1) // tk
    n_all = (q_start + tq + tk - 1) // tk

    def block(j, masked):
        ks = pl.multiple_of(j * tk, tk)
        kb = k_ref[0, pl.ds(ks, tk), :]
        vb = v_ref[0, pl.ds(ks, tk), :]
        if masked:
            qpos = q_start + lax.broadcasted_iota(jnp.int32, (tq, tk), 0)
            kpos = ks + lax.broadcasted_iota(jnp.int32, (tq, tk), 1)
            visible = kpos <= qpos
        for hh in range(2):
            s = lax.dot_general(qs[hh], kb, _NT, preferred_element_type=F32)
            s = s + dq_ref[0, 0, :, hh:hh + 1] - dk_ref[0, 0, hh:hh + 1, pl.ds(ks, tk)]
            if masked:
                s = jnp.where(visible, s, -jnp.inf)
            m_old = m_ref[hh]
            m_new = jnp.maximum(m_old, jnp.max(s, axis=1, keepdims=True))
            p = jnp.exp(s - m_new)
            alpha = jnp.exp(m_old - m_new)
            l_ref[hh] = alpha * l_ref[hh] + jnp.sum(p, axis=1, keepdims=True)
            acc_ref[hh] = alpha * acc_ref[hh] + _dot(p.astype(BF16), vb)
            m_ref[hh] = m_new

    def full_body(j, c):
        block(j, False)
        return c

    def masked_body(j, c):
        block(j, True)
        return c

    lax.fori_loop(0, n_full, full_body, 0)
    lax.fori_loop(n_full, n_all, masked_body, 0)
    o_ref[0] = jnp.where(first, acc_ref[0] / l_ref[0], acc_ref[1] / l_ref[1])


def _fox(qb, kb, vb, dq4, dk4, tq, tk, past):
    bsz, L, n_fox = qb.shape
    lk = kb.shape[1]
    n_pair = n_fox // LANES
    return pl.pallas_call(
        functools.partial(_fox_kernel, tq=tq, tk=tk, past=past),
        grid=(bsz, n_pair, L // tq),
        in_specs=[pl.BlockSpec((1, tq, LANES), lambda b, h, i: (b, i, h)),
                  pl.BlockSpec((1, lk, LANES), lambda b, h, i: (b, 0, h)),
                  pl.BlockSpec((1, lk, LANES), lambda b, h, i: (b, 0, h)),
                  pl.BlockSpec((1, 1, tq, 2), lambda b, h, i: (b, h, i, 0)),
                  pl.BlockSpec((1, 1, 2, lk), lambda b, h, i: (b, h, 0, 0))],
        out_specs=pl.BlockSpec((1, tq, LANES), lambda b, h, i: (b, i, h)),
        out_shape=jax.ShapeDtypeStruct((bsz, L, n_fox), F32),
        scratch_shapes=[pltpu.VMEM((2, tq, 1), F32), pltpu.VMEM((2, tq, 1), F32), pltpu.VMEM((2, tq, LANES), F32)],
        compiler_params=_params("parallel", "parallel", "arbitrary"),
        name="fox_attention",
    )(qb, kb, vb, dq4, dk4)


def _route_gates(logits_t, ebias_t, n_experts):
    per_group = n_experts // N_EXPERT_GROUPS
    tokens = logits_t.shape[1]
    score = _sigmoid(logits_t)
    sel = score + ebias_t
    row = lax.broadcasted_iota(jnp.int32, (per_group, tokens), 0).astype(F32)
    neg = jnp.float32(-jnp.inf)

    def first_argmax(tile, best):
        return jnp.min(jnp.where(tile == best, row, float(per_group)), axis=0, keepdims=True)

    score_g, sel_g, gsc = [], [], []
    for g in range(N_EXPERT_GROUPS):
        sg = sel[per_group * g:per_group * (g + 1), :]
        score_g.append(score[per_group * g:per_group * (g + 1), :])
        sel_g.append(sg)
        m1 = jnp.max(sg, axis=0, keepdims=True)
        rest = jnp.where(row == first_argmax(sg, m1), neg, sg)
        gsc.append(m1 + jnp.max(rest, axis=0, keepdims=True))
    cur = []
    for g in range(N_EXPERT_GROUPS):
        ahead = jnp.zeros((1, tokens), F32)
        for o in range(N_EXPERT_GROUPS):
            if o == g:
                continue
            beats = (gsc[o] >= gsc[g]) if o < g else (gsc[o] > gsc[g])
            ahead = ahead + jnp.where(beats, 1.0, 0.0)
        cur.append(jnp.where(ahead < TOPK_GROUPS, sel_g[g], neg))
    chosen = [jnp.zeros((per_group, tokens), jnp.bool_) for _ in range(N_EXPERT_GROUPS)]
    for _ in range(TOP_K):
        best = cur[0]
        for g in range(1, N_EXPERT_GROUPS):
            best = jnp.maximum(best, cur[g])
        best = jnp.max(best, axis=0, keepdims=True)
        idx = None
        for g in range(N_EXPERT_GROUPS):
            cand = jnp.min(jnp.where(cur[g] == best, row + float(per_group * g), float(n_experts)), axis=0,
                           keepdims=True)
            idx = cand if idx is None else jnp.minimum(idx, cand)
        for g in range(N_EXPERT_GROUPS):
            hit = (row + float(per_group * g)) == idx
            chosen[g] = chosen[g] | hit
            cur[g] = jnp.where(hit, neg, cur[g])
    w = [jnp.where(chosen[g], score_g[g], 0.0) for g in range(N_EXPERT_GROUPS)]
    total = w[0]
    for g in range(1, N_EXPERT_GROUPS):
        total = total + w[g]
    total = jnp.sum(total, axis=0, keepdims=True)
    return [w[g] / total * ROUTED_SCALE for g in range(N_EXPERT_GROUPS)]


def _mid_kernel(x_ref, ys_ref, yf_ref, gfox_ref, wout_ref, gmq_ref, wmq_ref, mk_ref, mv_ref, wmo_ref, gffn_ref,
                wrt_ref, ebt_ref, ws1_ref, ws3_ref, ws2_ref, xs_ref, xn_ref, gt_ref, *, n_experts):
    x = x_ref[0]
    yfn = _rms(yf_ref[0], gfox_ref[...]).astype(BF16)
    mix = jnp.concatenate([ys_ref[0], yfn], axis=1)
    x1 = x + _dot(mix, wout_ref[...])

    qm = _dot(_rms(x1, gmq_ref[...]).astype(BF16), wmq_ref[...])
    hd = qm.shape[1] // MEM_HEADS
    heads = []
    for h in range(MEM_HEADS):
        qh = (qm[:, hd * h:hd * (h + 1)] * (hd ** -0.5)).astype(BF16)
        s = lax.dot_general(qh, mk_ref[0, :, hd * h:hd * (h + 1)], _NT, preferred_element_type=F32)
        p = jnp.exp(s - jnp.max(s, axis=1, keepdims=True))
        o = _dot(p.astype(BF16), mv_ref[0, :, hd * h:hd * (h + 1)]) / jnp.sum(p, axis=1, keepdims=True)
        heads.append(o.astype(BF16))
    x2 = x1 + _dot(jnp.concatenate(heads, axis=1), wmo_ref[...])

    xn = _rms(x2, gffn_ref[...])
    xnb = xn.astype(BF16)
    hidden = _silu(_dot(xnb, ws1_ref[...])) * _dot(xnb, ws3_ref[...])
    xs_ref[0] = x2 + _dot(hidden.astype(BF16), ws2_ref[...])
    xn_ref[0] = xnb

    logits_t = lax.dot_general(wrt_ref[...], xn, _NT, preferred_element_type=F32, precision=_HIGHEST)
    gates = _route_gates(logits_t, ebt_ref[...], n_experts)
    per_group = n_experts // N_EXPERT_GROUPS
    for g in range(N_EXPERT_GROUPS):
        gt_ref[0, per_group * g:per_group * (g + 1), :] = gates[g]


def _mid(x, ys, yf, gfox, wout, gmq, wmq, mkb, mvb, wmo, gffn, wrt, ebt, ws1, ws3, ws2, tm):
    bsz, L, d = x.shape
    n_ssm, n_fox = ys.shape[2], yf.shape[2]
    n_mem = mkb.shape[1]
    n_experts = wrt.shape[0]
    fs = ws1.shape[1]
    row = lambda n: pl.BlockSpec((1, tm, n), lambda b, i: (b, i, 0))
    memspec = pl.BlockSpec((1, n_mem, d), lambda b, i: (b, 0, 0))
    return pl.pallas_call(
        functools.partial(_mid_kernel, n_experts=n_experts),
        grid=(bsz, L // tm),
        in_specs=[row(d), row(n_ssm), row(n_fox), _const_spec((1, n_fox)), _const_spec((d, d)), _const_spec((1, d)),
                  _const_spec((d, d)), memspec, memspec, _const_spec((d, d)), _const_spec((1, d)),
                  _const_spec((n_experts, d)), _const_spec((n_experts, 1)), _const_spec((d, fs)), _const_spec((d, fs)),
                  _const_spec((fs, d))],
        out_specs=[row(d), row(d), pl.BlockSpec((1, n_experts, tm), lambda b, i: (b, 0, i))],
        out_shape=[jax.ShapeDtypeStruct((bsz, L, d), F32), jax.ShapeDtypeStruct((bsz, L, d), BF16),
                   jax.ShapeDtypeStruct((bsz, n_experts, L), F32)],
        compiler_params=_params("parallel", "parallel"),
        name="mid_block",
    )(x, ys, yf, gfox, wout, gmq, wmq, mkb, mvb, wmo, gffn, wrt, ebt, ws1, ws3, ws2)


def _moe_kernel(xn_ref, g_ref, xs_ref, w1_ref, w3_ref, w2_ref, gfin_ref, o_ref, *, n_experts):
    e = pl.program_id(1)

    @pl.when(e == 0)
    def _():
        o_ref[...] = xs_ref[...]

    xn = xn_ref[...]
    g = g_ref[...]
    lane = lax.broadcasted_iota(jnp.int32, g.shape, 1)
    col = jnp.sum(jnp.where(lane == e, g, 0.0), axis=1, keepdims=True)
    h = _silu(_dot(xn, w1_ref[0])) * _dot(xn, w3_ref[0])
    o_ref[...] += _dot((h * col).astype(BF16), w2_ref[0])

    @pl.when(e == n_experts - 1)
    def _():
        o_ref[...] = _rms(o_ref[...], gfin_ref[...])


def _moe(xn, gates, xs, w1, w3, w2, gfin, tm):
    m, d = xn.shape
    n_experts, _, f = w1.shape
    row = lambda n: pl.BlockSpec((tm, n), lambda i, e: (i, 0))
    return pl.pallas_call(
        functools.partial(_moe_kernel, n_experts=n_experts),
        grid=(m // tm, n_experts),
        in_specs=[row(d), row(n_experts), row(d),
                  pl.BlockSpec((1, d, f), lambda i, e: (e, 0, 0)), pl.BlockSpec((1, d, f), lambda i, e: (e, 0, 0)),
                  pl.BlockSpec((1, f, d), lambda i, e: (e, 0, 0)), _const_spec((1, d))],
        out_specs=row(d),
        out_shape=jax.ShapeDtypeStruct((m, d), F32),
        compiler_params=_params("parallel", "arbitrary"),
        name="moe_experts",
    )(xn, gates, xs, w1, w3, w2, gfin)


def _tiles(L):
    t_row = min(L, 512)
    t_key = 512
    t_scan = min(L, 64)
    return t_row, t_key, t_scan


def _group(x, h0_re, h0_im, past_k, past_v, past_lf, mkb, mvb, p, w):
    bsz, L, d = x.shape
    n_heads = p["b_f"].shape[0]
    n_fox = n_heads * FOX_HEAD_DIM
    n_ssm = p["d_skip"].shape[0]
    G, P = p["lam_re"].shape
    t_row, t_key, t_scan = _tiles(L)

    u, qb, k, v, kb, vb, lf = _proj_in(x, w["g_mix"], w["w_main"], w["w_f"], w["b_f"], t_row, n_ssm, n_fox, n_heads)

    u_tb = u.transpose(1, 0, 2).reshape(L * bsz, n_ssm)
    if h0_re is None:
        h0 = jnp.zeros((bsz, 2 * G * P), F32)
    else:
        h0 = _state_to_lanes(h0_re.astype(F32), h0_im.astype(F32))
    ys_tb, h_last = _s5(u_tb, h0, w["ar"], w["ai"], w["wb"], w["wc"], w["d_skip"], w["w_glu"], w["b_glu"],
                        w["g_ssm_out"], t_scan, bsz)
    ys = ys_tb.reshape(L, bsz, n_ssm).transpose(1, 0, 2)
    hl_re, hl_im = _lanes_to_state(h_last, G, P)

    if past_k is None:
        past = 0
        lf_all, k_all, v_all = lf, kb, vb
    else:
        past = past_k.shape[1]
        lf_all = jnp.concatenate([past_lf.astype(F32), lf], axis=1)
        k_all = jnp.concatenate([past_k.reshape(bsz, past, n_fox).astype(BF16), kb], axis=1)
        v_all = jnp.concatenate([past_v.reshape(bsz, past, n_fox).astype(BF16), vb], axis=1)
    lk = lf_all.shape[1]
    lk_pad = -(-lk // t_key) * t_key
    lf_t = jnp.pad(lf_all.transpose(0, 2, 1), ((0, 0), (0, 0), (0, lk_pad - lk)))
    dcum_t = _decay_cumsum(lf_t, jnp.zeros((bsz, n_heads, 1), F32), t_key)
    dq4 = dcum_t[:, :, past:past + L].reshape(bsz, n_heads // 2, 2, L).transpose(0, 1, 3, 2)
    dk4 = dcum_t.reshape(bsz, n_heads // 2, 2, lk_pad)
    k_all = jnp.pad(k_all, ((0, 0), (0, lk_pad - lk), (0, 0)))
    v_all = jnp.pad(v_all, ((0, 0), (0, lk_pad - lk), (0, 0)))
    yf = _fox(qb, k_all, v_all, dq4, dk4, t_row, t_key, past)

    xs, xnb, gates_t = _mid(x, ys, yf, w["g_fox_out"], w["w_out"], w["g_mem_q"], w["w_mq"], mkb, mvb, w["w_mo"],
                            w["g_ffn"], w["w_router_t"], w["e_bias_t"], w["ws1"], w["ws3"], w["ws2"], t_row)
    m = bsz * L
    y = _moe(xnb.reshape(m, d), gates_t.transpose(0, 2, 1).reshape(m, -1), xs.reshape(m, d), w["w1"], w["w3"], w["w2"], w["g_final"], min(m, 1024))
    return (y.reshape(bsz, L, d), hl_re, hl_im, k.reshape(bsz, L, n_heads, FOX_HEAD_DIM),
            v.reshape(bsz, L, n_heads, FOX_HEAD_DIM), lf)


def kernel(x_prompt, x_sample, state_ssm_re, state_ssm_im, cache_fox_k, cache_fox_v, cache_fox_logf, cache_mem_k, cache_mem_v, mem_prompt, g_mix, w_in, b_f, lam_re, lam_im, log_dt, b_re, b_im, c_re, c_im, d_skip, w_glu, b_glu, g_ssm_out, g_fox_out, w_out, g_mem_q, g_mem_kv, w_mq, w_mk, w_mv, w_mo, g_ffn, w_router, e_bias, w1, w3, w2, ws1, ws3, ws2, g_final):
    depth = w_in.shape[0]
    bsz = x_prompt.shape[0]
    hp, hs = x_prompt, x_sample
    outs_p, outs_s = [], []
    for l in range(depth):
        p = dict(b_f=b_f[l], lam_re=lam_re[l], d_skip=d_skip[l])
        n_heads = b_f.shape[1]
        n_ssm = d_skip.shape[1]
        n_main = w_in.shape[2] - n_heads
        row = lambda a: a.reshape(1, -1).astype(F32)
        ar, ai, wb, wc = _s5_tables(lam_re[l], lam_im[l], log_dt[l], b_re[l], b_im[l], c_re[l], c_im[l], bsz)
        w = dict(
            g_mix=row(g_mix[l]), w_main=w_in[l][:, :n_main].astype(BF16),
            w_f=jnp.pad(w_in[l][:, n_main:], ((0, 0), (0, LANES - n_heads))).astype(BF16),
            b_f=jnp.pad(b_f[l].astype(F32), (0, LANES - n_heads)).reshape(1, LANES),
            ar=ar, ai=ai, wb=wb, wc=wc, d_skip=row(d_skip[l]), w_glu=w_glu[l].astype(BF16), b_glu=row(b_glu[l]),
            g_ssm_out=row(g_ssm_out[l]), g_fox_out=row(g_fox_out[l]), w_out=w_out[l].astype(BF16),
            g_mem_q=row(g_mem_q[l]), w_mq=w_mq[l].astype(BF16), w_mo=w_mo[l].astype(BF16), g_ffn=row(g_ffn[l]),
            w_router_t=w_router[l].astype(F32).T, e_bias_t=e_bias[l].astype(F32).reshape(-1, 1),
            ws1=ws1[l].astype(BF16), ws3=ws3[l].astype(BF16), ws2=ws2[l].astype(BF16),
            w1=w1[l].astype(BF16), w3=w3[l].astype(BF16), w2=w2[l].astype(BF16), g_final=row(g_final))
        assert depth == 1, "final norm fusion assumes a single layer"
        mk_p, mv_p, mkb_p, mvb_p = _mem_kv(mem_prompt, row(g_mem_kv[l]), w_mk[l].astype(BF16), w_mv[l].astype(BF16))
        nm, mh = mem_prompt.shape[1], MEM_HEADS
        hp, re_p, im_p, k_p, v_p, lf_p = _group(hp, None, None, None, None, None, mkb_p, mvb_p, p, w)
        cm_k = cache_mem_k[l].reshape(bsz, nm, -1).astype(BF16)
        cm_v = cache_mem_v[l].reshape(bsz, nm, -1).astype(BF16)
        hs, re_s, im_s, k_s, v_s, lf_s = _group(hs, state_ssm_re[l], state_ssm_im[l], cache_fox_k[l], cache_fox_v[l],
                                                cache_fox_logf[l], cm_k, cm_v, p, w)
        outs_p.append((re_p, im_p, k_p, v_p, lf_p, mk_p.reshape(bsz, nm, mh, -1), mv_p.reshape(bsz, nm, mh, -1)))
        outs_s.append((re_s, im_s, k_s, v_s, lf_s))
    stack = lambda outs, i: jnp.stack([o[i] for o in outs])
    return (hp, hs) + tuple(stack(outs_p, i) for i in range(7)) + tuple(stack(outs_s, i) for i in range(5))
```

```python
import functools
import math

import jax
import jax.numpy as jnp
from jax import lax
from jax.experimental import pallas as pl
from jax.experimental.pallas import tpu as pltpu

F32 = jnp.float32
BF16 = jnp.bfloat16

SSM_GROUP = 16
SSM_STATE = 64
FOX_HEAD_DIM = 64
MEM_HEADS = 4
TOP_K = 8
N_EXPERT_GROUPS = 8
TOPK_GROUPS = 4
ROUTED_SCALE = 2.5
RMS_EPS = 1e-6

LANES = 128
SUBLANES = 8
MXU_DIM = 256
VMEM_LIMIT_BYTES = 56 * 1024 * 1024

_HIGHEST = lax.Precision.HIGHEST
_NT = (((1,), (1,)), ((), ()))


def _params(*sem):
    return pltpu.CompilerParams(dimension_semantics=sem, vmem_limit_bytes=VMEM_LIMIT_BYTES)


def _rms(x, g):
    return x * lax.rsqrt(jnp.mean(x * x, axis=-1, keepdims=True) + RMS_EPS) * g


def _sigmoid(x):
    return 1.0 / (1.0 + jnp.exp(-x))


def _silu(x):
    return x * _sigmoid(x)


def _gelu_tanh(x):
    return x * (0.5 * (1.0 + jnp.tanh(math.sqrt(2.0 / math.pi) * (x + 0.044715 * (x * x * x)))))


def _log_sigmoid(x):
    return jnp.minimum(x, 0.0) - jnp.log1p(jnp.exp(-jnp.abs(x)))


def _dot(a, b):
    return jnp.dot(a, b, preferred_element_type=F32)


def _const_spec(shape):
    nd = len(shape)
    return pl.BlockSpec(shape, lambda *_: (0,) * nd)


def _memkv_kernel(m_ref, g_ref, wk_ref, wv_ref, k_ref, v_ref, kb_ref, vb_ref):
    mn = _rms(m_ref[0], g_ref[...]).astype(BF16)
    k = _dot(mn, wk_ref[...])
    v = _dot(mn, wv_ref[...])
    k_ref[0] = k
    v_ref[0] = v
    kb_ref[0] = k.astype(BF16)
    vb_ref[0] = v.astype(BF16)


def _mem_kv(mem, g, wk, wv):
    bsz, n, d = mem.shape
    blk = pl.BlockSpec((1, n, d), lambda b: (b, 0, 0))
    return pl.pallas_call(
        _memkv_kernel,
        grid=(bsz,),
        in_specs=[blk, _const_spec((1, d)), _const_spec((d, d)), _const_spec((d, d))],
        out_specs=[blk, blk, blk, blk],
        out_shape=[jax.ShapeDtypeStruct((bsz, n, d), F32)] * 2 + [jax.ShapeDtypeStruct((bsz, n, d), BF16)] * 2,
        compiler_params=_params("parallel"),
        name="mem_kv",
    )(mem, g, wk, wv)


def _proj_in_kernel(x_ref, g_ref, w_ref, wf_ref, bf_ref, u_ref, q_ref, k_ref, v_ref, kb_ref, vb_ref, lf_ref,
                    *, n_ssm, n_fox, n_heads):
    xb = _rms(x_ref[0], g_ref[...]).astype(BF16)
    z = _dot(xb, w_ref[...])
    u_ref[0] = z[:, :n_ssm]
    o = n_ssm
    q_ref[0] = (z[:, o:o + n_fox] * (FOX_HEAD_DIM ** -0.5 * math.log2(math.e))).astype(BF16)
    k = z[:, o + n_fox:o + 2 * n_fox]
    v = z[:, o + 2 * n_fox:o + 3 * n_fox]
    k_ref[0] = k
    v_ref[0] = v
    kb_ref[0] = k.astype(BF16)
    vb_ref[0] = v.astype(BF16)
    zf = _dot(xb, wf_ref[...])
    lf_ref[0] = _log_sigmoid(zf + bf_ref[...])[:, :n_heads]


def _proj_in(x, g, w_main, w_f, b_f, tm, n_ssm, n_fox, n_heads):
    bsz, L, d = x.shape
    nmain = w_main.shape[1]
    row = lambda n: pl.BlockSpec((1, tm, n), lambda b, i: (b, i, 0))
    outs = [jax.ShapeDtypeStruct((bsz, L, n_ssm), F32), jax.ShapeDtypeStruct((bsz, L, n_fox), BF16),
            jax.ShapeDtypeStruct((bsz, L, n_fox), F32), jax.ShapeDtypeStruct((bsz, L, n_fox), F32),
            jax.ShapeDtypeStruct((bsz, L, n_fox), BF16), jax.ShapeDtypeStruct((bsz, L, n_fox), BF16),
            jax.ShapeDtypeStruct((bsz, L, n_heads), F32)]
    return pl.pallas_call(
        functools.partial(_proj_in_kernel, n_ssm=n_ssm, n_fox=n_fox, n_heads=n_heads),
        grid=(bsz, L // tm),
        in_specs=[row(d), _const_spec((1, d)), _const_spec((d, nmain)), _const_spec((d, LANES)),
                  _const_spec((1, LANES))],
        out_specs=[row(n_ssm), row(n_fox), row(n_fox), row(n_fox), row(n_fox), row(n_fox), row(n_heads)],
        out_shape=outs,
        compiler_params=_params("parallel", "parallel"),
        name="proj_in",
    )(x, g, w_main, w_f, b_f)


def _decay_kernel(lft_ref, c0_ref, dt_ref, car_ref, *, tl):
    @pl.when(pl.program_id(1) == 0)
    def _():
        car_ref[...] = c0_ref[0]
    r = lax.broadcasted_iota(jnp.int32, (tl, tl), 0)
    c = lax.broadcasted_iota(jnp.int32, (tl, tl), 1)
    tri = (r <= c).astype(F32)
    d = jnp.dot(lft_ref[0], tri, preferred_element_type=F32, precision=_HIGHEST) + car_ref[...]
    car_ref[...] = d[:, tl - 1:tl]
    rest = d * math.log2(math.e)
    for piece in range(3):
        part = rest.astype(BF16).astype(F32)
        dt_ref[0, piece] = part
        rest = rest - part


def _decay_cumsum(lft, c0, tl):
    bsz, nh, L = lft.shape
    return pl.pallas_call(
        functools.partial(_decay_kernel, tl=tl),
        grid=(bsz, L // tl),
        in_specs=[pl.BlockSpec((1, nh, tl), lambda b, i: (b, 0, i)), pl.BlockSpec((1, nh, 1), lambda b, i: (b, 0, 0))],
        out_specs=pl.BlockSpec((1, 3, nh, tl), lambda b, i: (b, 0, 0, i)),
        out_shape=jax.ShapeDtypeStruct((bsz, 3, nh, L), F32),
        scratch_shapes=[pltpu.VMEM((nh, 1), F32)],
        compiler_params=_params("parallel", "arbitrary"),
        name="decay_cumsum",
    )(lft, c0)


def _s5_kernel(u_ref, h0_ref, ar_ref, ai_ref, wb_ref, wc_ref, dsk_ref, wglu_ref, bglu_ref, gout_ref,
               y_ref, hl_ref, hs_ref, hst_ref, *, t_chunk, bsz, n_slab):
    @pl.when(pl.program_id(0) == 0)
    def _():
        hst_ref[...] = h0_ref[...]

    u = u_ref[...]
    ub = u.astype(BF16)
    slab_per_k = MXU_DIM // (2 * SSM_GROUP)
    for j in range(n_slab):
        kt = j // slab_per_k
        hs_ref[:, MXU_DIM * j:MXU_DIM * (j + 1)] = _dot(ub[:, MXU_DIM * kt:MXU_DIM * (kt + 1)], wb_ref[j])

    def step(t, h):
        r0 = pl.multiple_of(t * bsz, bsz)
        bu = hs_ref[pl.ds(r0, bsz), :]
        parts = []
        for j in range(n_slab):
            lo, mid, hi = MXU_DIM * j, MXU_DIM * j + LANES, MXU_DIM * (j + 1)
            re, im = h[:, lo:mid], h[:, mid:hi]
            ar = ar_ref[:, LANES * j:LANES * (j + 1)]
            ai = ai_ref[:, LANES * j:LANES * (j + 1)]
            parts.append(ar * re - ai * im + bu[:, lo:mid])
            parts.append(ar * im + ai * re + bu[:, mid:hi])
        hn = jnp.concatenate(parts, axis=1)
        hs_ref[pl.ds(r0, bsz), :] = hn
        return hn

    h = lax.fori_loop(0, t_chunk, step, hst_ref[...])
    hst_ref[...] = h
    hl_ref[...] = h

    halves = []
    for hf in range(n_slab // slab_per_k):
        acc = None
        for jj in range(slab_per_k):
            j = hf * slab_per_k + jj
            d = _dot(hs_ref[:, MXU_DIM * j:MXU_DIM * (j + 1)].astype(BF16), wc_ref[j])
            acc = d if acc is None else acc + d
        halves.append(acc)
    y = jnp.concatenate(halves, axis=1) + dsk_ref[...] * u
    y = _gelu_tanh(y)
    y = y * _sigmoid(_dot(y.astype(BF16), wglu_ref[...]) + bglu_ref[...])
    y_ref[...] = _rms(y, gout_ref[...]).astype(BF16)


def _s5(u_tb, h0, ar, ai, wb, wc, dsk, wglu, bglu, gout, t_chunk, bsz):
    rows, n_ssm = u_tb.shape
    n_state = h0.shape[1]
    n_slab = n_state // MXU_DIM
    r = t_chunk * bsz
    return pl.pallas_call(
        functools.partial(_s5_kernel, t_chunk=t_chunk, bsz=bsz, n_slab=n_slab),
        grid=(rows // r,),
        in_specs=[pl.BlockSpec((r, n_ssm), lambda c: (c, 0)), _const_spec((bsz, n_state)),
                  _const_spec((bsz, n_state // 2)), _const_spec((bsz, n_state // 2)),
                  _const_spec((n_slab, MXU_DIM, MXU_DIM)), _const_spec((n_slab, MXU_DIM, MXU_DIM)),
                  _const_spec((1, n_ssm)), _const_spec((n_ssm, n_ssm)), _const_spec((1, n_ssm)),
                  _const_spec((1, n_ssm))],
        out_specs=[pl.BlockSpec((r, n_ssm), lambda c: (c, 0)), _const_spec((bsz, n_state))],
        out_shape=[jax.ShapeDtypeStruct((rows, n_ssm), BF16), jax.ShapeDtypeStruct((bsz, n_state), F32)],
        scratch_shapes=[pltpu.VMEM((r, n_state), F32), pltpu.VMEM((bsz, n_state), F32)],
        compiler_params=_params("arbitrary"),
        name="s5_mixer",
    )(u_tb, h0, ar, ai, wb, wc, dsk, wglu, bglu, gout)


def _s5_tables(lam_re, lam_im, log_dt, b_re, b_im, c_re, c_im, bsz):
    G, P = lam_re.shape
    H = b_re.shape[-1]
    lr, li = lam_re.astype(F32), lam_im.astype(F32)
    dt = jnp.exp(log_dt.astype(F32))[:, None]
    mag = jnp.exp(lr * dt)
    a_re, a_im = mag * jnp.cos(li * dt), mag * jnp.sin(li * dt)
    den = lr * lr + li * li
    c_r = ((a_re - 1.0) * lr + a_im * li) / den
    c_i = (a_im * lr - (a_re - 1.0) * li) / den
    br, bi = b_re.astype(F32), b_im.astype(F32)
    bbar_re = c_r[..., None] * br - c_i[..., None] * bi
    bbar_im = c_r[..., None] * bi + c_i[..., None] * br
    n_pair = G // 2
    per_k = MXU_DIM // (2 * H)
    eye2 = jnp.eye(2, dtype=F32)
    place = jax.nn.one_hot(jnp.arange(n_pair) % per_k, per_k, dtype=F32)

    ar = jnp.broadcast_to(a_re.reshape(1, -1), (bsz, G * P))
    ai = jnp.broadcast_to(a_im.reshape(1, -1), (bsz, G * P))

    bb = jnp.stack([bbar_re, bbar_im]).reshape(2, n_pair, 2, P, H)
    wpair = jnp.einsum("ajgph,gk->jghakp", bb, eye2).reshape(n_pair, 2 * H, MXU_DIM)
    wb = jnp.einsum("jrc,jk->jkrc", wpair, place).reshape(n_pair, MXU_DIM, MXU_DIM)

    cc = jnp.stack([c_re.astype(F32), -c_im.astype(F32)]).reshape(2, n_pair, 2, H, P)
    cpair = jnp.einsum("ajghp,gk->jagpkh", cc, eye2).reshape(n_pair, MXU_DIM, 2 * H)
    wc = jnp.einsum("jnc,jk->jnkc", cpair, place).reshape(n_pair, MXU_DIM, MXU_DIM)
    return ar, ai, wb.astype(BF16), wc.astype(BF16)


def _state_to_lanes(re, im):
    bsz, G, P = re.shape
    s = jnp.stack([re, im], axis=1).reshape(bsz, 2, G // 2, 2, P)
    return s.transpose(0, 2, 1, 3, 4).reshape(bsz, 2 * G * P)


def _lanes_to_state(h, G, P):
    bsz = h.shape[0]
    s = h.reshape(bsz, G // 2, 2, 2, P).transpose(0, 2, 1, 3, 4).reshape(bsz, 2, G, P)
    return s[:, 0], s[:, 1]


FOX_AUG_ROWS = 16


def _fox_kernel(qt_ref, qa_ref, k_ref, vt_ref, o_ref, m_ref, l_ref, acc_ref, *, tq, tk, past):
    qi = pl.program_id(2)
    qt = qt_ref[0]
    row = lax.broadcasted_iota(jnp.int32, (LANES, tq), 0)
    zero = jnp.zeros_like(qt)
    pad = jnp.zeros((LANES - FOX_AUG_ROWS, tq), BF16)
    qts = []
    for hh in range(2):
        own = (row < FOX_HEAD_DIM) if hh == 0 else (row >= FOX_HEAD_DIM)
        aug = qa_ref[0, 0, FOX_AUG_ROWS * hh:FOX_AUG_ROWS * (hh + 1), :]
        qts.append(jnp.concatenate([jnp.where(own, qt, zero), aug, pad], axis=0))
    m_ref[...] = jnp.full(m_ref.shape, -1e30, F32)
    l_ref[...] = jnp.zeros(l_ref.shape, F32)
    acc_ref[...] = jnp.zeros(acc_ref.shape, F32)
    q_start = past + qi * tq
    n_full = (q_start + 1) // tk
    n_all = (q_start + tq + tk - 1) // tk

    def block(j, masked):
        ks = pl.multiple_of(j * tk, tk)
        kb = k_ref[0, 0, pl.ds(ks, tk), :]
        if masked:
            kpos = ks + lax.broadcasted_iota(jnp.int32, (tk, tq), 0)
            qpos = q_start + lax.broadcasted_iota(jnp.int32, (tk, tq), 1)
            visible = kpos <= qpos
        for hh in range(2):
            st = _dot(kb, qts[hh])
            if masked:
                st = jnp.where(visible, st, -jnp.inf)
            m_old = m_ref[hh]
            m_new = jnp.maximum(m_old, jnp.max(st, axis=0, keepdims=True))
            p = jnp.exp2(st - m_new)
            alpha = jnp.exp2(m_old - m_new)
            l_ref[hh] = alpha * l_ref[hh] + jnp.sum(p, axis=0, keepdims=True)
            vt = vt_ref[0, FOX_HEAD_DIM * hh:FOX_HEAD_DIM * (hh + 1), pl.ds(ks, tk)]
            acc_ref[hh] = alpha * acc_ref[hh] + _dot(vt, p.astype(BF16))
            m_ref[hh] = m_new

    def full_body(j, c):
        block(j, False)
        return c

    def masked_body(j, c):
        block(j, True)
        return c

    lax.fori_loop(0, n_full, full_body, 0)
    lax.fori_loop(n_full, n_all, masked_body, 0)
    out_t = jnp.concatenate([acc_ref[0] / l_ref[0], acc_ref[1] / l_ref[1]], axis=0)
    o_ref[0] = out_t.T


def _fox(q_t, q_aug, k_aug, v_t, tq, tk, past):
    bsz, n_fox, L = q_t.shape
    n_pair = n_fox // LANES
    lk = k_aug.shape[2]
    return pl.pallas_call(
        functools.partial(_fox_kernel, tq=tq, tk=tk, past=past),
        grid=(bsz, n_pair, L // tq),
        in_specs=[pl.BlockSpec((1, LANES, tq), lambda b, h, i: (b, h, i)),
                  pl.BlockSpec((1, 1, 2 * FOX_AUG_ROWS, tq), lambda b, h, i: (b, h, 0, i)),
                  pl.BlockSpec((1, 1, lk, 2 * LANES), lambda b, h, i: (b, h, 0, 0)),
                  pl.BlockSpec((1, LANES, lk), lambda b, h, i: (b, h, 0))],
        out_specs=pl.BlockSpec((1, tq, LANES), lambda b, h, i: (b, i, h)),
        out_shape=jax.ShapeDtypeStruct((bsz, L, n_fox), F32),
        scratch_shapes=[pltpu.VMEM((2, 1, tq), F32), pltpu.VMEM((2, 1, tq), F32),
                        pltpu.VMEM((2, FOX_HEAD_DIM, tq), F32)],
        compiler_params=_params("parallel", "parallel", "arbitrary"),
        name="fox_attention",
    )(q_t, q_aug, k_aug, v_t)


def _fox_operands(qb, k_all, v_all, d_pieces, past, L):
    bsz, lk_pad, n_fox = k_all.shape
    n_heads = n_fox // FOX_HEAD_DIM
    n_pair = n_heads // 2
    pieces = d_pieces.astype(BF16)
    dk = (-pieces).reshape(bsz, 3, n_pair, 2, lk_pad)
    dk = dk.transpose(0, 2, 4, 3, 1).reshape(bsz, n_pair, lk_pad, 6)
    ones_k = jnp.ones((bsz, n_pair, lk_pad, 3), BF16)
    zeros_k = jnp.zeros((bsz, n_pair, lk_pad, LANES - 9), BF16)
    k_pair = k_all.reshape(bsz, lk_pad, n_pair, LANES).transpose(0, 2, 1, 3)
    k_aug = jnp.concatenate([k_pair, dk, ones_k, zeros_k], axis=-1)

    dq = pieces[:, :, :, past:past + L].transpose(0, 2, 1, 3)
    sel = jax.nn.one_hot(jnp.arange(n_heads) % 2, 2, dtype=BF16)
    ones_q = jnp.broadcast_to(jnp.repeat(sel, 3, axis=1)[None, :, :, None], (bsz, n_heads, 6, L))
    zeros_q = jnp.zeros((bsz, n_heads, FOX_AUG_ROWS - 9, L), BF16)
    q_aug = jnp.concatenate([ones_q, dq, zeros_q], axis=2).reshape(bsz, n_pair, 2 * FOX_AUG_ROWS, L)
    return qb.transpose(0, 2, 1), q_aug, k_aug, v_all.transpose(0, 2, 1)


def _route_gates(logits_t, ebias_t, n_experts):
    per_group = n_experts // N_EXPERT_GROUPS
    tokens = logits_t.shape[1]
    score = _sigmoid(logits_t)
    sel = score + ebias_t
    row = lax.broadcasted_iota(jnp.int32, (per_group, tokens), 0).astype(F32)
    neg = jnp.float32(-jnp.inf)

    def first_argmax(tile, best):
        return jnp.min(jnp.where(tile == best, row, float(per_group)), axis=0, keepdims=True)

    score_g, sel_g, gsc = [], [], []
    for g in range(N_EXPERT_GROUPS):
        sg = sel[per_group * g:per_group * (g + 1), :]
        score_g.append(score[per_group * g:per_group * (g + 1), :])
        sel_g.append(sg)
        m1 = jnp.max(sg, axis=0, keepdims=True)
        rest = jnp.where(row == first_argmax(sg, m1), neg, sg)
        gsc.append(m1 + jnp.max(rest, axis=0, keepdims=True))
    cur = []
    for g in range(N_EXPERT_GROUPS):
        ahead = jnp.zeros((1, tokens), F32)
        for o in range(N_EXPERT_GROUPS):
            if o == g:
                continue
            beats = (gsc[o] >= gsc[g]) if o < g else (gsc[o] > gsc[g])
            ahead = ahead + jnp.where(beats, 1.0, 0.0)
        cur.append(jnp.where(ahead < TOPK_GROUPS, sel_g[g], neg))
    chosen = [jnp.zeros((per_group, tokens), jnp.bool_) for _ in range(N_EXPERT_GROUPS)]
    for _ in range(TOP_K):
        best = cur[0]
        for g in range(1, N_EXPERT_GROUPS):
            best = jnp.maximum(best, cur[g])
        best = jnp.max(best, axis=0, keepdims=True)
        idx = None
        for g in range(N_EXPERT_GROUPS):
            cand = jnp.min(jnp.where(cur[g] == best, row + float(per_group * g), float(n_experts)), axis=0,
                           keepdims=True)
            idx = cand if idx is None else jnp.minimum(idx, cand)
        for g in range(N_EXPERT_GROUPS):
            hit = (row + float(per_group * g)) == idx
            chosen[g] = chosen[g] | hit
            cur[g] = jnp.where(hit, neg, cur[g])
    w = [jnp.where(chosen[g], score_g[g], 0.0) for g in range(N_EXPERT_GROUPS)]
    total = w[0]
    for g in range(1, N_EXPERT_GROUPS):
        total = total + w[g]
    total = jnp.sum(total, axis=0, keepdims=True)
    return [w[g] / total * ROUTED_SCALE for g in range(N_EXPERT_GROUPS)]


def _mid_kernel(x_ref, ys_ref, yf_ref, gfox_ref, wout_ref, gmq_ref, wmq_ref, mk_ref, mv_ref, wmo_ref, gffn_ref,
                wrt_ref, ebt_ref, ws1_ref, ws3_ref, ws2_ref, xs_ref, xn_ref, gt_ref, *, n_experts):
    x = x_ref[0]
    yfn = _rms(yf_ref[0], gfox_ref[...]).astype(BF16)
    mix = jnp.concatenate([ys_ref[0], yfn], axis=1)
    x1 = x + _dot(mix, wout_ref[...])

    qm = _dot(_rms(x1, gmq_ref[...]).astype(BF16), wmq_ref[...])
    hd = qm.shape[1] // MEM_HEADS
    heads = []
    for h in range(MEM_HEADS):
        qh = (qm[:, hd * h:hd * (h + 1)] * (hd ** -0.5)).astype(BF16)
        s = lax.dot_general(qh, mk_ref[0, :, hd * h:hd * (h + 1)], _NT, preferred_element_type=F32)
        p = jnp.exp(s - jnp.max(s, axis=1, keepdims=True))
        o = _dot(p.astype(BF16), mv_ref[0, :, hd * h:hd * (h + 1)]) / jnp.sum(p, axis=1, keepdims=True)
        heads.append(o.astype(BF16))
    x2 = x1 + _dot(jnp.concatenate(heads, axis=1), wmo_ref[...])

    xn = _rms(x2, gffn_ref[...])
    xnb = xn.astype(BF16)
    hidden = _silu(_dot(xnb, ws1_ref[...])) * _dot(xnb, ws3_ref[...])
    xs_ref[0] = x2 + _dot(hidden.astype(BF16), ws2_ref[...])
    xn_ref[0] = xnb

    logits_t = lax.dot_general(wrt_ref[...], xn, _NT, preferred_element_type=F32, precision=_HIGHEST)
    gates = _route_gates(logits_t, ebt_ref[...], n_experts)
    per_group = n_experts // N_EXPERT_GROUPS
    for g in range(N_EXPERT_GROUPS):
        gt_ref[0, per_group * g:per_group * (g + 1), :] = gates[g]


def _mid(x, ys, yf, gfox, wout, gmq, wmq, mkb, mvb, wmo, gffn, wrt, ebt, ws1, ws3, ws2, tm):
    bsz, L, d = x.shape
    n_ssm, n_fox = ys.shape[2], yf.shape[2]
    n_mem = mkb.shape[1]
    n_experts = wrt.shape[0]
    fs = ws1.shape[1]
    row = lambda n: pl.BlockSpec((1, tm, n), lambda b, i: (b, i, 0))
    memspec = pl.BlockSpec((1, n_mem, d), lambda b, i: (b, 0, 0))
    return pl.pallas_call(
        functools.partial(_mid_kernel, n_experts=n_experts),
        grid=(bsz, L // tm),
        in_specs=[row(d), row(n_ssm), row(n_fox), _const_spec((1, n_fox)), _const_spec((d, d)), _const_spec((1, d)),
                  _const_spec((d, d)), memspec, memspec, _const_spec((d, d)), _const_spec((1, d)),
                  _const_spec((n_experts, d)), _const_spec((n_experts, 1)), _const_spec((d, fs)), _const_spec((d, fs)),
                  _const_spec((fs, d))],
        out_specs=[row(d), row(d), pl.BlockSpec((1, n_experts, tm), lambda b, i: (b, 0, i))],
        out_shape=[jax.ShapeDtypeStruct((bsz, L, d), F32), jax.ShapeDtypeStruct((bsz, L, d), BF16),
                   jax.ShapeDtypeStruct((bsz, n_experts, L), F32)],
        compiler_params=_params("parallel", "parallel"),
        name="mid_block",
    )(x, ys, yf, gfox, wout, gmq, wmq, mkb, mvb, wmo, gffn, wrt, ebt, ws1, ws3, ws2)


def _moe_kernel(xn_ref, g_ref, xs_ref, w1_ref, w3_ref, w2_ref, gfin_ref, o_ref, *, n_experts):
    e = pl.program_id(1)

    @pl.when(e == 0)
    def _():
        o_ref[...] = xs_ref[...]

    xn = xn_ref[...]
    g = g_ref[...]
    lane = lax.broadcasted_iota(jnp.int32, g.shape, 1)
    col = jnp.sum(jnp.where(lane == e, g, 0.0), axis=1, keepdims=True)
    h = _silu(_dot(xn, w1_ref[0])) * _dot(xn, w3_ref[0])
    o_ref[...] += _dot((h * col).astype(BF16), w2_ref[0])

    @pl.when(e == n_experts - 1)
    def _():
        o_ref[...] = _rms(o_ref[...], gfin_ref[...])


def _moe(xn, gates, xs, w1, w3, w2, gfin, tm):
    m, d = xn.shape
    n_experts, _, f = w1.shape
    row = lambda n: pl.BlockSpec((tm, n), lambda i, e: (i, 0))
    return pl.pallas_call(
        functools.partial(_moe_kernel, n_experts=n_experts),
        grid=(m // tm, n_experts),
        in_specs=[row(d), row(n_experts), row(d),
                  pl.BlockSpec((1, d, f), lambda i, e: (e, 0, 0)), pl.BlockSpec((1, d, f), lambda i, e: (e, 0, 0)),
                  pl.BlockSpec((1, f, d), lambda i, e: (e, 0, 0)), _const_spec((1, d))],
        out_specs=row(d),
        out_shape=jax.ShapeDtypeStruct((m, d), F32),
        compiler_params=_params("parallel", "arbitrary"),
        name="moe_experts",
    )(xn, gates, xs, w1, w3, w2, gfin)


def _tiles(L):
    t_row = min(L, 512)
    t_key = 512
    t_scan = min(L, 64)
    return t_row, t_key, t_scan


def _group(x, h0_re, h0_im, past_k, past_v, past_lf, mkb, mvb, p, w):
    bsz, L, d = x.shape
    n_heads = p["b_f"].shape[0]
    n_fox = n_heads * FOX_HEAD_DIM
    n_ssm = p["d_skip"].shape[0]
    G, P = p["lam_re"].shape
    t_row, t_key, t_scan = _tiles(L)

    u, qb, k, v, kb, vb, lf = _proj_in(x, w["g_mix"], w["w_main"], w["w_f"], w["b_f"], t_row, n_ssm, n_fox, n_heads)

    u_tb = u.transpose(1, 0, 2).reshape(L * bsz, n_ssm)
    if h0_re is None:
        h0 = jnp.zeros((bsz, 2 * G * P), F32)
    else:
        h0 = _state_to_lanes(h0_re.astype(F32), h0_im.astype(F32))
    ys_tb, h_last = _s5(u_tb, h0, w["ar"], w["ai"], w["wb"], w["wc"], w["d_skip"], w["w_glu"], w["b_glu"],
                        w["g_ssm_out"], t_scan, bsz)
    ys = ys_tb.reshape(L, bsz, n_ssm).transpose(1, 0, 2)
    hl_re, hl_im = _lanes_to_state(h_last, G, P)

    if past_k is None:
        past = 0
        lf_all, k_all, v_all = lf, kb, vb
    else:
        past = past_k.shape[1]
        lf_all = jnp.concatenate([past_lf.astype(F32), lf], axis=1)
        k_all = jnp.concatenate([past_k.reshape(bsz, past, n_fox).astype(BF16), kb], axis=1)
        v_all = jnp.concatenate([past_v.reshape(bsz, past, n_fox).astype(BF16), vb], axis=1)
    lk = lf_all.shape[1]
    lk_pad = -(-lk // t_key) * t_key
    lf_t = jnp.pad(lf_all.transpose(0, 2, 1), ((0, 0), (0, 0), (0, lk_pad - lk)))
    d_pieces = _decay_cumsum(lf_t, jnp.zeros((bsz, n_heads, 1), F32), t_key)
    k_all = jnp.pad(k_all, ((0, 0), (0, lk_pad - lk), (0, 0)))
    v_all = jnp.pad(v_all, ((0, 0), (0, lk_pad - lk), (0, 0)))
    yf = _fox(*_fox_operands(qb, k_all, v_all, d_pieces, past, L), t_row, t_key, past)

    xs, xnb, gates_t = _mid(x, ys, yf, w["g_fox_out"], w["w_out"], w["g_mem_q"], w["w_mq"], mkb, mvb, w["w_mo"],
                            w["g_ffn"], w["w_router_t"], w["e_bias_t"], w["ws1"], w["ws3"], w["ws2"], t_row)
    m = bsz * L
    y = _moe(xnb.reshape(m, d), gates_t.transpose(0, 2, 1).reshape(m, -1), xs.reshape(m, d), w["w1"], w["w3"], w["w2"], w["g_final"], min(m, 1024))
    return (y.reshape(bsz, L, d), hl_re, hl_im, k.reshape(bsz, L, n_heads, FOX_HEAD_DIM),
            v.reshape(bsz, L, n_heads, FOX_HEAD_DIM), lf)


def kernel(x_prompt, x_sample, state_ssm_re, state_ssm_im, cache_fox_k, cache_fox_v, cache_fox_logf, cache_mem_k, cache_mem_v, mem_prompt, g_mix, w_in, b_f, lam_re, lam_im, log_dt, b_re, b_im, c_re, c_im, d_skip, w_glu, b_glu, g_ssm_out, g_fox_out, w_out, g_mem_q, g_mem_kv, w_mq, w_mk, w_mv, w_mo, g_ffn, w_router, e_bias, w1, w3, w2, ws1, ws3, ws2, g_final):
    depth = w_in.shape[0]
    bsz = x_prompt.shape[0]
    hp, hs = x_prompt, x_sample
    outs_p, outs_s = [], []
    for l in range(depth):
        p = dict(b_f=b_f[l], lam_re=lam_re[l], d_skip=d_skip[l])
        n_heads = b_f.shape[1]
        n_ssm = d_skip.shape[1]
        n_main = w_in.shape[2] - n_heads
        row = lambda a: a.reshape(1, -1).astype(F32)
        ar, ai, wb, wc = _s5_tables(lam_re[l], lam_im[l], log_dt[l], b_re[l], b_im[l], c_re[l], c_im[l], bsz)
        w = dict(
            g_mix=row(g_mix[l]), w_main=w_in[l][:, :n_main].astype(BF16),
            w_f=jnp.pad(w_in[l][:, n_main:], ((0, 0), (0, LANES - n_heads))).astype(BF16),
            b_f=jnp.pad(b_f[l].astype(F32), (0, LANES - n_heads)).reshape(1, LANES),
            ar=ar, ai=ai, wb=wb, wc=wc, d_skip=row(d_skip[l]), w_glu=w_glu[l].astype(BF16), b_glu=row(b_glu[l]),
            g_ssm_out=row(g_ssm_out[l]), g_fox_out=row(g_fox_out[l]), w_out=w_out[l].astype(BF16),
            g_mem_q=row(g_mem_q[l]), w_mq=w_mq[l].astype(BF16), w_mo=w_mo[l].astype(BF16), g_ffn=row(g_ffn[l]),
            w_router_t=w_router[l].astype(F32).T, e_bias_t=e_bias[l].astype(F32).reshape(-1, 1),
            ws1=ws1[l].astype(BF16), ws3=ws3[l].astype(BF16), ws2=ws2[l].astype(BF16),
            w1=w1[l].astype(BF16), w3=w3[l].astype(BF16), w2=w2[l].astype(BF16), g_final=row(g_final))
        assert depth == 1, "final norm fusion assumes a single layer"
        mk_p, mv_p, mkb_p, mvb_p = _mem_kv(mem_prompt, row(g_mem_kv[l]), w_mk[l].astype(BF16), w_mv[l].astype(BF16))
        nm, mh = mem_prompt.shape[1], MEM_HEADS
        hp, re_p, im_p, k_p, v_p, lf_p = _group(hp, None, None, None, None, None, mkb_p, mvb_p, p, w)
        cm_k = cache_mem_k[l].reshape(bsz, nm, -1).astype(BF16)
        cm_v = cache_mem_v[l].reshape(bsz, nm, -1).astype(BF16)
        hs, re_s, im_s, k_s, v_s, lf_s = _group(hs, state_ssm_re[l], state_ssm_im[l], cache_fox_k[l], cache_fox_v[l],
                                                cache_fox_logf[l], cm_k, cm_v, p, w)
        outs_p.append((re_p, im_p, k_p, v_p, lf_p, mk_p.reshape(bsz, nm, mh, -1), mv_p.reshape(bsz, nm, mh, -1)))
        outs_s.append((re_s, im_s, k_s, v_s, lf_s))
    stack = lambda outs, i: jnp.stack([o[i] for o in outs])
    return (hp, hs) + tuple(stack(outs_p, i) for i in range(7)) + tuple(stack(outs_s, i) for i in range(5))
```

```python
import functools
import math

import jax
import jax.numpy as jnp
from jax import lax
from jax.experimental import pallas as pl
from jax.experimental.pallas import tpu as pltpu

F32 = jnp.float32
BF16 = jnp.bfloat16

SSM_GROUP = 16
SSM_STATE = 64
FOX_HEAD_DIM = 64
MEM_HEADS = 4
TOP_K = 8
N_EXPERT_GROUPS = 8
TOPK_GROUPS = 4
ROUTED_SCALE = 2.5
RMS_EPS = 1e-6

LANES = 128
SUBLANES = 8
MXU_DIM = 256
VMEM_LIMIT_BYTES = 56 * 1024 * 1024

_HIGHEST = lax.Precision.HIGHEST
_NT = (((1,), (1,)), ((), ()))


def _params(*sem):
    return pltpu.CompilerParams(dimension_semantics=sem, vmem_limit_bytes=VMEM_LIMIT_BYTES)


def _rms(x, g):
    return x * lax.rsqrt(jnp.mean(x * x, axis=-1, keepdims=True) + RMS_EPS) * g


def _sigmoid(x):
    return 1.0 / (1.0 + jnp.exp(-x))


def _silu(x):
    return x * _sigmoid(x)


def _gelu_tanh(x):
    return x * (0.5 * (1.0 + jnp.tanh(math.sqrt(2.0 / math.pi) * (x + 0.044715 * (x * x * x)))))


def _log_sigmoid(x):
    return jnp.minimum(x, 0.0) - jnp.log1p(jnp.exp(-jnp.abs(x)))


def _dot(a, b):
    return jnp.dot(a, b, preferred_element_type=F32)


def _const_spec(shape):
    nd = len(shape)
    return pl.BlockSpec(shape, lambda *_: (0,) * nd)


def _memkv_kernel(m_ref, g_ref, wk_ref, wv_ref, k_ref, v_ref, kb_ref, vb_ref):
    mn = _rms(m_ref[0], g_ref[...]).astype(BF16)
    k = _dot(mn, wk_ref[...])
    v = _dot(mn, wv_ref[...])
    k_ref[0] = k
    v_ref[0] = v
    kb_ref[0] = k.astype(BF16)
    vb_ref[0] = v.astype(BF16)


def _mem_kv(mem, g, wk, wv):
    bsz, n, d = mem.shape
    blk = pl.BlockSpec((1, n, d), lambda b: (b, 0, 0))
    return pl.pallas_call(
        _memkv_kernel,
        grid=(bsz,),
        in_specs=[blk, _const_spec((1, d)), _const_spec((d, d)), _const_spec((d, d))],
        out_specs=[blk, blk, blk, blk],
        out_shape=[jax.ShapeDtypeStruct((bsz, n, d), F32)] * 2 + [jax.ShapeDtypeStruct((bsz, n, d), BF16)] * 2,
        compiler_params=_params("parallel"),
        name="mem_kv",
    )(mem, g, wk, wv)


def _proj_in_kernel(x_ref, g_ref, w_ref, wf_ref, bf_ref, u_ref, q_ref, k_ref, v_ref, kb_ref, vb_ref, lf_ref,
                    *, n_ssm, n_fox, n_heads):
    xb = _rms(x_ref[0], g_ref[...]).astype(BF16)
    z = _dot(xb, w_ref[...])
    u_ref[0] = z[:, :n_ssm]
    o = n_ssm
    q_ref[0] = (z[:, o:o + n_fox] * (FOX_HEAD_DIM ** -0.5 * math.log2(math.e))).astype(BF16)
    k = z[:, o + n_fox:o + 2 * n_fox]
    v = z[:, o + 2 * n_fox:o + 3 * n_fox]
    k_ref[0] = k
    v_ref[0] = v
    kb_ref[0] = k.astype(BF16)
    vb_ref[0] = v.astype(BF16)
    zf = _dot(xb, wf_ref[...])
    lf_ref[0] = _log_sigmoid(zf + bf_ref[...])[:, :n_heads]


def _proj_in(x, g, w_main, w_f, b_f, tm, n_ssm, n_fox, n_heads):
    bsz, L, d = x.shape
    nmain = w_main.shape[1]
    row = lambda n: pl.BlockSpec((1, tm, n), lambda b, i: (b, i, 0))
    outs = [jax.ShapeDtypeStruct((bsz, L, n_ssm), F32), jax.ShapeDtypeStruct((bsz, L, n_fox), BF16),
            jax.ShapeDtypeStruct((bsz, L, n_fox), F32), jax.ShapeDtypeStruct((bsz, L, n_fox), F32),
            jax.ShapeDtypeStruct((bsz, L, n_fox), BF16), jax.ShapeDtypeStruct((bsz, L, n_fox), BF16),
            jax.ShapeDtypeStruct((bsz, L, n_heads), F32)]
    return pl.pallas_call(
        functools.partial(_proj_in_kernel, n_ssm=n_ssm, n_fox=n_fox, n_heads=n_heads),
        grid=(bsz, L // tm),
        in_specs=[row(d), _const_spec((1, d)), _const_spec((d, nmain)), _const_spec((d, LANES)),
                  _const_spec((1, LANES))],
        out_specs=[row(n_ssm), row(n_fox), row(n_fox), row(n_fox), row(n_fox), row(n_fox), row(n_heads)],
        out_shape=outs,
        compiler_params=_params("parallel", "parallel"),
        name="proj_in",
    )(x, g, w_main, w_f, b_f)


def _decay_kernel(lft_ref, c0_ref, dt_ref, car_ref, *, tl):
    @pl.when(pl.program_id(1) == 0)
    def _():
        car_ref[...] = c0_ref[0]
    r = lax.broadcasted_iota(jnp.int32, (tl, tl), 0)
    c = lax.broadcasted_iota(jnp.int32, (tl, tl), 1)
    tri = (r <= c).astype(F32)
    d = jnp.dot(lft_ref[0], tri, preferred_element_type=F32, precision=_HIGHEST) + car_ref[...]
    car_ref[...] = d[:, tl - 1:tl]
    rest = d * math.log2(math.e)
    for piece in range(3):
        part = rest.astype(BF16).astype(F32)
        dt_ref[0, piece] = part
        rest = rest - part


def _decay_cumsum(lft, c0, tl):
    bsz, nh, L = lft.shape
    return pl.pallas_call(
        functools.partial(_decay_kernel, tl=tl),
        grid=(bsz, L // tl),
        in_specs=[pl.BlockSpec((1, nh, tl), lambda b, i: (b, 0, i)), pl.BlockSpec((1, nh, 1), lambda b, i: (b, 0, 0))],
        out_specs=pl.BlockSpec((1, 3, nh, tl), lambda b, i: (b, 0, 0, i)),
        out_shape=jax.ShapeDtypeStruct((bsz, 3, nh, L), F32),
        scratch_shapes=[pltpu.VMEM((nh, 1), F32)],
        compiler_params=_params("parallel", "arbitrary"),
        name="decay_cumsum",
    )(lft, c0)


def _s5_kernel(u_ref, h0_ref, ar_ref, ai_ref, wb_ref, wc_ref, dsk_ref, wglu_ref, bglu_ref, gout_ref,
               y_ref, hl_ref, hs_ref, hst_ref, *, t_chunk, bsz, n_slab):
    @pl.when(pl.program_id(0) == 0)
    def _():
        hst_ref[...] = h0_ref[...]

    u = u_ref[...]
    ub = u.astype(BF16)
    slab_per_k = MXU_DIM // (2 * SSM_GROUP)
    for j in range(n_slab):
        kt = j // slab_per_k
        hs_ref[:, MXU_DIM * j:MXU_DIM * (j + 1)] = _dot(ub[:, MXU_DIM * kt:MXU_DIM * (kt + 1)], wb_ref[j])

    def step(t, h):
        r0 = pl.multiple_of(t * bsz, bsz)
        bu = hs_ref[pl.ds(r0, bsz), :]
        parts = []
        for j in range(n_slab):
            lo, mid, hi = MXU_DIM * j, MXU_DIM * j + LANES, MXU_DIM * (j + 1)
            re, im = h[:, lo:mid], h[:, mid:hi]
            ar = ar_ref[:, LANES * j:LANES * (j + 1)]
            ai = ai_ref[:, LANES * j:LANES * (j + 1)]
            parts.append(ar * re - ai * im + bu[:, lo:mid])
            parts.append(ar * im + ai * re + bu[:, mid:hi])
        hn = jnp.concatenate(parts, axis=1)
        hs_ref[pl.ds(r0, bsz), :] = hn
        return hn

    h = lax.fori_loop(0, t_chunk, step, hst_ref[...])
    hst_ref[...] = h
    hl_ref[...] = h

    halves = []
    for hf in range(n_slab // slab_per_k):
        acc = None
        for jj in range(slab_per_k):
            j = hf * slab_per_k + jj
            d = _dot(hs_ref[:, MXU_DIM * j:MXU_DIM * (j + 1)].astype(BF16), wc_ref[j])
            acc = d if acc is None else acc + d
        halves.append(acc)
    y = jnp.concatenate(halves, axis=1) + dsk_ref[...] * u
    y = _gelu_tanh(y)
    y = y * _sigmoid(_dot(y.astype(BF16), wglu_ref[...]) + bglu_ref[...])
    y_ref[...] = _rms(y, gout_ref[...]).astype(BF16)


def _s5(u_tb, h0, ar, ai, wb, wc, dsk, wglu, bglu, gout, t_chunk, bsz):
    rows, n_ssm = u_tb.shape
    n_state = h0.shape[1]
    n_slab = n_state // MXU_DIM
    r = t_chunk * bsz
    return pl.pallas_call(
        functools.partial(_s5_kernel, t_chunk=t_chunk, bsz=bsz, n_slab=n_slab),
        grid=(rows // r,),
        in_specs=[pl.BlockSpec((r, n_ssm), lambda c: (c, 0)), _const_spec((bsz, n_state)),
                  _const_spec((bsz, n_state // 2)), _const_spec((bsz, n_state // 2)),
                  _const_spec((n_slab, MXU_DIM, MXU_DIM)), _const_spec((n_slab, MXU_DIM, MXU_DIM)),
                  _const_spec((1, n_ssm)), _const_spec((n_ssm, n_ssm)), _const_spec((1, n_ssm)),
                  _const_spec((1, n_ssm))],
        out_specs=[pl.BlockSpec((r, n_ssm), lambda c: (c, 0)), _const_spec((bsz, n_state))],
        out_shape=[jax.ShapeDtypeStruct((rows, n_ssm), BF16), jax.ShapeDtypeStruct((bsz, n_state), F32)],
        scratch_shapes=[pltpu.VMEM((r, n_state), F32), pltpu.VMEM((bsz, n_state), F32)],
        compiler_params=_params("arbitrary"),
        name="s5_mixer",
    )(u_tb, h0, ar, ai, wb, wc, dsk, wglu, bglu, gout)


def _s5_tables(lam_re, lam_im, log_dt, b_re, b_im, c_re, c_im, bsz):
    G, P = lam_re.shape
    H = b_re.shape[-1]
    lr, li = lam_re.astype(F32), lam_im.astype(F32)
    dt = jnp.exp(log_dt.astype(F32))[:, None]
    mag = jnp.exp(lr * dt)
    a_re, a_im = mag * jnp.cos(li * dt), mag * jnp.sin(li * dt)
    den = lr * lr + li * li
    c_r = ((a_re - 1.0) * lr + a_im * li) / den
    c_i = (a_im * lr - (a_re - 1.0) * li) / den
    br, bi = b_re.astype(F32), b_im.astype(F32)
    bbar_re = c_r[..., None] * br - c_i[..., None] * bi
    bbar_im = c_r[..., None] * bi + c_i[..., None] * br
    n_pair = G // 2
    per_k = MXU_DIM // (2 * H)
    eye2 = jnp.eye(2, dtype=F32)
    place = jax.nn.one_hot(jnp.arange(n_pair) % per_k, per_k, dtype=F32)

    ar = jnp.broadcast_to(a_re.reshape(1, -1), (bsz, G * P))
    ai = jnp.broadcast_to(a_im.reshape(1, -1), (bsz, G * P))

    bb = jnp.stack([bbar_re, bbar_im]).reshape(2, n_pair, 2, P, H)
    wpair = jnp.einsum("ajgph,gk->jghakp", bb, eye2).reshape(n_pair, 2 * H, MXU_DIM)
    wb = jnp.einsum("jrc,jk->jkrc", wpair, place).reshape(n_pair, MXU_DIM, MXU_DIM)

    cc = jnp.stack([c_re.astype(F32), -c_im.astype(F32)]).reshape(2, n_pair, 2, H, P)
    cpair = jnp.einsum("ajghp,gk->jagpkh", cc, eye2).reshape(n_pair, MXU_DIM, 2 * H)
    wc = jnp.einsum("jnc,jk->jnkc", cpair, place).reshape(n_pair, MXU_DIM, MXU_DIM)
    return ar, ai, wb.astype(BF16), wc.astype(BF16)


def _state_to_lanes(re, im):
    bsz, G, P = re.shape
    s = jnp.stack([re, im], axis=1).reshape(bsz, 2, G // 2, 2, P)
    return s.transpose(0, 2, 1, 3, 4).reshape(bsz, 2 * G * P)


def _lanes_to_state(h, G, P):
    bsz = h.shape[0]
    s = h.reshape(bsz, G // 2, 2, 2, P).transpose(0, 2, 1, 3, 4).reshape(bsz, 2, G, P)
    return s[:, 0], s[:, 1]


FOX_AUG_ROWS = 16


def _fox_kernel(qt_ref, qa_ref, k_ref, vt_ref, o_ref, m_ref, l_ref, acc_ref, *, tq, tk, past):
    qi = pl.program_id(2)
    qt = qt_ref[0]
    row = lax.broadcasted_iota(jnp.int32, (LANES, tq), 0)
    zero = jnp.zeros_like(qt)
    pad = jnp.zeros((LANES - FOX_AUG_ROWS, tq), BF16)
    qts = []
    for hh in range(2):
        own = (row < FOX_HEAD_DIM) if hh == 0 else (row >= FOX_HEAD_DIM)
        aug = qa_ref[0, 0, FOX_AUG_ROWS * hh:FOX_AUG_ROWS * (hh + 1), :]
        qts.append(jnp.concatenate([jnp.where(own, qt, zero), aug, pad], axis=0))
    m_ref[...] = jnp.full(m_ref.shape, -1e30, F32)
    l_ref[...] = jnp.zeros(l_ref.shape, F32)
    acc_ref[...] = jnp.zeros(acc_ref.shape, F32)
    q_start = past + qi * tq
    n_full = (q_start + 1) // tk
    n_all = (q_start + tq + tk - 1) // tk

    def block(j, masked):
        ks = pl.multiple_of(j * tk, tk)
        kb = k_ref[0, 0, pl.ds(ks, tk), :]
        if masked:
            kpos = ks + lax.broadcasted_iota(jnp.int32, (tk, tq), 0)
            qpos = q_start + lax.broadcasted_iota(jnp.int32, (tk, tq), 1)
            visible = kpos <= qpos
        for hh in range(2):
            st = _dot(kb, qts[hh])
            if masked:
                st = jnp.where(visible, st, -jnp.inf)
            m_old = m_ref[hh]
            m_new = jnp.maximum(m_old, jnp.max(st, axis=0, keepdims=True))
            p = jnp.exp2(st - m_new)
            alpha = jnp.exp2(m_old - m_new)
            l_ref[hh] = alpha * l_ref[hh] + jnp.sum(p, axis=0, keepdims=True)
            vt = vt_ref[0, FOX_HEAD_DIM * hh:FOX_HEAD_DIM * (hh + 1), pl.ds(ks, tk)]
            acc_ref[hh] = alpha * acc_ref[hh] + _dot(vt, p.astype(BF16))
            m_ref[hh] = m_new

    def full_body(j, c):
        block(j, False)
        return c

    def masked_body(j, c):
        block(j, True)
        return c

    lax.fori_loop(0, n_full, full_body, 0)
    lax.fori_loop(n_full, n_all, masked_body, 0)
    out_t = jnp.concatenate([acc_ref[0] / l_ref[0], acc_ref[1] / l_ref[1]], axis=0)
    o_ref[0] = out_t.T


def _fox(q_t, q_aug, k_aug, v_t, tq, tk, past):
    bsz, n_fox, L = q_t.shape
    n_pair = n_fox // LANES
    lk = k_aug.shape[2]
    return pl.pallas_call(
        functools.partial(_fox_kernel, tq=tq, tk=tk, past=past),
        grid=(bsz, n_pair, L // tq),
        in_specs=[pl.BlockSpec((1, LANES, tq), lambda b, h, i: (b, h, i)),
                  pl.BlockSpec((1, 1, 2 * FOX_AUG_ROWS, tq), lambda b, h, i: (b, h, 0, i)),
                  pl.BlockSpec((1, 1, lk, 2 * LANES), lambda b, h, i: (b, h, 0, 0)),
                  pl.BlockSpec((1, LANES, lk), lambda b, h, i: (b, h, 0))],
        out_specs=pl.BlockSpec((1, tq, LANES), lambda b, h, i: (b, i, h)),
        out_shape=jax.ShapeDtypeStruct((bsz, L, n_fox), F32),
        scratch_shapes=[pltpu.VMEM((2, 1, tq), F32), pltpu.VMEM((2, 1, tq), F32),
                        pltpu.VMEM((2, FOX_HEAD_DIM, tq), F32)],
        compiler_params=_params("parallel", "parallel", "arbitrary"),
        name="fox_attention",
    )(q_t, q_aug, k_aug, v_t)


def _fox_operands(qb, k_all, v_all, d_pieces, past, L):
    bsz, lk_pad, n_fox = k_all.shape
    n_heads = n_fox // FOX_HEAD_DIM
    n_pair = n_heads // 2
    pieces = d_pieces.astype(BF16)
    dk = (-pieces).reshape(bsz, 3, n_pair, 2, lk_pad)
    dk = dk.transpose(0, 2, 4, 3, 1).reshape(bsz, n_pair, lk_pad, 6)
    ones_k = jnp.ones((bsz, n_pair, lk_pad, 3), BF16)
    zeros_k = jnp.zeros((bsz, n_pair, lk_pad, LANES - 9), BF16)
    k_pair = k_all.reshape(bsz, lk_pad, n_pair, LANES).transpose(0, 2, 1, 3)
    k_aug = jnp.concatenate([k_pair, dk, ones_k, zeros_k], axis=-1)

    dq = pieces[:, :, :, past:past + L].transpose(0, 2, 1, 3)
    sel = jax.nn.one_hot(jnp.arange(n_heads) % 2, 2, dtype=BF16)
    ones_q = jnp.broadcast_to(jnp.repeat(sel, 3, axis=1)[None, :, :, None], (bsz, n_heads, 6, L))
    zeros_q = jnp.zeros((bsz, n_heads, FOX_AUG_ROWS - 9, L), BF16)
    q_aug = jnp.concatenate([ones_q, dq, zeros_q], axis=2).reshape(bsz, n_pair, 2 * FOX_AUG_ROWS, L)
    return qb.transpose(0, 2, 1), q_aug, k_aug, v_all.transpose(0, 2, 1)


def _route_gates(logits_t, ebias_t, n_experts):
    per_group = n_experts // N_EXPERT_GROUPS
    tokens = logits_t.shape[1]
    score = _sigmoid(logits_t)
    sel = score + ebias_t
    row = lax.broadcasted_iota(jnp.int32, (per_group, tokens), 0).astype(F32)
    neg = jnp.float32(-jnp.inf)

    def first_argmax(tile, best):
        return jnp.min(jnp.where(tile == best, row, float(per_group)), axis=0, keepdims=True)

    score_g, sel_g, gsc = [], [], []
    for g in range(N_EXPERT_GROUPS):
        sg = sel[per_group * g:per_group * (g + 1), :]
        score_g.append(score[per_group * g:per_group * (g + 1), :])
        sel_g.append(sg)
        m1 = jnp.max(sg, axis=0, keepdims=True)
        rest = jnp.where(row == first_argmax(sg, m1), neg, sg)
        gsc.append(m1 + jnp.max(rest, axis=0, keepdims=True))
    cur = []
    for g in range(N_EXPERT_GROUPS):
        ahead = jnp.zeros((1, tokens), F32)
        for o in range(N_EXPERT_GROUPS):
            if o == g:
                continue
            beats = (gsc[o] >= gsc[g]) if o < g else (gsc[o] > gsc[g])
            ahead = ahead + jnp.where(beats, 1.0, 0.0)
        cur.append(jnp.where(ahead < TOPK_GROUPS, sel_g[g], neg))
    chosen = [jnp.zeros((per_group, tokens), jnp.bool_) for _ in range(N_EXPERT_GROUPS)]
    for _ in range(TOP_K):
        best = cur[0]
        for g in range(1, N_EXPERT_GROUPS):
            best = jnp.maximum(best, cur[g])
        best = jnp.max(best, axis=0, keepdims=True)
        idx = None
        for g in range(N_EXPERT_GROUPS):
            cand = jnp.min(jnp.where(cur[g] == best, row + float(per_group * g), float(n_experts)), axis=0,
                           keepdims=True)
            idx = cand if idx is None else jnp.minimum(idx, cand)
        for g in range(N_EXPERT_GROUPS):
            hit = (row + float(per_group * g)) == idx
            chosen[g] = chosen[g] | hit
            cur[g] = jnp.where(hit, neg, cur[g])
    w = [jnp.where(chosen[g], score_g[g], 0.0) for g in range(N_EXPERT_GROUPS)]
    total = w[0]
    for g in range(1, N_EXPERT_GROUPS):
        total = total + w[g]
    total = jnp.sum(total, axis=0, keepdims=True)
    return [w[g] / total * ROUTED_SCALE for g in range(N_EXPERT_GROUPS)]


def _mid_kernel(x_ref, ys_ref, yf_ref, gfox_ref, wout_ref, gmq_ref, wmq_ref, mk_ref, mv_ref, wmo_ref, gffn_ref,
                wrt_ref, ebt_ref, ws1_ref, ws3_ref, ws2_ref, xs_ref, xn_ref, gt_ref, cnt_ref, *, n_experts):
    x = x_ref[0]
    yfn = _rms(yf_ref[0], gfox_ref[...]).astype(BF16)
    mix = jnp.concatenate([ys_ref[0], yfn], axis=1)
    x1 = x + _dot(mix, wout_ref[...])

    qm = _dot(_rms(x1, gmq_ref[...]).astype(BF16), wmq_ref[...])
    hd = qm.shape[1] // MEM_HEADS
    heads = []
    for h in range(MEM_HEADS):
        qh = (qm[:, hd * h:hd * (h + 1)] * (hd ** -0.5)).astype(BF16)
        s = lax.dot_general(qh, mk_ref[0, :, hd * h:hd * (h + 1)], _NT, preferred_element_type=F32)
        p = jnp.exp(s - jnp.max(s, axis=1, keepdims=True))
        o = _dot(p.astype(BF16), mv_ref[0, :, hd * h:hd * (h + 1)]) / jnp.sum(p, axis=1, keepdims=True)
        heads.append(o.astype(BF16))
    x2 = x1 + _dot(jnp.concatenate(heads, axis=1), wmo_ref[...])

    xn = _rms(x2, gffn_ref[...])
    xnb = xn.astype(BF16)
    hidden = _silu(_dot(xnb, ws1_ref[...])) * _dot(xnb, ws3_ref[...])
    xs_ref[0] = x2 + _dot(hidden.astype(BF16), ws2_ref[...])
    xn_ref[0] = xnb

    logits_t = lax.dot_general(wrt_ref[...], xn, _NT, preferred_element_type=F32, precision=_HIGHEST)
    gates = _route_gates(logits_t, ebt_ref[...], n_experts)
    per_group = n_experts // N_EXPERT_GROUPS
    for g in range(N_EXPERT_GROUPS):
        gt_ref[0, per_group * g:per_group * (g + 1), :] = gates[g]
        cnt_ref[0, 0, per_group * g:per_group * (g + 1), :] = jnp.sum(
            jnp.where(gates[g] != 0.0, 1.0, 0.0), axis=1, keepdims=True)


def _mid(x, ys, yf, gfox, wout, gmq, wmq, mkb, mvb, wmo, gffn, wrt, ebt, ws1, ws3, ws2, tm):
    bsz, L, d = x.shape
    n_ssm, n_fox = ys.shape[2], yf.shape[2]
    n_mem = mkb.shape[1]
    n_experts = wrt.shape[0]
    fs = ws1.shape[1]
    row = lambda n: pl.BlockSpec((1, tm, n), lambda b, i: (b, i, 0))
    memspec = pl.BlockSpec((1, n_mem, d), lambda b, i: (b, 0, 0))
    return pl.pallas_call(
        functools.partial(_mid_kernel, n_experts=n_experts),
        grid=(bsz, L // tm),
        in_specs=[row(d), row(n_ssm), row(n_fox), _const_spec((1, n_fox)), _const_spec((d, d)), _const_spec((1, d)),
                  _const_spec((d, d)), memspec, memspec, _const_spec((d, d)), _const_spec((1, d)),
                  _const_spec((n_experts, d)), _const_spec((n_experts, 1)), _const_spec((d, fs)), _const_spec((d, fs)),
                  _const_spec((fs, d))],
        out_specs=[row(d), row(d), pl.BlockSpec((1, n_experts, tm), lambda b, i: (b, 0, i)),
                   pl.BlockSpec((1, 1, n_experts, 1), lambda b, i: (b, i, 0, 0))],
        out_shape=[jax.ShapeDtypeStruct((bsz, L, d), F32), jax.ShapeDtypeStruct((bsz, L, d), BF16),
                   jax.ShapeDtypeStruct((bsz, n_experts, L), F32),
                   jax.ShapeDtypeStruct((bsz, L // tm, n_experts, 1), F32)],
        compiler_params=_params("parallel", "parallel"),
        name="mid_block",
    )(x, ys, yf, gfox, wout, gmq, wmq, mkb, mvb, wmo, gffn, wrt, ebt, ws1, ws3, ws2)


SEG_ROWS = 16
MOE_TOKENS = 512
MOE_TILE = 512
_TN = (((0,), (0,)), ((), ()))


def _sorted_rows(tb, n_experts):
    rows = TOP_K * tb + n_experts * (SEG_ROWS - 1)
    return -(-rows // MOE_TILE) * MOE_TILE


def _dispatch_plan(cnt, ns):
    nblk, n_experts = cnt.shape
    cp = (cnt + SEG_ROWS - 1) // SEG_ROWS * SEG_ROWS
    o_loc = jnp.cumsum(cp, axis=1) - cp
    used = jnp.sum(cp, axis=1)
    tot_e = jnp.sum(cp, axis=0)
    reg_e = (tot_e + MOE_TILE - 1) // MOE_TILE * MOE_TILE
    reg_end = jnp.cumsum(reg_e)
    base_e = reg_end - reg_e
    dst = base_e[None, :] + jnp.cumsum(cp, axis=0) - cp
    rows_max = nblk * (ns - ns % SEG_ROWS) + n_experts * MOE_TILE
    n_tiles_max = -(-rows_max // MOE_TILE)
    n_tiles = reg_end[-1] // MOE_TILE
    tile_idx = jnp.minimum(jnp.arange(n_tiles_max, dtype=jnp.int32), n_tiles - 1)
    tile_expert = jnp.minimum(jnp.searchsorted(reg_end // MOE_TILE, tile_idx, side="right"), n_experts - 1)
    i32 = lambda a: a.astype(jnp.int32).reshape(-1)
    return dict(o_loc=i32(o_loc), n_chunk=i32(cp // SEG_ROWS), dst=i32(dst), unused_chunks=i32((ns - used) // SEG_ROWS),
                used=i32(used), tail_start=i32(base_e + tot_e), tail_chunks=i32((reg_e - tot_e) // SEG_ROWS),
                tile_idx=i32(tile_idx), tile_expert=i32(tile_expert), n_tiles=i32(n_tiles),
                n_tiles_max=n_tiles_max)


def _slot_ranks(gates, tb):
    sel = gates != 0.0
    r = lax.broadcasted_iota(jnp.int32, (tb, tb), 0)
    c = lax.broadcasted_iota(jnp.int32, (tb, tb), 1)
    earlier = jnp.where(r < c, 1.0, 0.0).astype(BF16)
    rank = _dot(jnp.where(sel, 1.0, 0.0).astype(BF16), earlier)
    return jnp.where(sel, rank, -1.0)


def _for_each_chunk(b, n_experts, o_loc_ref, n_chunk_ref, fn):
    def per_expert(e, carry):
        off = o_loc_ref[b * n_experts + e]

        def per_chunk(i, c2):
            fn(e, i, pl.multiple_of(off + i * SEG_ROWS, SEG_ROWS))
            return c2

        lax.fori_loop(0, n_chunk_ref[b * n_experts + e], per_chunk, 0)
        return carry

    lax.fori_loop(0, n_experts, per_expert, 0)


def _dispatch_kernel(o_loc_ref, n_chunk_ref, dst_ref, unused_ref, used_ref, tail_start_ref, tail_chunks_ref,
                     xn_ref, gt_ref, xg_ref, g_ref, xsb_ref, rm_ref, z_ref, sem, *, tb, ns, n_experts, nblk):
    b = pl.program_id(0)
    rm_ref[...] = _slot_ranks(gt_ref[...], tb)
    rows = lax.broadcasted_iota(jnp.int32, (SEG_ROWS, tb), 0).astype(F32)

    def build(e, i, row):
        hit = rows == (rm_ref[pl.ds(e, 1), :] - (i * SEG_ROWS).astype(F32))
        g_ref[pl.ds(row, SEG_ROWS), :] = jnp.where(hit, 1.0, 0.0).astype(BF16)

    _for_each_chunk(b, n_experts, o_loc_ref, n_chunk_ref, build)

    def clear(i, c):
        g_ref[pl.ds(pl.multiple_of(used_ref[b] + i * SEG_ROWS, SEG_ROWS), SEG_ROWS), :] = jnp.zeros((SEG_ROWS, tb), BF16)
        return c

    lax.fori_loop(0, unused_ref[b], clear, 0)

    xn = xn_ref[...]
    for t in range(ns // MOE_TILE):
        sl = slice(MOE_TILE * t, MOE_TILE * (t + 1))
        xsb_ref[sl, :] = _dot(g_ref[sl, :], xn).astype(BF16)

    def seg_copy(e, i, row):
        dst_row = pl.multiple_of(dst_ref[b * n_experts + e] + i * SEG_ROWS, SEG_ROWS)
        return pltpu.make_async_copy(xsb_ref.at[pl.ds(row, SEG_ROWS)], xg_ref.at[pl.ds(dst_row, SEG_ROWS)], sem)

    _for_each_chunk(b, n_experts, o_loc_ref, n_chunk_ref, lambda e, i, row: seg_copy(e, i, row).start())
    _for_each_chunk(b, n_experts, o_loc_ref, n_chunk_ref, lambda e, i, row: seg_copy(e, i, row).wait())

    @pl.when(b == nblk - 1)
    def _():
        z_ref[...] = jnp.zeros(z_ref.shape, BF16)

        def tail_copy(e, i):
            row = pl.multiple_of(tail_start_ref[e] + i * SEG_ROWS, SEG_ROWS)
            return pltpu.make_async_copy(z_ref, xg_ref.at[pl.ds(row, SEG_ROWS)], sem)

        def each_tail(action):
            def per_expert(e, carry):
                def per_chunk(i, c2):
                    action(tail_copy(e, i))
                    return c2
                lax.fori_loop(0, tail_chunks_ref[e], per_chunk, 0)
                return carry
            lax.fori_loop(0, n_experts, per_expert, 0)

        each_tail(lambda cp: cp.start())
        each_tail(lambda cp: cp.wait())


def _expert_kernel(tile_idx_ref, tile_expert_ref, n_tiles_ref, x_ref, w1_ref, w3_ref, w2_ref, o_ref):
    @pl.when(pl.program_id(0) < n_tiles_ref[0])
    def _():
        x = x_ref[...]
        h = _silu(_dot(x, w1_ref[0])) * _dot(x, w3_ref[0])
        o_ref[...] = _dot(h.astype(BF16), w2_ref[0]).astype(BF16)


def _combine_kernel(o_loc_ref, n_chunk_ref, dst_ref, unused_ref, used_ref,
                    og_ref, gt_ref, xs_ref, gfin_ref, y_ref, gw_ref, ob_ref, rm_ref, sem, *, tb, ns, n_experts):
    b = pl.program_id(0)

    def seg_copy(e, i, row):
        src_row = pl.multiple_of(dst_ref[b * n_experts + e] + i * SEG_ROWS, SEG_ROWS)
        return pltpu.make_async_copy(og_ref.at[pl.ds(src_row, SEG_ROWS)], ob_ref.at[pl.ds(row, SEG_ROWS)], sem)

    _for_each_chunk(b, n_experts, o_loc_ref, n_chunk_ref, lambda e, i, row: seg_copy(e, i, row).start())

    rm_ref[...] = _slot_ranks(gt_ref[...], tb)
    rows = lax.broadcasted_iota(jnp.int32, (SEG_ROWS, tb), 0).astype(F32)
    gw_ref[...] = jnp.zeros(gw_ref.shape, BF16)

    def build(e, i, row):
        hit = rows == (rm_ref[pl.ds(e, 1), :] - (i * SEG_ROWS).astype(F32))
        gw_ref[pl.ds(row, SEG_ROWS), :] = jnp.where(hit, gt_ref[pl.ds(e, 1), :], 0.0).astype(BF16)

    _for_each_chunk(b, n_experts, o_loc_ref, n_chunk_ref, build)
    _for_each_chunk(b, n_experts, o_loc_ref, n_chunk_ref, lambda e, i, row: seg_copy(e, i, row).wait())

    def clear(i, c):
        ob_ref[pl.ds(pl.multiple_of(used_ref[b] + i * SEG_ROWS, SEG_ROWS), SEG_ROWS), :] = jnp.zeros(
            (SEG_ROWS, ob_ref.shape[1]), BF16)
        return c

    lax.fori_loop(0, unused_ref[b], clear, 0)
    y = lax.dot_general(gw_ref[...], ob_ref[...], _TN, preferred_element_type=F32) + xs_ref[...]
    y_ref[...] = _rms(y, gfin_ref[...])


def _moe(xn, gates_t, cnt, xs, w1, w3, w2, gfin):
    m, d = xn.shape
    n_experts, _, f = w1.shape
    tb = min(m, MOE_TOKENS)
    nblk = m // tb
    ns = _sorted_rows(tb, n_experts)
    plan = _dispatch_plan(cnt, ns)
    n_tiles_max = plan["n_tiles_max"]
    rows_max = n_tiles_max * MOE_TILE
    seg_tables = (plan["o_loc"], plan["n_chunk"], plan["dst"], plan["unused_chunks"], plan["used"])

    xg = pl.pallas_call(
        functools.partial(_dispatch_kernel, tb=tb, ns=ns, n_experts=n_experts, nblk=nblk),
        grid_spec=pltpu.PrefetchScalarGridSpec(
            num_scalar_prefetch=7, grid=(nblk,),
            in_specs=[pl.BlockSpec((tb, d), lambda b, *_: (b, 0)), pl.BlockSpec((n_experts, tb), lambda b, *_: (0, b))],
            out_specs=pl.BlockSpec(memory_space=pl.ANY),
            scratch_shapes=[pltpu.VMEM((ns, tb), BF16), pltpu.VMEM((ns, d), BF16), pltpu.VMEM((n_experts, tb), F32),
                            pltpu.VMEM((SEG_ROWS, d), BF16), pltpu.SemaphoreType.DMA(())]),
        out_shape=jax.ShapeDtypeStruct((rows_max, d), BF16),
        compiler_params=_params("arbitrary"),
        name="moe_dispatch",
    )(*seg_tables, plan["tail_start"], plan["tail_chunks"], xn, gates_t)

    tile = lambda i, idx, ex, n: (idx[i], 0)
    og = pl.pallas_call(
        _expert_kernel,
        grid_spec=pltpu.PrefetchScalarGridSpec(
            num_scalar_prefetch=3, grid=(n_tiles_max,),
            in_specs=[pl.BlockSpec((MOE_TILE, d), tile),
                      pl.BlockSpec((1, d, f), lambda i, idx, ex, n: (ex[i], 0, 0)),
                      pl.BlockSpec((1, d, f), lambda i, idx, ex, n: (ex[i], 0, 0)),
                      pl.BlockSpec((1, f, d), lambda i, idx, ex, n: (ex[i], 0, 0))],
            out_specs=pl.BlockSpec((MOE_TILE, d), tile)),
        out_shape=jax.ShapeDtypeStruct((rows_max, d), BF16),
        compiler_params=_params("arbitrary"),
        name="moe_experts",
    )(plan["tile_idx"], plan["tile_expert"], plan["n_tiles"], xg, w1, w3, w2)

    return pl.pallas_call(
        functools.partial(_combine_kernel, tb=tb, ns=ns, n_experts=n_experts),
        grid_spec=pltpu.PrefetchScalarGridSpec(
            num_scalar_prefetch=5, grid=(nblk,),
            in_specs=[pl.BlockSpec(memory_space=pl.ANY), pl.BlockSpec((n_experts, tb), lambda b, *_: (0, b)),
                      pl.BlockSpec((tb, d), lambda b, *_: (b, 0)), pl.BlockSpec((1, d), lambda b, *_: (0, 0))],
            out_specs=pl.BlockSpec((tb, d), lambda b, *_: (b, 0)),
            scratch_shapes=[pltpu.VMEM((ns, tb), BF16), pltpu.VMEM((ns, d), BF16), pltpu.VMEM((n_experts, tb), F32),
                            pltpu.SemaphoreType.DMA(())]),
        out_shape=jax.ShapeDtypeStruct((m, d), F32),
        compiler_params=_params("arbitrary"),
        name="moe_combine",
    )(*seg_tables, og, gates_t, xs, gfin)


def _tiles(L):
    t_row = min(L, 512)
    t_key = 512
    t_scan = min(L, 64)
    return t_row, t_key, t_scan


def _group(x, h0_re, h0_im, past_k, past_v, past_lf, mkb, mvb, p, w):
    bsz, L, d = x.shape
    n_heads = p["b_f"].shape[0]
    n_fox = n_heads * FOX_HEAD_DIM
    n_ssm = p["d_skip"].shape[0]
    G, P = p["lam_re"].shape
    t_row, t_key, t_scan = _tiles(L)

    u, qb, k, v, kb, vb, lf = _proj_in(x, w["g_mix"], w["w_main"], w["w_f"], w["b_f"], t_row, n_ssm, n_fox, n_heads)

    u_tb = u.transpose(1, 0, 2).reshape(L * bsz, n_ssm)
    if h0_re is None:
        h0 = jnp.zeros((bsz, 2 * G * P), F32)
    else:
        h0 = _state_to_lanes(h0_re.astype(F32), h0_im.astype(F32))
    ys_tb, h_last = _s5(u_tb, h0, w["ar"], w["ai"], w["wb"], w["wc"], w["d_skip"], w["w_glu"], w["b_glu"],
                        w["g_ssm_out"], t_scan, bsz)
    ys = ys_tb.reshape(L, bsz, n_ssm).transpose(1, 0, 2)
    hl_re, hl_im = _lanes_to_state(h_last, G, P)

    if past_k is None:
        past = 0
        lf_all, k_all, v_all = lf, kb, vb
    else:
        past = past_k.shape[1]
        lf_all = jnp.concatenate([past_lf.astype(F32), lf], axis=1)
        k_all = jnp.concatenate([past_k.reshape(bsz, past, n_fox).astype(BF16), kb], axis=1)
        v_all = jnp.concatenate([past_v.reshape(bsz, past, n_fox).astype(BF16), vb], axis=1)
    lk = lf_all.shape[1]
    lk_pad = -(-lk // t_key) * t_key
    lf_t = jnp.pad(lf_all.transpose(0, 2, 1), ((0, 0), (0, 0), (0, lk_pad - lk)))
    d_pieces = _decay_cumsum(lf_t, jnp.zeros((bsz, n_heads, 1), F32), t_key)
    k_all = jnp.pad(k_all, ((0, 0), (0, lk_pad - lk), (0, 0)))
    v_all = jnp.pad(v_all, ((0, 0), (0, lk_pad - lk), (0, 0)))
    yf = _fox(*_fox_operands(qb, k_all, v_all, d_pieces, past, L), t_row, t_key, past)

    xs, xnb, gates_t, cnt = _mid(x, ys, yf, w["g_fox_out"], w["w_out"], w["g_mem_q"], w["w_mq"], mkb, mvb, w["w_mo"],
                                 w["g_ffn"], w["w_router_t"], w["e_bias_t"], w["ws1"], w["ws3"], w["ws2"], t_row)
    m = bsz * L
    n_experts = gates_t.shape[1]
    tb = min(m, MOE_TOKENS)
    cnt = cnt.reshape(m // tb, tb // t_row, n_experts).sum(axis=1).astype(jnp.int32)
    y = _moe(xnb.reshape(m, d), gates_t.transpose(1, 0, 2).reshape(n_experts, m), cnt, xs.reshape(m, d),
             w["w1"], w["w3"], w["w2"], w["g_final"])
    return (y.reshape(bsz, L, d), hl_re, hl_im, k.reshape(bsz, L, n_heads, FOX_HEAD_DIM),
            v.reshape(bsz, L, n_heads, FOX_HEAD_DIM), lf)


def kernel(x_prompt, x_sample, state_ssm_re, state_ssm_im, cache_fox_k, cache_fox_v, cache_fox_logf, cache_mem_k, cache_mem_v, mem_prompt, g_mix, w_in, b_f, lam_re, lam_im, log_dt, b_re, b_im, c_re, c_im, d_skip, w_glu, b_glu, g_ssm_out, g_fox_out, w_out, g_mem_q, g_mem_kv, w_mq, w_mk, w_mv, w_mo, g_ffn, w_router, e_bias, w1, w3, w2, ws1, ws3, ws2, g_final):
    depth = w_in.shape[0]
    bsz = x_prompt.shape[0]
    hp, hs = x_prompt, x_sample
    outs_p, outs_s = [], []
    for l in range(depth):
        p = dict(b_f=b_f[l], lam_re=lam_re[l], d_skip=d_skip[l])
        n_heads = b_f.shape[1]
        n_ssm = d_skip.shape[1]
        n_main = w_in.shape[2] - n_heads
        row = lambda a: a.reshape(1, -1).astype(F32)
        ar, ai, wb, wc = _s5_tables(lam_re[l], lam_im[l], log_dt[l], b_re[l], b_im[l], c_re[l], c_im[l], bsz)
        w = dict(
            g_mix=row(g_mix[l]), w_main=w_in[l][:, :n_main].astype(BF16),
            w_f=jnp.pad(w_in[l][:, n_main:], ((0, 0), (0, LANES - n_heads))).astype(BF16),
            b_f=jnp.pad(b_f[l].astype(F32), (0, LANES - n_heads)).reshape(1, LANES),
            ar=ar, ai=ai, wb=wb, wc=wc, d_skip=row(d_skip[l]), w_glu=w_glu[l].astype(BF16), b_glu=row(b_glu[l]),
            g_ssm_out=row(g_ssm_out[l]), g_fox_out=row(g_fox_out[l]), w_out=w_out[l].astype(BF16),
            g_mem_q=row(g_mem_q[l]), w_mq=w_mq[l].astype(BF16), w_mo=w_mo[l].astype(BF16), g_ffn=row(g_ffn[l]),
            w_router_t=w_router[l].astype(F32).T, e_bias_t=e_bias[l].astype(F32).reshape(-1, 1),
            ws1=ws1[l].astype(BF16), ws3=ws3[l].astype(BF16), ws2=ws2[l].astype(BF16),
            w1=w1[l].astype(BF16), w3=w3[l].astype(BF16), w2=w2[l].astype(BF16), g_final=row(g_final))
        assert depth == 1, "final norm fusion assumes a single layer"
        mk_p, mv_p, mkb_p, mvb_p = _mem_kv(mem_prompt, row(g_mem_kv[l]), w_mk[l].astype(BF16), w_mv[l].astype(BF16))
        nm, mh = mem_prompt.shape[1], MEM_HEADS
        hp, re_p, im_p, k_p, v_p, lf_p = _group(hp, None, None, None, None, None, mkb_p, mvb_p, p, w)
        cm_k = cache_mem_k[l].reshape(bsz, nm, -1).astype(BF16)
        cm_v = cache_mem_v[l].reshape(bsz, nm, -1).astype(BF16)
        hs, re_s, im_s, k_s, v_s, lf_s = _group(hs, state_ssm_re[l], state_ssm_im[l], cache_fox_k[l], cache_fox_v[l],
                                                cache_fox_logf[l], cm_k, cm_v, p, w)
        outs_p.append((re_p, im_p, k_p, v_p, lf_p, mk_p.reshape(bsz, nm, mh, -1), mv_p.reshape(bsz, nm, mh, -1)))
        outs_s.append((re_s, im_s, k_s, v_s, lf_s))
    stack = lambda outs, i: jnp.stack([o[i] for o in outs])
    return (hp, hs) + tuple(stack(outs_p, i) for i in range(7)) + tuple(stack(outs_s, i) for i in range(5))
```

```python
import functools
import math

import jax
import jax.numpy as jnp
from jax import lax
from jax.experimental import pallas as pl
from jax.experimental.pallas import tpu as pltpu

F32 = jnp.float32
BF16 = jnp.bfloat16

SSM_GROUP = 16
SSM_STATE = 64
FOX_HEAD_DIM = 64
MEM_HEADS = 4
TOP_K = 8
N_EXPERT_GROUPS = 8
TOPK_GROUPS = 4
ROUTED_SCALE = 2.5
RMS_EPS = 1e-6

LANES = 128
SUBLANES = 8
MXU_DIM = 256
VMEM_LIMIT_BYTES = 56 * 1024 * 1024

_HIGHEST = lax.Precision.HIGHEST
_NT = (((1,), (1,)), ((), ()))


def _params(*sem):
    return pltpu.CompilerParams(dimension_semantics=sem, vmem_limit_bytes=VMEM_LIMIT_BYTES)


def _rms(x, g):
    return x * lax.rsqrt(jnp.mean(x * x, axis=-1, keepdims=True) + RMS_EPS) * g


def _sigmoid(x):
    return 1.0 / (1.0 + jnp.exp(-x))


def _silu(x):
    return x * _sigmoid(x)


def _gelu_tanh(x):
    return x * (0.5 * (1.0 + jnp.tanh(math.sqrt(2.0 / math.pi) * (x + 0.044715 * (x * x * x)))))


def _log_sigmoid(x):
    return jnp.minimum(x, 0.0) - jnp.log1p(jnp.exp(-jnp.abs(x)))


def _dot(a, b):
    return jnp.dot(a, b, preferred_element_type=F32)


def _const_spec(shape):
    nd = len(shape)
    return pl.BlockSpec(shape, lambda *_: (0,) * nd)


def _memkv_kernel(m_ref, g_ref, wk_ref, wv_ref, k_ref, v_ref, kb_ref, vb_ref):
    mn = _rms(m_ref[0], g_ref[...]).astype(BF16)
    k = _dot(mn, wk_ref[...])
    v = _dot(mn, wv_ref[...])
    k_ref[0] = k
    v_ref[0] = v
    kb_ref[0] = k.astype(BF16)
    vb_ref[0] = v.astype(BF16)


def _mem_kv(mem, g, wk, wv):
    bsz, n, d = mem.shape
    blk = pl.BlockSpec((1, n, d), lambda b: (b, 0, 0))
    return pl.pallas_call(
        _memkv_kernel,
        grid=(bsz,),
        in_specs=[blk, _const_spec((1, d)), _const_spec((d, d)), _const_spec((d, d))],
        out_specs=[blk, blk, blk, blk],
        out_shape=[jax.ShapeDtypeStruct((bsz, n, d), F32)] * 2 + [jax.ShapeDtypeStruct((bsz, n, d), BF16)] * 2,
        compiler_params=_params("parallel"),
        name="mem_kv",
    )(mem, g, wk, wv)


def _proj_in_kernel(x_ref, g_ref, w_ref, wf_ref, bf_ref, u_ref, q_ref, k_ref, v_ref, kb_ref, vb_ref, lf_ref,
                    *, n_ssm, n_fox, n_heads):
    xb = _rms(x_ref[0], g_ref[...]).astype(BF16)
    z = _dot(xb, w_ref[...])
    u_ref[0] = z[:, :n_ssm]
    o = n_ssm
    q_ref[0] = (z[:, o:o + n_fox] * (FOX_HEAD_DIM ** -0.5 * math.log2(math.e))).astype(BF16)
    k = z[:, o + n_fox:o + 2 * n_fox]
    v = z[:, o + 2 * n_fox:o + 3 * n_fox]
    k_ref[0] = k
    v_ref[0] = v
    kb_ref[0] = k.astype(BF16)
    vb_ref[0] = v.astype(BF16)
    zf = _dot(xb, wf_ref[...])
    lf_ref[0] = _log_sigmoid(zf + bf_ref[...])[:, :n_heads]


def _proj_in(x, g, w_main, w_f, b_f, tm, n_ssm, n_fox, n_heads):
    bsz, L, d = x.shape
    nmain = w_main.shape[1]
    row = lambda n: pl.BlockSpec((1, tm, n), lambda b, i: (b, i, 0))
    outs = [jax.ShapeDtypeStruct((bsz, L, n_ssm), F32), jax.ShapeDtypeStruct((bsz, L, n_fox), BF16),
            jax.ShapeDtypeStruct((bsz, L, n_fox), F32), jax.ShapeDtypeStruct((bsz, L, n_fox), F32),
            jax.ShapeDtypeStruct((bsz, L, n_fox), BF16), jax.ShapeDtypeStruct((bsz, L, n_fox), BF16),
            jax.ShapeDtypeStruct((bsz, L, n_heads), F32)]
    return pl.pallas_call(
        functools.partial(_proj_in_kernel, n_ssm=n_ssm, n_fox=n_fox, n_heads=n_heads),
        grid=(bsz, L // tm),
        in_specs=[row(d), _const_spec((1, d)), _const_spec((d, nmain)), _const_spec((d, LANES)),
                  _const_spec((1, LANES))],
        out_specs=[row(n_ssm), row(n_fox), row(n_fox), row(n_fox), row(n_fox), row(n_fox), row(n_heads)],
        out_shape=outs,
        compiler_params=_params("parallel", "parallel"),
        name="proj_in",
    )(x, g, w_main, w_f, b_f)


def _decay_kernel(lft_ref, c0_ref, dt_ref, car_ref, *, tl):
    @pl.when(pl.program_id(1) == 0)
    def _():
        car_ref[...] = c0_ref[0]
    r = lax.broadcasted_iota(jnp.int32, (tl, tl), 0)
    c = lax.broadcasted_iota(jnp.int32, (tl, tl), 1)
    tri = (r <= c).astype(F32)
    d = jnp.dot(lft_ref[0], tri, preferred_element_type=F32, precision=_HIGHEST) + car_ref[...]
    car_ref[...] = d[:, tl - 1:tl]
    rest = d * math.log2(math.e)
    for piece in range(3):
        part = rest.astype(BF16).astype(F32)
        dt_ref[0, piece] = part
        rest = rest - part


def _decay_cumsum(lft, c0, tl):
    bsz, nh, L = lft.shape
    return pl.pallas_call(
        functools.partial(_decay_kernel, tl=tl),
        grid=(bsz, L // tl),
        in_specs=[pl.BlockSpec((1, nh, tl), lambda b, i: (b, 0, i)), pl.BlockSpec((1, nh, 1), lambda b, i: (b, 0, 0))],
        out_specs=pl.BlockSpec((1, 3, nh, tl), lambda b, i: (b, 0, 0, i)),
        out_shape=jax.ShapeDtypeStruct((bsz, 3, nh, L), F32),
        scratch_shapes=[pltpu.VMEM((nh, 1), F32)],
        compiler_params=_params("parallel", "arbitrary"),
        name="decay_cumsum",
    )(lft, c0)


def _s5_kernel(u_ref, h0_ref, ar_ref, ai_ref, wb_ref, wc_ref, dsk_ref, wglu_ref, bglu_ref, gout_ref,
               y_ref, hl_ref, hs_ref, hst_ref, *, t_chunk, bsz, n_slab):
    @pl.when(pl.program_id(0) == 0)
    def _():
        hst_ref[...] = h0_ref[...]

    u = u_ref[...]
    ub = u.astype(BF16)
    slab_per_k = MXU_DIM // (2 * SSM_GROUP)
    for j in range(n_slab):
        kt = j // slab_per_k
        hs_ref[:, MXU_DIM * j:MXU_DIM * (j + 1)] = _dot(ub[:, MXU_DIM * kt:MXU_DIM * (kt + 1)], wb_ref[j])

    def step(t, h):
        r0 = pl.multiple_of(t * bsz, bsz)
        bu = hs_ref[pl.ds(r0, bsz), :]
        parts = []
        for j in range(n_slab):
            lo, mid, hi = MXU_DIM * j, MXU_DIM * j + LANES, MXU_DIM * (j + 1)
            re, im = h[:, lo:mid], h[:, mid:hi]
            ar = ar_ref[:, LANES * j:LANES * (j + 1)]
            ai = ai_ref[:, LANES * j:LANES * (j + 1)]
            parts.append(ar * re - ai * im + bu[:, lo:mid])
            parts.append(ar * im + ai * re + bu[:, mid:hi])
        hn = jnp.concatenate(parts, axis=1)
        hs_ref[pl.ds(r0, bsz), :] = hn
        return hn

    h = lax.fori_loop(0, t_chunk, step, hst_ref[...])
    hst_ref[...] = h
    hl_ref[...] = h

    halves = []
    for hf in range(n_slab // slab_per_k):
        acc = None
        for jj in range(slab_per_k):
            j = hf * slab_per_k + jj
            d = _dot(hs_ref[:, MXU_DIM * j:MXU_DIM * (j + 1)].astype(BF16), wc_ref[j])
            acc = d if acc is None else acc + d
        halves.append(acc)
    y = jnp.concatenate(halves, axis=1) + dsk_ref[...] * u
    y = _gelu_tanh(y)
    y = y * _sigmoid(_dot(y.astype(BF16), wglu_ref[...]) + bglu_ref[...])
    y_ref[...] = _rms(y, gout_ref[...]).astype(BF16)


def _s5(u_tb, h0, ar, ai, wb, wc, dsk, wglu, bglu, gout, t_chunk, bsz):
    rows, n_ssm = u_tb.shape
    n_state = h0.shape[1]
    n_slab = n_state // MXU_DIM
    r = t_chunk * bsz
    return pl.pallas_call(
        functools.partial(_s5_kernel, t_chunk=t_chunk, bsz=bsz, n_slab=n_slab),
        grid=(rows // r,),
        in_specs=[pl.BlockSpec((r, n_ssm), lambda c: (c, 0)), _const_spec((bsz, n_state)),
                  _const_spec((bsz, n_state // 2)), _const_spec((bsz, n_state // 2)),
                  _const_spec((n_slab, MXU_DIM, MXU_DIM)), _const_spec((n_slab, MXU_DIM, MXU_DIM)),
                  _const_spec((1, n_ssm)), _const_spec((n_ssm, n_ssm)), _const_spec((1, n_ssm)),
                  _const_spec((1, n_ssm))],
        out_specs=[pl.BlockSpec((r, n_ssm), lambda c: (c, 0)), _const_spec((bsz, n_state))],
        out_shape=[jax.ShapeDtypeStruct((rows, n_ssm), BF16), jax.ShapeDtypeStruct((bsz, n_state), F32)],
        scratch_shapes=[pltpu.VMEM((r, n_state), F32), pltpu.VMEM((bsz, n_state), F32)],
        compiler_params=_params("arbitrary"),
        name="s5_mixer",
    )(u_tb, h0, ar, ai, wb, wc, dsk, wglu, bglu, gout)


def _s5_tables(lam_re, lam_im, log_dt, b_re, b_im, c_re, c_im, bsz):
    G, P = lam_re.shape
    H = b_re.shape[-1]
    lr, li = lam_re.astype(F32), lam_im.astype(F32)
    dt = jnp.exp(log_dt.astype(F32))[:, None]
    mag = jnp.exp(lr * dt)
    a_re, a_im = mag * jnp.cos(li * dt), mag * jnp.sin(li * dt)
    den = lr * lr + li * li
    c_r = ((a_re - 1.0) * lr + a_im * li) / den
    c_i = (a_im * lr - (a_re - 1.0) * li) / den
    br, bi = b_re.astype(F32), b_im.astype(F32)
    bbar_re = c_r[..., None] * br - c_i[..., None] * bi
    bbar_im = c_r[..., None] * bi + c_i[..., None] * br
    n_pair = G // 2
    per_k = MXU_DIM // (2 * H)
    eye2 = jnp.eye(2, dtype=F32)
    place = jax.nn.one_hot(jnp.arange(n_pair) % per_k, per_k, dtype=F32)

    ar = jnp.broadcast_to(a_re.reshape(1, -1), (bsz, G * P))
    ai = jnp.broadcast_to(a_im.reshape(1, -1), (bsz, G * P))

    bb = jnp.stack([bbar_re, bbar_im]).reshape(2, n_pair, 2, P, H)
    wpair = jnp.einsum("ajgph,gk->jghakp", bb, eye2).reshape(n_pair, 2 * H, MXU_DIM)
    wb = jnp.einsum("jrc,jk->jkrc", wpair, place).reshape(n_pair, MXU_DIM, MXU_DIM)

    cc = jnp.stack([c_re.astype(F32), -c_im.astype(F32)]).reshape(2, n_pair, 2, H, P)
    cpair = jnp.einsum("ajghp,gk->jagpkh", cc, eye2).reshape(n_pair, MXU_DIM, 2 * H)
    wc = jnp.einsum("jnc,jk->jnkc", cpair, place).reshape(n_pair, MXU_DIM, MXU_DIM)
    return ar, ai, wb.astype(BF16), wc.astype(BF16)


def _state_to_lanes(re, im):
    bsz, G, P = re.shape
    s = jnp.stack([re, im], axis=1).reshape(bsz, 2, G // 2, 2, P)
    return s.transpose(0, 2, 1, 3, 4).reshape(bsz, 2 * G * P)


def _lanes_to_state(h, G, P):
    bsz = h.shape[0]
    s = h.reshape(bsz, G // 2, 2, 2, P).transpose(0, 2, 1, 3, 4).reshape(bsz, 2, G, P)
    return s[:, 0], s[:, 1]


FOX_AUG_ROWS = 16


def _fox_kernel(qt_ref, qa_ref, k_ref, vt_ref, o_ref, m_ref, l_ref, acc_ref, *, tq, tk, past):
    qi = pl.program_id(2)
    qt = qt_ref[0]
    row = lax.broadcasted_iota(jnp.int32, (LANES, tq), 0)
    zero = jnp.zeros_like(qt)
    pad = jnp.zeros((LANES - FOX_AUG_ROWS, tq), BF16)
    qts = []
    for hh in range(2):
        own = (row < FOX_HEAD_DIM) if hh == 0 else (row >= FOX_HEAD_DIM)
        aug = qa_ref[0, 0, FOX_AUG_ROWS * hh:FOX_AUG_ROWS * (hh + 1), :]
        qts.append(jnp.concatenate([jnp.where(own, qt, zero), aug, pad], axis=0))
    m_ref[...] = jnp.full(m_ref.shape, -1e30, F32)
    l_ref[...] = jnp.zeros(l_ref.shape, F32)
    acc_ref[...] = jnp.zeros(acc_ref.shape, F32)
    q_start = past + qi * tq
    n_full = (q_start + 1) // tk
    n_all = (q_start + tq + tk - 1) // tk

    def block(j, masked):
        ks = pl.multiple_of(j * tk, tk)
        kb = k_ref[0, 0, pl.ds(ks, tk), :]
        if masked:
            kpos = ks + lax.broadcasted_iota(jnp.int32, (tk, tq), 0)
            qpos = q_start + lax.broadcasted_iota(jnp.int32, (tk, tq), 1)
            visible = kpos <= qpos
        for hh in range(2):
            st = _dot(kb, qts[hh])
            if masked:
                st = jnp.where(visible, st, -jnp.inf)
            m_old = m_ref[hh]
            m_new = jnp.maximum(m_old, jnp.max(st, axis=0, keepdims=True))
            p = jnp.exp2(st - m_new)
            alpha = jnp.exp2(m_old - m_new)
            l_ref[hh] = alpha * l_ref[hh] + jnp.sum(p, axis=0, keepdims=True)
            vt = vt_ref[0, FOX_HEAD_DIM * hh:FOX_HEAD_DIM * (hh + 1), pl.ds(ks, tk)]
            acc_ref[hh] = alpha * acc_ref[hh] + _dot(vt, p.astype(BF16))
            m_ref[hh] = m_new

    def full_body(j, c):
        block(j, False)
        return c

    def masked_body(j, c):
        block(j, True)
        return c

    lax.fori_loop(0, n_full, full_body, 0)
    lax.fori_loop(n_full, n_all, masked_body, 0)
    out_t = jnp.concatenate([acc_ref[0] / l_ref[0], acc_ref[1] / l_ref[1]], axis=0)
    o_ref[0] = out_t.T


def _fox(q_t, q_aug, k_aug, v_t, tq, tk, past):
    bsz, n_fox, L = q_t.shape
    n_pair = n_fox // LANES
    lk = k_aug.shape[2]
    return pl.pallas_call(
        functools.partial(_fox_kernel, tq=tq, tk=tk, past=past),
        grid=(bsz, n_pair, L // tq),
        in_specs=[pl.BlockSpec((1, LANES, tq), lambda b, h, i: (b, h, i)),
                  pl.BlockSpec((1, 1, 2 * FOX_AUG_ROWS, tq), lambda b, h, i: (b, h, 0, i)),
                  pl.BlockSpec((1, 1, lk, 2 * LANES), lambda b, h, i: (b, h, 0, 0)),
                  pl.BlockSpec((1, LANES, lk), lambda b, h, i: (b, h, 0))],
        out_specs=pl.BlockSpec((1, tq, LANES), lambda b, h, i: (b, i, h)),
        out_shape=jax.ShapeDtypeStruct((bsz, L, n_fox), F32),
        scratch_shapes=[pltpu.VMEM((2, 1, tq), F32), pltpu.VMEM((2, 1, tq), F32),
                        pltpu.VMEM((2, FOX_HEAD_DIM, tq), F32)],
        compiler_params=_params("parallel", "parallel", "arbitrary"),
        name="fox_attention",
    )(q_t, q_aug, k_aug, v_t)


def _fox_operands(qb, k_all, v_all, d_pieces, past, L):
    bsz, lk_pad, n_fox = k_all.shape
    n_heads = n_fox // FOX_HEAD_DIM
    n_pair = n_heads // 2
    pieces = d_pieces.astype(BF16)
    dk = (-pieces).reshape(bsz, 3, n_pair, 2, lk_pad)
    dk = dk.transpose(0, 2, 4, 3, 1).reshape(bsz, n_pair, lk_pad, 6)
    ones_k = jnp.ones((bsz, n_pair, lk_pad, 3), BF16)
    zeros_k = jnp.zeros((bsz, n_pair, lk_pad, LANES - 9), BF16)
    k_pair = k_all.reshape(bsz, lk_pad, n_pair, LANES).transpose(0, 2, 1, 3)
    k_aug = jnp.concatenate([k_pair, dk, ones_k, zeros_k], axis=-1)

    dq = pieces[:, :, :, past:past + L].transpose(0, 2, 1, 3)
    sel = jax.nn.one_hot(jnp.arange(n_heads) % 2, 2, dtype=BF16)
    ones_q = jnp.broadcast_to(jnp.repeat(sel, 3, axis=1)[None, :, :, None], (bsz, n_heads, 6, L))
    zeros_q = jnp.zeros((bsz, n_heads, FOX_AUG_ROWS - 9, L), BF16)
    q_aug = jnp.concatenate([ones_q, dq, zeros_q], axis=2).reshape(bsz, n_pair, 2 * FOX_AUG_ROWS, L)
    return qb.transpose(0, 2, 1), q_aug, k_aug, v_all.transpose(0, 2, 1)


def _route_gates(logits_t, ebias_t, n_experts):
    per_group = n_experts // N_EXPERT_GROUPS
    tokens = logits_t.shape[1]
    score = _sigmoid(logits_t)
    sel = score + ebias_t
    row = lax.broadcasted_iota(jnp.int32, (per_group, tokens), 0).astype(F32)
    neg = jnp.float32(-jnp.inf)

    def first_argmax(tile, best):
        return jnp.min(jnp.where(tile == best, row, float(per_group)), axis=0, keepdims=True)

    score_g, sel_g, gsc = [], [], []
    for g in range(N_EXPERT_GROUPS):
        sg = sel[per_group * g:per_group * (g + 1), :]
        score_g.append(score[per_group * g:per_group * (g + 1), :])
        sel_g.append(sg)
        m1 = jnp.max(sg, axis=0, keepdims=True)
        rest = jnp.where(row == first_argmax(sg, m1), neg, sg)
        gsc.append(m1 + jnp.max(rest, axis=0, keepdims=True))
    cur = []
    for g in range(N_EXPERT_GROUPS):
        ahead = jnp.zeros((1, tokens), F32)
        for o in range(N_EXPERT_GROUPS):
            if o == g:
                continue
            beats = (gsc[o] >= gsc[g]) if o < g else (gsc[o] > gsc[g])
            ahead = ahead + jnp.where(beats, 1.0, 0.0)
        cur.append(jnp.where(ahead < TOPK_GROUPS, sel_g[g], neg))
    chosen = [jnp.zeros((per_group, tokens), jnp.bool_) for _ in range(N_EXPERT_GROUPS)]
    for _ in range(TOP_K):
        best = cur[0]
        for g in range(1, N_EXPERT_GROUPS):
            best = jnp.maximum(best, cur[g])
        best = jnp.max(best, axis=0, keepdims=True)
        idx = None
        for g in range(N_EXPERT_GROUPS):
            cand = jnp.min(jnp.where(cur[g] == best, row + float(per_group * g), float(n_experts)), axis=0,
                           keepdims=True)
            idx = cand if idx is None else jnp.minimum(idx, cand)
        for g in range(N_EXPERT_GROUPS):
            hit = (row + float(per_group * g)) == idx
            chosen[g] = chosen[g] | hit
            cur[g] = jnp.where(hit, neg, cur[g])
    w = [jnp.where(chosen[g], score_g[g], 0.0) for g in range(N_EXPERT_GROUPS)]
    total = w[0]
    for g in range(1, N_EXPERT_GROUPS):
        total = total + w[g]
    total = jnp.sum(total, axis=0, keepdims=True)
    return [w[g] / total * ROUTED_SCALE for g in range(N_EXPERT_GROUPS)]


def _mid_kernel(x_ref, ys_ref, yf_ref, gfox_ref, wout_ref, gmq_ref, wmq_ref, mk_ref, mv_ref, wmo_ref, gffn_ref,
                wrt_ref, ebt_ref, ws1_ref, ws3_ref, ws2_ref, xs_ref, xn_ref, gt_ref, cnt_ref, *, n_experts):
    x = x_ref[0]
    yfn = _rms(yf_ref[0], gfox_ref[...]).astype(BF16)
    mix = jnp.concatenate([ys_ref[0], yfn], axis=1)
    x1 = x + _dot(mix, wout_ref[...])

    qm = _dot(_rms(x1, gmq_ref[...]).astype(BF16), wmq_ref[...])
    hd = qm.shape[1] // MEM_HEADS
    heads = []
    for h in range(MEM_HEADS):
        qh = (qm[:, hd * h:hd * (h + 1)] * (hd ** -0.5)).astype(BF16)
        s = lax.dot_general(qh, mk_ref[0, :, hd * h:hd * (h + 1)], _NT, preferred_element_type=F32)
        p = jnp.exp(s - jnp.max(s, axis=1, keepdims=True))
        o = _dot(p.astype(BF16), mv_ref[0, :, hd * h:hd * (h + 1)]) / jnp.sum(p, axis=1, keepdims=True)
        heads.append(o.astype(BF16))
    x2 = x1 + _dot(jnp.concatenate(heads, axis=1), wmo_ref[...])

    xn = _rms(x2, gffn_ref[...])
    xnb = xn.astype(BF16)
    hidden = _silu(_dot(xnb, ws1_ref[...])) * _dot(xnb, ws3_ref[...])
    xs_ref[0] = x2 + _dot(hidden.astype(BF16), ws2_ref[...])
    xn_ref[0] = xnb

    logits_t = lax.dot_general(wrt_ref[...], xn, _NT, preferred_element_type=F32, precision=_HIGHEST)
    gates = _route_gates(logits_t, ebt_ref[...], n_experts)
    per_group = n_experts // N_EXPERT_GROUPS
    for g in range(N_EXPERT_GROUPS):
        gt_ref[0, per_group * g:per_group * (g + 1), :] = gates[g]
        cnt_ref[0, 0, per_group * g:per_group * (g + 1), :] = jnp.sum(
            jnp.where(gates[g] != 0.0, 1.0, 0.0), axis=1, keepdims=True)


def _mid(x, ys, yf, gfox, wout, gmq, wmq, mkb, mvb, wmo, gffn, wrt, ebt, ws1, ws3, ws2, tm):
    bsz, L, d = x.shape
    n_ssm, n_fox = ys.shape[2], yf.shape[2]
    n_mem = mkb.shape[1]
    n_experts = wrt.shape[0]
    fs = ws1.shape[1]
    row = lambda n: pl.BlockSpec((1, tm, n), lambda b, i: (b, i, 0))
    memspec = pl.BlockSpec((1, n_mem, d), lambda b, i: (b, 0, 0))
    return pl.pallas_call(
        functools.partial(_mid_kernel, n_experts=n_experts),
        grid=(bsz, L // tm),
        in_specs=[row(d), row(n_ssm), row(n_fox), _const_spec((1, n_fox)), _const_spec((d, d)), _const_spec((1, d)),
                  _const_spec((d, d)), memspec, memspec, _const_spec((d, d)), _const_spec((1, d)),
                  _const_spec((n_experts, d)), _const_spec((n_experts, 1)), _const_spec((d, fs)), _const_spec((d, fs)),
                  _const_spec((fs, d))],
        out_specs=[row(d), row(d), pl.BlockSpec((1, n_experts, tm), lambda b, i: (b, 0, i)),
                   pl.BlockSpec((1, 1, n_experts, 1), lambda b, i: (b, i, 0, 0))],
        out_shape=[jax.ShapeDtypeStruct((bsz, L, d), F32), jax.ShapeDtypeStruct((bsz, L, d), BF16),
                   jax.ShapeDtypeStruct((bsz, n_experts, L), F32),
                   jax.ShapeDtypeStruct((bsz, L // tm, n_experts, 1), F32)],
        compiler_params=_params("parallel", "parallel"),
        name="mid_block",
    )(x, ys, yf, gfox, wout, gmq, wmq, mkb, mvb, wmo, gffn, wrt, ebt, ws1, ws3, ws2)


SEG_ROWS = 16
MOE_TOKENS = 512
MOE_TILE = 512
_TN = (((0,), (0,)), ((), ()))


def _sorted_rows(tb, n_experts):
    rows = TOP_K * tb + n_experts * (SEG_ROWS - 1)
    return -(-rows // MOE_TILE) * MOE_TILE


def _dispatch_plan(cnt, ns):
    nblk, n_experts = cnt.shape

    def before(a, axis):
        n = a.shape[axis]
        earlier = jnp.arange(n)[:, None] < jnp.arange(n)[None, :]
        if axis == 0:
            return jnp.sum(jnp.where(earlier[:, :, None], a[:, None, :], 0), axis=0)
        return jnp.sum(jnp.where(earlier[None, :, :], a[:, :, None], 0), axis=1)

    cp = (cnt + SEG_ROWS - 1) // SEG_ROWS * SEG_ROWS
    o_loc = before(cp, 1)
    used = jnp.sum(cp, axis=1)
    tot_e = jnp.sum(cp, axis=0)
    reg_e = (tot_e + MOE_TILE - 1) // MOE_TILE * MOE_TILE
    base_e = before(reg_e[None, :], 1)[0]
    reg_end = base_e + reg_e
    dst = base_e[None, :] + before(cp, 0)
    rows_max = nblk * ns + n_experts * MOE_TILE
    n_tiles_max = -(-rows_max // MOE_TILE)
    n_tiles = reg_end[-1] // MOE_TILE
    tile_idx = jnp.clip(jnp.arange(n_tiles_max, dtype=jnp.int32), 0, jnp.maximum(n_tiles - 1, 0))
    tile_expert = jnp.sum((reg_end[None, :] <= (tile_idx * MOE_TILE)[:, None]).astype(jnp.int32), axis=1)
    tile_expert = jnp.minimum(tile_expert, n_experts - 1)
    i32 = lambda a: a.astype(jnp.int32).reshape(-1)
    return dict(o_loc=i32(o_loc), n_chunk=i32(cp // SEG_ROWS), dst=i32(dst), unused_chunks=i32((ns - used) // SEG_ROWS),
                used=i32(used), tail_start=i32(base_e + tot_e), tail_chunks=i32((reg_e - tot_e) // SEG_ROWS),
                tile_idx=i32(tile_idx), tile_expert=i32(tile_expert), n_tiles=i32(n_tiles),
                n_tiles_max=n_tiles_max)


def _slot_ranks(gates, tb):
    sel = gates != 0.0
    r = lax.broadcasted_iota(jnp.int32, (tb, tb), 0)
    c = lax.broadcasted_iota(jnp.int32, (tb, tb), 1)
    earlier = jnp.where(r < c, 1.0, 0.0).astype(BF16)
    rank = _dot(jnp.where(sel, 1.0, 0.0).astype(BF16), earlier)
    return jnp.where(sel, rank, -1.0)


BIG_ROWS = 4 * SEG_ROWS


def _for_each_piece(b, n_experts, o_loc_ref, n_chunk_ref, fn):
    per_big = BIG_ROWS // SEG_ROWS

    def per_expert(e, carry):
        off = o_loc_ref[b * n_experts + e]
        n = n_chunk_ref[b * n_experts + e]
        n_big = n // per_big

        def big(i, c2):
            fn(e, i * BIG_ROWS, pl.multiple_of(off + i * BIG_ROWS, SEG_ROWS), BIG_ROWS)
            return c2

        def small(i, c2):
            rank0 = n_big * BIG_ROWS + i * SEG_ROWS
            fn(e, rank0, pl.multiple_of(off + rank0, SEG_ROWS), SEG_ROWS)
            return c2

        lax.fori_loop(0, n_big, big, 0)
        lax.fori_loop(0, n - n_big * per_big, small, 0)
        return carry

    lax.fori_loop(0, n_experts, per_expert, 0)


def _wait_rows(n_rows, make_copy):
    wide = 16 * SEG_ROWS
    n_wide = n_rows // wide

    def wide_step(i, c):
        make_copy(wide).wait()
        return c

    def seg_step(i, c):
        make_copy(SEG_ROWS).wait()
        return c

    lax.fori_loop(0, n_wide, wide_step, 0)
    lax.fori_loop(0, (n_rows - n_wide * wide) // SEG_ROWS, seg_step, 0)


def _dispatch_kernel(o_loc_ref, n_chunk_ref, dst_ref, unused_ref, used_ref, tail_start_ref, tail_chunks_ref,
                     xn_ref, gt_ref, xg_ref, g_ref, xsb_ref, rm_ref, z_ref, sem, *, tb, ns, n_experts, nblk):
    b = pl.program_id(0)
    rm_ref[...] = _slot_ranks(gt_ref[...], tb)
    rows = lax.broadcasted_iota(jnp.int32, (SEG_ROWS, tb), 0).astype(F32)

    def build(e, rank0, row, size):
        rel = rm_ref[pl.ds(e, 1), :] - rank0.astype(F32)
        for k in range(size // SEG_ROWS):
            hit = rows == (rel - float(k * SEG_ROWS))
            g_ref[pl.ds(pl.multiple_of(row + k * SEG_ROWS, SEG_ROWS), SEG_ROWS), :] = jnp.where(hit, 1.0, 0.0).astype(BF16)

    _for_each_piece(b, n_experts, o_loc_ref, n_chunk_ref, build)

    def clear(i, c):
        g_ref[pl.ds(pl.multiple_of(used_ref[b] + i * SEG_ROWS, SEG_ROWS), SEG_ROWS), :] = jnp.zeros((SEG_ROWS, tb), BF16)
        return c

    lax.fori_loop(0, unused_ref[b], clear, 0)

    xn = xn_ref[...]
    for t in range(ns // MOE_TILE):
        sl = slice(MOE_TILE * t, MOE_TILE * (t + 1))
        xsb_ref[sl, :] = _dot(g_ref[sl, :], xn).astype(BF16)

    def rows_copy(src_row, dst_row, size):
        return pltpu.make_async_copy(xsb_ref.at[pl.ds(src_row, size)], xg_ref.at[pl.ds(dst_row, size)], sem)

    def send(e, rank0, row, size):
        rows_copy(row, pl.multiple_of(dst_ref[b * n_experts + e] + rank0, SEG_ROWS), size).start()

    _for_each_piece(b, n_experts, o_loc_ref, n_chunk_ref, send)
    _wait_rows(used_ref[b], lambda size: rows_copy(0, 0, size))

    @pl.when(b == nblk - 1)
    def _():
        z_ref[...] = jnp.zeros(z_ref.shape, BF16)

        def tail_copy(e, i):
            row = pl.multiple_of(tail_start_ref[e] + i * SEG_ROWS, SEG_ROWS)
            return pltpu.make_async_copy(z_ref, xg_ref.at[pl.ds(row, SEG_ROWS)], sem)

        def each_tail(action):
            def per_expert(e, carry):
                def per_chunk(i, c2):
                    action(tail_copy(e, i))
                    return c2
                lax.fori_loop(0, tail_chunks_ref[e], per_chunk, 0)
                return carry
            lax.fori_loop(0, n_experts, per_expert, 0)

        each_tail(lambda cp: cp.start())
        each_tail(lambda cp: cp.wait())


def _expert_kernel(tile_idx_ref, tile_expert_ref, n_tiles_ref, x_ref, w1_ref, w3_ref, w2_ref, o_ref):
    @pl.when(pl.program_id(0) < n_tiles_ref[0])
    def _():
        x = x_ref[...]
        h = _silu(_dot(x, w1_ref[0])) * _dot(x, w3_ref[0])
        o_ref[...] = _dot(h.astype(BF16), w2_ref[0]).astype(BF16)


def _combine_kernel(o_loc_ref, n_chunk_ref, dst_ref, unused_ref, used_ref,
                    og_ref, gt_ref, xs_ref, gfin_ref, y_ref, gw_ref, ob_ref, rm_ref, sem, *, tb, ns, n_experts):
    b = pl.program_id(0)

    def rows_copy(src_row, dst_row, size):
        return pltpu.make_async_copy(og_ref.at[pl.ds(src_row, size)], ob_ref.at[pl.ds(dst_row, size)], sem)

    def fetch(e, rank0, row, size):
        rows_copy(pl.multiple_of(dst_ref[b * n_experts + e] + rank0, SEG_ROWS), row, size).start()

    _for_each_piece(b, n_experts, o_loc_ref, n_chunk_ref, fetch)

    rm_ref[...] = _slot_ranks(gt_ref[...], tb)
    rows = lax.broadcasted_iota(jnp.int32, (SEG_ROWS, tb), 0).astype(F32)

    def build(e, rank0, row, size):
        rel = rm_ref[pl.ds(e, 1), :] - rank0.astype(F32)
        gate = gt_ref[pl.ds(e, 1), :]
        for k in range(size // SEG_ROWS):
            hit = rows == (rel - float(k * SEG_ROWS))
            gw_ref[pl.ds(pl.multiple_of(row + k * SEG_ROWS, SEG_ROWS), SEG_ROWS), :] = jnp.where(hit, gate, 0.0).astype(BF16)

    _for_each_piece(b, n_experts, o_loc_ref, n_chunk_ref, build)

    def clear_gw(i, c):
        gw_ref[pl.ds(pl.multiple_of(used_ref[b] + i * SEG_ROWS, SEG_ROWS), SEG_ROWS), :] = jnp.zeros((SEG_ROWS, tb), BF16)
        return c

    lax.fori_loop(0, unused_ref[b], clear_gw, 0)
    _wait_rows(used_ref[b], lambda size: rows_copy(0, 0, size))

    def clear(i, c):
        ob_ref[pl.ds(pl.multiple_of(used_ref[b] + i * SEG_ROWS, SEG_ROWS), SEG_ROWS), :] = jnp.zeros(
            (SEG_ROWS, ob_ref.shape[1]), BF16)
        return c

    lax.fori_loop(0, unused_ref[b], clear, 0)
    y = lax.dot_general(gw_ref[...], ob_ref[...], _TN, preferred_element_type=F32) + xs_ref[...]
    y_ref[...] = _rms(y, gfin_ref[...])


def _moe(xn, gates_t, cnt, xs, w1, w3, w2, gfin):
    m, d = xn.shape
    n_experts, _, f = w1.shape
    tb = min(m, MOE_TOKENS)
    nblk = m // tb
    ns = _sorted_rows(tb, n_experts)
    plan = _dispatch_plan(cnt, ns)
    n_tiles_max = plan["n_tiles_max"]
    rows_max = n_tiles_max * MOE_TILE
    seg_tables = (plan["o_loc"], plan["n_chunk"], plan["dst"], plan["unused_chunks"], plan["used"])

    xg = pl.pallas_call(
        functools.partial(_dispatch_kernel, tb=tb, ns=ns, n_experts=n_experts, nblk=nblk),
        grid_spec=pltpu.PrefetchScalarGridSpec(
            num_scalar_prefetch=7, grid=(nblk,),
            in_specs=[pl.BlockSpec((tb, d), lambda b, *_: (b, 0)), pl.BlockSpec((n_experts, tb), lambda b, *_: (0, b))],
            out_specs=pl.BlockSpec(memory_space=pl.ANY),
            scratch_shapes=[pltpu.VMEM((ns, tb), BF16), pltpu.VMEM((ns, d), BF16), pltpu.VMEM((n_experts, tb), F32),
                            pltpu.VMEM((SEG_ROWS, d), BF16), pltpu.SemaphoreType.DMA(())]),
        out_shape=jax.ShapeDtypeStruct((rows_max, d), BF16),
        compiler_params=_params("arbitrary"),
        name="moe_dispatch",
    )(*seg_tables, plan["tail_start"], plan["tail_chunks"], xn, gates_t)

    tile = lambda i, idx, ex, n: (idx[i], 0)
    og = pl.pallas_call(
        _expert_kernel,
        grid_spec=pltpu.PrefetchScalarGridSpec(
            num_scalar_prefetch=3, grid=(n_tiles_max,),
            in_specs=[pl.BlockSpec((MOE_TILE, d), tile),
                      pl.BlockSpec((1, d, f), lambda i, idx, ex, n: (ex[i], 0, 0)),
                      pl.BlockSpec((1, d, f), lambda i, idx, ex, n: (ex[i], 0, 0)),
                      pl.BlockSpec((1, f, d), lambda i, idx, ex, n: (ex[i], 0, 0))],
            out_specs=pl.BlockSpec((MOE_TILE, d), tile)),
        out_shape=jax.ShapeDtypeStruct((rows_max, d), BF16),
        compiler_params=_params("arbitrary"),
        name="moe_experts",
    )(plan["tile_idx"], plan["tile_expert"], plan["n_tiles"], xg, w1, w3, w2)

    return pl.pallas_call(
        functools.partial(_combine_kernel, tb=tb, ns=ns, n_experts=n_experts),
        grid_spec=pltpu.PrefetchScalarGridSpec(
            num_scalar_prefetch=5, grid=(nblk,),
            in_specs=[pl.BlockSpec(memory_space=pl.ANY), pl.BlockSpec((n_experts, tb), lambda b, *_: (0, b)),
                      pl.BlockSpec((tb, d), lambda b, *_: (b, 0)), pl.BlockSpec((1, d), lambda b, *_: (0, 0))],
            out_specs=pl.BlockSpec((tb, d), lambda b, *_: (b, 0)),
            scratch_shapes=[pltpu.VMEM((ns, tb), BF16), pltpu.VMEM((ns, d), BF16), pltpu.VMEM((n_experts, tb), F32),
                            pltpu.SemaphoreType.DMA(())]),
        out_shape=jax.ShapeDtypeStruct((m, d), F32),
        compiler_params=_params("arbitrary"),
        name="moe_combine",
    )(*seg_tables, og, gates_t, xs, gfin)


def _tiles(L):
    t_row = min(L, 512)
    t_key = 512
    t_scan = min(L, 64)
    return t_row, t_key, t_scan


def _group(x, h0_re, h0_im, past_k, past_v, past_lf, mkb, mvb, p, w):
    bsz, L, d = x.shape
    n_heads = p["b_f"].shape[0]
    n_fox = n_heads * FOX_HEAD_DIM
    n_ssm = p["d_skip"].shape[0]
    G, P = p["lam_re"].shape
    t_row, t_key, t_scan = _tiles(L)

    u, qb, k, v, kb, vb, lf = _proj_in(x, w["g_mix"], w["w_main"], w["w_f"], w["b_f"], t_row, n_ssm, n_fox, n_heads)

    u_tb = u.transpose(1, 0, 2).reshape(L * bsz, n_ssm)
    if h0_re is None:
        h0 = jnp.zeros((bsz, 2 * G * P), F32)
    else:
        h0 = _state_to_lanes(h0_re.astype(F32), h0_im.astype(F32))
    ys_tb, h_last = _s5(u_tb, h0, w["ar"], w["ai"], w["wb"], w["wc"], w["d_skip"], w["w_glu"], w["b_glu"],
                        w["g_ssm_out"], t_scan, bsz)
    ys = ys_tb.reshape(L, bsz, n_ssm).transpose(1, 0, 2)
    hl_re, hl_im = _lanes_to_state(h_last, G, P)

    if past_k is None:
        past = 0
        lf_all, k_all, v_all = lf, kb, vb
    else:
        past = past_k.shape[1]
        lf_all = jnp.concatenate([past_lf.astype(F32), lf], axis=1)
        k_all = jnp.concatenate([past_k.reshape(bsz, past, n_fox).astype(BF16), kb], axis=1)
        v_all = jnp.concatenate([past_v.reshape(bsz, past, n_fox).astype(BF16), vb], axis=1)
    lk = lf_all.shape[1]
    lk_pad = -(-lk // t_key) * t_key
    lf_t = jnp.pad(lf_all.transpose(0, 2, 1), ((0, 0), (0, 0), (0, lk_pad - lk)))
    d_pieces = _decay_cumsum(lf_t, jnp.zeros((bsz, n_heads, 1), F32), t_key)
    k_all = jnp.pad(k_all, ((0, 0), (0, lk_pad - lk), (0, 0)))
    v_all = jnp.pad(v_all, ((0, 0), (0, lk_pad - lk), (0, 0)))
    yf = _fox(*_fox_operands(qb, k_all, v_all, d_pieces, past, L), t_row, t_key, past)

    xs, xnb, gates_t, cnt = _mid(x, ys, yf, w["g_fox_out"], w["w_out"], w["g_mem_q"], w["w_mq"], mkb, mvb, w["w_mo"],
                                 w["g_ffn"], w["w_router_t"], w["e_bias_t"], w["ws1"], w["ws3"], w["ws2"], t_row)
    m = bsz * L
    n_experts = gates_t.shape[1]
    tb = min(m, MOE_TOKENS)
    cnt = cnt.reshape(m // tb, tb // t_row, n_experts).sum(axis=1).astype(jnp.int32)
    y = _moe(xnb.reshape(m, d), gates_t.transpose(1, 0, 2).reshape(n_experts, m), cnt, xs.reshape(m, d),
             w["w1"], w["w3"], w["w2"], w["g_final"])
    return (y.reshape(bsz, L, d), hl_re, hl_im, k.reshape(bsz, L, n_heads, FOX_HEAD_DIM),
            v.reshape(bsz, L, n_heads, FOX_HEAD_DIM), lf)


def kernel(x_prompt, x_sample, state_ssm_re, state_ssm_im, cache_fox_k, cache_fox_v, cache_fox_logf, cache_mem_k, cache_mem_v, mem_prompt, g_mix, w_in, b_f, lam_re, lam_im, log_dt, b_re, b_im, c_re, c_im, d_skip, w_glu, b_glu, g_ssm_out, g_fox_out, w_out, g_mem_q, g_mem_kv, w_mq, w_mk, w_mv, w_mo, g_ffn, w_router, e_bias, w1, w3, w2, ws1, ws3, ws2, g_final):
    depth = w_in.shape[0]
    bsz = x_prompt.shape[0]
    hp, hs = x_prompt, x_sample
    outs_p, outs_s = [], []
    for l in range(depth):
        p = dict(b_f=b_f[l], lam_re=lam_re[l], d_skip=d_skip[l])
        n_heads = b_f.shape[1]
        n_ssm = d_skip.shape[1]
        n_main = w_in.shape[2] - n_heads
        row = lambda a: a.reshape(1, -1).astype(F32)
        ar, ai, wb, wc = _s5_tables(lam_re[l], lam_im[l], log_dt[l], b_re[l], b_im[l], c_re[l], c_im[l], bsz)
        w = dict(
            g_mix=row(g_mix[l]), w_main=w_in[l][:, :n_main].astype(BF16),
            w_f=jnp.pad(w_in[l][:, n_main:], ((0, 0), (0, LANES - n_heads))).astype(BF16),
            b_f=jnp.pad(b_f[l].astype(F32), (0, LANES - n_heads)).reshape(1, LANES),
            ar=ar, ai=ai, wb=wb, wc=wc, d_skip=row(d_skip[l]), w_glu=w_glu[l].astype(BF16), b_glu=row(b_glu[l]),
            g_ssm_out=row(g_ssm_out[l]), g_fox_out=row(g_fox_out[l]), w_out=w_out[l].astype(BF16),
            g_mem_q=row(g_mem_q[l]), w_mq=w_mq[l].astype(BF16), w_mo=w_mo[l].astype(BF16), g_ffn=row(g_ffn[l]),
            w_router_t=w_router[l].astype(F32).T, e_bias_t=e_bias[l].astype(F32).reshape(-1, 1),
            ws1=ws1[l].astype(BF16), ws3=ws3[l].astype(BF16), ws2=ws2[l].astype(BF16),
            w1=w1[l].astype(BF16), w3=w3[l].astype(BF16), w2=w2[l].astype(BF16), g_final=row(g_final))
        assert depth == 1, "final norm fusion assumes a single layer"
        mk_p, mv_p, mkb_p, mvb_p = _mem_kv(mem_prompt, row(g_mem_kv[l]), w_mk[l].astype(BF16), w_mv[l].astype(BF16))
        nm, mh = mem_prompt.shape[1], MEM_HEADS
        hp, re_p, im_p, k_p, v_p, lf_p = _group(hp, None, None, None, None, None, mkb_p, mvb_p, p, w)
        cm_k = cache_mem_k[l].reshape(bsz, nm, -1).astype(BF16)
        cm_v = cache_mem_v[l].reshape(bsz, nm, -1).astype(BF16)
        hs, re_s, im_s, k_s, v_s, lf_s = _group(hs, state_ssm_re[l], state_ssm_im[l], cache_fox_k[l], cache_fox_v[l],
                                                cache_fox_logf[l], cm_k, cm_v, p, w)
        outs_p.append((re_p, im_p, k_p, v_p, lf_p, mk_p.reshape(bsz, nm, mh, -1), mv_p.reshape(bsz, nm, mh, -1)))
        outs_s.append((re_s, im_s, k_s, v_s, lf_s))
    stack = lambda outs, i: jnp.stack([o[i] for o in outs])
    return (hp, hs) + tuple(stack(outs_p, i) for i in range(7)) + tuple(stack(outs_s, i) for i in range(5))
```

```python
import functools
import math

import jax
import jax.numpy as jnp
from jax import lax
from jax.experimental import pallas as pl
from jax.experimental.pallas import tpu as pltpu

F32 = jnp.float32
BF16 = jnp.bfloat16

SSM_GROUP = 16
SSM_STATE = 64
FOX_HEAD_DIM = 64
MEM_HEADS = 4
TOP_K = 8
N_EXPERT_GROUPS = 8
TOPK_GROUPS = 4
ROUTED_SCALE = 2.5
RMS_EPS = 1e-6

LANES = 128
SUBLANES = 8
MXU_DIM = 256
VMEM_LIMIT_BYTES = 56 * 1024 * 1024

_HIGHEST = lax.Precision.HIGHEST
_NT = (((1,), (1,)), ((), ()))


def _params(*sem):
    return pltpu.CompilerParams(dimension_semantics=sem, vmem_limit_bytes=VMEM_LIMIT_BYTES)


def _rms(x, g):
    return x * lax.rsqrt(jnp.mean(x * x, axis=-1, keepdims=True) + RMS_EPS) * g


def _sigmoid(x):
    return 1.0 / (1.0 + jnp.exp(-x))


def _silu(x):
    return x * _sigmoid(x)


def _gelu_tanh(x):
    return x * (0.5 * (1.0 + jnp.tanh(math.sqrt(2.0 / math.pi) * (x + 0.044715 * (x * x * x)))))


def _log_sigmoid(x):
    return jnp.minimum(x, 0.0) - jnp.log1p(jnp.exp(-jnp.abs(x)))


def _dot(a, b):
    return jnp.dot(a, b, preferred_element_type=F32)


def _const_spec(shape):
    nd = len(shape)
    return pl.BlockSpec(shape, lambda *_: (0,) * nd)


def _memkv_kernel(m_ref, g_ref, wk_ref, wv_ref, k_ref, v_ref, kb_ref, vb_ref):
    mn = _rms(m_ref[0], g_ref[...]).astype(BF16)
    k = _dot(mn, wk_ref[...])
    v = _dot(mn, wv_ref[...])
    k_ref[0] = k
    v_ref[0] = v
    kb_ref[0] = k.astype(BF16)
    vb_ref[0] = v.astype(BF16)


def _mem_kv(mem, g, wk, wv):
    bsz, n, d = mem.shape
    blk = pl.BlockSpec((1, n, d), lambda b: (b, 0, 0))
    return pl.pallas_call(
        _memkv_kernel,
        grid=(bsz,),
        in_specs=[blk, _const_spec((1, d)), _const_spec((d, d)), _const_spec((d, d))],
        out_specs=[blk, blk, blk, blk],
        out_shape=[jax.ShapeDtypeStruct((bsz, n, d), F32)] * 2 + [jax.ShapeDtypeStruct((bsz, n, d), BF16)] * 2,
        compiler_params=_params("parallel"),
        name="mem_kv",
    )(mem, g, wk, wv)


def _proj_in_kernel(x_ref, g_ref, w_ref, wf_ref, bf_ref, u_ref, q_ref, k_ref, v_ref, kb_ref, vb_ref, lf_ref,
                    *, n_ssm, n_fox, n_heads):
    xb = _rms(x_ref[0], g_ref[...]).astype(BF16)
    z = _dot(xb, w_ref[...])
    u_ref[...] = z[:, :n_ssm]
    o = n_ssm
    q_ref[0] = (z[:, o:o + n_fox] * (FOX_HEAD_DIM ** -0.5 * math.log2(math.e))).astype(BF16)
    k = z[:, o + n_fox:o + 2 * n_fox]
    v = z[:, o + 2 * n_fox:o + 3 * n_fox]
    k_ref[0] = k
    v_ref[0] = v
    kb_ref[0] = k.astype(BF16)
    vb_ref[0] = v.astype(BF16)
    zf = _dot(xb, wf_ref[...])
    lf_ref[0] = _log_sigmoid(zf + bf_ref[...])[:, :n_heads]


def _proj_in(x, g, w_main, w_f, b_f, tm, n_ssm, n_fox, n_heads):
    bsz, L, d = x.shape
    nmain = w_main.shape[1]
    row = lambda n: pl.BlockSpec((1, tm, n), lambda b, i: (b, i, 0))
    u_spec = pl.BlockSpec((tm, n_ssm), lambda b, i: (i, b))
    outs = [jax.ShapeDtypeStruct((L, bsz * n_ssm), F32), jax.ShapeDtypeStruct((bsz, L, n_fox), BF16),
            jax.ShapeDtypeStruct((bsz, L, n_fox), F32), jax.ShapeDtypeStruct((bsz, L, n_fox), F32),
            jax.ShapeDtypeStruct((bsz, L, n_fox), BF16), jax.ShapeDtypeStruct((bsz, L, n_fox), BF16),
            jax.ShapeDtypeStruct((bsz, L, n_heads), F32)]
    return pl.pallas_call(
        functools.partial(_proj_in_kernel, n_ssm=n_ssm, n_fox=n_fox, n_heads=n_heads),
        grid=(bsz, L // tm),
        in_specs=[row(d), _const_spec((1, d)), _const_spec((d, nmain)), _const_spec((d, LANES)),
                  _const_spec((1, LANES))],
        out_specs=[u_spec, row(n_fox), row(n_fox), row(n_fox), row(n_fox), row(n_fox), row(n_heads)],
        out_shape=outs,
        compiler_params=_params("parallel", "parallel"),
        name="proj_in",
    )(x, g, w_main, w_f, b_f)


def _decay_kernel(lft_ref, c0_ref, dt_ref, car_ref, *, tl):
    @pl.when(pl.program_id(1) == 0)
    def _():
        car_ref[...] = c0_ref[0]
    r = lax.broadcasted_iota(jnp.int32, (tl, tl), 0)
    c = lax.broadcasted_iota(jnp.int32, (tl, tl), 1)
    tri = (r <= c).astype(F32)
    d = jnp.dot(lft_ref[0], tri, preferred_element_type=F32, precision=_HIGHEST) + car_ref[...]
    car_ref[...] = d[:, tl - 1:tl]
    rest = d * math.log2(math.e)
    for piece in range(3):
        part = rest.astype(BF16).astype(F32)
        dt_ref[0, piece] = part
        rest = rest - part


def _decay_cumsum(lft, c0, tl):
    bsz, nh, L = lft.shape
    return pl.pallas_call(
        functools.partial(_decay_kernel, tl=tl),
        grid=(bsz, L // tl),
        in_specs=[pl.BlockSpec((1, nh, tl), lambda b, i: (b, 0, i)), pl.BlockSpec((1, nh, 1), lambda b, i: (b, 0, 0))],
        out_specs=pl.BlockSpec((1, 3, nh, tl), lambda b, i: (b, 0, 0, i)),
        out_shape=jax.ShapeDtypeStruct((bsz, 3, nh, L), F32),
        scratch_shapes=[pltpu.VMEM((nh, 1), F32)],
        compiler_params=_params("parallel", "arbitrary"),
        name="decay_cumsum",
    )(lft, c0)


def _s5_kernel(u_ref, h0_ref, ar_ref, ai_ref, wb_ref, wc_ref, dsk_ref, wglu_ref, bglu_ref, gout_ref,
               y_ref, hl_ref, hs_ref, hst_ref, *, t_chunk, bsz, n_slab):
    @pl.when(pl.program_id(0) == 0)
    def _():
        hst_ref[...] = h0_ref[...]

    u = u_ref[...]
    ub = u.astype(BF16)
    slab_per_k = MXU_DIM // (2 * SSM_GROUP)
    for j in range(n_slab):
        kt = j // slab_per_k
        hs_ref[:, MXU_DIM * j:MXU_DIM * (j + 1)] = _dot(ub[:, MXU_DIM * kt:MXU_DIM * (kt + 1)], wb_ref[j])

    def step(t, h):
        r0 = pl.multiple_of(t * bsz, bsz)
        bu = hs_ref[pl.ds(r0, bsz), :]
        parts = []
        for j in range(n_slab):
            lo, mid, hi = MXU_DIM * j, MXU_DIM * j + LANES, MXU_DIM * (j + 1)
            re, im = h[:, lo:mid], h[:, mid:hi]
            ar = ar_ref[:, LANES * j:LANES * (j + 1)]
            ai = ai_ref[:, LANES * j:LANES * (j + 1)]
            parts.append(ar * re - ai * im + bu[:, lo:mid])
            parts.append(ar * im + ai * re + bu[:, mid:hi])
        hn = jnp.concatenate(parts, axis=1)
        hs_ref[pl.ds(r0, bsz), :] = hn
        return hn

    h = lax.fori_loop(0, t_chunk, step, hst_ref[...])
    hst_ref[...] = h
    hl_ref[...] = h

    halves = []
    for hf in range(n_slab // slab_per_k):
        acc = None
        for jj in range(slab_per_k):
            j = hf * slab_per_k + jj
            d = _dot(hs_ref[:, MXU_DIM * j:MXU_DIM * (j + 1)].astype(BF16), wc_ref[j])
            acc = d if acc is None else acc + d
        halves.append(acc)
    y = jnp.concatenate(halves, axis=1) + dsk_ref[...] * u
    y = _gelu_tanh(y)
    y = y * _sigmoid(_dot(y.astype(BF16), wglu_ref[...]) + bglu_ref[...])
    y_ref[...] = _rms(y, gout_ref[...]).astype(BF16)


def _s5(u_tb, h0, ar, ai, wb, wc, dsk, wglu, bglu, gout, t_chunk, bsz):
    rows, n_ssm = u_tb.shape
    n_state = h0.shape[1]
    n_slab = n_state // MXU_DIM
    r = t_chunk * bsz
    return pl.pallas_call(
        functools.partial(_s5_kernel, t_chunk=t_chunk, bsz=bsz, n_slab=n_slab),
        grid=(rows // r,),
        in_specs=[pl.BlockSpec((r, n_ssm), lambda c: (c, 0)), _const_spec((bsz, n_state)),
                  _const_spec((bsz, n_state // 2)), _const_spec((bsz, n_state // 2)),
                  _const_spec((n_slab, MXU_DIM, MXU_DIM)), _const_spec((n_slab, MXU_DIM, MXU_DIM)),
                  _const_spec((1, n_ssm)), _const_spec((n_ssm, n_ssm)), _const_spec((1, n_ssm)),
                  _const_spec((1, n_ssm))],
        out_specs=[pl.BlockSpec((r, n_ssm), lambda c: (c, 0)), _const_spec((bsz, n_state))],
        out_shape=[jax.ShapeDtypeStruct((rows, n_ssm), BF16), jax.ShapeDtypeStruct((bsz, n_state), F32)],
        scratch_shapes=[pltpu.VMEM((r, n_state), F32), pltpu.VMEM((bsz, n_state), F32)],
        compiler_params=_params("arbitrary"),
        name="s5_mixer",
    )(u_tb, h0, ar, ai, wb, wc, dsk, wglu, bglu, gout)


def _s5_tables(lam_re, lam_im, log_dt, b_re, b_im, c_re, c_im, bsz):
    G, P = lam_re.shape
    H = b_re.shape[-1]
    lr, li = lam_re.astype(F32), lam_im.astype(F32)
    dt = jnp.exp(log_dt.astype(F32))[:, None]
    mag = jnp.exp(lr * dt)
    a_re, a_im = mag * jnp.cos(li * dt), mag * jnp.sin(li * dt)
    den = lr * lr + li * li
    c_r = ((a_re - 1.0) * lr + a_im * li) / den
    c_i = (a_im * lr - (a_re - 1.0) * li) / den
    br, bi = b_re.astype(F32), b_im.astype(F32)
    bbar_re = c_r[..., None] * br - c_i[..., None] * bi
    bbar_im = c_r[..., None] * bi + c_i[..., None] * br
    n_pair = G // 2
    per_k = MXU_DIM // (2 * H)
    eye2 = jnp.eye(2, dtype=F32)
    place = jax.nn.one_hot(jnp.arange(n_pair) % per_k, per_k, dtype=F32)

    ar = jnp.broadcast_to(a_re.reshape(1, -1), (bsz, G * P))
    ai = jnp.broadcast_to(a_im.reshape(1, -1), (bsz, G * P))

    bb = jnp.stack([bbar_re, bbar_im]).reshape(2, n_pair, 2, P, H)
    wpair = jnp.einsum("ajgph,gk->jghakp", bb, eye2).reshape(n_pair, 2 * H, MXU_DIM)
    wb = jnp.einsum("jrc,jk->jkrc", wpair, place).reshape(n_pair, MXU_DIM, MXU_DIM)

    cc = jnp.stack([c_re.astype(F32), -c_im.astype(F32)]).reshape(2, n_pair, 2, H, P)
    cpair = jnp.einsum("ajghp,gk->jagpkh", cc, eye2).reshape(n_pair, MXU_DIM, 2 * H)
    wc = jnp.einsum("jnc,jk->jnkc", cpair, place).reshape(n_pair, MXU_DIM, MXU_DIM)
    return ar, ai, wb.astype(BF16), wc.astype(BF16)


def _state_to_lanes(re, im):
    bsz, G, P = re.shape
    s = jnp.stack([re, im], axis=1).reshape(bsz, 2, G // 2, 2, P)
    return s.transpose(0, 2, 1, 3, 4).reshape(bsz, 2 * G * P)


def _lanes_to_state(h, G, P):
    bsz = h.shape[0]
    s = h.reshape(bsz, G // 2, 2, 2, P).transpose(0, 2, 1, 3, 4).reshape(bsz, 2, G, P)
    return s[:, 0], s[:, 1]


FOX_AUG_ROWS = 16


def _fox_kernel(qt_ref, qa_ref, k_ref, vt_ref, o_ref, m_ref, l_ref, acc_ref, *, tq, tk, past):
    qi = pl.program_id(2)
    qt = qt_ref[0]
    row = lax.broadcasted_iota(jnp.int32, (LANES, tq), 0)
    zero = jnp.zeros_like(qt)
    pad = jnp.zeros((LANES - FOX_AUG_ROWS, tq), BF16)
    qts = []
    for hh in range(2):
        own = (row < FOX_HEAD_DIM) if hh == 0 else (row >= FOX_HEAD_DIM)
        aug = qa_ref[0, 0, FOX_AUG_ROWS * hh:FOX_AUG_ROWS * (hh + 1), :]
        qts.append(jnp.concatenate([jnp.where(own, qt, zero), aug, pad], axis=0))
    m_ref[...] = jnp.full(m_ref.shape, -1e30, F32)
    l_ref[...] = jnp.zeros(l_ref.shape, F32)
    acc_ref[...] = jnp.zeros(acc_ref.shape, F32)
    q_start = past + qi * tq
    n_full = (q_start + 1) // tk
    n_all = (q_start + tq + tk - 1) // tk

    def block(j, masked):
        ks = pl.multiple_of(j * tk, tk)
        kb = k_ref[0, 0, pl.ds(ks, tk), :]
        if masked:
            kpos = ks + lax.broadcasted_iota(jnp.int32, (tk, tq), 0)
            qpos = q_start + lax.broadcasted_iota(jnp.int32, (tk, tq), 1)
            visible = kpos <= qpos
        for hh in range(2):
            st = _dot(kb, qts[hh])
            if masked:
                st = jnp.where(visible, st, -jnp.inf)
            m_old = m_ref[hh]
            m_new = jnp.maximum(m_old, jnp.max(st, axis=0, keepdims=True))
            p = jnp.exp2(st - m_new)
            alpha = jnp.exp2(m_old - m_new)
            l_ref[hh] = alpha * l_ref[hh] + jnp.sum(p, axis=0, keepdims=True)
            vt = vt_ref[0, FOX_HEAD_DIM * hh:FOX_HEAD_DIM * (hh + 1), pl.ds(ks, tk)]
            acc_ref[hh] = alpha * acc_ref[hh] + _dot(vt, p.astype(BF16))
            m_ref[hh] = m_new

    def full_body(j, c):
        block(j, False)
        return c

    def masked_body(j, c):
        block(j, True)
        return c

    lax.fori_loop(0, n_full, full_body, 0)
    lax.fori_loop(n_full, n_all, masked_body, 0)
    out_t = jnp.concatenate([acc_ref[0] / l_ref[0], acc_ref[1] / l_ref[1]], axis=0)
    o_ref[0] = out_t.T


def _fox(q_t, q_aug, k_aug, v_t, tq, tk, past):
    bsz, n_fox, L = q_t.shape
    n_pair = n_fox // LANES
    lk = k_aug.shape[2]
    return pl.pallas_call(
        functools.partial(_fox_kernel, tq=tq, tk=tk, past=past),
        grid=(bsz, n_pair, L // tq),
        in_specs=[pl.BlockSpec((1, LANES, tq), lambda b, h, i: (b, h, i)),
                  pl.BlockSpec((1, 1, 2 * FOX_AUG_ROWS, tq), lambda b, h, i: (b, h, 0, i)),
                  pl.BlockSpec((1, 1, lk, 2 * LANES), lambda b, h, i: (b, h, 0, 0)),
                  pl.BlockSpec((1, LANES, lk), lambda b, h, i: (b, h, 0))],
        out_specs=pl.BlockSpec((1, tq, LANES), lambda b, h, i: (b, i, h)),
        out_shape=jax.ShapeDtypeStruct((bsz, L, n_fox), F32),
        scratch_shapes=[pltpu.VMEM((2, 1, tq), F32), pltpu.VMEM((2, 1, tq), F32),
                        pltpu.VMEM((2, FOX_HEAD_DIM, tq), F32)],
        compiler_params=_params("parallel", "parallel", "arbitrary"),
        name="fox_attention",
    )(q_t, q_aug, k_aug, v_t)


def _fox_operands(qb, k_all, v_all, d_pieces, past, L):
    bsz, lk_pad, n_fox = k_all.shape
    n_heads = n_fox // FOX_HEAD_DIM
    n_pair = n_heads // 2
    pieces = d_pieces.astype(BF16)
    dk = (-pieces).reshape(bsz, 3, n_pair, 2, lk_pad)
    dk = dk.transpose(0, 2, 4, 3, 1).reshape(bsz, n_pair, lk_pad, 6)
    ones_k = jnp.ones((bsz, n_pair, lk_pad, 3), BF16)
    zeros_k = jnp.zeros((bsz, n_pair, lk_pad, LANES - 9), BF16)
    k_pair = k_all.reshape(bsz, lk_pad, n_pair, LANES).transpose(0, 2, 1, 3)
    k_aug = jnp.concatenate([k_pair, dk, ones_k, zeros_k], axis=-1)

    dq = pieces[:, :, :, past:past + L].transpose(0, 2, 1, 3)
    sel = jax.nn.one_hot(jnp.arange(n_heads) % 2, 2, dtype=BF16)
    ones_q = jnp.broadcast_to(jnp.repeat(sel, 3, axis=1)[None, :, :, None], (bsz, n_heads, 6, L))
    zeros_q = jnp.zeros((bsz, n_heads, FOX_AUG_ROWS - 9, L), BF16)
    q_aug = jnp.concatenate([ones_q, dq, zeros_q], axis=2).reshape(bsz, n_pair, 2 * FOX_AUG_ROWS, L)
    return qb.transpose(0, 2, 1), q_aug, k_aug, v_all.transpose(0, 2, 1)


def _route_gates(logits_t, ebias_t, n_experts):
    per_group = n_experts // N_EXPERT_GROUPS
    tokens = logits_t.shape[1]
    score = _sigmoid(logits_t)
    sel = score + ebias_t
    row = lax.broadcasted_iota(jnp.int32, (per_group, tokens), 0).astype(F32)
    neg = jnp.float32(-jnp.inf)

    def first_argmax(tile, best):
        return jnp.min(jnp.where(tile == best, row, float(per_group)), axis=0, keepdims=True)

    score_g, sel_g, gsc = [], [], []
    for g in range(N_EXPERT_GROUPS):
        sg = sel[per_group * g:per_group * (g + 1), :]
        score_g.append(score[per_group * g:per_group * (g + 1), :])
        sel_g.append(sg)
        m1 = jnp.max(sg, axis=0, keepdims=True)
        rest = jnp.where(row == first_argmax(sg, m1), neg, sg)
        gsc.append(m1 + jnp.max(rest, axis=0, keepdims=True))
    cur = []
    for g in range(N_EXPERT_GROUPS):
        ahead = jnp.zeros((1, tokens), F32)
        for o in range(N_EXPERT_GROUPS):
            if o == g:
                continue
            beats = (gsc[o] >= gsc[g]) if o < g else (gsc[o] > gsc[g])
            ahead = ahead + jnp.where(beats, 1.0, 0.0)
        cur.append(jnp.where(ahead < TOPK_GROUPS, sel_g[g], neg))
    chosen = [jnp.zeros((per_group, tokens), jnp.bool_) for _ in range(N_EXPERT_GROUPS)]
    for _ in range(TOP_K):
        best = cur[0]
        for g in range(1, N_EXPERT_GROUPS):
            best = jnp.maximum(best, cur[g])
        best = jnp.max(best, axis=0, keepdims=True)
        idx = None
        for g in range(N_EXPERT_GROUPS):
            cand = jnp.min(jnp.where(cur[g] == best, row + float(per_group * g), float(n_experts)), axis=0,
                           keepdims=True)
            idx = cand if idx is None else jnp.minimum(idx, cand)
        for g in range(N_EXPERT_GROUPS):
            hit = (row + float(per_group * g)) == idx
            chosen[g] = chosen[g] | hit
            cur[g] = jnp.where(hit, neg, cur[g])
    w = [jnp.where(chosen[g], score_g[g], 0.0) for g in range(N_EXPERT_GROUPS)]
    total = w[0]
    for g in range(1, N_EXPERT_GROUPS):
        total = total + w[g]
    total = jnp.sum(total, axis=0, keepdims=True)
    return [w[g] / total * ROUTED_SCALE for g in range(N_EXPERT_GROUPS)]


def _mid_kernel(x_ref, ys_ref, yf_ref, gfox_ref, wout_ref, gmq_ref, wmq_ref, mk_ref, mv_ref, wmo_ref, gffn_ref,
                wrt_ref, ebt_ref, ws1_ref, ws3_ref, ws2_ref, xs_ref, xn_ref, gt_ref, cnt_ref, *, n_experts):
    x = x_ref[0]
    yfn = _rms(yf_ref[0], gfox_ref[...]).astype(BF16)
    mix = jnp.concatenate([ys_ref[...], yfn], axis=1)
    x1 = x + _dot(mix, wout_ref[...])

    qm = _dot(_rms(x1, gmq_ref[...]).astype(BF16), wmq_ref[...])
    hd = qm.shape[1] // MEM_HEADS
    heads = []
    for h in range(MEM_HEADS):
        qh = (qm[:, hd * h:hd * (h + 1)] * (hd ** -0.5)).astype(BF16)
        s = lax.dot_general(qh, mk_ref[0, :, hd * h:hd * (h + 1)], _NT, preferred_element_type=F32)
        p = jnp.exp(s - jnp.max(s, axis=1, keepdims=True))
        o = _dot(p.astype(BF16), mv_ref[0, :, hd * h:hd * (h + 1)]) / jnp.sum(p, axis=1, keepdims=True)
        heads.append(o.astype(BF16))
    x2 = x1 + _dot(jnp.concatenate(heads, axis=1), wmo_ref[...])

    xn = _rms(x2, gffn_ref[...])
    xnb = xn.astype(BF16)
    hidden = _silu(_dot(xnb, ws1_ref[...])) * _dot(xnb, ws3_ref[...])
    xs_ref[0] = x2 + _dot(hidden.astype(BF16), ws2_ref[...])
    xn_ref[0] = xnb

    logits_t = lax.dot_general(wrt_ref[...], xn, _NT, preferred_element_type=F32, precision=_HIGHEST)
    gates = _route_gates(logits_t, ebt_ref[...], n_experts)
    per_group = n_experts // N_EXPERT_GROUPS
    for g in range(N_EXPERT_GROUPS):
        gt_ref[0, per_group * g:per_group * (g + 1), :] = gates[g]
        cnt_ref[0, 0, per_group * g:per_group * (g + 1), :] = jnp.sum(
            jnp.where(gates[g] != 0.0, 1.0, 0.0), axis=1, keepdims=True)


def _mid(x, ys, yf, gfox, wout, gmq, wmq, mkb, mvb, wmo, gffn, wrt, ebt, ws1, ws3, ws2, tm):
    bsz, L, d = x.shape
    n_fox = yf.shape[2]
    n_ssm = ys.shape[1] // bsz
    n_mem = mkb.shape[1]
    n_experts = wrt.shape[0]
    fs = ws1.shape[1]
    row = lambda n: pl.BlockSpec((1, tm, n), lambda b, i: (b, i, 0))
    memspec = pl.BlockSpec((1, n_mem, d), lambda b, i: (b, 0, 0))
    return pl.pallas_call(
        functools.partial(_mid_kernel, n_experts=n_experts),
        grid=(bsz, L // tm),
        in_specs=[row(d), pl.BlockSpec((tm, n_ssm), lambda b, i: (i, b)), row(n_fox), _const_spec((1, n_fox)),
                  _const_spec((d, d)), _const_spec((1, d)),
                  _const_spec((d, d)), memspec, memspec, _const_spec((d, d)), _const_spec((1, d)),
                  _const_spec((n_experts, d)), _const_spec((n_experts, 1)), _const_spec((d, fs)), _const_spec((d, fs)),
                  _const_spec((fs, d))],
        out_specs=[row(d), row(d), pl.BlockSpec((1, n_experts, tm), lambda b, i: (b, 0, i)),
                   pl.BlockSpec((1, 1, n_experts, 1), lambda b, i: (b, i, 0, 0))],
        out_shape=[jax.ShapeDtypeStruct((bsz, L, d), F32), jax.ShapeDtypeStruct((bsz, L, d), BF16),
                   jax.ShapeDtypeStruct((bsz, n_experts, L), F32),
                   jax.ShapeDtypeStruct((bsz, L // tm, n_experts, 1), F32)],
        compiler_params=_params("parallel", "parallel"),
        name="mid_block",
    )(x, ys, yf, gfox, wout, gmq, wmq, mkb, mvb, wmo, gffn, wrt, ebt, ws1, ws3, ws2)


SEG_ROWS = 16
MOE_TOKENS = 512
MOE_TILE = 512
_TN = (((0,), (0,)), ((), ()))


def _sorted_rows(tb, n_experts):
    rows = TOP_K * tb + n_experts * (SEG_ROWS - 1)
    return -(-rows // MOE_TILE) * MOE_TILE


def _dispatch_plan(cnt, ns):
    nblk, n_experts = cnt.shape

    def before(a, axis):
        n = a.shape[axis]
        earlier = jnp.arange(n)[:, None] < jnp.arange(n)[None, :]
        if axis == 0:
            return jnp.sum(jnp.where(earlier[:, :, None], a[:, None, :], 0), axis=0)
        return jnp.sum(jnp.where(earlier[None, :, :], a[:, :, None], 0), axis=1)

    cp = (cnt + SEG_ROWS - 1) // SEG_ROWS * SEG_ROWS
    o_loc = before(cp, 1)
    used = jnp.sum(cp, axis=1)
    tot_e = jnp.sum(cp, axis=0)
    reg_e = (tot_e + MOE_TILE - 1) // MOE_TILE * MOE_TILE
    base_e = before(reg_e[None, :], 1)[0]
    reg_end = base_e + reg_e
    dst = base_e[None, :] + before(cp, 0)
    rows_max = nblk * ns + n_experts * MOE_TILE
    n_tiles_max = -(-rows_max // MOE_TILE)
    n_tiles = reg_end[-1] // MOE_TILE
    tile_idx = jnp.clip(jnp.arange(n_tiles_max, dtype=jnp.int32), 0, jnp.maximum(n_tiles - 1, 0))
    tile_expert = jnp.sum((reg_end[None, :] <= (tile_idx * MOE_TILE)[:, None]).astype(jnp.int32), axis=1)
    tile_expert = jnp.minimum(tile_expert, n_experts - 1)
    i32 = lambda a: a.astype(jnp.int32).reshape(-1)
    return dict(o_loc=i32(o_loc), n_chunk=i32(cp // SEG_ROWS), dst=i32(dst), unused_chunks=i32((ns - used) // SEG_ROWS),
                used=i32(used), tail_start=i32(base_e + tot_e), tail_chunks=i32((reg_e - tot_e) // SEG_ROWS),
                tile_idx=i32(tile_idx), tile_expert=i32(tile_expert), n_tiles=i32(n_tiles),
                n_tiles_max=n_tiles_max)


def _slot_ranks(gates, tb):
    sel = gates != 0.0
    r = lax.broadcasted_iota(jnp.int32, (tb, tb), 0)
    c = lax.broadcasted_iota(jnp.int32, (tb, tb), 1)
    earlier = jnp.where(r < c, 1.0, 0.0).astype(BF16)
    rank = _dot(jnp.where(sel, 1.0, 0.0).astype(BF16), earlier)
    return jnp.where(sel, rank, -1.0)


BIG_ROWS = 4 * SEG_ROWS


def _for_each_piece(b, n_experts, o_loc_ref, n_chunk_ref, fn):
    per_big = BIG_ROWS // SEG_ROWS

    def per_expert(e, carry):
        off = o_loc_ref[b * n_experts + e]
        n = n_chunk_ref[b * n_experts + e]
        n_big = n // per_big

        def big(i, c2):
            fn(e, i * BIG_ROWS, pl.multiple_of(off + i * BIG_ROWS, SEG_ROWS), BIG_ROWS)
            return c2

        def small(i, c2):
            rank0 = n_big * BIG_ROWS + i * SEG_ROWS
            fn(e, rank0, pl.multiple_of(off + rank0, SEG_ROWS), SEG_ROWS)
            return c2

        lax.fori_loop(0, n_big, big, 0)
        lax.fori_loop(0, n - n_big * per_big, small, 0)
        return carry

    lax.fori_loop(0, n_experts, per_expert, 0)


def _wait_rows(n_rows, make_copy):
    wide = 16 * SEG_ROWS
    n_wide = n_rows // wide

    def wide_step(i, c):
        make_copy(wide).wait()
        return c

    def seg_step(i, c):
        make_copy(SEG_ROWS).wait()
        return c

    lax.fori_loop(0, n_wide, wide_step, 0)
    lax.fori_loop(0, (n_rows - n_wide * wide) // SEG_ROWS, seg_step, 0)


def _dispatch_kernel(o_loc_ref, n_chunk_ref, dst_ref, unused_ref, used_ref, tail_start_ref, tail_chunks_ref,
                     xn_ref, gt_ref, xg_ref, g_ref, xsb_ref, rm_ref, z_ref, sem, *, tb, ns, n_experts, nblk):
    b = pl.program_id(0)
    rm_ref[...] = _slot_ranks(gt_ref[...], tb)
    rows = lax.broadcasted_iota(jnp.int32, (SEG_ROWS, tb), 0).astype(F32)

    def build(e, rank0, row, size):
        rel = rm_ref[pl.ds(e, 1), :] - rank0.astype(F32)
        for k in range(size // SEG_ROWS):
            hit = rows == (rel - float(k * SEG_ROWS))
            g_ref[pl.ds(pl.multiple_of(row + k * SEG_ROWS, SEG_ROWS), SEG_ROWS), :] = jnp.where(hit, 1.0, 0.0).astype(BF16)

    _for_each_piece(b, n_experts, o_loc_ref, n_chunk_ref, build)

    def clear(i, c):
        g_ref[pl.ds(pl.multiple_of(used_ref[b] + i * SEG_ROWS, SEG_ROWS), SEG_ROWS), :] = jnp.zeros((SEG_ROWS, tb), BF16)
        return c

    lax.fori_loop(0, unused_ref[b], clear, 0)

    slot = b % 2
    xn = xn_ref[...]
    for t in range(ns // MOE_TILE):
        sl = slice(MOE_TILE * t, MOE_TILE * (t + 1))
        xsb_ref[slot, sl, :] = _dot(g_ref[sl, :], xn).astype(BF16)

    def rows_copy(buf, src_row, dst_row, size):
        return pltpu.make_async_copy(xsb_ref.at[buf, pl.ds(src_row, size)], xg_ref.at[pl.ds(dst_row, size)],
                                     sem.at[buf])

    def send(e, rank0, row, size):
        rows_copy(slot, row, pl.multiple_of(dst_ref[b * n_experts + e] + rank0, SEG_ROWS), size).start()

    _for_each_piece(b, n_experts, o_loc_ref, n_chunk_ref, send)

    @pl.when(b > 0)
    def _():
        _wait_rows(used_ref[b - 1], lambda size: rows_copy(1 - slot, 0, 0, size))

    @pl.when(b == nblk - 1)
    def _():
        _wait_rows(used_ref[b], lambda size: rows_copy(slot, 0, 0, size))
        z_ref[...] = jnp.zeros(z_ref.shape, BF16)

        def tail_copy(e, i):
            row = pl.multiple_of(tail_start_ref[e] + i * SEG_ROWS, SEG_ROWS)
            return pltpu.make_async_copy(z_ref, xg_ref.at[pl.ds(row, SEG_ROWS)], sem.at[slot])

        def each_tail(action):
            def per_expert(e, carry):
                def per_chunk(i, c2):
                    action(tail_copy(e, i))
                    return c2
                lax.fori_loop(0, tail_chunks_ref[e], per_chunk, 0)
                return carry
            lax.fori_loop(0, n_experts, per_expert, 0)

        each_tail(lambda cp: cp.start())
        each_tail(lambda cp: cp.wait())


def _expert_kernel(tile_idx_ref, tile_expert_ref, n_tiles_ref, x_ref, w1_ref, w3_ref, w2_ref, o_ref):
    @pl.when(pl.program_id(0) < n_tiles_ref[0])
    def _():
        x = x_ref[...]
        h = _silu(_dot(x, w1_ref[0].astype(BF16))) * _dot(x, w3_ref[0].astype(BF16))
        o_ref[...] = _dot(h.astype(BF16), w2_ref[0].astype(BF16)).astype(BF16)


def _combine_kernel(o_loc_ref, n_chunk_ref, dst_ref, unused_ref, used_ref,
                    og_ref, gt_ref, xs_ref, gfin_ref, y_ref, gw_ref, ob_ref, rm_ref, sem, *, tb, ns, n_experts, nblk):
    b = pl.program_id(0)
    slot = b % 2

    def rows_copy(buf, src_row, dst_row, size):
        return pltpu.make_async_copy(og_ref.at[pl.ds(src_row, size)], ob_ref.at[buf, pl.ds(dst_row, size)],
                                     sem.at[buf])

    def fetch_block(blk, buf):
        def fetch(e, rank0, row, size):
            rows_copy(buf, pl.multiple_of(dst_ref[blk * n_experts + e] + rank0, SEG_ROWS), row, size).start()
        _for_each_piece(blk, n_experts, o_loc_ref, n_chunk_ref, fetch)

    @pl.when(b == 0)
    def _():
        fetch_block(b, slot)

    @pl.when(b + 1 < nblk)
    def _():
        fetch_block(b + 1, 1 - slot)

    rm_ref[...] = _slot_ranks(gt_ref[...], tb)
    rows = lax.broadcasted_iota(jnp.int32, (SEG_ROWS, tb), 0).astype(F32)

    def build(e, rank0, row, size):
        rel = rm_ref[pl.ds(e, 1), :] - rank0.astype(F32)
        gate = gt_ref[pl.ds(e, 1), :]
        for k in range(size // SEG_ROWS):
            hit = rows == (rel - float(k * SEG_ROWS))
            gw_ref[pl.ds(pl.multiple_of(row + k * SEG_ROWS, SEG_ROWS), SEG_ROWS), :] = jnp.where(hit, gate, 0.0).astype(BF16)

    _for_each_piece(b, n_experts, o_loc_ref, n_chunk_ref, build)

    def clear_gw(i, c):
        gw_ref[pl.ds(pl.multiple_of(used_ref[b] + i * SEG_ROWS, SEG_ROWS), SEG_ROWS), :] = jnp.zeros((SEG_ROWS, tb), BF16)
        return c

    lax.fori_loop(0, unused_ref[b], clear_gw, 0)
    _wait_rows(used_ref[b], lambda size: rows_copy(slot, 0, 0, size))

    def clear(i, c):
        ob_ref[slot, pl.ds(pl.multiple_of(used_ref[b] + i * SEG_ROWS, SEG_ROWS), SEG_ROWS), :] = jnp.zeros(
            (SEG_ROWS, ob_ref.shape[2]), BF16)
        return c

    lax.fori_loop(0, unused_ref[b], clear, 0)
    y = lax.dot_general(gw_ref[...], ob_ref[slot], _TN, preferred_element_type=F32) + xs_ref[...]
    y_ref[...] = _rms(y, gfin_ref[...])


def _moe(xn, gates_t, cnt, xs, w1, w3, w2, gfin):
    m, d = xn.shape
    n_experts, _, f = w1.shape
    tb = min(m, MOE_TOKENS)
    nblk = m // tb
    ns = _sorted_rows(tb, n_experts)
    plan = _dispatch_plan(cnt, ns)
    n_tiles_max = plan["n_tiles_max"]
    rows_max = n_tiles_max * MOE_TILE
    seg_tables = (plan["o_loc"], plan["n_chunk"], plan["dst"], plan["unused_chunks"], plan["used"])

    xg = pl.pallas_call(
        functools.partial(_dispatch_kernel, tb=tb, ns=ns, n_experts=n_experts, nblk=nblk),
        grid_spec=pltpu.PrefetchScalarGridSpec(
            num_scalar_prefetch=7, grid=(nblk,),
            in_specs=[pl.BlockSpec((tb, d), lambda b, *_: (b, 0)), pl.BlockSpec((n_experts, tb), lambda b, *_: (0, b))],
            out_specs=pl.BlockSpec(memory_space=pl.ANY),
            scratch_shapes=[pltpu.VMEM((ns, tb), BF16), pltpu.VMEM((2, ns, d), BF16), pltpu.VMEM((n_experts, tb), F32),
                            pltpu.VMEM((SEG_ROWS, d), BF16), pltpu.SemaphoreType.DMA((2,))]),
        out_shape=jax.ShapeDtypeStruct((rows_max, d), BF16),
        compiler_params=_params("arbitrary"),
        name="moe_dispatch",
    )(*seg_tables, plan["tail_start"], plan["tail_chunks"], xn, gates_t)

    tile = lambda i, idx, ex, n: (idx[i], 0)
    og = pl.pallas_call(
        _expert_kernel,
        grid_spec=pltpu.PrefetchScalarGridSpec(
            num_scalar_prefetch=3, grid=(n_tiles_max,),
            in_specs=[pl.BlockSpec((MOE_TILE, d), tile),
                      pl.BlockSpec((1, d, f), lambda i, idx, ex, n: (ex[i], 0, 0)),
                      pl.BlockSpec((1, d, f), lambda i, idx, ex, n: (ex[i], 0, 0)),
                      pl.BlockSpec((1, f, d), lambda i, idx, ex, n: (ex[i], 0, 0))],
            out_specs=pl.BlockSpec((MOE_TILE, d), tile)),
        out_shape=jax.ShapeDtypeStruct((rows_max, d), BF16),
        compiler_params=_params("arbitrary"),
        name="moe_experts",
    )(plan["tile_idx"], plan["tile_expert"], plan["n_tiles"], xg, w1, w3, w2)

    return pl.pallas_call(
        functools.partial(_combine_kernel, tb=tb, ns=ns, n_experts=n_experts, nblk=nblk),
        grid_spec=pltpu.PrefetchScalarGridSpec(
            num_scalar_prefetch=5, grid=(nblk,),
            in_specs=[pl.BlockSpec(memory_space=pl.ANY), pl.BlockSpec((n_experts, tb), lambda b, *_: (0, b)),
                      pl.BlockSpec((tb, d), lambda b, *_: (b, 0)), pl.BlockSpec((1, d), lambda b, *_: (0, 0))],
            out_specs=pl.BlockSpec((tb, d), lambda b, *_: (b, 0)),
            scratch_shapes=[pltpu.VMEM((ns, tb), BF16), pltpu.VMEM((2, ns, d), BF16), pltpu.VMEM((n_experts, tb), F32),
                            pltpu.SemaphoreType.DMA((2,))]),
        out_shape=jax.ShapeDtypeStruct((m, d), F32),
        compiler_params=_params("arbitrary"),
        name="moe_combine",
    )(*seg_tables, og, gates_t, xs, gfin)


def _tiles(L):
    t_row = min(L, 512)
    t_key = 512
    t_scan = min(L, 64)
    return t_row, t_key, t_scan


def _group(x, h0_re, h0_im, past_k, past_v, past_lf, mkb, mvb, p, w):
    bsz, L, d = x.shape
    n_heads = p["b_f"].shape[0]
    n_fox = n_heads * FOX_HEAD_DIM
    n_ssm = p["d_skip"].shape[0]
    G, P = p["lam_re"].shape
    t_row, t_key, t_scan = _tiles(L)

    u, qb, k, v, kb, vb, lf = _proj_in(x, w["g_mix"], w["w_main"], w["w_f"], w["b_f"], t_row, n_ssm, n_fox, n_heads)

    u_tb = u.reshape(L * bsz, n_ssm)
    if h0_re is None:
        h0 = jnp.zeros((bsz, 2 * G * P), F32)
    else:
        h0 = _state_to_lanes(h0_re.astype(F32), h0_im.astype(F32))
    ys_tb, h_last = _s5(u_tb, h0, w["ar"], w["ai"], w["wb"], w["wc"], w["d_skip"], w["w_glu"], w["b_glu"],
                        w["g_ssm_out"], t_scan, bsz)
    ys = ys_tb.reshape(L, bsz * n_ssm)
    hl_re, hl_im = _lanes_to_state(h_last, G, P)

    if past_k is None:
        past = 0
        lf_all, k_all, v_all = lf, kb, vb
    else:
        past = past_k.shape[1]
        lf_all = jnp.concatenate([past_lf.astype(F32), lf], axis=1)
        k_all = jnp.concatenate([past_k.reshape(bsz, past, n_fox).astype(BF16), kb], axis=1)
        v_all = jnp.concatenate([past_v.reshape(bsz, past, n_fox).astype(BF16), vb], axis=1)
    lk = lf_all.shape[1]
    lk_pad = -(-lk // t_key) * t_key
    lf_t = jnp.pad(lf_all.transpose(0, 2, 1), ((0, 0), (0, 0), (0, lk_pad - lk)))
    d_pieces = _decay_cumsum(lf_t, jnp.zeros((bsz, n_heads, 1), F32), t_key)
    k_all = jnp.pad(k_all, ((0, 0), (0, lk_pad - lk), (0, 0)))
    v_all = jnp.pad(v_all, ((0, 0), (0, lk_pad - lk), (0, 0)))
    yf = _fox(*_fox_operands(qb, k_all, v_all, d_pieces, past, L), t_row, t_key, past)

    xs, xnb, gates_t, cnt = _mid(x, ys, yf, w["g_fox_out"], w["w_out"], w["g_mem_q"], w["w_mq"], mkb, mvb, w["w_mo"],
                                 w["g_ffn"], w["w_router_t"], w["e_bias_t"], w["ws1"], w["ws3"], w["ws2"], t_row)
    m = bsz * L
    n_experts = gates_t.shape[1]
    tb = min(m, MOE_TOKENS)
    cnt = cnt.reshape(m // tb, tb // t_row, n_experts).sum(axis=1).astype(jnp.int32)
    y = _moe(xnb.reshape(m, d), gates_t.transpose(1, 0, 2).reshape(n_experts, m), cnt, xs.reshape(m, d),
             w["w1"], w["w3"], w["w2"], w["g_final"])
    return (y.reshape(bsz, L, d), hl_re, hl_im, k.reshape(bsz, L, n_heads, FOX_HEAD_DIM),
            v.reshape(bsz, L, n_heads, FOX_HEAD_DIM), lf)


def kernel(x_prompt, x_sample, state_ssm_re, state_ssm_im, cache_fox_k, cache_fox_v, cache_fox_logf, cache_mem_k, cache_mem_v, mem_prompt, g_mix, w_in, b_f, lam_re, lam_im, log_dt, b_re, b_im, c_re, c_im, d_skip, w_glu, b_glu, g_ssm_out, g_fox_out, w_out, g_mem_q, g_mem_kv, w_mq, w_mk, w_mv, w_mo, g_ffn, w_router, e_bias, w1, w3, w2, ws1, ws3, ws2, g_final):
    depth = w_in.shape[0]
    bsz = x_prompt.shape[0]
    hp, hs = x_prompt, x_sample
    outs_p, outs_s = [], []
    for l in range(depth):
        p = dict(b_f=b_f[l], lam_re=lam_re[l], d_skip=d_skip[l])
        n_heads = b_f.shape[1]
        n_ssm = d_skip.shape[1]
        n_main = w_in.shape[2] - n_heads
        row = lambda a: a.reshape(1, -1).astype(F32)
        ar, ai, wb, wc = _s5_tables(lam_re[l], lam_im[l], log_dt[l], b_re[l], b_im[l], c_re[l], c_im[l], bsz)
        w = dict(
            g_mix=row(g_mix[l]), w_main=w_in[l][:, :n_main].astype(BF16),
            w_f=jnp.pad(w_in[l][:, n_main:], ((0, 0), (0, LANES - n_heads))).astype(BF16),
            b_f=jnp.pad(b_f[l].astype(F32), (0, LANES - n_heads)).reshape(1, LANES),
            ar=ar, ai=ai, wb=wb, wc=wc, d_skip=row(d_skip[l]), w_glu=w_glu[l].astype(BF16), b_glu=row(b_glu[l]),
            g_ssm_out=row(g_ssm_out[l]), g_fox_out=row(g_fox_out[l]), w_out=w_out[l].astype(BF16),
            g_mem_q=row(g_mem_q[l]), w_mq=w_mq[l].astype(BF16), w_mo=w_mo[l].astype(BF16), g_ffn=row(g_ffn[l]),
            w_router_t=w_router[l].astype(F32).T, e_bias_t=e_bias[l].astype(F32).reshape(-1, 1),
            ws1=ws1[l].astype(BF16), ws3=ws3[l].astype(BF16), ws2=ws2[l].astype(BF16),
            w1=w1[l], w3=w3[l], w2=w2[l], g_final=row(g_final))
        assert depth == 1, "final norm fusion assumes a single layer"
        mk_p, mv_p, mkb_p, mvb_p = _mem_kv(mem_prompt, row(g_mem_kv[l]), w_mk[l].astype(BF16), w_mv[l].astype(BF16))
        nm, mh = mem_prompt.shape[1], MEM_HEADS
        hp, re_p, im_p, k_p, v_p, lf_p = _group(hp, None, None, None, None, None, mkb_p, mvb_p, p, w)
        cm_k = cache_mem_k[l].reshape(bsz, nm, -1).astype(BF16)
        cm_v = cache_mem_v[l].reshape(bsz, nm, -1).astype(BF16)
        hs, re_s, im_s, k_s, v_s, lf_s = _group(hs, state_ssm_re[l], state_ssm_im[l], cache_fox_k[l], cache_fox_v[l],
                                                cache_fox_logf[l], cm_k, cm_v, p, w)
        outs_p.append((re_p, im_p, k_p, v_p, lf_p, mk_p.reshape(bsz, nm, mh, -1), mv_p.reshape(bsz, nm, mh, -1)))
        outs_s.append((re_s, im_s, k_s, v_s, lf_s))
    stack = lambda outs, i: jnp.stack([o[i] for o in outs])
    return (hp, hs) + tuple(stack(outs_p, i) for i in range(7)) + tuple(stack(outs_s, i) for i in range(5))
```

```python
import functools
import math

import jax
import jax.numpy as jnp
from jax import lax
from jax.experimental import pallas as pl
from jax.experimental.pallas import tpu as pltpu

F32 = jnp.float32
BF16 = jnp.bfloat16

SSM_GROUP = 16
SSM_STATE = 64
FOX_HEAD_DIM = 64
MEM_HEADS = 4
TOP_K = 8
N_EXPERT_GROUPS = 8
TOPK_GROUPS = 4
ROUTED_SCALE = 2.5
RMS_EPS = 1e-6

LANES = 128
SUBLANES = 8
MXU_DIM = 256
VMEM_LIMIT_BYTES = 56 * 1024 * 1024

_HIGHEST = lax.Precision.HIGHEST
_NT = (((1,), (1,)), ((), ()))


def _params(*sem):
    return pltpu.CompilerParams(dimension_semantics=sem, vmem_limit_bytes=VMEM_LIMIT_BYTES)


def _rms(x, g):
    return x * lax.rsqrt(jnp.mean(x * x, axis=-1, keepdims=True) + RMS_EPS) * g


def _sigmoid(x):
    return 1.0 / (1.0 + jnp.exp(-x))


def _silu(x):
    return x * _sigmoid(x)


def _gelu_tanh(x):
    return x * (0.5 * (1.0 + jnp.tanh(math.sqrt(2.0 / math.pi) * (x + 0.044715 * (x * x * x)))))


def _log_sigmoid(x):
    return jnp.minimum(x, 0.0) - jnp.log1p(jnp.exp(-jnp.abs(x)))


def _dot(a, b):
    return jnp.dot(a, b, preferred_element_type=F32)


def _const_spec(shape):
    nd = len(shape)
    return pl.BlockSpec(shape, lambda *_: (0,) * nd)


def _memkv_kernel(m_ref, g_ref, wk_ref, wv_ref, k_ref, v_ref, kb_ref, vb_ref):
    mn = _rms(m_ref[0], g_ref[...]).astype(BF16)
    k = _dot(mn, wk_ref[...])
    v = _dot(mn, wv_ref[...])
    k_ref[0] = k
    v_ref[0] = v
    kb_ref[0] = k.astype(BF16)
    vb_ref[0] = v.astype(BF16)


def _mem_kv(mem, g, wk, wv):
    bsz, n, d = mem.shape
    blk = pl.BlockSpec((1, n, d), lambda b: (b, 0, 0))
    return pl.pallas_call(
        _memkv_kernel,
        grid=(bsz,),
        in_specs=[blk, _const_spec((1, d)), _const_spec((d, d)), _const_spec((d, d))],
        out_specs=[blk, blk, blk, blk],
        out_shape=[jax.ShapeDtypeStruct((bsz, n, d), F32)] * 2 + [jax.ShapeDtypeStruct((bsz, n, d), BF16)] * 2,
        compiler_params=_params("parallel"),
        name="mem_kv",
    )(mem, g, wk, wv)


def _proj_in_kernel(x_ref, g_ref, w_ref, wf_ref, bf_ref, u_ref, q_ref, k_ref, v_ref, kb_ref, vb_ref, lf_ref,
                    *, n_ssm, n_fox, n_heads):
    xb = _rms(x_ref[0], g_ref[...]).astype(BF16)
    z = _dot(xb, w_ref[...])
    u_ref[...] = z[:, :n_ssm]
    o = n_ssm
    q_ref[0] = (z[:, o:o + n_fox] * (FOX_HEAD_DIM ** -0.5 * math.log2(math.e))).astype(BF16)
    k = z[:, o + n_fox:o + 2 * n_fox]
    v = z[:, o + 2 * n_fox:o + 3 * n_fox]
    k_ref[0] = k
    v_ref[0] = v
    kb_ref[0] = k.astype(BF16)
    vb_ref[0] = v.astype(BF16)
    zf = _dot(xb, wf_ref[...])
    lf_ref[0] = _log_sigmoid(zf + bf_ref[...])[:, :n_heads]


def _proj_in(x, g, w_main, w_f, b_f, tm, n_ssm, n_fox, n_heads):
    bsz, L, d = x.shape
    nmain = w_main.shape[1]
    row = lambda n: pl.BlockSpec((1, tm, n), lambda b, i: (b, i, 0))
    u_spec = pl.BlockSpec((tm, n_ssm), lambda b, i: (i, b))
    outs = [jax.ShapeDtypeStruct((L, bsz * n_ssm), F32), jax.ShapeDtypeStruct((bsz, L, n_fox), BF16),
            jax.ShapeDtypeStruct((bsz, L, n_fox), F32), jax.ShapeDtypeStruct((bsz, L, n_fox), F32),
            jax.ShapeDtypeStruct((bsz, L, n_fox), BF16), jax.ShapeDtypeStruct((bsz, L, n_fox), BF16),
            jax.ShapeDtypeStruct((bsz, L, n_heads), F32)]
    return pl.pallas_call(
        functools.partial(_proj_in_kernel, n_ssm=n_ssm, n_fox=n_fox, n_heads=n_heads),
        grid=(bsz, L // tm),
        in_specs=[row(d), _const_spec((1, d)), _const_spec((d, nmain)), _const_spec((d, LANES)),
                  _const_spec((1, LANES))],
        out_specs=[u_spec, row(n_fox), row(n_fox), row(n_fox), row(n_fox), row(n_fox), row(n_heads)],
        out_shape=outs,
        compiler_params=_params("parallel", "parallel"),
        name="proj_in",
    )(x, g, w_main, w_f, b_f)


def _decay_kernel(lft_ref, c0_ref, dt_ref, car_ref, *, tl):
    @pl.when(pl.program_id(1) == 0)
    def _():
        car_ref[...] = c0_ref[0]
    r = lax.broadcasted_iota(jnp.int32, (tl, tl), 0)
    c = lax.broadcasted_iota(jnp.int32, (tl, tl), 1)
    tri = (r <= c).astype(F32)
    d = jnp.dot(lft_ref[0], tri, preferred_element_type=F32, precision=_HIGHEST) + car_ref[...]
    car_ref[...] = d[:, tl - 1:tl]
    rest = d * math.log2(math.e)
    for piece in range(3):
        part = rest.astype(BF16).astype(F32)
        dt_ref[0, piece] = part
        rest = rest - part


def _decay_cumsum(lft, c0, tl):
    bsz, nh, L = lft.shape
    return pl.pallas_call(
        functools.partial(_decay_kernel, tl=tl),
        grid=(bsz, L // tl),
        in_specs=[pl.BlockSpec((1, nh, tl), lambda b, i: (b, 0, i)), pl.BlockSpec((1, nh, 1), lambda b, i: (b, 0, 0))],
        out_specs=pl.BlockSpec((1, 3, nh, tl), lambda b, i: (b, 0, 0, i)),
        out_shape=jax.ShapeDtypeStruct((bsz, 3, nh, L), F32),
        scratch_shapes=[pltpu.VMEM((nh, 1), F32)],
        compiler_params=_params("parallel", "arbitrary"),
        name="decay_cumsum",
    )(lft, c0)


def _s5_kernel(u_ref, h0_ref, ar_ref, ai_ref, wb_ref, wc_ref, dsk_ref, wglu_ref, bglu_ref, gout_ref,
               y_ref, hl_ref, hs_ref, hst_ref, *, t_chunk, bsz, n_slab):
    @pl.when(pl.program_id(0) == 0)
    def _():
        hst_ref[...] = h0_ref[...]

    u = u_ref[...]
    ub = u.astype(BF16)
    slab_per_k = MXU_DIM // (2 * SSM_GROUP)
    for j in range(n_slab):
        kt = j // slab_per_k
        hs_ref[:, MXU_DIM * j:MXU_DIM * (j + 1)] = _dot(ub[:, MXU_DIM * kt:MXU_DIM * (kt + 1)], wb_ref[j])

    def step(t, h):
        r0 = pl.multiple_of(t * bsz, bsz)
        bu = hs_ref[pl.ds(r0, bsz), :]
        parts = []
        for j in range(n_slab):
            lo, mid, hi = MXU_DIM * j, MXU_DIM * j + LANES, MXU_DIM * (j + 1)
            re, im = h[:, lo:mid], h[:, mid:hi]
            ar = ar_ref[:, LANES * j:LANES * (j + 1)]
            ai = ai_ref[:, LANES * j:LANES * (j + 1)]
            parts.append(ar * re - ai * im + bu[:, lo:mid])
            parts.append(ar * im + ai * re + bu[:, mid:hi])
        hn = jnp.concatenate(parts, axis=1)
        hs_ref[pl.ds(r0, bsz), :] = hn
        return hn

    h = lax.fori_loop(0, t_chunk, step, hst_ref[...])
    hst_ref[...] = h
    hl_ref[...] = h

    halves = []
    for hf in range(n_slab // slab_per_k):
        acc = None
        for jj in range(slab_per_k):
            j = hf * slab_per_k + jj
            d = _dot(hs_ref[:, MXU_DIM * j:MXU_DIM * (j + 1)].astype(BF16), wc_ref[j])
            acc = d if acc is None else acc + d
        halves.append(acc)
    y = jnp.concatenate(halves, axis=1) + dsk_ref[...] * u
    y = _gelu_tanh(y)
    y = y * _sigmoid(_dot(y.astype(BF16), wglu_ref[...]) + bglu_ref[...])
    y_ref[...] = _rms(y, gout_ref[...]).astype(BF16)


def _s5(u_tb, h0, ar, ai, wb, wc, dsk, wglu, bglu, gout, t_chunk, bsz):
    rows, n_ssm = u_tb.shape
    n_state = h0.shape[1]
    n_slab = n_state // MXU_DIM
    r = t_chunk * bsz
    return pl.pallas_call(
        functools.partial(_s5_kernel, t_chunk=t_chunk, bsz=bsz, n_slab=n_slab),
        grid=(rows // r,),
        in_specs=[pl.BlockSpec((r, n_ssm), lambda c: (c, 0)), _const_spec((bsz, n_state)),
                  _const_spec((bsz, n_state // 2)), _const_spec((bsz, n_state // 2)),
                  _const_spec((n_slab, MXU_DIM, MXU_DIM)), _const_spec((n_slab, MXU_DIM, MXU_DIM)),
                  _const_spec((1, n_ssm)), _const_spec((n_ssm, n_ssm)), _const_spec((1, n_ssm)),
                  _const_spec((1, n_ssm))],
        out_specs=[pl.BlockSpec((r, n_ssm), lambda c: (c, 0)), _const_spec((bsz, n_state))],
        out_shape=[jax.ShapeDtypeStruct((rows, n_ssm), BF16), jax.ShapeDtypeStruct((bsz, n_state), F32)],
        scratch_shapes=[pltpu.VMEM((r, n_state), F32), pltpu.VMEM((bsz, n_state), F32)],
        compiler_params=_params("arbitrary"),
        name="s5_mixer",
    )(u_tb, h0, ar, ai, wb, wc, dsk, wglu, bglu, gout)


def _s5_tables(lam_re, lam_im, log_dt, b_re, b_im, c_re, c_im, bsz):
    G, P = lam_re.shape
    H = b_re.shape[-1]
    lr, li = lam_re.astype(F32), lam_im.astype(F32)
    dt = jnp.exp(log_dt.astype(F32))[:, None]
    mag = jnp.exp(lr * dt)
    a_re, a_im = mag * jnp.cos(li * dt), mag * jnp.sin(li * dt)
    den = lr * lr + li * li
    c_r = ((a_re - 1.0) * lr + a_im * li) / den
    c_i = (a_im * lr - (a_re - 1.0) * li) / den
    br, bi = b_re.astype(F32), b_im.astype(F32)
    bbar_re = c_r[..., None] * br - c_i[..., None] * bi
    bbar_im = c_r[..., None] * bi + c_i[..., None] * br
    n_pair = G // 2
    per_k = MXU_DIM // (2 * H)
    eye2 = jnp.eye(2, dtype=F32)
    place = jax.nn.one_hot(jnp.arange(n_pair) % per_k, per_k, dtype=F32)

    ar = jnp.broadcast_to(a_re.reshape(1, -1), (bsz, G * P))
    ai = jnp.broadcast_to(a_im.reshape(1, -1), (bsz, G * P))

    bb = jnp.stack([bbar_re, bbar_im]).reshape(2, n_pair, 2, P, H)
    wpair = jnp.einsum("ajgph,gk->jghakp", bb, eye2).reshape(n_pair, 2 * H, MXU_DIM)
    wb = jnp.einsum("jrc,jk->jkrc", wpair, place).reshape(n_pair, MXU_DIM, MXU_DIM)

    cc = jnp.stack([c_re.astype(F32), -c_im.astype(F32)]).reshape(2, n_pair, 2, H, P)
    cpair = jnp.einsum("ajghp,gk->jagpkh", cc, eye2).reshape(n_pair, MXU_DIM, 2 * H)
    wc = jnp.einsum("jnc,jk->jnkc", cpair, place).reshape(n_pair, MXU_DIM, MXU_DIM)
    return ar, ai, wb.astype(BF16), wc.astype(BF16)


def _state_to_lanes(re, im):
    bsz, G, P = re.shape
    s = jnp.stack([re, im], axis=1).reshape(bsz, 2, G // 2, 2, P)
    return s.transpose(0, 2, 1, 3, 4).reshape(bsz, 2 * G * P)


def _lanes_to_state(h, G, P):
    bsz = h.shape[0]
    s = h.reshape(bsz, G // 2, 2, 2, P).transpose(0, 2, 1, 3, 4).reshape(bsz, 2, G, P)
    return s[:, 0], s[:, 1]


FOX_AUG_ROWS = 16


def _fox_kernel(qt_ref, qa_ref, k_ref, vt_ref, o_ref, m_ref, l_ref, acc_ref, *, tq, tk, past):
    qi = pl.program_id(2)
    qt = qt_ref[0]
    row = lax.broadcasted_iota(jnp.int32, (LANES, tq), 0)
    zero = jnp.zeros_like(qt)
    pad = jnp.zeros((LANES - FOX_AUG_ROWS, tq), BF16)
    qts = []
    for hh in range(2):
        own = (row < FOX_HEAD_DIM) if hh == 0 else (row >= FOX_HEAD_DIM)
        aug = qa_ref[0, 0, FOX_AUG_ROWS * hh:FOX_AUG_ROWS * (hh + 1), :]
        qts.append(jnp.concatenate([jnp.where(own, qt, zero), aug, pad], axis=0))
    m_ref[...] = jnp.full(m_ref.shape, -1e30, F32)
    l_ref[...] = jnp.zeros(l_ref.shape, F32)
    acc_ref[...] = jnp.zeros(acc_ref.shape, F32)
    q_start = past + qi * tq
    n_full = (q_start + 1) // tk
    n_all = (q_start + tq + tk - 1) // tk

    def block(j, masked):
        ks = pl.multiple_of(j * tk, tk)
        kb = k_ref[0, 0, pl.ds(ks, tk), :]
        if masked:
            kpos = ks + lax.broadcasted_iota(jnp.int32, (tk, tq), 0)
            qpos = q_start + lax.broadcasted_iota(jnp.int32, (tk, tq), 1)
            visible = kpos <= qpos
        for hh in range(2):
            st = _dot(kb, qts[hh])
            if masked:
                st = jnp.where(visible, st, -jnp.inf)
            m_old = m_ref[hh]
            m_new = jnp.maximum(m_old, jnp.max(st, axis=0, keepdims=True))
            p = jnp.exp2(st - m_new)
            alpha = jnp.exp2(m_old - m_new)
            l_ref[hh] = alpha * l_ref[hh] + jnp.sum(p, axis=0, keepdims=True)
            vt = vt_ref[0, FOX_HEAD_DIM * hh:FOX_HEAD_DIM * (hh + 1), pl.ds(ks, tk)]
            acc_ref[hh] = alpha * acc_ref[hh] + _dot(vt, p.astype(BF16))
            m_ref[hh] = m_new

    def full_body(j, c):
        block(j, False)
        return c

    def masked_body(j, c):
        block(j, True)
        return c

    lax.fori_loop(0, n_full, full_body, 0)
    lax.fori_loop(n_full, n_all, masked_body, 0)
    out_t = jnp.concatenate([acc_ref[0] / l_ref[0], acc_ref[1] / l_ref[1]], axis=0)
    o_ref[0] = out_t.T


def _fox(q_t, q_aug, k_aug, v_t, tq, tk, past):
    bsz, n_fox, L = q_t.shape
    n_pair = n_fox // LANES
    lk = k_aug.shape[2]
    return pl.pallas_call(
        functools.partial(_fox_kernel, tq=tq, tk=tk, past=past),
        grid=(bsz, n_pair, L // tq),
        in_specs=[pl.BlockSpec((1, LANES, tq), lambda b, h, i: (b, h, i)),
                  pl.BlockSpec((1, 1, 2 * FOX_AUG_ROWS, tq), lambda b, h, i: (b, h, 0, i)),
                  pl.BlockSpec((1, 1, lk, 2 * LANES), lambda b, h, i: (b, h, 0, 0)),
                  pl.BlockSpec((1, LANES, lk), lambda b, h, i: (b, h, 0))],
        out_specs=pl.BlockSpec((1, tq, LANES), lambda b, h, i: (b, i, h)),
        out_shape=jax.ShapeDtypeStruct((bsz, L, n_fox), F32),
        scratch_shapes=[pltpu.VMEM((2, 1, tq), F32), pltpu.VMEM((2, 1, tq), F32),
                        pltpu.VMEM((2, FOX_HEAD_DIM, tq), F32)],
        compiler_params=_params("parallel", "parallel", "arbitrary"),
        name="fox_attention",
    )(q_t, q_aug, k_aug, v_t)


def _fox_operands(qb, k_all, v_all, d_pieces, past, L):
    bsz, lk_pad, n_fox = k_all.shape
    n_heads = n_fox // FOX_HEAD_DIM
    n_pair = n_heads // 2
    pieces = d_pieces.astype(BF16)
    dk = (-pieces).reshape(bsz, 3, n_pair, 2, lk_pad)
    dk = dk.transpose(0, 2, 4, 3, 1).reshape(bsz, n_pair, lk_pad, 6)
    ones_k = jnp.ones((bsz, n_pair, lk_pad, 3), BF16)
    zeros_k = jnp.zeros((bsz, n_pair, lk_pad, LANES - 9), BF16)
    k_pair = k_all.reshape(bsz, lk_pad, n_pair, LANES).transpose(0, 2, 1, 3)
    k_aug = jnp.concatenate([k_pair, dk, ones_k, zeros_k], axis=-1)

    dq = pieces[:, :, :, past:past + L].transpose(0, 2, 1, 3)
    sel = jax.nn.one_hot(jnp.arange(n_heads) % 2, 2, dtype=BF16)
    ones_q = jnp.broadcast_to(jnp.repeat(sel, 3, axis=1)[None, :, :, None], (bsz, n_heads, 6, L))
    zeros_q = jnp.zeros((bsz, n_heads, FOX_AUG_ROWS - 9, L), BF16)
    q_aug = jnp.concatenate([ones_q, dq, zeros_q], axis=2).reshape(bsz, n_pair, 2 * FOX_AUG_ROWS, L)
    return qb.transpose(0, 2, 1), q_aug, k_aug, v_all.transpose(0, 2, 1)


def _route_gates(logits_t, ebias_t, n_experts):
    per_group = n_experts // N_EXPERT_GROUPS
    tokens = logits_t.shape[1]
    score = _sigmoid(logits_t)
    sel = score + ebias_t
    row = lax.broadcasted_iota(jnp.int32, (per_group, tokens), 0).astype(F32)
    neg = jnp.float32(-jnp.inf)

    def first_argmax(tile, best):
        return jnp.min(jnp.where(tile == best, row, float(per_group)), axis=0, keepdims=True)

    score_g, sel_g, gsc = [], [], []
    for g in range(N_EXPERT_GROUPS):
        sg = sel[per_group * g:per_group * (g + 1), :]
        score_g.append(score[per_group * g:per_group * (g + 1), :])
        sel_g.append(sg)
        m1 = jnp.max(sg, axis=0, keepdims=True)
        rest = jnp.where(row == first_argmax(sg, m1), neg, sg)
        gsc.append(m1 + jnp.max(rest, axis=0, keepdims=True))
    cur = []
    for g in range(N_EXPERT_GROUPS):
        ahead = jnp.zeros((1, tokens), F32)
        for o in range(N_EXPERT_GROUPS):
            if o == g:
                continue
            beats = (gsc[o] >= gsc[g]) if o < g else (gsc[o] > gsc[g])
            ahead = ahead + jnp.where(beats, 1.0, 0.0)
        cur.append(jnp.where(ahead < TOPK_GROUPS, sel_g[g], neg))
    chosen = [jnp.zeros((per_group, tokens), jnp.bool_) for _ in range(N_EXPERT_GROUPS)]
    for _ in range(TOP_K):
        best = cur[0]
        for g in range(1, N_EXPERT_GROUPS):
            best = jnp.maximum(best, cur[g])
        best = jnp.max(best, axis=0, keepdims=True)
        idx = None
        for g in range(N_EXPERT_GROUPS):
            cand = jnp.min(jnp.where(cur[g] == best, row + float(per_group * g), float(n_experts)), axis=0,
                           keepdims=True)
            idx = cand if idx is None else jnp.minimum(idx, cand)
        for g in range(N_EXPERT_GROUPS):
            hit = (row + float(per_group * g)) == idx
            chosen[g] = chosen[g] | hit
            cur[g] = jnp.where(hit, neg, cur[g])
    w = [jnp.where(chosen[g], score_g[g], 0.0) for g in range(N_EXPERT_GROUPS)]
    total = w[0]
    for g in range(1, N_EXPERT_GROUPS):
        total = total + w[g]
    total = jnp.sum(total, axis=0, keepdims=True)
    return [w[g] / total * ROUTED_SCALE for g in range(N_EXPERT_GROUPS)]


def _mid_kernel(x_ref, ys_ref, yf_ref, gfox_ref, wout_ref, gmq_ref, wmq_ref, mk_ref, mv_ref, wmo_ref, gffn_ref,
                wrt_ref, ebt_ref, ws1_ref, ws3_ref, ws2_ref, xs_ref, xn_ref, gt_ref, cnt_ref, *, n_experts):
    x = x_ref[0]
    yfn = _rms(yf_ref[0], gfox_ref[...]).astype(BF16)
    mix = jnp.concatenate([ys_ref[...], yfn], axis=1)
    x1 = x + _dot(mix, wout_ref[...])

    qm = _dot(_rms(x1, gmq_ref[...]).astype(BF16), wmq_ref[...])
    hd = qm.shape[1] // MEM_HEADS
    heads = []
    for h in range(MEM_HEADS):
        qh = (qm[:, hd * h:hd * (h + 1)] * (hd ** -0.5)).astype(BF16)
        s = lax.dot_general(qh, mk_ref[0, :, hd * h:hd * (h + 1)], _NT, preferred_element_type=F32)
        p = jnp.exp(s - jnp.max(s, axis=1, keepdims=True))
        o = _dot(p.astype(BF16), mv_ref[0, :, hd * h:hd * (h + 1)]) / jnp.sum(p, axis=1, keepdims=True)
        heads.append(o.astype(BF16))
    x2 = x1 + _dot(jnp.concatenate(heads, axis=1), wmo_ref[...])

    xn = _rms(x2, gffn_ref[...])
    xnb = xn.astype(BF16)
    hidden = _silu(_dot(xnb, ws1_ref[...])) * _dot(xnb, ws3_ref[...])
    xs_ref[0] = x2 + _dot(hidden.astype(BF16), ws2_ref[...])
    xn_ref[0] = xnb

    logits_t = lax.dot_general(wrt_ref[...], xn, _NT, preferred_element_type=F32, precision=_HIGHEST)
    gates = _route_gates(logits_t, ebt_ref[...], n_experts)
    per_group = n_experts // N_EXPERT_GROUPS
    for g in range(N_EXPERT_GROUPS):
        gt_ref[0, per_group * g:per_group * (g + 1), :] = gates[g]
        cnt_ref[0, 0, per_group * g:per_group * (g + 1), :] = jnp.sum(
            jnp.where(gates[g] != 0.0, 1.0, 0.0), axis=1, keepdims=True)


def _mid(x, ys, yf, gfox, wout, gmq, wmq, mkb, mvb, wmo, gffn, wrt, ebt, ws1, ws3, ws2, tm):
    bsz, L, d = x.shape
    n_fox = yf.shape[2]
    n_ssm = ys.shape[1] // bsz
    n_mem = mkb.shape[1]
    n_experts = wrt.shape[0]
    fs = ws1.shape[1]
    row = lambda n: pl.BlockSpec((1, tm, n), lambda b, i: (b, i, 0))
    memspec = pl.BlockSpec((1, n_mem, d), lambda b, i: (b, 0, 0))
    return pl.pallas_call(
        functools.partial(_mid_kernel, n_experts=n_experts),
        grid=(bsz, L // tm),
        in_specs=[row(d), pl.BlockSpec((tm, n_ssm), lambda b, i: (i, b)), row(n_fox), _const_spec((1, n_fox)),
                  _const_spec((d, d)), _const_spec((1, d)),
                  _const_spec((d, d)), memspec, memspec, _const_spec((d, d)), _const_spec((1, d)),
                  _const_spec((n_experts, d)), _const_spec((n_experts, 1)), _const_spec((d, fs)), _const_spec((d, fs)),
                  _const_spec((fs, d))],
        out_specs=[row(d), row(d), pl.BlockSpec((1, n_experts, tm), lambda b, i: (b, 0, i)),
                   pl.BlockSpec((1, 1, n_experts, 1), lambda b, i: (b, i, 0, 0))],
        out_shape=[jax.ShapeDtypeStruct((bsz, L, d), F32), jax.ShapeDtypeStruct((bsz, L, d), BF16),
                   jax.ShapeDtypeStruct((bsz, n_experts, L), F32),
                   jax.ShapeDtypeStruct((bsz, L // tm, n_experts, 1), F32)],
        compiler_params=_params("parallel", "parallel"),
        name="mid_block",
    )(x, ys, yf, gfox, wout, gmq, wmq, mkb, mvb, wmo, gffn, wrt, ebt, ws1, ws3, ws2)


SEG_ROWS = 16
MOE_TOKENS = 512
GATHER_ROWS = 1024
_TN = (((0,), (0,)), ((), ()))
_NO_RANK = -(1 << 20)


def _sorted_rows(tb, n_experts):
    rows = TOP_K * tb + n_experts * (SEG_ROWS - 1)
    return -(-rows // GATHER_ROWS) * GATHER_ROWS


def _expert_tile(m):
    return 512 if m >= 4096 else 128


def _dispatch_plan(cnt, ns, MOE_TILE):
    nblk, n_experts = cnt.shape

    def before(a, axis):
        n = a.shape[axis]
        earlier = jnp.arange(n)[:, None] < jnp.arange(n)[None, :]
        if axis == 0:
            return jnp.sum(jnp.where(earlier[:, :, None], a[:, None, :], 0), axis=0)
        return jnp.sum(jnp.where(earlier[None, :, :], a[:, :, None], 0), axis=1)

    cp = (cnt + SEG_ROWS - 1) // SEG_ROWS * SEG_ROWS
    o_loc = before(cp, 1)
    used = jnp.sum(cp, axis=1)
    tot_e = jnp.sum(cp, axis=0)
    reg_e = (tot_e + MOE_TILE - 1) // MOE_TILE * MOE_TILE
    base_e = before(reg_e[None, :], 1)[0]
    reg_end = base_e + reg_e
    dst = base_e[None, :] + before(cp, 0)
    rows_max = nblk * ns + n_experts * MOE_TILE
    n_tiles_max = -(-rows_max // MOE_TILE)
    n_tiles = reg_end[-1] // MOE_TILE
    tile_idx = jnp.clip(jnp.arange(n_tiles_max, dtype=jnp.int32), 0, jnp.maximum(n_tiles - 1, 0))
    tile_expert = jnp.sum((reg_end[None, :] <= (tile_idx * MOE_TILE)[:, None]).astype(jnp.int32), axis=1)
    tile_expert = jnp.minimum(tile_expert, n_experts - 1)
    n_pc = ns // SEG_ROWS
    piece_row = jnp.arange(n_pc, dtype=jnp.int32) * SEG_ROWS
    seg_end = o_loc + cp
    piece_e = jnp.sum((seg_end[:, None, :] <= piece_row[None, :, None]).astype(jnp.int32), axis=2)
    live = piece_row[None, :] < used[:, None]
    piece_e = jnp.minimum(piece_e, n_experts - 1)
    own = piece_e[:, :, None] == jnp.arange(n_experts, dtype=jnp.int32)[None, None, :]
    pick = lambda table: jnp.sum(jnp.where(own, table[:, None, :], 0), axis=2)
    rank0 = piece_row[None, :] - pick(o_loc)
    piece_rank = jnp.where(live, rank0, _NO_RANK)
    piece_dst = jnp.where(live, pick(dst) + rank0, 0)
    i32 = lambda a: a.astype(jnp.int32).reshape(-1)
    per_block = lambda a: a.astype(jnp.int32).reshape(nblk, 1, n_pc)
    return dict(n_piece=i32(used // SEG_ROWS), piece_e=per_block(piece_e), piece_rank=per_block(piece_rank),
                piece_dst=per_block(piece_dst), tail_start=i32(base_e + tot_e),
                tail_chunks=i32((reg_e - tot_e) // SEG_ROWS), tile_idx=i32(tile_idx), tile_expert=i32(tile_expert),
                n_tiles=i32(n_tiles), n_tiles_max=n_tiles_max)


def _slot_ranks(gates, tb):
    sel = gates != 0.0
    r = lax.broadcasted_iota(jnp.int32, (tb, tb), 0)
    c = lax.broadcasted_iota(jnp.int32, (tb, tb), 1)
    earlier = jnp.where(r < c, 1.0, 0.0).astype(BF16)
    rank = _dot(jnp.where(sel, 1.0, 0.0).astype(BF16), earlier)
    return jnp.where(sel, rank, -1.0)


def _build_one_hot(out_ref, rm_ref, pe_ref, pr_ref, n_pc, tb, value_row):
    rows = lax.broadcasted_iota(jnp.int32, (SEG_ROWS, tb), 0).astype(F32)

    def piece(i, c):
        e = pe_ref[0, 0, i]
        hit = rows == (rm_ref[pl.ds(e, 1), :] - pr_ref[0, 0, i].astype(F32))
        out_ref[pl.ds(pl.multiple_of(i * SEG_ROWS, SEG_ROWS), SEG_ROWS), :] = jnp.where(hit, value_row(e), 0.0).astype(BF16)
        return c

    lax.fori_loop(0, n_pc, piece, 0, unroll=8)


def _wait_rows(n_rows, make_copy):
    wide = 16 * SEG_ROWS
    n_wide = n_rows // wide

    def wide_step(i, c):
        make_copy(wide).wait()
        return c

    def seg_step(i, c):
        make_copy(SEG_ROWS).wait()
        return c

    lax.fori_loop(0, n_wide, wide_step, 0)
    lax.fori_loop(0, (n_rows - n_wide * wide) // SEG_ROWS, seg_step, 0)


def _dispatch_kernel(n_piece_ref, tail_start_ref, tail_chunks_ref, pe_ref, pr_ref, pd_ref,
                     xn_ref, gt_ref, xg_ref, g_ref, xsb_ref, rm_ref, z_ref, sem, *, tb, ns, n_experts, nblk):
    b = pl.program_id(0)
    n_pc = ns // SEG_ROWS
    used = n_piece_ref[b] * SEG_ROWS
    rm_ref[...] = _slot_ranks(gt_ref[...], tb)
    _build_one_hot(g_ref, rm_ref, pe_ref, pr_ref, n_pc, tb, lambda e: 1.0)

    slot = b % 2
    for t in range(ns // GATHER_ROWS):
        sl = slice(GATHER_ROWS * t, GATHER_ROWS * (t + 1))

        @pl.when(GATHER_ROWS * t < used)
        def _():
            xsb_ref[slot, sl, :] = _dot(g_ref[sl, :], xn_ref[...]).astype(BF16)

    def rows_copy(buf, src_row, dst_row, size):
        return pltpu.make_async_copy(xsb_ref.at[buf, pl.ds(src_row, size)], xg_ref.at[pl.ds(dst_row, size)],
                                     sem.at[buf])

    def send(i, c):
        rows_copy(slot, pl.multiple_of(i * SEG_ROWS, SEG_ROWS), pl.multiple_of(pd_ref[0, 0, i], SEG_ROWS),
                  SEG_ROWS).start()
        return c

    lax.fori_loop(0, n_piece_ref[b], send, 0)

    @pl.when(b > 0)
    def _():
        _wait_rows(n_piece_ref[b - 1] * SEG_ROWS, lambda size: rows_copy(1 - slot, 0, 0, size))

    @pl.when(b == nblk - 1)
    def _():
        _wait_rows(used, lambda size: rows_copy(slot, 0, 0, size))
        z_ref[...] = jnp.zeros(z_ref.shape, BF16)

        def tail_copy(e, i):
            row = pl.multiple_of(tail_start_ref[e] + i * SEG_ROWS, SEG_ROWS)
            return pltpu.make_async_copy(z_ref, xg_ref.at[pl.ds(row, SEG_ROWS)], sem.at[slot])

        def each_tail(action):
            def per_expert(e, carry):
                def per_chunk(i, c2):
                    action(tail_copy(e, i))
                    return c2
                lax.fori_loop(0, tail_chunks_ref[e], per_chunk, 0)
                return carry
            lax.fori_loop(0, n_experts, per_expert, 0)

        each_tail(lambda cp: cp.start())
        each_tail(lambda cp: cp.wait())


def _expert_kernel(tile_idx_ref, tile_expert_ref, n_tiles_ref, x_ref, w1_ref, w3_ref, w2_ref, o_ref):
    @pl.when(pl.program_id(0) < n_tiles_ref[0])
    def _():
        x = x_ref[...]
        h = _silu(_dot(x, w1_ref[0].astype(BF16))) * _dot(x, w3_ref[0].astype(BF16))
        o_ref[...] = _dot(h.astype(BF16), w2_ref[0].astype(BF16)).astype(BF16)


def _combine_kernel(n_piece_ref, pe_ref, pr_ref, pd_ref, pd_next_ref,
                    og_ref, gt_ref, xs_ref, gfin_ref, y_ref, gw_ref, ob_ref, rm_ref, sem, *, tb, ns, n_experts, nblk):
    b = pl.program_id(0)
    slot = b % 2
    n_pc = ns // SEG_ROWS

    def rows_copy(buf, src_row, dst_row, size):
        return pltpu.make_async_copy(og_ref.at[pl.ds(src_row, size)], ob_ref.at[buf, pl.ds(dst_row, size)],
                                     sem.at[buf])

    def fetch_block(table_ref, n, buf):
        def fetch(i, c):
            rows_copy(buf, pl.multiple_of(table_ref[0, 0, i], SEG_ROWS), pl.multiple_of(i * SEG_ROWS, SEG_ROWS),
                      SEG_ROWS).start()
            return c
        lax.fori_loop(0, n, fetch, 0)

    @pl.when(b == 0)
    def _():
        fetch_block(pd_ref, n_piece_ref[b], slot)

    @pl.when(b + 1 < nblk)
    def _():
        fetch_block(pd_next_ref, n_piece_ref[jnp.minimum(b + 1, nblk - 1)], 1 - slot)

    rm_ref[...] = _slot_ranks(gt_ref[...], tb)
    _build_one_hot(gw_ref, rm_ref, pe_ref, pr_ref, n_pc, tb, lambda e: gt_ref[pl.ds(e, 1), :])
    _wait_rows(n_piece_ref[b] * SEG_ROWS, lambda size: rows_copy(slot, 0, 0, size))

    def clear(i, c):
        ob_ref[slot, pl.ds(pl.multiple_of((n_piece_ref[b] + i) * SEG_ROWS, SEG_ROWS), SEG_ROWS), :] = jnp.zeros(
            (SEG_ROWS, ob_ref.shape[2]), BF16)
        return c

    lax.fori_loop(0, n_pc - n_piece_ref[b], clear, 0)
    y = lax.dot_general(gw_ref[...], ob_ref[slot], _TN, preferred_element_type=F32) + xs_ref[...]
    y_ref[...] = _rms(y, gfin_ref[...])


def _moe(xn, gates_t, cnt, xs, w1, w3, w2, gfin):
    m, d = xn.shape
    n_experts, _, f = w1.shape
    tb = min(m, MOE_TOKENS)
    nblk = m // tb
    ns = _sorted_rows(tb, n_experts)
    MOE_TILE = _expert_tile(m)
    plan = _dispatch_plan(cnt, ns, MOE_TILE)
    n_tiles_max = plan["n_tiles_max"]
    rows_max = n_tiles_max * MOE_TILE
    n_pc = ns // SEG_ROWS
    piece = lambda index: pl.BlockSpec((1, 1, n_pc), index, memory_space=pltpu.SMEM)
    this_block = piece(lambda b, *_: (b, 0, 0))
    next_block = piece(lambda b, *_: (jnp.minimum(b + 1, nblk - 1), 0, 0))

    xg = pl.pallas_call(
        functools.partial(_dispatch_kernel, tb=tb, ns=ns, n_experts=n_experts, nblk=nblk),
        grid_spec=pltpu.PrefetchScalarGridSpec(
            num_scalar_prefetch=3, grid=(nblk,),
            in_specs=[this_block, this_block, this_block,
                      pl.BlockSpec((tb, d), lambda b, *_: (b, 0)), pl.BlockSpec((n_experts, tb), lambda b, *_: (0, b))],
            out_specs=pl.BlockSpec(memory_space=pl.ANY),
            scratch_shapes=[pltpu.VMEM((ns, tb), BF16), pltpu.VMEM((2, ns, d), BF16), pltpu.VMEM((n_experts, tb), F32),
                            pltpu.VMEM((SEG_ROWS, d), BF16), pltpu.SemaphoreType.DMA((2,))]),
        out_shape=jax.ShapeDtypeStruct((rows_max, d), BF16),
        compiler_params=_params("arbitrary"),
        name="moe_dispatch",
    )(plan["n_piece"], plan["tail_start"], plan["tail_chunks"], plan["piece_e"], plan["piece_rank"], plan["piece_dst"],
      xn, gates_t)

    tile = lambda i, idx, ex, n: (idx[i], 0)
    og = pl.pallas_call(
        _expert_kernel,
        grid_spec=pltpu.PrefetchScalarGridSpec(
            num_scalar_prefetch=3, grid=(n_tiles_max,),
            in_specs=[pl.BlockSpec((MOE_TILE, d), tile),
                      pl.BlockSpec((1, d, f), lambda i, idx, ex, n: (ex[i], 0, 0)),
                      pl.BlockSpec((1, d, f), lambda i, idx, ex, n: (ex[i], 0, 0)),
                      pl.BlockSpec((1, f, d), lambda i, idx, ex, n: (ex[i], 0, 0))],
            out_specs=pl.BlockSpec((MOE_TILE, d), tile)),
        out_shape=jax.ShapeDtypeStruct((rows_max, d), BF16),
        compiler_params=_params("arbitrary"),
        name="moe_experts",
    )(plan["tile_idx"], plan["tile_expert"], plan["n_tiles"], xg, w1, w3, w2)

    return pl.pallas_call(
        functools.partial(_combine_kernel, tb=tb, ns=ns, n_experts=n_experts, nblk=nblk),
        grid_spec=pltpu.PrefetchScalarGridSpec(
            num_scalar_prefetch=1, grid=(nblk,),
            in_specs=[this_block, this_block, this_block, next_block,
                      pl.BlockSpec(memory_space=pl.ANY), pl.BlockSpec((n_experts, tb), lambda b, *_: (0, b)),
                      pl.BlockSpec((tb, d), lambda b, *_: (b, 0)), pl.BlockSpec((1, d), lambda b, *_: (0, 0))],
            out_specs=pl.BlockSpec((tb, d), lambda b, *_: (b, 0)),
            scratch_shapes=[pltpu.VMEM((ns, tb), BF16), pltpu.VMEM((2, ns, d), BF16), pltpu.VMEM((n_experts, tb), F32),
                            pltpu.SemaphoreType.DMA((2,))]),
        out_shape=jax.ShapeDtypeStruct((m, d), F32),
        compiler_params=_params("arbitrary"),
        name="moe_combine",
    )(plan["n_piece"], plan["piece_e"], plan["piece_rank"], plan["piece_dst"], plan["piece_dst"], og, gates_t, xs, gfin)


def _tiles(L):
    t_row = min(L, 512)
    t_key = 512
    t_scan = min(L, 64)
    return t_row, t_key, t_scan


def _group(x, h0_re, h0_im, past_k, past_v, past_lf, mkb, mvb, p, w):
    bsz, L, d = x.shape
    n_heads = p["b_f"].shape[0]
    n_fox = n_heads * FOX_HEAD_DIM
    n_ssm = p["d_skip"].shape[0]
    G, P = p["lam_re"].shape
    t_row, t_key, t_scan = _tiles(L)

    u, qb, k, v, kb, vb, lf = _proj_in(x, w["g_mix"], w["w_main"], w["w_f"], w["b_f"], t_row, n_ssm, n_fox, n_heads)

    u_tb = u.reshape(L * bsz, n_ssm)
    if h0_re is None:
        h0 = jnp.zeros((bsz, 2 * G * P), F32)
    else:
        h0 = _state_to_lanes(h0_re.astype(F32), h0_im.astype(F32))
    ys_tb, h_last = _s5(u_tb, h0, w["ar"], w["ai"], w["wb"], w["wc"], w["d_skip"], w["w_glu"], w["b_glu"],
                        w["g_ssm_out"], t_scan, bsz)
    ys = ys_tb.reshape(L, bsz * n_ssm)
    hl_re, hl_im = _lanes_to_state(h_last, G, P)

    if past_k is None:
        past = 0
        lf_all, k_all, v_all = lf, kb, vb
    else:
        past = past_k.shape[1]
        lf_all = jnp.concatenate([past_lf.astype(F32), lf], axis=1)
        k_all = jnp.concatenate([past_k.reshape(bsz, past, n_fox).astype(BF16), kb], axis=1)
        v_all = jnp.concatenate([past_v.reshape(bsz, past, n_fox).astype(BF16), vb], axis=1)
    lk = lf_all.shape[1]
    lk_pad = -(-lk // t_key) * t_key
    lf_t = jnp.pad(lf_all.transpose(0, 2, 1), ((0, 0), (0, 0), (0, lk_pad - lk)))
    d_pieces = _decay_cumsum(lf_t, jnp.zeros((bsz, n_heads, 1), F32), t_key)
    k_all = jnp.pad(k_all, ((0, 0), (0, lk_pad - lk), (0, 0)))
    v_all = jnp.pad(v_all, ((0, 0), (0, lk_pad - lk), (0, 0)))
    yf = _fox(*_fox_operands(qb, k_all, v_all, d_pieces, past, L), t_row, t_key, past)

    xs, xnb, gates_t, cnt = _mid(x, ys, yf, w["g_fox_out"], w["w_out"], w["g_mem_q"], w["w_mq"], mkb, mvb, w["w_mo"],
                                 w["g_ffn"], w["w_router_t"], w["e_bias_t"], w["ws1"], w["ws3"], w["ws2"], t_row)
    m = bsz * L
    n_experts = gates_t.shape[1]
    tb = min(m, MOE_TOKENS)
    cnt = cnt.reshape(m // tb, tb // t_row, n_experts).sum(axis=1).astype(jnp.int32)
    y = _moe(xnb.reshape(m, d), gates_t.transpose(1, 0, 2).reshape(n_experts, m), cnt, xs.reshape(m, d),
             w["w1"], w["w3"], w["w2"], w["g_final"])
    return (y.reshape(bsz, L, d), hl_re, hl_im, k.reshape(bsz, L, n_heads, FOX_HEAD_DIM),
            v.reshape(bsz, L, n_heads, FOX_HEAD_DIM), lf)


def kernel(x_prompt, x_sample, state_ssm_re, state_ssm_im, cache_fox_k, cache_fox_v, cache_fox_logf, cache_mem_k, cache_mem_v, mem_prompt, g_mix, w_in, b_f, lam_re, lam_im, log_dt, b_re, b_im, c_re, c_im, d_skip, w_glu, b_glu, g_ssm_out, g_fox_out, w_out, g_mem_q, g_mem_kv, w_mq, w_mk, w_mv, w_mo, g_ffn, w_router, e_bias, w1, w3, w2, ws1, ws3, ws2, g_final):
    depth = w_in.shape[0]
    bsz = x_prompt.shape[0]
    hp, hs = x_prompt, x_sample
    outs_p, outs_s = [], []
    for l in range(depth):
        p = dict(b_f=b_f[l], lam_re=lam_re[l], d_skip=d_skip[l])
        n_heads = b_f.shape[1]
        n_ssm = d_skip.shape[1]
        n_main = w_in.shape[2] - n_heads
        row = lambda a: a.reshape(1, -1).astype(F32)
        ar, ai, wb, wc = _s5_tables(lam_re[l], lam_im[l], log_dt[l], b_re[l], b_im[l], c_re[l], c_im[l], bsz)
        w = dict(
            g_mix=row(g_mix[l]), w_main=w_in[l][:, :n_main].astype(BF16),
            w_f=jnp.pad(w_in[l][:, n_main:], ((0, 0), (0, LANES - n_heads))).astype(BF16),
            b_f=jnp.pad(b_f[l].astype(F32), (0, LANES - n_heads)).reshape(1, LANES),
            ar=ar, ai=ai, wb=wb, wc=wc, d_skip=row(d_skip[l]), w_glu=w_glu[l].astype(BF16), b_glu=row(b_glu[l]),
            g_ssm_out=row(g_ssm_out[l]), g_fox_out=row(g_fox_out[l]), w_out=w_out[l].astype(BF16),
            g_mem_q=row(g_mem_q[l]), w_mq=w_mq[l].astype(BF16), w_mo=w_mo[l].astype(BF16), g_ffn=row(g_ffn[l]),
            w_router_t=w_router[l].astype(F32).T, e_bias_t=e_bias[l].astype(F32).reshape(-1, 1),
            ws1=ws1[l].astype(BF16), ws3=ws3[l].astype(BF16), ws2=ws2[l].astype(BF16),
            w1=w1[l], w3=w3[l], w2=w2[l], g_final=row(g_final))
        assert depth == 1, "final norm fusion assumes a single layer"
        mk_p, mv_p, mkb_p, mvb_p = _mem_kv(mem_prompt, row(g_mem_kv[l]), w_mk[l].astype(BF16), w_mv[l].astype(BF16))
        nm, mh = mem_prompt.shape[1], MEM_HEADS
        hp, re_p, im_p, k_p, v_p, lf_p = _group(hp, None, None, None, None, None, mkb_p, mvb_p, p, w)
        cm_k = cache_mem_k[l].reshape(bsz, nm, -1).astype(BF16)
        cm_v = cache_mem_v[l].reshape(bsz, nm, -1).astype(BF16)
        hs, re_s, im_s, k_s, v_s, lf_s = _group(hs, state_ssm_re[l], state_ssm_im[l], cache_fox_k[l], cache_fox_v[l],
                                                cache_fox_logf[l], cm_k, cm_v, p, w)
        outs_p.append((re_p, im_p, k_p, v_p, lf_p, mk_p.reshape(bsz, nm, mh, -1), mv_p.reshape(bsz, nm, mh, -1)))
        outs_s.append((re_s, im_s, k_s, v_s, lf_s))
    stack = lambda outs, i: jnp.stack([o[i] for o in outs])
    return (hp, hs) + tuple(stack(outs_p, i) for i in range(7)) + tuple(stack(outs_s, i) for i in range(5))
```

```python
import functools
import math

import jax
import jax.numpy as jnp
from jax import lax
from jax.experimental import pallas as pl
from jax.experimental.pallas import tpu as pltpu

F32 = jnp.float32
BF16 = jnp.bfloat16

SSM_GROUP = 16
SSM_STATE = 64
FOX_HEAD_DIM = 64
MEM_HEADS = 4
TOP_K = 8
N_EXPERT_GROUPS = 8
TOPK_GROUPS = 4
ROUTED_SCALE = 2.5
RMS_EPS = 1e-6

LANES = 128
SUBLANES = 8
MXU_DIM = 256
VMEM_LIMIT_BYTES = 56 * 1024 * 1024

_HIGHEST = lax.Precision.HIGHEST
_NT = (((1,), (1,)), ((), ()))


def _params(*sem):
    return pltpu.CompilerParams(dimension_semantics=sem, vmem_limit_bytes=VMEM_LIMIT_BYTES)


def _rms(x, g):
    return x * lax.rsqrt(jnp.mean(x * x, axis=-1, keepdims=True) + RMS_EPS) * g


def _sigmoid(x):
    return 1.0 / (1.0 + jnp.exp(-x))


def _silu(x):
    return x * _sigmoid(x)


def _gelu_tanh(x):
    return x * (0.5 * (1.0 + jnp.tanh(math.sqrt(2.0 / math.pi) * (x + 0.044715 * (x * x * x)))))


def _log_sigmoid(x):
    return jnp.minimum(x, 0.0) - jnp.log1p(jnp.exp(-jnp.abs(x)))


def _dot(a, b):
    return jnp.dot(a, b, preferred_element_type=F32)


def _const_spec(shape):
    nd = len(shape)
    return pl.BlockSpec(shape, lambda *_: (0,) * nd)


def _memkv_kernel(m_ref, g_ref, wk_ref, wv_ref, k_ref, v_ref, kb_ref, vb_ref):
    mn = _rms(m_ref[0], g_ref[...]).astype(BF16)
    k = _dot(mn, wk_ref[...])
    v = _dot(mn, wv_ref[...])
    k_ref[0] = k
    v_ref[0] = v
    kb_ref[0] = k.astype(BF16)
    vb_ref[0] = v.astype(BF16)


def _mem_kv(mem, g, wk, wv):
    bsz, n, d = mem.shape
    blk = pl.BlockSpec((1, n, d), lambda b: (b, 0, 0))
    return pl.pallas_call(
        _memkv_kernel,
        grid=(bsz,),
        in_specs=[blk, _const_spec((1, d)), _const_spec((d, d)), _const_spec((d, d))],
        out_specs=[blk, blk, blk, blk],
        out_shape=[jax.ShapeDtypeStruct((bsz, n, d), F32)] * 2 + [jax.ShapeDtypeStruct((bsz, n, d), BF16)] * 2,
        compiler_params=_params("parallel"),
        name="mem_kv",
    )(mem, g, wk, wv)


def _proj_in_kernel(x_ref, g_ref, w_ref, wf_ref, bf_ref, u_ref, q_ref, k_ref, v_ref, kb_ref, vb_ref, lf_ref,
                    *, n_ssm, n_fox, n_heads):
    xb = _rms(x_ref[0], g_ref[...]).astype(BF16)
    z = _dot(xb, w_ref[...])
    u_ref[...] = z[:, :n_ssm]
    o = n_ssm
    q_ref[0] = (z[:, o:o + n_fox] * (FOX_HEAD_DIM ** -0.5 * math.log2(math.e))).astype(BF16)
    k = z[:, o + n_fox:o + 2 * n_fox]
    v = z[:, o + 2 * n_fox:o + 3 * n_fox]
    k_ref[0] = k
    v_ref[0] = v
    kb_ref[0] = k.astype(BF16)
    vb_ref[0] = v.astype(BF16)
    zf = _dot(xb, wf_ref[...])
    lf_ref[0] = _log_sigmoid(zf + bf_ref[...])[:, :n_heads]


def _proj_in(x, g, w_main, w_f, b_f, tm, n_ssm, n_fox, n_heads):
    bsz, L, d = x.shape
    nmain = w_main.shape[1]
    row = lambda n: pl.BlockSpec((1, tm, n), lambda b, i: (b, i, 0))
    u_spec = pl.BlockSpec((tm, n_ssm), lambda b, i: (i, b))
    outs = [jax.ShapeDtypeStruct((L, bsz * n_ssm), F32), jax.ShapeDtypeStruct((bsz, L, n_fox), BF16),
            jax.ShapeDtypeStruct((bsz, L, n_fox), F32), jax.ShapeDtypeStruct((bsz, L, n_fox), F32),
            jax.ShapeDtypeStruct((bsz, L, n_fox), BF16), jax.ShapeDtypeStruct((bsz, L, n_fox), BF16),
            jax.ShapeDtypeStruct((bsz, L, n_heads), F32)]
    return pl.pallas_call(
        functools.partial(_proj_in_kernel, n_ssm=n_ssm, n_fox=n_fox, n_heads=n_heads),
        grid=(bsz, L // tm),
        in_specs=[row(d), _const_spec((1, d)), _const_spec((d, nmain)), _const_spec((d, LANES)),
                  _const_spec((1, LANES))],
        out_specs=[u_spec, row(n_fox), row(n_fox), row(n_fox), row(n_fox), row(n_fox), row(n_heads)],
        out_shape=outs,
        compiler_params=_params("parallel", "parallel"),
        name="proj_in",
    )(x, g, w_main, w_f, b_f)


def _decay_kernel(lft_ref, c0_ref, dt_ref, car_ref, *, tl):
    @pl.when(pl.program_id(1) == 0)
    def _():
        car_ref[...] = c0_ref[0]
    r = lax.broadcasted_iota(jnp.int32, (tl, tl), 0)
    c = lax.broadcasted_iota(jnp.int32, (tl, tl), 1)
    tri = (r <= c).astype(F32)
    d = jnp.dot(lft_ref[0], tri, preferred_element_type=F32, precision=_HIGHEST) + car_ref[...]
    car_ref[...] = d[:, tl - 1:tl]
    rest = d * math.log2(math.e)
    for piece in range(3):
        part = rest.astype(BF16).astype(F32)
        dt_ref[0, piece] = part
        rest = rest - part


def _decay_cumsum(lft, c0, tl):
    bsz, nh, L = lft.shape
    return pl.pallas_call(
        functools.partial(_decay_kernel, tl=tl),
        grid=(bsz, L // tl),
        in_specs=[pl.BlockSpec((1, nh, tl), lambda b, i: (b, 0, i)), pl.BlockSpec((1, nh, 1), lambda b, i: (b, 0, 0))],
        out_specs=pl.BlockSpec((1, 3, nh, tl), lambda b, i: (b, 0, 0, i)),
        out_shape=jax.ShapeDtypeStruct((bsz, 3, nh, L), F32),
        scratch_shapes=[pltpu.VMEM((nh, 1), F32)],
        compiler_params=_params("parallel", "arbitrary"),
        name="decay_cumsum",
    )(lft, c0)


def _s5_kernel(u_ref, h0_ref, ar_ref, ai_ref, wb_ref, wc_ref, dsk_ref, wglu_ref, bglu_ref, gout_ref,
               y_ref, hl_ref, hs_ref, hst_ref, *, t_chunk, bsz, n_slab):
    @pl.when(pl.program_id(0) == 0)
    def _():
        hst_ref[...] = h0_ref[...]

    u = u_ref[...]
    ub = u.astype(BF16)
    slab_per_k = MXU_DIM // (2 * SSM_GROUP)
    for j in range(n_slab):
        kt = j // slab_per_k
        hs_ref[:, MXU_DIM * j:MXU_DIM * (j + 1)] = _dot(ub[:, MXU_DIM * kt:MXU_DIM * (kt + 1)], wb_ref[j])

    def step(t, h):
        r0 = pl.multiple_of(t * bsz, bsz)
        bu = hs_ref[pl.ds(r0, bsz), :]
        parts = []
        for j in range(n_slab):
            lo, mid, hi = MXU_DIM * j, MXU_DIM * j + LANES, MXU_DIM * (j + 1)
            re, im = h[:, lo:mid], h[:, mid:hi]
            ar = ar_ref[:, LANES * j:LANES * (j + 1)]
            ai = ai_ref[:, LANES * j:LANES * (j + 1)]
            parts.append(ar * re - ai * im + bu[:, lo:mid])
            parts.append(ar * im + ai * re + bu[:, mid:hi])
        hn = jnp.concatenate(parts, axis=1)
        hs_ref[pl.ds(r0, bsz), :] = hn
        return hn

    h = lax.fori_loop(0, t_chunk, step, hst_ref[...])
    hst_ref[...] = h
    hl_ref[...] = h

    halves = []
    for hf in range(n_slab // slab_per_k):
        acc = None
        for jj in range(slab_per_k):
            j = hf * slab_per_k + jj
            d = _dot(hs_ref[:, MXU_DIM * j:MXU_DIM * (j + 1)].astype(BF16), wc_ref[j])
            acc = d if acc is None else acc + d
        halves.append(acc)
    y = jnp.concatenate(halves, axis=1) + dsk_ref[...] * u
    y = _gelu_tanh(y)
    y = y * _sigmoid(_dot(y.astype(BF16), wglu_ref[...]) + bglu_ref[...])
    y_ref[...] = _rms(y, gout_ref[...]).astype(BF16)


def _s5(u_tb, h0, ar, ai, wb, wc, dsk, wglu, bglu, gout, t_chunk, bsz):
    rows, n_ssm = u_tb.shape
    n_state = h0.shape[1]
    n_slab = n_state // MXU_DIM
    r = t_chunk * bsz
    return pl.pallas_call(
        functools.partial(_s5_kernel, t_chunk=t_chunk, bsz=bsz, n_slab=n_slab),
        grid=(rows // r,),
        in_specs=[pl.BlockSpec((r, n_ssm), lambda c: (c, 0)), _const_spec((bsz, n_state)),
                  _const_spec((bsz, n_state // 2)), _const_spec((bsz, n_state // 2)),
                  _const_spec((n_slab, MXU_DIM, MXU_DIM)), _const_spec((n_slab, MXU_DIM, MXU_DIM)),
                  _const_spec((1, n_ssm)), _const_spec((n_ssm, n_ssm)), _const_spec((1, n_ssm)),
                  _const_spec((1, n_ssm))],
        out_specs=[pl.BlockSpec((r, n_ssm), lambda c: (c, 0)), _const_spec((bsz, n_state))],
        out_shape=[jax.ShapeDtypeStruct((rows, n_ssm), BF16), jax.ShapeDtypeStruct((bsz, n_state), F32)],
        scratch_shapes=[pltpu.VMEM((r, n_state), F32), pltpu.VMEM((bsz, n_state), F32)],
        compiler_params=_params("arbitrary"),
        name="s5_mixer",
    )(u_tb, h0, ar, ai, wb, wc, dsk, wglu, bglu, gout)


def _s5_tables(lam_re, lam_im, log_dt, b_re, b_im, c_re, c_im, bsz):
    G, P = lam_re.shape
    H = b_re.shape[-1]
    lr, li = lam_re.astype(F32), lam_im.astype(F32)
    dt = jnp.exp(log_dt.astype(F32))[:, None]
    mag = jnp.exp(lr * dt)
    a_re, a_im = mag * jnp.cos(li * dt), mag * jnp.sin(li * dt)
    den = lr * lr + li * li
    c_r = ((a_re - 1.0) * lr + a_im * li) / den
    c_i = (a_im * lr - (a_re - 1.0) * li) / den
    br, bi = b_re.astype(F32), b_im.astype(F32)
    bbar_re = c_r[..., None] * br - c_i[..., None] * bi
    bbar_im = c_r[..., None] * bi + c_i[..., None] * br
    n_pair = G // 2
    per_k = MXU_DIM // (2 * H)
    eye2 = jnp.eye(2, dtype=F32)
    place = jax.nn.one_hot(jnp.arange(n_pair) % per_k, per_k, dtype=F32)

    ar = jnp.broadcast_to(a_re.reshape(1, -1), (bsz, G * P))
    ai = jnp.broadcast_to(a_im.reshape(1, -1), (bsz, G * P))

    bb = jnp.stack([bbar_re, bbar_im]).reshape(2, n_pair, 2, P, H)
    wpair = jnp.einsum("ajgph,gk->jghakp", bb, eye2).reshape(n_pair, 2 * H, MXU_DIM)
    wb = jnp.einsum("jrc,jk->jkrc", wpair, place).reshape(n_pair, MXU_DIM, MXU_DIM)

    cc = jnp.stack([c_re.astype(F32), -c_im.astype(F32)]).reshape(2, n_pair, 2, H, P)
    cpair = jnp.einsum("ajghp,gk->jagpkh", cc, eye2).reshape(n_pair, MXU_DIM, 2 * H)
    wc = jnp.einsum("jnc,jk->jnkc", cpair, place).reshape(n_pair, MXU_DIM, MXU_DIM)
    return ar, ai, wb.astype(BF16), wc.astype(BF16)


def _state_to_lanes(re, im):
    bsz, G, P = re.shape
    s = jnp.stack([re, im], axis=1).reshape(bsz, 2, G // 2, 2, P)
    return s.transpose(0, 2, 1, 3, 4).reshape(bsz, 2 * G * P)


def _lanes_to_state(h, G, P):
    bsz = h.shape[0]
    s = h.reshape(bsz, G // 2, 2, 2, P).transpose(0, 2, 1, 3, 4).reshape(bsz, 2, G, P)
    return s[:, 0], s[:, 1]


FOX_AUG_ROWS = 16


def _fox_kernel(qt_ref, qa_ref, k_ref, vt_ref, o_ref, m_ref, l_ref, acc_ref, *, tq, tk, past):
    qi = pl.program_id(2)
    qt = qt_ref[0]
    row = lax.broadcasted_iota(jnp.int32, (LANES, tq), 0)
    zero = jnp.zeros_like(qt)
    pad = jnp.zeros((LANES - FOX_AUG_ROWS, tq), BF16)
    qts = []
    for hh in range(2):
        own = (row < FOX_HEAD_DIM) if hh == 0 else (row >= FOX_HEAD_DIM)
        aug = qa_ref[0, 0, FOX_AUG_ROWS * hh:FOX_AUG_ROWS * (hh + 1), :]
        qts.append(jnp.concatenate([jnp.where(own, qt, zero), aug, pad], axis=0))
    m_ref[...] = jnp.full(m_ref.shape, -1e30, F32)
    l_ref[...] = jnp.zeros(l_ref.shape, F32)
    acc_ref[...] = jnp.zeros(acc_ref.shape, F32)
    q_start = past + qi * tq
    n_full = (q_start + 1) // tk
    n_all = (q_start + tq + tk - 1) // tk

    def block(j, masked):
        ks = pl.multiple_of(j * tk, tk)
        kb = k_ref[0, 0, pl.ds(ks, tk), :]
        if masked:
            kpos = ks + lax.broadcasted_iota(jnp.int32, (tk, tq), 0)
            qpos = q_start + lax.broadcasted_iota(jnp.int32, (tk, tq), 1)
            visible = kpos <= qpos
        for hh in range(2):
            st = _dot(kb, qts[hh])
            if masked:
                st = jnp.where(visible, st, -jnp.inf)
            m_old = m_ref[hh]
            m_new = jnp.maximum(m_old, jnp.max(st, axis=0, keepdims=True))
            p = jnp.exp2(st - m_new)
            alpha = jnp.exp2(m_old - m_new)
            l_ref[hh] = alpha * l_ref[hh] + jnp.sum(p, axis=0, keepdims=True)
            vt = vt_ref[0, FOX_HEAD_DIM * hh:FOX_HEAD_DIM * (hh + 1), pl.ds(ks, tk)]
            acc_ref[hh] = alpha * acc_ref[hh] + _dot(vt, p.astype(BF16))
            m_ref[hh] = m_new

    def full_body(j, c):
        block(j, False)
        return c

    def masked_body(j, c):
        block(j, True)
        return c

    lax.fori_loop(0, n_full, full_body, 0)
    lax.fori_loop(n_full, n_all, masked_body, 0)
    out_t = jnp.concatenate([acc_ref[0] / l_ref[0], acc_ref[1] / l_ref[1]], axis=0)
    o_ref[0] = out_t.T


def _fox(q_t, q_aug, k_aug, v_t, tq, tk, past):
    bsz, n_fox, L = q_t.shape
    n_pair = n_fox // LANES
    lk = k_aug.shape[2]
    return pl.pallas_call(
        functools.partial(_fox_kernel, tq=tq, tk=tk, past=past),
        grid=(bsz, n_pair, L // tq),
        in_specs=[pl.BlockSpec((1, LANES, tq), lambda b, h, i: (b, h, i)),
                  pl.BlockSpec((1, 1, 2 * FOX_AUG_ROWS, tq), lambda b, h, i: (b, h, 0, i)),
                  pl.BlockSpec((1, 1, lk, 2 * LANES), lambda b, h, i: (b, h, 0, 0)),
                  pl.BlockSpec((1, LANES, lk), lambda b, h, i: (b, h, 0))],
        out_specs=pl.BlockSpec((1, tq, LANES), lambda b, h, i: (b, i, h)),
        out_shape=jax.ShapeDtypeStruct((bsz, L, n_fox), F32),
        scratch_shapes=[pltpu.VMEM((2, 1, tq), F32), pltpu.VMEM((2, 1, tq), F32),
                        pltpu.VMEM((2, FOX_HEAD_DIM, tq), F32)],
        compiler_params=_params("parallel", "parallel", "arbitrary"),
        name="fox_attention",
    )(q_t, q_aug, k_aug, v_t)


def _fox_operands(qb, k_all, v_all, d_pieces, past, L):
    bsz, lk_pad, n_fox = k_all.shape
    n_heads = n_fox // FOX_HEAD_DIM
    n_pair = n_heads // 2
    pieces = d_pieces.astype(BF16)
    dk = (-pieces).reshape(bsz, 3, n_pair, 2, lk_pad)
    dk = dk.transpose(0, 2, 4, 3, 1).reshape(bsz, n_pair, lk_pad, 6)
    ones_k = jnp.ones((bsz, n_pair, lk_pad, 3), BF16)
    zeros_k = jnp.zeros((bsz, n_pair, lk_pad, LANES - 9), BF16)
    k_pair = k_all.reshape(bsz, lk_pad, n_pair, LANES).transpose(0, 2, 1, 3)
    k_aug = jnp.concatenate([k_pair, dk, ones_k, zeros_k], axis=-1)

    dq = pieces[:, :, :, past:past + L].transpose(0, 2, 1, 3)
    sel = jax.nn.one_hot(jnp.arange(n_heads) % 2, 2, dtype=BF16)
    ones_q = jnp.broadcast_to(jnp.repeat(sel, 3, axis=1)[None, :, :, None], (bsz, n_heads, 6, L))
    zeros_q = jnp.zeros((bsz, n_heads, FOX_AUG_ROWS - 9, L), BF16)
    q_aug = jnp.concatenate([ones_q, dq, zeros_q], axis=2).reshape(bsz, n_pair, 2 * FOX_AUG_ROWS, L)
    return qb.transpose(0, 2, 1), q_aug, k_aug, v_all.transpose(0, 2, 1)


def _route_gates(logits_t, ebias_t, n_experts):
    per_group = n_experts // N_EXPERT_GROUPS
    tokens = logits_t.shape[1]
    score = _sigmoid(logits_t)
    sel = score + ebias_t
    row = lax.broadcasted_iota(jnp.int32, (per_group, tokens), 0).astype(F32)
    neg = jnp.float32(-jnp.inf)

    def first_argmax(tile, best):
        return jnp.min(jnp.where(tile == best, row, float(per_group)), axis=0, keepdims=True)

    score_g, sel_g, gsc = [], [], []
    for g in range(N_EXPERT_GROUPS):
        sg = sel[per_group * g:per_group * (g + 1), :]
        score_g.append(score[per_group * g:per_group * (g + 1), :])
        sel_g.append(sg)
        m1 = jnp.max(sg, axis=0, keepdims=True)
        rest = jnp.where(row == first_argmax(sg, m1), neg, sg)
        gsc.append(m1 + jnp.max(rest, axis=0, keepdims=True))
    cur = []
    for g in range(N_EXPERT_GROUPS):
        ahead = jnp.zeros((1, tokens), F32)
        for o in range(N_EXPERT_GROUPS):
            if o == g:
                continue
            beats = (gsc[o] >= gsc[g]) if o < g else (gsc[o] > gsc[g])
            ahead = ahead + jnp.where(beats, 1.0, 0.0)
        cur.append(jnp.where(ahead < TOPK_GROUPS, sel_g[g], neg))
    chosen = [jnp.zeros((per_group, tokens), jnp.bool_) for _ in range(N_EXPERT_GROUPS)]
    for _ in range(TOP_K):
        best = cur[0]
        for g in range(1, N_EXPERT_GROUPS):
            best = jnp.maximum(best, cur[g])
        best = jnp.max(best, axis=0, keepdims=True)
        idx = None
        for g in range(N_EXPERT_GROUPS):
            cand = jnp.min(jnp.where(cur[g] == best, row + float(per_group * g), float(n_experts)), axis=0,
                           keepdims=True)
            idx = cand if idx is None else jnp.minimum(idx, cand)
        for g in range(N_EXPERT_GROUPS):
            hit = (row + float(per_group * g)) == idx
            chosen[g] = chosen[g] | hit
            cur[g] = jnp.where(hit, neg, cur[g])
    w = [jnp.where(chosen[g], score_g[g], 0.0) for g in range(N_EXPERT_GROUPS)]
    total = w[0]
    for g in range(1, N_EXPERT_GROUPS):
        total = total + w[g]
    total = jnp.sum(total, axis=0, keepdims=True)
    return [w[g] / total * ROUTED_SCALE for g in range(N_EXPERT_GROUPS)]


def _mid_kernel(x_ref, ys_ref, yf_ref, gfox_ref, wout_ref, gmq_ref, wmq_ref, mk_ref, mv_ref, wmo_ref, gffn_ref,
                wrt_ref, ebt_ref, ws1_ref, ws3_ref, ws2_ref, xs_ref, xn_ref, gt_ref, cnt_ref, *, n_experts):
    x = x_ref[0]
    yfn = _rms(yf_ref[0], gfox_ref[...]).astype(BF16)
    mix = jnp.concatenate([ys_ref[...], yfn], axis=1)
    x1 = x + _dot(mix, wout_ref[...])

    qm = _dot(_rms(x1, gmq_ref[...]).astype(BF16), wmq_ref[...])
    hd = qm.shape[1] // MEM_HEADS
    heads = []
    for h in range(MEM_HEADS):
        qh = (qm[:, hd * h:hd * (h + 1)] * (hd ** -0.5)).astype(BF16)
        s = lax.dot_general(qh, mk_ref[0, :, hd * h:hd * (h + 1)], _NT, preferred_element_type=F32)
        p = jnp.exp(s - jnp.max(s, axis=1, keepdims=True))
        o = _dot(p.astype(BF16), mv_ref[0, :, hd * h:hd * (h + 1)]) / jnp.sum(p, axis=1, keepdims=True)
        heads.append(o.astype(BF16))
    x2 = x1 + _dot(jnp.concatenate(heads, axis=1), wmo_ref[...])

    xn = _rms(x2, gffn_ref[...])
    xnb = xn.astype(BF16)
    hidden = _silu(_dot(xnb, ws1_ref[...])) * _dot(xnb, ws3_ref[...])
    xs_ref[0] = x2 + _dot(hidden.astype(BF16), ws2_ref[...])
    xn_ref[0] = xnb

    logits_t = lax.dot_general(wrt_ref[...], xn, _NT, preferred_element_type=F32, precision=_HIGHEST)
    gates = _route_gates(logits_t, ebt_ref[...], n_experts)
    per_group = n_experts // N_EXPERT_GROUPS
    for g in range(N_EXPERT_GROUPS):
        gt_ref[0, per_group * g:per_group * (g + 1), :] = gates[g]
        cnt_ref[0, 0, per_group * g:per_group * (g + 1), :] = jnp.sum(
            jnp.where(gates[g] != 0.0, 1.0, 0.0), axis=1, keepdims=True)


def _mid(x, ys, yf, gfox, wout, gmq, wmq, mkb, mvb, wmo, gffn, wrt, ebt, ws1, ws3, ws2, tm):
    bsz, L, d = x.shape
    n_fox = yf.shape[2]
    n_ssm = ys.shape[1] // bsz
    n_mem = mkb.shape[1]
    n_experts = wrt.shape[0]
    fs = ws1.shape[1]
    row = lambda n: pl.BlockSpec((1, tm, n), lambda b, i: (b, i, 0))
    memspec = pl.BlockSpec((1, n_mem, d), lambda b, i: (b, 0, 0))
    return pl.pallas_call(
        functools.partial(_mid_kernel, n_experts=n_experts),
        grid=(bsz, L // tm),
        in_specs=[row(d), pl.BlockSpec((tm, n_ssm), lambda b, i: (i, b)), row(n_fox), _const_spec((1, n_fox)),
                  _const_spec((d, d)), _const_spec((1, d)),
                  _const_spec((d, d)), memspec, memspec, _const_spec((d, d)), _const_spec((1, d)),
                  _const_spec((n_experts, d)), _const_spec((n_experts, 1)), _const_spec((d, fs)), _const_spec((d, fs)),
                  _const_spec((fs, d))],
        out_specs=[row(d), row(d), pl.BlockSpec((1, n_experts, tm), lambda b, i: (b, 0, i)),
                   pl.BlockSpec((1, 1, n_experts, 1), lambda b, i: (b, i, 0, 0))],
        out_shape=[jax.ShapeDtypeStruct((bsz, L, d), F32), jax.ShapeDtypeStruct((bsz, L, d), BF16),
                   jax.ShapeDtypeStruct((bsz, n_experts, L), F32),
                   jax.ShapeDtypeStruct((bsz, L // tm, n_experts, 1), F32)],
        compiler_params=_params("parallel", "parallel"),
        name="mid_block",
    )(x, ys, yf, gfox, wout, gmq, wmq, mkb, mvb, wmo, gffn, wrt, ebt, ws1, ws3, ws2)


SEG_ROWS = 16
MOE_TOKENS = 512
COPY_ROWS = 4 * SEG_ROWS
GATHER_ROWS = 1024
_TN = (((0,), (0,)), ((), ()))
_NO_RANK = -(1 << 20)


def _sorted_rows(tb, n_experts):
    rows = TOP_K * tb + n_experts * (SEG_ROWS - 1)
    return -(-rows // GATHER_ROWS) * GATHER_ROWS


def _expert_tile(m):
    return 1024 if m >= 4096 else 128


def _dispatch_plan(cnt, ns, MOE_TILE):
    nblk, n_experts = cnt.shape

    def before(a, axis):
        n = a.shape[axis]
        earlier = jnp.arange(n)[:, None] < jnp.arange(n)[None, :]
        if axis == 0:
            return jnp.sum(jnp.where(earlier[:, :, None], a[:, None, :], 0), axis=0)
        return jnp.sum(jnp.where(earlier[None, :, :], a[:, :, None], 0), axis=1)

    cp = (cnt + SEG_ROWS - 1) // SEG_ROWS * SEG_ROWS
    o_loc = before(cp, 1)
    used = jnp.sum(cp, axis=1)
    tot_e = jnp.sum(cp, axis=0)
    reg_e = (tot_e + MOE_TILE - 1) // MOE_TILE * MOE_TILE
    base_e = before(reg_e[None, :], 1)[0]
    reg_end = base_e + reg_e
    dst = base_e[None, :] + before(cp, 0)
    rows_max = nblk * ns + n_experts * MOE_TILE
    n_tiles_max = -(-rows_max // MOE_TILE)
    n_tiles = reg_end[-1] // MOE_TILE
    tile_idx = jnp.clip(jnp.arange(n_tiles_max, dtype=jnp.int32), 0, jnp.maximum(n_tiles - 1, 0))
    tile_expert = jnp.sum((reg_end[None, :] <= (tile_idx * MOE_TILE)[:, None]).astype(jnp.int32), axis=1)
    tile_expert = jnp.minimum(tile_expert, n_experts - 1)
    n_pc = ns // SEG_ROWS
    piece_row = jnp.arange(n_pc, dtype=jnp.int32) * SEG_ROWS
    seg_end = o_loc + cp
    piece_e = jnp.sum((seg_end[:, None, :] <= piece_row[None, :, None]).astype(jnp.int32), axis=2)
    live = piece_row[None, :] < used[:, None]
    piece_e = jnp.minimum(piece_e, n_experts - 1)
    own = piece_e[:, :, None] == jnp.arange(n_experts, dtype=jnp.int32)[None, None, :]
    pick = lambda table: jnp.sum(jnp.where(own, table[:, None, :], 0), axis=2)
    rank0 = piece_row[None, :] - pick(o_loc)
    piece_rank = jnp.where(live, rank0, _NO_RANK)
    i32 = lambda a: a.astype(jnp.int32).reshape(-1)
    per_block = lambda a: a.astype(jnp.int32).reshape(nblk, 1, -1)

    def copy_list(n_e, first_rank, rows_each, n_max):
        end = before(n_e, 1) + n_e
        j = jnp.arange(n_max, dtype=jnp.int32)
        e_of = jnp.minimum(jnp.sum((end[:, None, :] <= j[None, :, None]).astype(jnp.int32), axis=2), n_experts - 1)
        mine = e_of[:, :, None] == jnp.arange(n_experts, dtype=jnp.int32)[None, None, :]
        take = lambda table: jnp.sum(jnp.where(mine, table[:, None, :], 0), axis=2)
        rank = take(first_rank) + (j[None, :] - take(end - n_e)) * rows_each
        ok = j[None, :] < jnp.sum(n_e, axis=1)[:, None]
        return (per_block(jnp.where(ok, take(o_loc) + rank, 0)), per_block(jnp.where(ok, take(dst) + rank, 0)),
                i32(jnp.sum(n_e, axis=1)))

    per_copy = COPY_ROWS // SEG_ROWS
    n_wide_e = (cp // SEG_ROWS) // per_copy
    n_seg_e = cp // SEG_ROWS - n_wide_e * per_copy
    wide_src, wide_dst, n_wide = copy_list(n_wide_e, jnp.zeros_like(cp), COPY_ROWS, ns // COPY_ROWS)
    seg_src, seg_dst, n_seg = copy_list(n_seg_e, n_wide_e * COPY_ROWS, SEG_ROWS, n_experts * (per_copy - 1))
    return dict(n_piece=i32(used // SEG_ROWS), piece_e=per_block(piece_e), piece_rank=per_block(piece_rank),
                wide_src=wide_src, wide_dst=wide_dst, n_wide=n_wide, seg_src=seg_src, seg_dst=seg_dst, n_seg=n_seg,
                tail_start=i32(base_e + tot_e),
                tail_chunks=i32((reg_e - tot_e) // SEG_ROWS), tile_idx=i32(tile_idx), tile_expert=i32(tile_expert),
                n_tiles=i32(n_tiles), n_tiles_max=n_tiles_max)


def _slot_ranks(gates, tb):
    sel = gates != 0.0
    r = lax.broadcasted_iota(jnp.int32, (tb, tb), 0)
    c = lax.broadcasted_iota(jnp.int32, (tb, tb), 1)
    earlier = jnp.where(r < c, 1.0, 0.0).astype(BF16)
    rank = _dot(jnp.where(sel, 1.0, 0.0).astype(BF16), earlier)
    return jnp.where(sel, rank, -1.0)


def _build_one_hot(out_ref, rm_ref, pe_ref, pr_ref, n_pc, tb, value_row):
    rows = lax.broadcasted_iota(jnp.int32, (SEG_ROWS, tb), 0).astype(F32)

    def piece(i, c):
        e = pe_ref[0, 0, i]
        hit = rows == (rm_ref[pl.ds(e, 1), :] - pr_ref[0, 0, i].astype(F32))
        out_ref[pl.ds(pl.multiple_of(i * SEG_ROWS, SEG_ROWS), SEG_ROWS), :] = jnp.where(hit, value_row(e), 0.0).astype(BF16)
        return c

    lax.fori_loop(0, n_pc, piece, 0, unroll=8)


def _wait_rows(n_rows, make_copy):
    wide = 16 * SEG_ROWS
    n_wide = n_rows // wide

    def wide_step(i, c):
        make_copy(wide).wait()
        return c

    def seg_step(i, c):
        make_copy(SEG_ROWS).wait()
        return c

    lax.fori_loop(0, n_wide, wide_step, 0)
    lax.fori_loop(0, (n_rows - n_wide * wide) // SEG_ROWS, seg_step, 0)


def _issue_copies(n_wide, n_seg, tables, make_copy):
    wide_src, wide_dst, seg_src, seg_dst = tables

    def wide(i, c):
        make_copy(pl.multiple_of(wide_src[0, 0, i], SEG_ROWS), pl.multiple_of(wide_dst[0, 0, i], SEG_ROWS),
                  COPY_ROWS).start()
        return c

    def seg(i, c):
        make_copy(pl.multiple_of(seg_src[0, 0, i], SEG_ROWS), pl.multiple_of(seg_dst[0, 0, i], SEG_ROWS),
                  SEG_ROWS).start()
        return c

    lax.fori_loop(0, n_wide, wide, 0)
    lax.fori_loop(0, n_seg, seg, 0)


def _dispatch_kernel(n_piece_ref, n_wide_ref, n_seg_ref, tail_start_ref, tail_chunks_ref, pe_ref, pr_ref,
                     ws_ref, wd_ref, ss_ref, sd_ref,
                     xn_ref, gt_ref, xg_ref, g_ref, xsb_ref, rm_ref, z_ref, sem, *, tb, ns, n_experts, nblk):
    b = pl.program_id(0)
    n_pc = ns // SEG_ROWS
    used = n_piece_ref[b] * SEG_ROWS
    rm_ref[...] = _slot_ranks(gt_ref[...], tb)
    _build_one_hot(g_ref, rm_ref, pe_ref, pr_ref, n_pc, tb, lambda e: 1.0)

    slot = b % 2
    for t in range(ns // GATHER_ROWS):
        sl = slice(GATHER_ROWS * t, GATHER_ROWS * (t + 1))

        @pl.when(GATHER_ROWS * t < used)
        def _():
            xsb_ref[slot, sl, :] = _dot(g_ref[sl, :], xn_ref[...]).astype(BF16)

    def rows_copy(buf, src_row, dst_row, size):
        return pltpu.make_async_copy(xsb_ref.at[buf, pl.ds(src_row, size)], xg_ref.at[pl.ds(dst_row, size)],
                                     sem.at[buf])

    _issue_copies(n_wide_ref[b], n_seg_ref[b], (ws_ref, wd_ref, ss_ref, sd_ref),
                  lambda block_row, global_row, size: rows_copy(slot, block_row, global_row, size))

    @pl.when(b > 0)
    def _():
        _wait_rows(n_piece_ref[b - 1] * SEG_ROWS, lambda size: rows_copy(1 - slot, 0, 0, size))

    @pl.when(b == nblk - 1)
    def _():
        _wait_rows(used, lambda size: rows_copy(slot, 0, 0, size))
        z_ref[...] = jnp.zeros(z_ref.shape, BF16)

        wide = z_ref.shape[0]
        per_wide = wide // SEG_ROWS

        def tail_copy(row, size):
            return pltpu.make_async_copy(z_ref.at[pl.ds(0, size)], xg_ref.at[pl.ds(pl.multiple_of(row, SEG_ROWS), size)],
                                         sem.at[slot])

        def each_tail(action):
            def per_expert(e, carry):
                n_wide = tail_chunks_ref[e] // per_wide

                def wide_piece(i, c2):
                    action(tail_copy(tail_start_ref[e] + i * wide, wide))
                    return c2

                def seg_piece(i, c2):
                    action(tail_copy(tail_start_ref[e] + n_wide * wide + i * SEG_ROWS, SEG_ROWS))
                    return c2

                lax.fori_loop(0, n_wide, wide_piece, 0)
                lax.fori_loop(0, tail_chunks_ref[e] - n_wide * per_wide, seg_piece, 0)
                return carry
            lax.fori_loop(0, n_experts, per_expert, 0)

        each_tail(lambda cp: cp.start())
        each_tail(lambda cp: cp.wait())


def _expert_kernel(tile_idx_ref, tile_expert_ref, n_tiles_ref, x_ref, w1_ref, w3_ref, w2_ref, o_ref):
    @pl.when(pl.program_id(0) < n_tiles_ref[0])
    def _():
        x = x_ref[...]
        h = _silu(_dot(x, w1_ref[0].astype(BF16))) * _dot(x, w3_ref[0].astype(BF16))
        o_ref[...] = _dot(h.astype(BF16), w2_ref[0].astype(BF16)).astype(BF16)


def _combine_kernel(n_piece_ref, n_wide_ref, n_seg_ref, pe_ref, pr_ref, ws_ref, wd_ref, ss_ref, sd_ref,
                    ws_next_ref, wd_next_ref, ss_next_ref, sd_next_ref,
                    og_ref, gt_ref, xs_ref, gfin_ref, y_ref, gw_ref, ob_ref, rm_ref, sem, *, tb, ns, n_experts, nblk):
    b = pl.program_id(0)
    slot = b % 2
    n_pc = ns // SEG_ROWS

    def rows_copy(buf, src_row, dst_row, size):
        return pltpu.make_async_copy(og_ref.at[pl.ds(src_row, size)], ob_ref.at[buf, pl.ds(dst_row, size)],
                                     sem.at[buf])

    def fetch_block(blk, tables, buf):
        _issue_copies(n_wide_ref[blk], n_seg_ref[blk], tables,
                      lambda block_row, global_row, size: rows_copy(buf, global_row, block_row, size))

    @pl.when(b == 0)
    def _():
        fetch_block(b, (ws_ref, wd_ref, ss_ref, sd_ref), slot)

    @pl.when(b + 1 < nblk)
    def _():
        fetch_block(jnp.minimum(b + 1, nblk - 1), (ws_next_ref, wd_next_ref, ss_next_ref, sd_next_ref), 1 - slot)

    rm_ref[...] = _slot_ranks(gt_ref[...], tb)
    _build_one_hot(gw_ref, rm_ref, pe_ref, pr_ref, n_pc, tb, lambda e: gt_ref[pl.ds(e, 1), :])
    _wait_rows(n_piece_ref[b] * SEG_ROWS, lambda size: rows_copy(slot, 0, 0, size))

    def clear(i, c):
        ob_ref[slot, pl.ds(pl.multiple_of((n_piece_ref[b] + i) * SEG_ROWS, SEG_ROWS), SEG_ROWS), :] = jnp.zeros(
            (SEG_ROWS, ob_ref.shape[2]), BF16)
        return c

    lax.fori_loop(0, n_pc - n_piece_ref[b], clear, 0)
    y = lax.dot_general(gw_ref[...], ob_ref[slot], _TN, preferred_element_type=F32) + xs_ref[...]
    y_ref[...] = _rms(y, gfin_ref[...])


def _moe(xn, gates_t, cnt, xs, w1, w3, w2, gfin):
    m, d = xn.shape
    n_experts, _, f = w1.shape
    tb = min(m, MOE_TOKENS)
    nblk = m // tb
    ns = _sorted_rows(tb, n_experts)
    MOE_TILE = _expert_tile(m)
    plan = _dispatch_plan(cnt, ns, MOE_TILE)
    n_tiles_max = plan["n_tiles_max"]
    rows_max = n_tiles_max * MOE_TILE
    this_block = lambda t: pl.BlockSpec((1, 1, t.shape[2]), lambda b, *_: (b, 0, 0), memory_space=pltpu.SMEM)
    next_block = lambda t: pl.BlockSpec((1, 1, t.shape[2]), lambda b, *_: (jnp.minimum(b + 1, nblk - 1), 0, 0),
                                        memory_space=pltpu.SMEM)
    build_tables = (plan["piece_e"], plan["piece_rank"])
    copy_tables = (plan["wide_src"], plan["wide_dst"], plan["seg_src"], plan["seg_dst"])
    counts = (plan["n_piece"], plan["n_wide"], plan["n_seg"])

    xg = pl.pallas_call(
        functools.partial(_dispatch_kernel, tb=tb, ns=ns, n_experts=n_experts, nblk=nblk),
        grid_spec=pltpu.PrefetchScalarGridSpec(
            num_scalar_prefetch=5, grid=(nblk,),
            in_specs=[this_block(t) for t in build_tables + copy_tables] + [
                      pl.BlockSpec((tb, d), lambda b, *_: (b, 0)), pl.BlockSpec((n_experts, tb), lambda b, *_: (0, b))],
            out_specs=pl.BlockSpec(memory_space=pl.ANY),
            scratch_shapes=[pltpu.VMEM((ns, tb), BF16), pltpu.VMEM((2, ns, d), BF16), pltpu.VMEM((n_experts, tb), F32),
                            pltpu.VMEM((16 * SEG_ROWS, d), BF16), pltpu.SemaphoreType.DMA((2,))]),
        out_shape=jax.ShapeDtypeStruct((rows_max, d), BF16),
        compiler_params=_params("arbitrary"),
        name="moe_dispatch",
    )(*counts, plan["tail_start"], plan["tail_chunks"], *build_tables, *copy_tables, xn, gates_t)

    tile = lambda i, idx, ex, n: (idx[i], 0)
    og = pl.pallas_call(
        _expert_kernel,
        grid_spec=pltpu.PrefetchScalarGridSpec(
            num_scalar_prefetch=3, grid=(n_tiles_max,),
            in_specs=[pl.BlockSpec((MOE_TILE, d), tile),
                      pl.BlockSpec((1, d, f), lambda i, idx, ex, n: (ex[i], 0, 0)),
                      pl.BlockSpec((1, d, f), lambda i, idx, ex, n: (ex[i], 0, 0)),
                      pl.BlockSpec((1, f, d), lambda i, idx, ex, n: (ex[i], 0, 0))],
            out_specs=pl.BlockSpec((MOE_TILE, d), tile)),
        out_shape=jax.ShapeDtypeStruct((rows_max, d), BF16),
        compiler_params=_params("arbitrary"),
        name="moe_experts",
    )(plan["tile_idx"], plan["tile_expert"], plan["n_tiles"], xg, w1, w3, w2)

    return pl.pallas_call(
        functools.partial(_combine_kernel, tb=tb, ns=ns, n_experts=n_experts, nblk=nblk),
        grid_spec=pltpu.PrefetchScalarGridSpec(
            num_scalar_prefetch=3, grid=(nblk,),
            in_specs=[this_block(t) for t in build_tables + copy_tables] + [next_block(t) for t in copy_tables] + [
                      pl.BlockSpec(memory_space=pl.ANY), pl.BlockSpec((n_experts, tb), lambda b, *_: (0, b)),
                      pl.BlockSpec((tb, d), lambda b, *_: (b, 0)), pl.BlockSpec((1, d), lambda b, *_: (0, 0))],
            out_specs=pl.BlockSpec((tb, d), lambda b, *_: (b, 0)),
            scratch_shapes=[pltpu.VMEM((ns, tb), BF16), pltpu.VMEM((2, ns, d), BF16), pltpu.VMEM((n_experts, tb), F32),
                            pltpu.SemaphoreType.DMA((2,))]),
        out_shape=jax.ShapeDtypeStruct((m, d), F32),
        compiler_params=_params("arbitrary"),
        name="moe_combine",
    )(*counts, *build_tables, *copy_tables, *copy_tables, og, gates_t, xs, gfin)


def _tiles(L):
    t_row = min(L, 512)
    t_key = 512
    t_scan = min(L, 64)
    return t_row, t_key, t_scan


def _group(x, h0_re, h0_im, past_k, past_v, past_lf, mkb, mvb, p, w):
    bsz, L, d = x.shape
    n_heads = p["b_f"].shape[0]
    n_fox = n_heads * FOX_HEAD_DIM
    n_ssm = p["d_skip"].shape[0]
    G, P = p["lam_re"].shape
    t_row, t_key, t_scan = _tiles(L)

    u, qb, k, v, kb, vb, lf = _proj_in(x, w["g_mix"], w["w_main"], w["w_f"], w["b_f"], t_row, n_ssm, n_fox, n_heads)

    u_tb = u.reshape(L * bsz, n_ssm)
    if h0_re is None:
        h0 = jnp.zeros((bsz, 2 * G * P), F32)
    else:
        h0 = _state_to_lanes(h0_re.astype(F32), h0_im.astype(F32))
    ys_tb, h_last = _s5(u_tb, h0, w["ar"], w["ai"], w["wb"], w["wc"], w["d_skip"], w["w_glu"], w["b_glu"],
                        w["g_ssm_out"], t_scan, bsz)
    ys = ys_tb.reshape(L, bsz * n_ssm)
    hl_re, hl_im = _lanes_to_state(h_last, G, P)

    if past_k is None:
        past = 0
        lf_all, k_all, v_all = lf, kb, vb
    else:
        past = past_k.shape[1]
        lf_all = jnp.concatenate([past_lf.astype(F32), lf], axis=1)
        k_all = jnp.concatenate([past_k.reshape(bsz, past, n_fox).astype(BF16), kb], axis=1)
        v_all = jnp.concatenate([past_v.reshape(bsz, past, n_fox).astype(BF16), vb], axis=1)
    lk = lf_all.shape[1]
    lk_pad = -(-lk // t_key) * t_key
    lf_t = jnp.pad(lf_all.transpose(0, 2, 1), ((0, 0), (0, 0), (0, lk_pad - lk)))
    d_pieces = _decay_cumsum(lf_t, jnp.zeros((bsz, n_heads, 1), F32), t_key)
    k_all = jnp.pad(k_all, ((0, 0), (0, lk_pad - lk), (0, 0)))
    v_all = jnp.pad(v_all, ((0, 0), (0, lk_pad - lk), (0, 0)))
    yf = _fox(*_fox_operands(qb, k_all, v_all, d_pieces, past, L), t_row, t_key, past)

    xs, xnb, gates_t, cnt = _mid(x, ys, yf, w["g_fox_out"], w["w_out"], w["g_mem_q"], w["w_mq"], mkb, mvb, w["w_mo"],
                                 w["g_ffn"], w["w_router_t"], w["e_bias_t"], w["ws1"], w["ws3"], w["ws2"], t_row)
    m = bsz * L
    n_experts = gates_t.shape[1]
    tb = min(m, MOE_TOKENS)
    cnt = cnt.reshape(m // tb, tb // t_row, n_experts).sum(axis=1).astype(jnp.int32)
    y = _moe(xnb.reshape(m, d), gates_t.transpose(1, 0, 2).reshape(n_experts, m), cnt, xs.reshape(m, d),
             w["w1"], w["w3"], w["w2"], w["g_final"])
    return (y.reshape(bsz, L, d), hl_re, hl_im, k.reshape(bsz, L, n_heads, FOX_HEAD_DIM),
            v.reshape(bsz, L, n_heads, FOX_HEAD_DIM), lf)


def kernel(x_prompt, x_sample, state_ssm_re, state_ssm_im, cache_fox_k, cache_fox_v, cache_fox_logf, cache_mem_k, cache_mem_v, mem_prompt, g_mix, w_in, b_f, lam_re, lam_im, log_dt, b_re, b_im, c_re, c_im, d_skip, w_glu, b_glu, g_ssm_out, g_fox_out, w_out, g_mem_q, g_mem_kv, w_mq, w_mk, w_mv, w_mo, g_ffn, w_router, e_bias, w1, w3, w2, ws1, ws3, ws2, g_final):
    depth = w_in.shape[0]
    bsz = x_prompt.shape[0]
    hp, hs = x_prompt, x_sample
    outs_p, outs_s = [], []
    for l in range(depth):
        p = dict(b_f=b_f[l], lam_re=lam_re[l], d_skip=d_skip[l])
        n_heads = b_f.shape[1]
        n_ssm = d_skip.shape[1]
        n_main = w_in.shape[2] - n_heads
        row = lambda a: a.reshape(1, -1).astype(F32)
        ar, ai, wb, wc = _s5_tables(lam_re[l], lam_im[l], log_dt[l], b_re[l], b_im[l], c_re[l], c_im[l], bsz)
        w = dict(
            g_mix=row(g_mix[l]), w_main=w_in[l][:, :n_main].astype(BF16),
            w_f=jnp.pad(w_in[l][:, n_main:], ((0, 0), (0, LANES - n_heads))).astype(BF16),
            b_f=jnp.pad(b_f[l].astype(F32), (0, LANES - n_heads)).reshape(1, LANES),
            ar=ar, ai=ai, wb=wb, wc=wc, d_skip=row(d_skip[l]), w_glu=w_glu[l].astype(BF16), b_glu=row(b_glu[l]),
            g_ssm_out=row(g_ssm_out[l]), g_fox_out=row(g_fox_out[l]), w_out=w_out[l].astype(BF16),
            g_mem_q=row(g_mem_q[l]), w_mq=w_mq[l].astype(BF16), w_mo=w_mo[l].astype(BF16), g_ffn=row(g_ffn[l]),
            w_router_t=w_router[l].astype(F32).T, e_bias_t=e_bias[l].astype(F32).reshape(-1, 1),
            ws1=ws1[l].astype(BF16), ws3=ws3[l].astype(BF16), ws2=ws2[l].astype(BF16),
            w1=w1[l], w3=w3[l], w2=w2[l], g_final=row(g_final))
        assert depth == 1, "final norm fusion assumes a single layer"
        mk_p, mv_p, mkb_p, mvb_p = _mem_kv(mem_prompt, row(g_mem_kv[l]), w_mk[l].astype(BF16), w_mv[l].astype(BF16))
        nm, mh = mem_prompt.shape[1], MEM_HEADS
        hp, re_p, im_p, k_p, v_p, lf_p = _group(hp, None, None, None, None, None, mkb_p, mvb_p, p, w)
        cm_k = cache_mem_k[l].reshape(bsz, nm, -1).astype(BF16)
        cm_v = cache_mem_v[l].reshape(bsz, nm, -1).astype(BF16)
        hs, re_s, im_s, k_s, v_s, lf_s = _group(hs, state_ssm_re[l], state_ssm_im[l], cache_fox_k[l], cache_fox_v[l],
                                                cache_fox_logf[l], cm_k, cm_v, p, w)
        outs_p.append((re_p, im_p, k_p, v_p, lf_p, mk_p.reshape(bsz, nm, mh, -1), mv_p.reshape(bsz, nm, mh, -1)))
        outs_s.append((re_s, im_s, k_s, v_s, lf_s))
    stack = lambda outs, i: jnp.stack([o[i] for o in outs])
    return (hp, hs) + tuple(stack(outs_p, i) for i in range(7)) + tuple(stack(outs_s, i) for i in range(5))
```

```python
import functools
import math

import jax
import jax.numpy as jnp
from jax import lax
from jax.experimental import pallas as pl
from jax.experimental.pallas import tpu as pltpu

F32 = jnp.float32
BF16 = jnp.bfloat16

SSM_GROUP = 16
SSM_STATE = 64
FOX_HEAD_DIM = 64
MEM_HEADS = 4
TOP_K = 8
N_EXPERT_GROUPS = 8
TOPK_GROUPS = 4
ROUTED_SCALE = 2.5
RMS_EPS = 1e-6

LANES = 128
SUBLANES = 8
MXU_DIM = 256
VMEM_LIMIT_BYTES = 56 * 1024 * 1024

_HIGHEST = lax.Precision.HIGHEST
_NT = (((1,), (1,)), ((), ()))


def _params(*sem):
    return pltpu.CompilerParams(dimension_semantics=sem, vmem_limit_bytes=VMEM_LIMIT_BYTES)


def _rms(x, g):
    return x * lax.rsqrt(jnp.mean(x * x, axis=-1, keepdims=True) + RMS_EPS) * g


def _sigmoid(x):
    return 1.0 / (1.0 + jnp.exp(-x))


def _silu(x):
    return x * _sigmoid(x)


def _gelu_tanh(x):
    return x * (0.5 * (1.0 + jnp.tanh(math.sqrt(2.0 / math.pi) * (x + 0.044715 * (x * x * x)))))


def _log_sigmoid(x):
    return jnp.minimum(x, 0.0) - jnp.log1p(jnp.exp(-jnp.abs(x)))


def _dot(a, b):
    return jnp.dot(a, b, preferred_element_type=F32)


def _const_spec(shape):
    nd = len(shape)
    return pl.BlockSpec(shape, lambda *_: (0,) * nd)


def _memkv_kernel(m_ref, g_ref, wk_ref, wv_ref, k_ref, v_ref, kb_ref, vb_ref):
    mn = _rms(m_ref[0], g_ref[...]).astype(BF16)
    k = _dot(mn, wk_ref[...])
    v = _dot(mn, wv_ref[...])
    k_ref[0] = k
    v_ref[0] = v
    kb_ref[0] = k.astype(BF16)
    vb_ref[0] = v.astype(BF16)


def _mem_kv(mem, g, wk, wv):
    bsz, n, d = mem.shape
    blk = pl.BlockSpec((1, n, d), lambda b: (b, 0, 0))
    return pl.pallas_call(
        _memkv_kernel,
        grid=(bsz,),
        in_specs=[blk, _const_spec((1, d)), _const_spec((d, d)), _const_spec((d, d))],
        out_specs=[blk, blk, blk, blk],
        out_shape=[jax.ShapeDtypeStruct((bsz, n, d), F32)] * 2 + [jax.ShapeDtypeStruct((bsz, n, d), BF16)] * 2,
        compiler_params=_params("parallel"),
        name="mem_kv",
    )(mem, g, wk, wv)


def _proj_in_kernel(x_ref, g_ref, w_ref, wf_ref, bf_ref, u_ref, q_ref, k_ref, v_ref, kb_ref, vb_ref, lf_ref,
                    *, n_ssm, n_fox, n_heads):
    xb = _rms(x_ref[0], g_ref[...]).astype(BF16)
    z = _dot(xb, w_ref[...])
    u_ref[...] = z[:, :n_ssm]
    o = n_ssm
    q_ref[0] = (z[:, o:o + n_fox] * (FOX_HEAD_DIM ** -0.5 * math.log2(math.e))).astype(BF16)
    k = z[:, o + n_fox:o + 2 * n_fox]
    v = z[:, o + 2 * n_fox:o + 3 * n_fox]
    k_ref[0] = k
    v_ref[0] = v
    kb_ref[0] = k.astype(BF16)
    vb_ref[0] = v.astype(BF16)
    zf = _dot(xb, wf_ref[...])
    lf_ref[0] = _log_sigmoid(zf + bf_ref[...])[:, :n_heads]


def _proj_in(x, g, w_main, w_f, b_f, tm, n_ssm, n_fox, n_heads):
    bsz, L, d = x.shape
    nmain = w_main.shape[1]
    row = lambda n: pl.BlockSpec((1, tm, n), lambda b, i: (b, i, 0))
    u_spec = pl.BlockSpec((tm, n_ssm), lambda b, i: (i, b))
    outs = [jax.ShapeDtypeStruct((L, bsz * n_ssm), F32), jax.ShapeDtypeStruct((bsz, L, n_fox), BF16),
            jax.ShapeDtypeStruct((bsz, L, n_fox), F32), jax.ShapeDtypeStruct((bsz, L, n_fox), F32),
            jax.ShapeDtypeStruct((bsz, L, n_fox), BF16), jax.ShapeDtypeStruct((bsz, L, n_fox), BF16),
            jax.ShapeDtypeStruct((bsz, L, n_heads), F32)]
    return pl.pallas_call(
        functools.partial(_proj_in_kernel, n_ssm=n_ssm, n_fox=n_fox, n_heads=n_heads),
        grid=(bsz, L // tm),
        in_specs=[row(d), _const_spec((1, d)), _const_spec((d, nmain)), _const_spec((d, LANES)),
                  _const_spec((1, LANES))],
        out_specs=[u_spec, row(n_fox), row(n_fox), row(n_fox), row(n_fox), row(n_fox), row(n_heads)],
        out_shape=outs,
        compiler_params=_params("parallel", "parallel"),
        name="proj_in",
    )(x, g, w_main, w_f, b_f)


def _decay_kernel(lft_ref, c0_ref, dt_ref, car_ref, *, tl):
    @pl.when(pl.program_id(1) == 0)
    def _():
        car_ref[...] = c0_ref[0]
    r = lax.broadcasted_iota(jnp.int32, (tl, tl), 0)
    c = lax.broadcasted_iota(jnp.int32, (tl, tl), 1)
    tri = (r <= c).astype(F32)
    d = jnp.dot(lft_ref[0], tri, preferred_element_type=F32, precision=_HIGHEST) + car_ref[...]
    car_ref[...] = d[:, tl - 1:tl]
    rest = d * math.log2(math.e)
    for piece in range(3):
        part = rest.astype(BF16).astype(F32)
        dt_ref[0, piece] = part
        rest = rest - part


def _decay_cumsum(lft, c0, tl):
    bsz, nh, L = lft.shape
    return pl.pallas_call(
        functools.partial(_decay_kernel, tl=tl),
        grid=(bsz, L // tl),
        in_specs=[pl.BlockSpec((1, nh, tl), lambda b, i: (b, 0, i)), pl.BlockSpec((1, nh, 1), lambda b, i: (b, 0, 0))],
        out_specs=pl.BlockSpec((1, 3, nh, tl), lambda b, i: (b, 0, 0, i)),
        out_shape=jax.ShapeDtypeStruct((bsz, 3, nh, L), F32),
        scratch_shapes=[pltpu.VMEM((nh, 1), F32)],
        compiler_params=_params("parallel", "arbitrary"),
        name="decay_cumsum",
    )(lft, c0)


def _s5_kernel(u_ref, h0_ref, ar_ref, ai_ref, wb_ref, wc_ref, dsk_ref, wglu_ref, bglu_ref, gout_ref,
               y_ref, hl_ref, hs_ref, hst_ref, *, t_chunk, bsz, n_slab):
    @pl.when(pl.program_id(0) == 0)
    def _():
        hst_ref[...] = h0_ref[...]

    u = u_ref[...]
    ub = u.astype(BF16)
    slab_per_k = MXU_DIM // (2 * SSM_GROUP)
    for j in range(n_slab):
        kt = j // slab_per_k
        hs_ref[:, MXU_DIM * j:MXU_DIM * (j + 1)] = _dot(ub[:, MXU_DIM * kt:MXU_DIM * (kt + 1)], wb_ref[j])

    def step(t, h):
        r0 = pl.multiple_of(t * bsz, bsz)
        bu = hs_ref[pl.ds(r0, bsz), :]
        parts = []
        for j in range(n_slab):
            lo, mid, hi = MXU_DIM * j, MXU_DIM * j + LANES, MXU_DIM * (j + 1)
            re, im = h[:, lo:mid], h[:, mid:hi]
            ar = ar_ref[:, LANES * j:LANES * (j + 1)]
            ai = ai_ref[:, LANES * j:LANES * (j + 1)]
            parts.append(ar * re - ai * im + bu[:, lo:mid])
            parts.append(ar * im + ai * re + bu[:, mid:hi])
        hn = jnp.concatenate(parts, axis=1)
        hs_ref[pl.ds(r0, bsz), :] = hn
        return hn

    h = lax.fori_loop(0, t_chunk, step, hst_ref[...])
    hst_ref[...] = h
    hl_ref[...] = h

    halves = []
    for hf in range(n_slab // slab_per_k):
        acc = None
        for jj in range(slab_per_k):
            j = hf * slab_per_k + jj
            d = _dot(hs_ref[:, MXU_DIM * j:MXU_DIM * (j + 1)].astype(BF16), wc_ref[j])
            acc = d if acc is None else acc + d
        halves.append(acc)
    y = jnp.concatenate(halves, axis=1) + dsk_ref[...] * u
    y = _gelu_tanh(y)
    y = y * _sigmoid(_dot(y.astype(BF16), wglu_ref[...]) + bglu_ref[...])
    y_ref[...] = _rms(y, gout_ref[...]).astype(BF16)


def _s5(u_tb, h0, ar, ai, wb, wc, dsk, wglu, bglu, gout, t_chunk, bsz):
    rows, n_ssm = u_tb.shape
    n_state = h0.shape[1]
    n_slab = n_state // MXU_DIM
    r = t_chunk * bsz
    return pl.pallas_call(
        functools.partial(_s5_kernel, t_chunk=t_chunk, bsz=bsz, n_slab=n_slab),
        grid=(rows // r,),
        in_specs=[pl.BlockSpec((r, n_ssm), lambda c: (c, 0)), _const_spec((bsz, n_state)),
                  _const_spec((bsz, n_state // 2)), _const_spec((bsz, n_state // 2)),
                  _const_spec((n_slab, MXU_DIM, MXU_DIM)), _const_spec((n_slab, MXU_DIM, MXU_DIM)),
                  _const_spec((1, n_ssm)), _const_spec((n_ssm, n_ssm)), _const_spec((1, n_ssm)),
                  _const_spec((1, n_ssm))],
        out_specs=[pl.BlockSpec((r, n_ssm), lambda c: (c, 0)), _const_spec((bsz, n_state))],
        out_shape=[jax.ShapeDtypeStruct((rows, n_ssm), BF16), jax.ShapeDtypeStruct((bsz, n_state), F32)],
        scratch_shapes=[pltpu.VMEM((r, n_state), F32), pltpu.VMEM((bsz, n_state), F32)],
        compiler_params=_params("arbitrary"),
        name="s5_mixer",
    )(u_tb, h0, ar, ai, wb, wc, dsk, wglu, bglu, gout)


def _s5_tables(lam_re, lam_im, log_dt, b_re, b_im, c_re, c_im, bsz):
    G, P = lam_re.shape
    H = b_re.shape[-1]
    lr, li = lam_re.astype(F32), lam_im.astype(F32)
    dt = jnp.exp(log_dt.astype(F32))[:, None]
    mag = jnp.exp(lr * dt)
    a_re, a_im = mag * jnp.cos(li * dt), mag * jnp.sin(li * dt)
    den = lr * lr + li * li
    c_r = ((a_re - 1.0) * lr + a_im * li) / den
    c_i = (a_im * lr - (a_re - 1.0) * li) / den
    br, bi = b_re.astype(F32), b_im.astype(F32)
    bbar_re = c_r[..., None] * br - c_i[..., None] * bi
    bbar_im = c_r[..., None] * bi + c_i[..., None] * br
    n_pair = G // 2
    per_k = MXU_DIM // (2 * H)
    eye2 = jnp.eye(2, dtype=F32)
    place = jax.nn.one_hot(jnp.arange(n_pair) % per_k, per_k, dtype=F32)

    ar = jnp.broadcast_to(a_re.reshape(1, -1), (bsz, G * P))
    ai = jnp.broadcast_to(a_im.reshape(1, -1), (bsz, G * P))

    bb = jnp.stack([bbar_re, bbar_im]).reshape(2, n_pair, 2, P, H)
    wpair = jnp.einsum("ajgph,gk->jghakp", bb, eye2).reshape(n_pair, 2 * H, MXU_DIM)
    wb = jnp.einsum("jrc,jk->jkrc", wpair, place).reshape(n_pair, MXU_DIM, MXU_DIM)

    cc = jnp.stack([c_re.astype(F32), -c_im.astype(F32)]).reshape(2, n_pair, 2, H, P)
    cpair = jnp.einsum("ajghp,gk->jagpkh", cc, eye2).reshape(n_pair, MXU_DIM, 2 * H)
    wc = jnp.einsum("jnc,jk->jnkc", cpair, place).reshape(n_pair, MXU_DIM, MXU_DIM)
    return ar, ai, wb.astype(BF16), wc.astype(BF16)


def _state_to_lanes(re, im):
    bsz, G, P = re.shape
    s = jnp.stack([re, im], axis=1).reshape(bsz, 2, G // 2, 2, P)
    return s.transpose(0, 2, 1, 3, 4).reshape(bsz, 2 * G * P)


def _lanes_to_state(h, G, P):
    bsz = h.shape[0]
    s = h.reshape(bsz, G // 2, 2, 2, P).transpose(0, 2, 1, 3, 4).reshape(bsz, 2, G, P)
    return s[:, 0], s[:, 1]


FOX_AUG_ROWS = 16


def _fox_kernel(qt_ref, qa_ref, k_ref, vt_ref, o_ref, m_ref, l_ref, acc_ref, *, tq, tk, past):
    qi = pl.program_id(2)
    qt = qt_ref[0]
    row = lax.broadcasted_iota(jnp.int32, (LANES, tq), 0)
    zero = jnp.zeros_like(qt)
    pad = jnp.zeros((LANES - FOX_AUG_ROWS, tq), BF16)
    qts = []
    for hh in range(2):
        own = (row < FOX_HEAD_DIM) if hh == 0 else (row >= FOX_HEAD_DIM)
        aug = qa_ref[0, 0, FOX_AUG_ROWS * hh:FOX_AUG_ROWS * (hh + 1), :]
        qts.append(jnp.concatenate([jnp.where(own, qt, zero), aug, pad], axis=0))
    m_ref[...] = jnp.full(m_ref.shape, -1e30, F32)
    l_ref[...] = jnp.zeros(l_ref.shape, F32)
    acc_ref[...] = jnp.zeros(acc_ref.shape, F32)
    q_start = past + qi * tq
    n_full = (q_start + 1) // tk
    n_all = (q_start + tq + tk - 1) // tk

    def block(j, masked):
        ks = pl.multiple_of(j * tk, tk)
        kb = k_ref[0, 0, pl.ds(ks, tk), :]
        if masked:
            kpos = ks + lax.broadcasted_iota(jnp.int32, (tk, tq), 0)
            qpos = q_start + lax.broadcasted_iota(jnp.int32, (tk, tq), 1)
            visible = kpos <= qpos
        for hh in range(2):
            st = _dot(kb, qts[hh])
            if masked:
                st = jnp.where(visible, st, -jnp.inf)
            m_old = m_ref[hh]
            m_new = jnp.maximum(m_old, jnp.max(st, axis=0, keepdims=True))
            p = jnp.exp2(st - m_new)
            alpha = jnp.exp2(m_old - m_new)
            l_ref[hh] = alpha * l_ref[hh] + jnp.sum(p, axis=0, keepdims=True)
            vt = vt_ref[0, FOX_HEAD_DIM * hh:FOX_HEAD_DIM * (hh + 1), pl.ds(ks, tk)]
            acc_ref[hh] = alpha * acc_ref[hh] + _dot(vt, p.astype(BF16))
            m_ref[hh] = m_new

    def full_body(j, c):
        block(j, False)
        return c

    def masked_body(j, c):
        block(j, True)
        return c

    lax.fori_loop(0, n_full, full_body, 0)
    lax.fori_loop(n_full, n_all, masked_body, 0)
    out_t = jnp.concatenate([acc_ref[0] / l_ref[0], acc_ref[1] / l_ref[1]], axis=0)
    o_ref[0] = out_t.T


def _fox(q_t, q_aug, k_aug, v_t, tq, tk, past):
    bsz, n_fox, L = q_t.shape
    n_pair = n_fox // LANES
    lk = k_aug.shape[2]
    return pl.pallas_call(
        functools.partial(_fox_kernel, tq=tq, tk=tk, past=past),
        grid=(bsz, n_pair, L // tq),
        in_specs=[pl.BlockSpec((1, LANES, tq), lambda b, h, i: (b, h, i)),
                  pl.BlockSpec((1, 1, 2 * FOX_AUG_ROWS, tq), lambda b, h, i: (b, h, 0, i)),
                  pl.BlockSpec((1, 1, lk, 2 * LANES), lambda b, h, i: (b, h, 0, 0)),
                  pl.BlockSpec((1, LANES, lk), lambda b, h, i: (b, h, 0))],
        out_specs=pl.BlockSpec((1, tq, LANES), lambda b, h, i: (b, i, h)),
        out_shape=jax.ShapeDtypeStruct((bsz, L, n_fox), F32),
        scratch_shapes=[pltpu.VMEM((2, 1, tq), F32), pltpu.VMEM((2, 1, tq), F32),
                        pltpu.VMEM((2, FOX_HEAD_DIM, tq), F32)],
        compiler_params=_params("parallel", "parallel", "arbitrary"),
        name="fox_attention",
    )(q_t, q_aug, k_aug, v_t)


def _fox_operands(qb, k_all, v_all, d_pieces, past, L):
    bsz, lk_pad, n_fox = k_all.shape
    n_heads = n_fox // FOX_HEAD_DIM
    n_pair = n_heads // 2
    pieces = d_pieces.astype(BF16)
    dk = (-pieces).reshape(bsz, 3, n_pair, 2, lk_pad)
    dk = dk.transpose(0, 2, 4, 3, 1).reshape(bsz, n_pair, lk_pad, 6)
    ones_k = jnp.ones((bsz, n_pair, lk_pad, 3), BF16)
    zeros_k = jnp.zeros((bsz, n_pair, lk_pad, LANES - 9), BF16)
    k_pair = k_all.reshape(bsz, lk_pad, n_pair, LANES).transpose(0, 2, 1, 3)
    k_aug = jnp.concatenate([k_pair, dk, ones_k, zeros_k], axis=-1)

    dq = pieces[:, :, :, past:past + L].transpose(0, 2, 1, 3)
    sel = jax.nn.one_hot(jnp.arange(n_heads) % 2, 2, dtype=BF16)
    ones_q = jnp.broadcast_to(jnp.repeat(sel, 3, axis=1)[None, :, :, None], (bsz, n_heads, 6, L))
    zeros_q = jnp.zeros((bsz, n_heads, FOX_AUG_ROWS - 9, L), BF16)
    q_aug = jnp.concatenate([ones_q, dq, zeros_q], axis=2).reshape(bsz, n_pair, 2 * FOX_AUG_ROWS, L)
    return qb.transpose(0, 2, 1), q_aug, k_aug, v_all.transpose(0, 2, 1)


def _route_gates(logits_t, ebias_t, n_experts):
    per_group = n_experts // N_EXPERT_GROUPS
    tokens = logits_t.shape[1]
    score = _sigmoid(logits_t)
    sel = score + ebias_t
    row = lax.broadcasted_iota(jnp.int32, (per_group, tokens), 0).astype(F32)
    neg = jnp.float32(-jnp.inf)

    def first_argmax(tile, best):
        return jnp.min(jnp.where(tile == best, row, float(per_group)), axis=0, keepdims=True)

    score_g, sel_g, gsc = [], [], []
    for g in range(N_EXPERT_GROUPS):
        sg = sel[per_group * g:per_group * (g + 1), :]
        score_g.append(score[per_group * g:per_group * (g + 1), :])
        sel_g.append(sg)
        m1 = jnp.max(sg, axis=0, keepdims=True)
        rest = jnp.where(row == first_argmax(sg, m1), neg, sg)
        gsc.append(m1 + jnp.max(rest, axis=0, keepdims=True))
    cur = []
    for g in range(N_EXPERT_GROUPS):
        ahead = jnp.zeros((1, tokens), F32)
        for o in range(N_EXPERT_GROUPS):
            if o == g:
                continue
            beats = (gsc[o] >= gsc[g]) if o < g else (gsc[o] > gsc[g])
            ahead = ahead + jnp.where(beats, 1.0, 0.0)
        cur.append(jnp.where(ahead < TOPK_GROUPS, sel_g[g], neg))
    chosen = [jnp.zeros((per_group, tokens), jnp.bool_) for _ in range(N_EXPERT_GROUPS)]
    for _ in range(TOP_K):
        best = cur[0]
        for g in range(1, N_EXPERT_GROUPS):
            best = jnp.maximum(best, cur[g])
        best = jnp.max(best, axis=0, keepdims=True)
        idx = None
        for g in range(N_EXPERT_GROUPS):
            cand = jnp.min(jnp.where(cur[g] == best, row + float(per_group * g), float(n_experts)), axis=0,
                           keepdims=True)
            idx = cand if idx is None else jnp.minimum(idx, cand)
        for g in range(N_EXPERT_GROUPS):
            hit = (row + float(per_group * g)) == idx
            chosen[g] = chosen[g] | hit
            cur[g] = jnp.where(hit, neg, cur[g])
    w = [jnp.where(chosen[g], score_g[g], 0.0) for g in range(N_EXPERT_GROUPS)]
    total = w[0]
    for g in range(1, N_EXPERT_GROUPS):
        total = total + w[g]
    total = jnp.sum(total, axis=0, keepdims=True)
    return [w[g] / total * ROUTED_SCALE for g in range(N_EXPERT_GROUPS)]


def _mid_kernel(x_ref, ys_ref, yf_ref, gfox_ref, wout_ref, gmq_ref, wmq_ref, mk_ref, mv_ref, wmo_ref, gffn_ref,
                wrt_ref, ebt_ref, ws1_ref, ws3_ref, ws2_ref, xs_ref, xn_ref, gt_ref, cnt_ref, *, n_experts):
    x = x_ref[0]
    yfn = _rms(yf_ref[0], gfox_ref[...]).astype(BF16)
    mix = jnp.concatenate([ys_ref[...], yfn], axis=1)
    x1 = x + _dot(mix, wout_ref[...])

    qm = _dot(_rms(x1, gmq_ref[...]).astype(BF16), wmq_ref[...])
    hd = qm.shape[1] // MEM_HEADS
    heads = []
    for h in range(MEM_HEADS):
        qh = (qm[:, hd * h:hd * (h + 1)] * (hd ** -0.5)).astype(BF16)
        s = lax.dot_general(qh, mk_ref[0, :, hd * h:hd * (h + 1)], _NT, preferred_element_type=F32)
        p = jnp.exp(s - jnp.max(s, axis=1, keepdims=True))
        o = _dot(p.astype(BF16), mv_ref[0, :, hd * h:hd * (h + 1)]) / jnp.sum(p, axis=1, keepdims=True)
        heads.append(o.astype(BF16))
    x2 = x1 + _dot(jnp.concatenate(heads, axis=1), wmo_ref[...])

    xn = _rms(x2, gffn_ref[...])
    xnb = xn.astype(BF16)
    hidden = _silu(_dot(xnb, ws1_ref[...])) * _dot(xnb, ws3_ref[...])
    xs_ref[0] = x2 + _dot(hidden.astype(BF16), ws2_ref[...])
    xn_ref[0] = xnb

    logits_t = lax.dot_general(wrt_ref[...], xn, _NT, preferred_element_type=F32, precision=_HIGHEST)
    gates = _route_gates(logits_t, ebt_ref[...], n_experts)
    per_group = n_experts // N_EXPERT_GROUPS
    for g in range(N_EXPERT_GROUPS):
        gt_ref[0, per_group * g:per_group * (g + 1), :] = gates[g]
        cnt_ref[0, 0, per_group * g:per_group * (g + 1), :] = jnp.sum(
            jnp.where(gates[g] != 0.0, 1.0, 0.0), axis=1, keepdims=True)


def _mid(x, ys, yf, gfox, wout, gmq, wmq, mkb, mvb, wmo, gffn, wrt, ebt, ws1, ws3, ws2, tm):
    bsz, L, d = x.shape
    n_fox = yf.shape[2]
    n_ssm = ys.shape[1] // bsz
    n_mem = mkb.shape[1]
    n_experts = wrt.shape[0]
    fs = ws1.shape[1]
    row = lambda n: pl.BlockSpec((1, tm, n), lambda b, i: (b, i, 0))
    memspec = pl.BlockSpec((1, n_mem, d), lambda b, i: (b, 0, 0))
    return pl.pallas_call(
        functools.partial(_mid_kernel, n_experts=n_experts),
        grid=(bsz, L // tm),
        in_specs=[row(d), pl.BlockSpec((tm, n_ssm), lambda b, i: (i, b)), row(n_fox), _const_spec((1, n_fox)),
                  _const_spec((d, d)), _const_spec((1, d)),
                  _const_spec((d, d)), memspec, memspec, _const_spec((d, d)), _const_spec((1, d)),
                  _const_spec((n_experts, d)), _const_spec((n_experts, 1)), _const_spec((d, fs)), _const_spec((d, fs)),
                  _const_spec((fs, d))],
        out_specs=[row(d), row(d), pl.BlockSpec((1, n_experts, tm), lambda b, i: (b, 0, i)),
                   pl.BlockSpec((1, 1, n_experts, 1), lambda b, i: (b, i, 0, 0))],
        out_shape=[jax.ShapeDtypeStruct((bsz, L, d), F32), jax.ShapeDtypeStruct((bsz, L, d), BF16),
                   jax.ShapeDtypeStruct((bsz, n_experts, L), F32),
                   jax.ShapeDtypeStruct((bsz, L // tm, n_experts, 1), F32)],
        compiler_params=_params("parallel", "parallel"),
        name="mid_block",
    )(x, ys, yf, gfox, wout, gmq, wmq, mkb, mvb, wmo, gffn, wrt, ebt, ws1, ws3, ws2)


SEG_ROWS = 16
MOE_TOKENS = 512
COPY_ROWS = 4 * SEG_ROWS
GATHER_ROWS = 1024
_TN = (((0,), (0,)), ((), ()))
_NO_RANK = -(1 << 20)


def _sorted_rows(tb, n_experts):
    rows = TOP_K * tb + n_experts * (SEG_ROWS - 1)
    return -(-rows // GATHER_ROWS) * GATHER_ROWS


def _expert_tile(m):
    return 1024 if m >= 4096 else 128


def _dispatch_plan(cnt, ns, MOE_TILE):
    nblk, n_experts = cnt.shape

    def before(a, axis):
        n = a.shape[axis]
        earlier = jnp.arange(n)[:, None] < jnp.arange(n)[None, :]
        if axis == 0:
            return jnp.sum(jnp.where(earlier[:, :, None], a[:, None, :], 0), axis=0)
        return jnp.sum(jnp.where(earlier[None, :, :], a[:, :, None], 0), axis=1)

    cp = (cnt + SEG_ROWS - 1) // SEG_ROWS * SEG_ROWS
    o_loc = before(cp, 1)
    used = jnp.sum(cp, axis=1)
    tot_e = jnp.sum(cp, axis=0)
    reg_e = (tot_e + MOE_TILE - 1) // MOE_TILE * MOE_TILE
    base_e = before(reg_e[None, :], 1)[0]
    reg_end = base_e + reg_e
    dst = base_e[None, :] + before(cp, 0)
    rows_max = nblk * ns + n_experts * MOE_TILE
    n_tiles_max = -(-rows_max // MOE_TILE)
    n_tiles = reg_end[-1] // MOE_TILE
    tile_idx = jnp.clip(jnp.arange(n_tiles_max, dtype=jnp.int32), 0, jnp.maximum(n_tiles - 1, 0))
    tile_expert = jnp.sum((reg_end[None, :] <= (tile_idx * MOE_TILE)[:, None]).astype(jnp.int32), axis=1)
    tile_expert = jnp.minimum(tile_expert, n_experts - 1)
    n_pc = ns // SEG_ROWS
    piece_row = jnp.arange(n_pc, dtype=jnp.int32) * SEG_ROWS
    seg_end = o_loc + cp
    piece_e = jnp.sum((seg_end[:, None, :] <= piece_row[None, :, None]).astype(jnp.int32), axis=2)
    live = piece_row[None, :] < used[:, None]
    piece_e = jnp.minimum(piece_e, n_experts - 1)
    own = piece_e[:, :, None] == jnp.arange(n_experts, dtype=jnp.int32)[None, None, :]
    pick = lambda table: jnp.sum(jnp.where(own, table[:, None, :], 0), axis=2)
    rank0 = piece_row[None, :] - pick(o_loc)
    piece_rank = jnp.where(live, rank0, _NO_RANK)
    i32 = lambda a: a.astype(jnp.int32).reshape(-1)
    per_block = lambda a: a.astype(jnp.int32).reshape(nblk, 1, -1)

    def copy_list(n_e, first_rank, rows_each, n_max):
        end = before(n_e, 1) + n_e
        j = jnp.arange(n_max, dtype=jnp.int32)
        e_of = jnp.minimum(jnp.sum((end[:, None, :] <= j[None, :, None]).astype(jnp.int32), axis=2), n_experts - 1)
        mine = e_of[:, :, None] == jnp.arange(n_experts, dtype=jnp.int32)[None, None, :]
        take = lambda table: jnp.sum(jnp.where(mine, table[:, None, :], 0), axis=2)
        rank = take(first_rank) + (j[None, :] - take(end - n_e)) * rows_each
        ok = j[None, :] < jnp.sum(n_e, axis=1)[:, None]
        return (per_block(jnp.where(ok, take(o_loc) + rank, 0)), per_block(jnp.where(ok, take(dst) + rank, 0)),
                i32(jnp.sum(n_e, axis=1)))

    per_copy = COPY_ROWS // SEG_ROWS
    n_wide_e = (cp // SEG_ROWS) // per_copy
    n_seg_e = cp // SEG_ROWS - n_wide_e * per_copy
    wide_src, wide_dst, n_wide = copy_list(n_wide_e, jnp.zeros_like(cp), COPY_ROWS, ns // COPY_ROWS)
    seg_src, seg_dst, n_seg = copy_list(n_seg_e, n_wide_e * COPY_ROWS, SEG_ROWS, n_experts * (per_copy - 1))
    return dict(n_piece=i32(used // SEG_ROWS), piece_e=per_block(piece_e), piece_rank=per_block(piece_rank),
                wide_src=wide_src, wide_dst=wide_dst, n_wide=n_wide, seg_src=seg_src, seg_dst=seg_dst, n_seg=n_seg,
                tail_start=i32(base_e + tot_e),
                tail_chunks=i32((reg_e - tot_e) // SEG_ROWS), tile_idx=i32(tile_idx), tile_expert=i32(tile_expert),
                n_tiles=i32(n_tiles), n_tiles_max=n_tiles_max)


def _slot_ranks(gates, tb):
    sel = gates != 0.0
    r = lax.broadcasted_iota(jnp.int32, (tb, tb), 0)
    c = lax.broadcasted_iota(jnp.int32, (tb, tb), 1)
    earlier = jnp.where(r < c, 1.0, 0.0).astype(BF16)
    rank = _dot(jnp.where(sel, 1.0, 0.0).astype(BF16), earlier)
    return jnp.where(sel, rank, -1.0)


def _build_one_hot(out_ref, rm_ref, pe_ref, pr_ref, first, count, tb, value_row):
    rows = lax.broadcasted_iota(jnp.int32, (SEG_ROWS, tb), 0).astype(F32)
    for i in range(first, first + count):
        e = pe_ref[0, 0, i]
        hit = rows == (rm_ref[pl.ds(e, 1), :] - pr_ref[0, 0, i].astype(F32))
        out_ref[SEG_ROWS * i:SEG_ROWS * (i + 1), :] = jnp.where(hit, value_row(e), 0.0).astype(BF16)


def _wait_rows(n_rows, make_copy):
    wide = 16 * SEG_ROWS
    n_wide = n_rows // wide

    def wide_step(i, c):
        make_copy(wide).wait()
        return c

    def seg_step(i, c):
        make_copy(SEG_ROWS).wait()
        return c

    lax.fori_loop(0, n_wide, wide_step, 0)
    lax.fori_loop(0, (n_rows - n_wide * wide) // SEG_ROWS, seg_step, 0)


def _issue_copies(n_wide, n_seg, tables, make_copy):
    wide_src, wide_dst, seg_src, seg_dst = tables

    def wide(i, c):
        make_copy(pl.multiple_of(wide_src[0, 0, i], SEG_ROWS), pl.multiple_of(wide_dst[0, 0, i], SEG_ROWS),
                  COPY_ROWS).start()
        return c

    def seg(i, c):
        make_copy(pl.multiple_of(seg_src[0, 0, i], SEG_ROWS), pl.multiple_of(seg_dst[0, 0, i], SEG_ROWS),
                  SEG_ROWS).start()
        return c

    lax.fori_loop(0, n_wide, wide, 0)
    lax.fori_loop(0, n_seg, seg, 0)


def _dispatch_kernel(n_piece_ref, n_wide_ref, n_seg_ref, tail_start_ref, tail_chunks_ref, pe_ref, pr_ref,
                     ws_ref, wd_ref, ss_ref, sd_ref,
                     xn_ref, gt_ref, xg_ref, g_ref, xsb_ref, rm_ref, z_ref, sem, *, tb, ns, n_experts, nblk):
    b = pl.program_id(0)
    n_pc = ns // SEG_ROWS
    used = n_piece_ref[b] * SEG_ROWS
    rm_ref[...] = _slot_ranks(gt_ref[...], tb)
    per_slice = GATHER_ROWS // SEG_ROWS
    n_slices = ns // GATHER_ROWS
    build = lambda t: _build_one_hot(g_ref, rm_ref, pe_ref, pr_ref, per_slice * t, per_slice, tb, lambda e: 1.0)

    slot = b % 2
    build(0)
    for t in range(n_slices):
        if t + 1 < n_slices:
            build(t + 1)
        sl = slice(GATHER_ROWS * t, GATHER_ROWS * (t + 1))
        xsb_ref[slot, sl, :] = _dot(g_ref[sl, :], xn_ref[...]).astype(BF16)

    def rows_copy(buf, src_row, dst_row, size):
        return pltpu.make_async_copy(xsb_ref.at[buf, pl.ds(src_row, size)], xg_ref.at[pl.ds(dst_row, size)],
                                     sem.at[buf])

    _issue_copies(n_wide_ref[b], n_seg_ref[b], (ws_ref, wd_ref, ss_ref, sd_ref),
                  lambda block_row, global_row, size: rows_copy(slot, block_row, global_row, size))

    @pl.when(b > 0)
    def _():
        _wait_rows(n_piece_ref[b - 1] * SEG_ROWS, lambda size: rows_copy(1 - slot, 0, 0, size))

    @pl.when(b == nblk - 1)
    def _():
        _wait_rows(used, lambda size: rows_copy(slot, 0, 0, size))
        z_ref[...] = jnp.zeros(z_ref.shape, BF16)

        wide = z_ref.shape[0]
        per_wide = wide // SEG_ROWS

        def tail_copy(row, size):
            return pltpu.make_async_copy(z_ref.at[pl.ds(0, size)], xg_ref.at[pl.ds(pl.multiple_of(row, SEG_ROWS), size)],
                                         sem.at[slot])

        def each_tail(action):
            def per_expert(e, carry):
                n_wide = tail_chunks_ref[e] // per_wide

                def wide_piece(i, c2):
                    action(tail_copy(tail_start_ref[e] + i * wide, wide))
                    return c2

                def seg_piece(i, c2):
                    action(tail_copy(tail_start_ref[e] + n_wide * wide + i * SEG_ROWS, SEG_ROWS))
                    return c2

                lax.fori_loop(0, n_wide, wide_piece, 0)
                lax.fori_loop(0, tail_chunks_ref[e] - n_wide * per_wide, seg_piece, 0)
                return carry
            lax.fori_loop(0, n_experts, per_expert, 0)

        each_tail(lambda cp: cp.start())
        each_tail(lambda cp: cp.wait())


def _expert_kernel(tile_idx_ref, tile_expert_ref, n_tiles_ref, x_ref, w1_ref, w3_ref, w2_ref, o_ref):
    @pl.when(pl.program_id(0) < n_tiles_ref[0])
    def _():
        x = x_ref[...]
        h = _silu(_dot(x, w1_ref[0].astype(BF16))) * _dot(x, w3_ref[0].astype(BF16))
        o_ref[...] = _dot(h.astype(BF16), w2_ref[0].astype(BF16)).astype(BF16)


def _combine_kernel(n_piece_ref, n_wide_ref, n_seg_ref, pe_ref, pr_ref, ws_ref, wd_ref, ss_ref, sd_ref,
                    ws_next_ref, wd_next_ref, ss_next_ref, sd_next_ref,
                    og_ref, gt_ref, xs_ref, gfin_ref, y_ref, gw_ref, ob_ref, rm_ref, sem, *, tb, ns, n_experts, nblk):
    b = pl.program_id(0)
    slot = b % 2
    n_pc = ns // SEG_ROWS

    def rows_copy(buf, src_row, dst_row, size):
        return pltpu.make_async_copy(og_ref.at[pl.ds(src_row, size)], ob_ref.at[buf, pl.ds(dst_row, size)],
                                     sem.at[buf])

    def fetch_block(blk, tables, buf):
        _issue_copies(n_wide_ref[blk], n_seg_ref[blk], tables,
                      lambda block_row, global_row, size: rows_copy(buf, global_row, block_row, size))

    @pl.when(b == 0)
    def _():
        fetch_block(b, (ws_ref, wd_ref, ss_ref, sd_ref), slot)

    @pl.when(b + 1 < nblk)
    def _():
        fetch_block(jnp.minimum(b + 1, nblk - 1), (ws_next_ref, wd_next_ref, ss_next_ref, sd_next_ref), 1 - slot)

    rm_ref[...] = _slot_ranks(gt_ref[...], tb)
    _wait_rows(n_piece_ref[b] * SEG_ROWS, lambda size: rows_copy(slot, 0, 0, size))

    def clear(i, c):
        ob_ref[slot, pl.ds(pl.multiple_of((n_piece_ref[b] + i) * SEG_ROWS, SEG_ROWS), SEG_ROWS), :] = jnp.zeros(
            (SEG_ROWS, ob_ref.shape[2]), BF16)
        return c

    lax.fori_loop(0, n_pc - n_piece_ref[b], clear, 0)

    per_slice = GATHER_ROWS // SEG_ROWS
    n_slices = ns // GATHER_ROWS
    build = lambda t: _build_one_hot(gw_ref, rm_ref, pe_ref, pr_ref, per_slice * t, per_slice, tb,
                                     lambda e: gt_ref[pl.ds(e, 1), :])
    build(0)
    y = xs_ref[...]
    for t in range(n_slices):
        if t + 1 < n_slices:
            build(t + 1)
        sl = slice(GATHER_ROWS * t, GATHER_ROWS * (t + 1))
        y = y + lax.dot_general(gw_ref[sl, :], ob_ref[slot, sl, :], _TN, preferred_element_type=F32)
    y_ref[...] = _rms(y, gfin_ref[...])


def _moe(xn, gates_t, cnt, xs, w1, w3, w2, gfin):
    m, d = xn.shape
    n_experts, _, f = w1.shape
    tb = min(m, MOE_TOKENS)
    nblk = m // tb
    ns = _sorted_rows(tb, n_experts)
    MOE_TILE = _expert_tile(m)
    plan = _dispatch_plan(cnt, ns, MOE_TILE)
    n_tiles_max = plan["n_tiles_max"]
    rows_max = n_tiles_max * MOE_TILE
    this_block = lambda t: pl.BlockSpec((1, 1, t.shape[2]), lambda b, *_: (b, 0, 0), memory_space=pltpu.SMEM)
    next_block = lambda t: pl.BlockSpec((1, 1, t.shape[2]), lambda b, *_: (jnp.minimum(b + 1, nblk - 1), 0, 0),
                                        memory_space=pltpu.SMEM)
    build_tables = (plan["piece_e"], plan["piece_rank"])
    copy_tables = (plan["wide_src"], plan["wide_dst"], plan["seg_src"], plan["seg_dst"])
    counts = (plan["n_piece"], plan["n_wide"], plan["n_seg"])

    xg = pl.pallas_call(
        functools.partial(_dispatch_kernel, tb=tb, ns=ns, n_experts=n_experts, nblk=nblk),
        grid_spec=pltpu.PrefetchScalarGridSpec(
            num_scalar_prefetch=5, grid=(nblk,),
            in_specs=[this_block(t) for t in build_tables + copy_tables] + [
                      pl.BlockSpec((tb, d), lambda b, *_: (b, 0)), pl.BlockSpec((n_experts, tb), lambda b, *_: (0, b))],
            out_specs=pl.BlockSpec(memory_space=pl.ANY),
            scratch_shapes=[pltpu.VMEM((ns, tb), BF16), pltpu.VMEM((2, ns, d), BF16), pltpu.VMEM((n_experts, tb), F32),
                            pltpu.VMEM((16 * SEG_ROWS, d), BF16), pltpu.SemaphoreType.DMA((2,))]),
        out_shape=jax.ShapeDtypeStruct((rows_max, d), BF16),
        compiler_params=_params("arbitrary"),
        name="moe_dispatch",
    )(*counts, plan["tail_start"], plan["tail_chunks"], *build_tables, *copy_tables, xn, gates_t)

    tile = lambda i, idx, ex, n: (idx[i], 0)
    og = pl.pallas_call(
        _expert_kernel,
        grid_spec=pltpu.PrefetchScalarGridSpec(
            num_scalar_prefetch=3, grid=(n_tiles_max,),
            in_specs=[pl.BlockSpec((MOE_TILE, d), tile),
                      pl.BlockSpec((1, d, f), lambda i, idx, ex, n: (ex[i], 0, 0)),
                      pl.BlockSpec((1, d, f), lambda i, idx, ex, n: (ex[i], 0, 0)),
                      pl.BlockSpec((1, f, d), lambda i, idx, ex, n: (ex[i], 0, 0))],
            out_specs=pl.BlockSpec((MOE_TILE, d), tile)),
        out_shape=jax.ShapeDtypeStruct((rows_max, d), BF16),
        compiler_params=_params("arbitrary"),
        name="moe_experts",
    )(plan["tile_idx"], plan["tile_expert"], plan["n_tiles"], xg, w1, w3, w2)

    return pl.pallas_call(
        functools.partial(_combine_kernel, tb=tb, ns=ns, n_experts=n_experts, nblk=nblk),
        grid_spec=pltpu.PrefetchScalarGridSpec(
            num_scalar_prefetch=3, grid=(nblk,),
            in_specs=[this_block(t) for t in build_tables + copy_tables] + [next_block(t) for t in copy_tables] + [
                      pl.BlockSpec(memory_space=pl.ANY), pl.BlockSpec((n_experts, tb), lambda b, *_: (0, b)),
                      pl.BlockSpec((tb, d), lambda b, *_: (b, 0)), pl.BlockSpec((1, d), lambda b, *_: (0, 0))],
            out_specs=pl.BlockSpec((tb, d), lambda b, *_: (b, 0)),
            scratch_shapes=[pltpu.VMEM((ns, tb), BF16), pltpu.VMEM((2, ns, d), BF16), pltpu.VMEM((n_experts, tb), F32),
                            pltpu.SemaphoreType.DMA((2,))]),
        out_shape=jax.ShapeDtypeStruct((m, d), F32),
        compiler_params=_params("arbitrary"),
        name="moe_combine",
    )(*counts, *build_tables, *copy_tables, *copy_tables, og, gates_t, xs, gfin)


def _tiles(L):
    t_row = min(L, 512)
    t_key = 512
    t_scan = min(L, 64)
    return t_row, t_key, t_scan


def _group(x, h0_re, h0_im, past_k, past_v, past_lf, mkb, mvb, p, w):
    bsz, L, d = x.shape
    n_heads = p["b_f"].shape[0]
    n_fox = n_heads * FOX_HEAD_DIM
    n_ssm = p["d_skip"].shape[0]
    G, P = p["lam_re"].shape
    t_row, t_key, t_scan = _tiles(L)

    u, qb, k, v, kb, vb, lf = _proj_in(x, w["g_mix"], w["w_main"], w["w_f"], w["b_f"], t_row, n_ssm, n_fox, n_heads)

    u_tb = u.reshape(L * bsz, n_ssm)
    if h0_re is None:
        h0 = jnp.zeros((bsz, 2 * G * P), F32)
    else:
        h0 = _state_to_lanes(h0_re.astype(F32), h0_im.astype(F32))
    ys_tb, h_last = _s5(u_tb, h0, w["ar"], w["ai"], w["wb"], w["wc"], w["d_skip"], w["w_glu"], w["b_glu"],
                        w["g_ssm_out"], t_scan, bsz)
    ys = ys_tb.reshape(L, bsz * n_ssm)
    hl_re, hl_im = _lanes_to_state(h_last, G, P)

    if past_k is None:
        past = 0
        lf_all, k_all, v_all = lf, kb, vb
    else:
        past = past_k.shape[1]
        lf_all = jnp.concatenate([past_lf.astype(F32), lf], axis=1)
        k_all = jnp.concatenate([past_k.reshape(bsz, past, n_fox).astype(BF16), kb], axis=1)
        v_all = jnp.concatenate([past_v.reshape(bsz, past, n_fox).astype(BF16), vb], axis=1)
    lk = lf_all.shape[1]
    lk_pad = -(-lk // t_key) * t_key
    lf_t = jnp.pad(lf_all.transpose(0, 2, 1), ((0, 0), (0, 0), (0, lk_pad - lk)))
    d_pieces = _decay_cumsum(lf_t.reshape(1, bsz * n_heads, lk_pad), jnp.zeros((1, bsz * n_heads, 1), F32), t_key)
    d_pieces = d_pieces.reshape(3, bsz, n_heads, lk_pad).transpose(1, 0, 2, 3)
    k_all = jnp.pad(k_all, ((0, 0), (0, lk_pad - lk), (0, 0)))
    v_all = jnp.pad(v_all, ((0, 0), (0, lk_pad - lk), (0, 0)))
    yf = _fox(*_fox_operands(qb, k_all, v_all, d_pieces, past, L), t_row, t_key, past)

    xs, xnb, gates_t, cnt = _mid(x, ys, yf, w["g_fox_out"], w["w_out"], w["g_mem_q"], w["w_mq"], mkb, mvb, w["w_mo"],
                                 w["g_ffn"], w["w_router_t"], w["e_bias_t"], w["ws1"], w["ws3"], w["ws2"], t_row)
    m = bsz * L
    n_experts = gates_t.shape[1]
    tb = min(m, MOE_TOKENS)
    cnt = cnt.reshape(m // tb, tb // t_row, n_experts).sum(axis=1).astype(jnp.int32)
    y = _moe(xnb.reshape(m, d), gates_t.transpose(1, 0, 2).reshape(n_experts, m), cnt, xs.reshape(m, d),
             w["w1"], w["w3"], w["w2"], w["g_final"])
    return (y.reshape(bsz, L, d), hl_re, hl_im, k.reshape(bsz, L, n_heads, FOX_HEAD_DIM),
            v.reshape(bsz, L, n_heads, FOX_HEAD_DIM), lf)


def kernel(x_prompt, x_sample, state_ssm_re, state_ssm_im, cache_fox_k, cache_fox_v, cache_fox_logf, cache_mem_k, cache_mem_v, mem_prompt, g_mix, w_in, b_f, lam_re, lam_im, log_dt, b_re, b_im, c_re, c_im, d_skip, w_glu, b_glu, g_ssm_out, g_fox_out, w_out, g_mem_q, g_mem_kv, w_mq, w_mk, w_mv, w_mo, g_ffn, w_router, e_bias, w1, w3, w2, ws1, ws3, ws2, g_final):
    depth = w_in.shape[0]
    bsz = x_prompt.shape[0]
    hp, hs = x_prompt, x_sample
    outs_p, outs_s = [], []
    for l in range(depth):
        p = dict(b_f=b_f[l], lam_re=lam_re[l], d_skip=d_skip[l])
        n_heads = b_f.shape[1]
        n_ssm = d_skip.shape[1]
        n_main = w_in.shape[2] - n_heads
        row = lambda a: a.reshape(1, -1).astype(F32)
        ar, ai, wb, wc = _s5_tables(lam_re[l], lam_im[l], log_dt[l], b_re[l], b_im[l], c_re[l], c_im[l], bsz)
        w = dict(
            g_mix=row(g_mix[l]), w_main=w_in[l][:, :n_main].astype(BF16),
            w_f=jnp.pad(w_in[l][:, n_main:], ((0, 0), (0, LANES - n_heads))).astype(BF16),
            b_f=jnp.pad(b_f[l].astype(F32), (0, LANES - n_heads)).reshape(1, LANES),
            ar=ar, ai=ai, wb=wb, wc=wc, d_skip=row(d_skip[l]), w_glu=w_glu[l].astype(BF16), b_glu=row(b_glu[l]),
            g_ssm_out=row(g_ssm_out[l]), g_fox_out=row(g_fox_out[l]), w_out=w_out[l].astype(BF16),
            g_mem_q=row(g_mem_q[l]), w_mq=w_mq[l].astype(BF16), w_mo=w_mo[l].astype(BF16), g_ffn=row(g_ffn[l]),
            w_router_t=w_router[l].astype(F32).T, e_bias_t=e_bias[l].astype(F32).reshape(-1, 1),
            ws1=ws1[l].astype(BF16), ws3=ws3[l].astype(BF16), ws2=ws2[l].astype(BF16),
            w1=w1[l], w3=w3[l], w2=w2[l], g_final=row(g_final))
        assert depth == 1, "final norm fusion assumes a single layer"
        mk_p, mv_p, mkb_p, mvb_p = _mem_kv(mem_prompt, row(g_mem_kv[l]), w_mk[l].astype(BF16), w_mv[l].astype(BF16))
        nm, mh = mem_prompt.shape[1], MEM_HEADS
        hp, re_p, im_p, k_p, v_p, lf_p = _group(hp, None, None, None, None, None, mkb_p, mvb_p, p, w)
        cm_k = cache_mem_k[l].reshape(bsz, nm, -1).astype(BF16)
        cm_v = cache_mem_v[l].reshape(bsz, nm, -1).astype(BF16)
        hs, re_s, im_s, k_s, v_s, lf_s = _group(hs, state_ssm_re[l], state_ssm_im[l], cache_fox_k[l], cache_fox_v[l],
                                                cache_fox_logf[l], cm_k, cm_v, p, w)
        outs_p.append((re_p, im_p, k_p, v_p, lf_p, mk_p.reshape(bsz, nm, mh, -1), mv_p.reshape(bsz, nm, mh, -1)))
        outs_s.append((re_s, im_s, k_s, v_s, lf_s))
    stack = lambda outs, i: jnp.stack([o[i] for o in outs])
    return (hp, hs) + tuple(stack(outs_p, i) for i in range(7)) + tuple(stack(outs_s, i) for i in range(5))
```

```python
import functools
import math

import jax
import jax.numpy as jnp
from jax import lax
from jax.experimental import pallas as pl
from jax.experimental.pallas import tpu as pltpu

F32 = jnp.float32
BF16 = jnp.bfloat16

SSM_GROUP = 16
SSM_STATE = 64
FOX_HEAD_DIM = 64
MEM_HEADS = 4
TOP_K = 8
N_EXPERT_GROUPS = 8
TOPK_GROUPS = 4
ROUTED_SCALE = 2.5
RMS_EPS = 1e-6

LANES = 128
SUBLANES = 8
MXU_DIM = 256
VMEM_LIMIT_BYTES = 56 * 1024 * 1024

_HIGHEST = lax.Precision.HIGHEST
_NT = (((1,), (1,)), ((), ()))


def _params(*sem):
    return pltpu.CompilerParams(dimension_semantics=sem, vmem_limit_bytes=VMEM_LIMIT_BYTES)


def _rms(x, g):
    return x * lax.rsqrt(jnp.mean(x * x, axis=-1, keepdims=True) + RMS_EPS) * g


def _sigmoid(x):
    return 1.0 / (1.0 + jnp.exp(-x))


def _silu(x):
    return x * _sigmoid(x)


def _gelu_tanh(x):
    return x * (0.5 * (1.0 + jnp.tanh(math.sqrt(2.0 / math.pi) * (x + 0.044715 * (x * x * x)))))


def _log_sigmoid(x):
    return jnp.minimum(x, 0.0) - jnp.log1p(jnp.exp(-jnp.abs(x)))


def _dot(a, b):
    return jnp.dot(a, b, preferred_element_type=F32)


def _const_spec(shape):
    nd = len(shape)
    return pl.BlockSpec(shape, lambda *_: (0,) * nd)


def _memkv_kernel(m_ref, g_ref, wk_ref, wv_ref, k_ref, v_ref, kb_ref, vb_ref):
    mn = _rms(m_ref[0], g_ref[...]).astype(BF16)
    k = _dot(mn, wk_ref[...])
    v = _dot(mn, wv_ref[...])
    k_ref[0] = k
    v_ref[0] = v
    kb_ref[0] = k.astype(BF16)
    vb_ref[0] = v.astype(BF16)


def _mem_kv(mem, g, wk, wv):
    bsz, n, d = mem.shape
    blk = pl.BlockSpec((1, n, d), lambda b: (b, 0, 0))
    return pl.pallas_call(
        _memkv_kernel,
        grid=(bsz,),
        in_specs=[blk, _const_spec((1, d)), _const_spec((d, d)), _const_spec((d, d))],
        out_specs=[blk, blk, blk, blk],
        out_shape=[jax.ShapeDtypeStruct((bsz, n, d), F32)] * 2 + [jax.ShapeDtypeStruct((bsz, n, d), BF16)] * 2,
        compiler_params=_params("parallel"),
        name="mem_kv",
    )(mem, g, wk, wv)


def _proj_in_kernel(x_ref, g_ref, w_ref, wf_ref, bf_ref, u_ref, q_ref, k_ref, v_ref, kb_ref, vb_ref, lf_ref,
                    *, n_ssm, n_fox, n_heads):
    xb = _rms(x_ref[0], g_ref[...]).astype(BF16)
    z = _dot(xb, w_ref[...])
    u_ref[...] = z[:, :n_ssm]
    o = n_ssm
    q_ref[0] = (z[:, o:o + n_fox] * (FOX_HEAD_DIM ** -0.5 * math.log2(math.e))).astype(BF16)
    k = z[:, o + n_fox:o + 2 * n_fox]
    v = z[:, o + 2 * n_fox:o + 3 * n_fox]
    k_ref[0] = k
    v_ref[0] = v
    kb_ref[0] = k.astype(BF16)
    vb_ref[0] = v.astype(BF16)
    zf = _dot(xb, wf_ref[...])
    lf_ref[0] = _log_sigmoid(zf + bf_ref[...])[:, :n_heads]


def _proj_in(x, g, w_main, w_f, b_f, tm, n_ssm, n_fox, n_heads):
    bsz, L, d = x.shape
    nmain = w_main.shape[1]
    row = lambda n: pl.BlockSpec((1, tm, n), lambda b, i: (b, i, 0))
    u_spec = pl.BlockSpec((tm, n_ssm), lambda b, i: (i, b))
    outs = [jax.ShapeDtypeStruct((L, bsz * n_ssm), F32), jax.ShapeDtypeStruct((bsz, L, n_fox), BF16),
            jax.ShapeDtypeStruct((bsz, L, n_fox), F32), jax.ShapeDtypeStruct((bsz, L, n_fox), F32),
            jax.ShapeDtypeStruct((bsz, L, n_fox), BF16), jax.ShapeDtypeStruct((bsz, L, n_fox), BF16),
            jax.ShapeDtypeStruct((bsz, L, n_heads), F32)]
    return pl.pallas_call(
        functools.partial(_proj_in_kernel, n_ssm=n_ssm, n_fox=n_fox, n_heads=n_heads),
        grid=(bsz, L // tm),
        in_specs=[row(d), _const_spec((1, d)), _const_spec((d, nmain)), _const_spec((d, LANES)),
                  _const_spec((1, LANES))],
        out_specs=[u_spec, row(n_fox), row(n_fox), row(n_fox), row(n_fox), row(n_fox), row(n_heads)],
        out_shape=outs,
        compiler_params=_params("parallel", "parallel"),
        name="proj_in",
    )(x, g, w_main, w_f, b_f)


def _decay_kernel(lft_ref, c0_ref, dt_ref, car_ref, *, tl):
    @pl.when(pl.program_id(1) == 0)
    def _():
        car_ref[...] = c0_ref[0]
    r = lax.broadcasted_iota(jnp.int32, (tl, tl), 0)
    c = lax.broadcasted_iota(jnp.int32, (tl, tl), 1)
    tri = (r <= c).astype(F32)
    d = jnp.dot(lft_ref[0], tri, preferred_element_type=F32, precision=_HIGHEST) + car_ref[...]
    car_ref[...] = d[:, tl - 1:tl]
    rest = d * math.log2(math.e)
    for piece in range(3):
        part = rest.astype(BF16).astype(F32)
        dt_ref[0, piece] = part
        rest = rest - part


def _decay_cumsum(lft, c0, tl):
    bsz, nh, L = lft.shape
    return pl.pallas_call(
        functools.partial(_decay_kernel, tl=tl),
        grid=(bsz, L // tl),
        in_specs=[pl.BlockSpec((1, nh, tl), lambda b, i: (b, 0, i)), pl.BlockSpec((1, nh, 1), lambda b, i: (b, 0, 0))],
        out_specs=pl.BlockSpec((1, 3, nh, tl), lambda b, i: (b, 0, 0, i)),
        out_shape=jax.ShapeDtypeStruct((bsz, 3, nh, L), F32),
        scratch_shapes=[pltpu.VMEM((nh, 1), F32)],
        compiler_params=_params("parallel", "arbitrary"),
        name="decay_cumsum",
    )(lft, c0)


def _s5_kernel(u_ref, h0_ref, ar_ref, ai_ref, wb_ref, wc_ref, dsk_ref, wglu_ref, bglu_ref, gout_ref,
               y_ref, hl_ref, hs_ref, hst_ref, *, t_chunk, bsz, n_slab):
    @pl.when(pl.program_id(0) == 0)
    def _():
        hst_ref[...] = h0_ref[...]

    u = u_ref[...]
    ub = u.astype(BF16)
    slab_per_k = MXU_DIM // (2 * SSM_GROUP)
    for j in range(n_slab):
        kt = j // slab_per_k
        hs_ref[:, MXU_DIM * j:MXU_DIM * (j + 1)] = _dot(ub[:, MXU_DIM * kt:MXU_DIM * (kt + 1)], wb_ref[j])

    def step(t, h):
        r0 = pl.multiple_of(t * bsz, bsz)
        bu = hs_ref[pl.ds(r0, bsz), :]
        parts = []
        for j in range(n_slab):
            lo, mid, hi = MXU_DIM * j, MXU_DIM * j + LANES, MXU_DIM * (j + 1)
            re, im = h[:, lo:mid], h[:, mid:hi]
            ar = ar_ref[:, LANES * j:LANES * (j + 1)]
            ai = ai_ref[:, LANES * j:LANES * (j + 1)]
            parts.append(ar * re - ai * im + bu[:, lo:mid])
            parts.append(ar * im + ai * re + bu[:, mid:hi])
        hn = jnp.concatenate(parts, axis=1)
        hs_ref[pl.ds(r0, bsz), :] = hn
        return hn

    h = lax.fori_loop(0, t_chunk, step, hst_ref[...])
    hst_ref[...] = h
    hl_ref[...] = h

    halves = []
    for hf in range(n_slab // slab_per_k):
        acc = None
        for jj in range(slab_per_k):
            j = hf * slab_per_k + jj
            d = _dot(hs_ref[:, MXU_DIM * j:MXU_DIM * (j + 1)].astype(BF16), wc_ref[j])
            acc = d if acc is None else acc + d
        halves.append(acc)
    y = jnp.concatenate(halves, axis=1) + dsk_ref[...] * u
    y = _gelu_tanh(y)
    y = y * _sigmoid(_dot(y.astype(BF16), wglu_ref[...]) + bglu_ref[...])
    y_ref[...] = _rms(y, gout_ref[...]).astype(BF16)


def _s5(u_tb, h0, ar, ai, wb, wc, dsk, wglu, bglu, gout, t_chunk, bsz):
    rows, n_ssm = u_tb.shape
    n_state = h0.shape[1]
    n_slab = n_state // MXU_DIM
    r = t_chunk * bsz
    return pl.pallas_call(
        functools.partial(_s5_kernel, t_chunk=t_chunk, bsz=bsz, n_slab=n_slab),
        grid=(rows // r,),
        in_specs=[pl.BlockSpec((r, n_ssm), lambda c: (c, 0)), _const_spec((bsz, n_state)),
                  _const_spec((bsz, n_state // 2)), _const_spec((bsz, n_state // 2)),
                  _const_spec((n_slab, MXU_DIM, MXU_DIM)), _const_spec((n_slab, MXU_DIM, MXU_DIM)),
                  _const_spec((1, n_ssm)), _const_spec((n_ssm, n_ssm)), _const_spec((1, n_ssm)),
                  _const_spec((1, n_ssm))],
        out_specs=[pl.BlockSpec((r, n_ssm), lambda c: (c, 0)), _const_spec((bsz, n_state))],
        out_shape=[jax.ShapeDtypeStruct((rows, n_ssm), BF16), jax.ShapeDtypeStruct((bsz, n_state), F32)],
        scratch_shapes=[pltpu.VMEM((r, n_state), F32), pltpu.VMEM((bsz, n_state), F32)],
        compiler_params=_params("arbitrary"),
        name="s5_mixer",
    )(u_tb, h0, ar, ai, wb, wc, dsk, wglu, bglu, gout)


def _s5_tables(lam_re, lam_im, log_dt, b_re, b_im, c_re, c_im, bsz):
    G, P = lam_re.shape
    H = b_re.shape[-1]
    lr, li = lam_re.astype(F32), lam_im.astype(F32)
    dt = jnp.exp(log_dt.astype(F32))[:, None]
    mag = jnp.exp(lr * dt)
    a_re, a_im = mag * jnp.cos(li * dt), mag * jnp.sin(li * dt)
    den = lr * lr + li * li
    c_r = ((a_re - 1.0) * lr + a_im * li) / den
    c_i = (a_im * lr - (a_re - 1.0) * li) / den
    br, bi = b_re.astype(F32), b_im.astype(F32)
    bbar_re = c_r[..., None] * br - c_i[..., None] * bi
    bbar_im = c_r[..., None] * bi + c_i[..., None] * br
    n_pair = G // 2
    per_k = MXU_DIM // (2 * H)
    eye2 = jnp.eye(2, dtype=F32)
    place = jax.nn.one_hot(jnp.arange(n_pair) % per_k, per_k, dtype=F32)

    ar = jnp.broadcast_to(a_re.reshape(1, -1), (bsz, G * P))
    ai = jnp.broadcast_to(a_im.reshape(1, -1), (bsz, G * P))

    bb = jnp.stack([bbar_re, bbar_im]).reshape(2, n_pair, 2, P, H)
    wpair = jnp.einsum("ajgph,gk->jghakp", bb, eye2).reshape(n_pair, 2 * H, MXU_DIM)
    wb = jnp.einsum("jrc,jk->jkrc", wpair, place).reshape(n_pair, MXU_DIM, MXU_DIM)

    cc = jnp.stack([c_re.astype(F32), -c_im.astype(F32)]).reshape(2, n_pair, 2, H, P)
    cpair = jnp.einsum("ajghp,gk->jagpkh", cc, eye2).reshape(n_pair, MXU_DIM, 2 * H)
    wc = jnp.einsum("jnc,jk->jnkc", cpair, place).reshape(n_pair, MXU_DIM, MXU_DIM)
    return ar, ai, wb.astype(BF16), wc.astype(BF16)


def _state_to_lanes(re, im):
    bsz, G, P = re.shape
    s = jnp.stack([re, im], axis=1).reshape(bsz, 2, G // 2, 2, P)
    return s.transpose(0, 2, 1, 3, 4).reshape(bsz, 2 * G * P)


def _lanes_to_state(h, G, P):
    bsz = h.shape[0]
    s = h.reshape(bsz, G // 2, 2, 2, P).transpose(0, 2, 1, 3, 4).reshape(bsz, 2, G, P)
    return s[:, 0], s[:, 1]


FOX_AUG_ROWS = 16


def _fox_kernel(qt_ref, qa_ref, k_ref, vt_ref, o_ref, m_ref, l_ref, acc_ref, *, tq, tk, past):
    qi = pl.program_id(2)
    qt = qt_ref[0]
    row = lax.broadcasted_iota(jnp.int32, (LANES, tq), 0)
    zero = jnp.zeros_like(qt)
    pad = jnp.zeros((LANES - FOX_AUG_ROWS, tq), BF16)
    qts = []
    for hh in range(2):
        own = (row < FOX_HEAD_DIM) if hh == 0 else (row >= FOX_HEAD_DIM)
        aug = qa_ref[0, 0, FOX_AUG_ROWS * hh:FOX_AUG_ROWS * (hh + 1), :]
        qts.append(jnp.concatenate([jnp.where(own, qt, zero), aug, pad], axis=0))
    m_ref[...] = jnp.full(m_ref.shape, -1e30, F32)
    l_ref[...] = jnp.zeros(l_ref.shape, F32)
    acc_ref[...] = jnp.zeros(acc_ref.shape, F32)
    q_start = past + qi * tq
    n_full = (q_start + 1) // tk
    n_all = (q_start + tq + tk - 1) // tk

    def block(j, masked):
        ks = pl.multiple_of(j * tk, tk)
        kb = k_ref[0, 0, pl.ds(ks, tk), :]
        if masked:
            kpos = ks + lax.broadcasted_iota(jnp.int32, (tk, tq), 0)
            qpos = q_start + lax.broadcasted_iota(jnp.int32, (tk, tq), 1)
            visible = kpos <= qpos
        for hh in range(2):
            st = _dot(kb, qts[hh])
            if masked:
                st = jnp.where(visible, st, -jnp.inf)
            m_old = m_ref[hh]
            m_new = jnp.maximum(m_old, jnp.max(st, axis=0, keepdims=True))
            p = jnp.exp2(st - m_new)
            alpha = jnp.exp2(m_old - m_new)
            l_ref[hh] = alpha * l_ref[hh] + jnp.sum(p, axis=0, keepdims=True)
            vt = vt_ref[0, FOX_HEAD_DIM * hh:FOX_HEAD_DIM * (hh + 1), pl.ds(ks, tk)]
            acc_ref[hh] = alpha * acc_ref[hh] + _dot(vt, p.astype(BF16))
            m_ref[hh] = m_new

    def full_body(j, c):
        block(j, False)
        return c

    def masked_body(j, c):
        block(j, True)
        return c

    lax.fori_loop(0, n_full, full_body, 0)
    lax.fori_loop(n_full, n_all, masked_body, 0)
    out_t = jnp.concatenate([acc_ref[0] / l_ref[0], acc_ref[1] / l_ref[1]], axis=0)
    o_ref[0] = out_t.T


def _fox(q_t, q_aug, k_aug, v_t, tq, tk, past):
    bsz, n_fox, L = q_t.shape
    n_pair = n_fox // LANES
    lk = k_aug.shape[2]
    return pl.pallas_call(
        functools.partial(_fox_kernel, tq=tq, tk=tk, past=past),
        grid=(bsz, n_pair, L // tq),
        in_specs=[pl.BlockSpec((1, LANES, tq), lambda b, h, i: (b, h, i)),
                  pl.BlockSpec((1, 1, 2 * FOX_AUG_ROWS, tq), lambda b, h, i: (b, h, 0, i)),
                  pl.BlockSpec((1, 1, lk, 2 * LANES), lambda b, h, i: (b, h, 0, 0)),
                  pl.BlockSpec((1, LANES, lk), lambda b, h, i: (b, h, 0))],
        out_specs=pl.BlockSpec((1, tq, LANES), lambda b, h, i: (b, i, h)),
        out_shape=jax.ShapeDtypeStruct((bsz, L, n_fox), F32),
        scratch_shapes=[pltpu.VMEM((2, 1, tq), F32), pltpu.VMEM((2, 1, tq), F32),
                        pltpu.VMEM((2, FOX_HEAD_DIM, tq), F32)],
        compiler_params=_params("parallel", "parallel", "arbitrary"),
        name="fox_attention",
    )(q_t, q_aug, k_aug, v_t)


def _fox_operands(qb, k_all, v_all, d_pieces, past, L):
    bsz, lk_pad, n_fox = k_all.shape
    n_heads = n_fox // FOX_HEAD_DIM
    n_pair = n_heads // 2
    pieces = d_pieces.astype(BF16)
    dk = (-pieces).reshape(bsz, 3, n_pair, 2, lk_pad)
    dk = dk.transpose(0, 2, 4, 3, 1).reshape(bsz, n_pair, lk_pad, 6)
    ones_k = jnp.ones((bsz, n_pair, lk_pad, 3), BF16)
    zeros_k = jnp.zeros((bsz, n_pair, lk_pad, LANES - 9), BF16)
    k_pair = k_all.reshape(bsz, lk_pad, n_pair, LANES).transpose(0, 2, 1, 3)
    k_aug = jnp.concatenate([k_pair, dk, ones_k, zeros_k], axis=-1)

    dq = pieces[:, :, :, past:past + L].transpose(0, 2, 1, 3)
    sel = jax.nn.one_hot(jnp.arange(n_heads) % 2, 2, dtype=BF16)
    ones_q = jnp.broadcast_to(jnp.repeat(sel, 3, axis=1)[None, :, :, None], (bsz, n_heads, 6, L))
    zeros_q = jnp.zeros((bsz, n_heads, FOX_AUG_ROWS - 9, L), BF16)
    q_aug = jnp.concatenate([ones_q, dq, zeros_q], axis=2).reshape(bsz, n_pair, 2 * FOX_AUG_ROWS, L)
    return qb.transpose(0, 2, 1), q_aug, k_aug, v_all.transpose(0, 2, 1)


def _route_gates(logits_t, ebias_t, n_experts):
    per_group = n_experts // N_EXPERT_GROUPS
    tokens = logits_t.shape[1]
    score = _sigmoid(logits_t)
    sel = score + ebias_t
    row = lax.broadcasted_iota(jnp.int32, (per_group, tokens), 0).astype(F32)
    neg = jnp.float32(-jnp.inf)

    def first_argmax(tile, best):
        return jnp.min(jnp.where(tile == best, row, float(per_group)), axis=0, keepdims=True)

    score_g, sel_g, gsc = [], [], []
    for g in range(N_EXPERT_GROUPS):
        sg = sel[per_group * g:per_group * (g + 1), :]
        score_g.append(score[per_group * g:per_group * (g + 1), :])
        sel_g.append(sg)
        m1 = jnp.max(sg, axis=0, keepdims=True)
        rest = jnp.where(row == first_argmax(sg, m1), neg, sg)
        gsc.append(m1 + jnp.max(rest, axis=0, keepdims=True))
    cur = []
    for g in range(N_EXPERT_GROUPS):
        ahead = jnp.zeros((1, tokens), F32)
        for o in range(N_EXPERT_GROUPS):
            if o == g:
                continue
            beats = (gsc[o] >= gsc[g]) if o < g else (gsc[o] > gsc[g])
            ahead = ahead + jnp.where(beats, 1.0, 0.0)
        cur.append(jnp.where(ahead < TOPK_GROUPS, sel_g[g], neg))
    chosen = [jnp.zeros((per_group, tokens), jnp.bool_) for _ in range(N_EXPERT_GROUPS)]
    for _ in range(TOP_K):
        best = cur[0]
        for g in range(1, N_EXPERT_GROUPS):
            best = jnp.maximum(best, cur[g])
        best = jnp.max(best, axis=0, keepdims=True)
        idx = None
        for g in range(N_EXPERT_GROUPS):
            cand = jnp.min(jnp.where(cur[g] == best, row + float(per_group * g), float(n_experts)), axis=0,
                           keepdims=True)
            idx = cand if idx is None else jnp.minimum(idx, cand)
        for g in range(N_EXPERT_GROUPS):
            hit = (row + float(per_group * g)) == idx
            chosen[g] = chosen[g] | hit
            cur[g] = jnp.where(hit, neg, cur[g])
    w = [jnp.where(chosen[g], score_g[g], 0.0) for g in range(N_EXPERT_GROUPS)]
    total = w[0]
    for g in range(1, N_EXPERT_GROUPS):
        total = total + w[g]
    total = jnp.sum(total, axis=0, keepdims=True)
    return [w[g] / total * ROUTED_SCALE for g in range(N_EXPERT_GROUPS)]


def _mid_kernel(x_ref, ys_ref, yf_ref, gfox_ref, wout_ref, gmq_ref, wmq_ref, mk_ref, mv_ref, wmo_ref, gffn_ref,
                wrt_ref, ebt_ref, ws1_ref, ws3_ref, ws2_ref, xs_ref, xn_ref, gt_ref, cnt_ref, *, n_experts):
    x = x_ref[0]
    yfn = _rms(yf_ref[0], gfox_ref[...]).astype(BF16)
    mix = jnp.concatenate([ys_ref[...], yfn], axis=1)
    x1 = x + _dot(mix, wout_ref[...])

    qm = _dot(_rms(x1, gmq_ref[...]).astype(BF16), wmq_ref[...])
    hd = qm.shape[1] // MEM_HEADS
    heads = []
    for h in range(MEM_HEADS):
        qh = (qm[:, hd * h:hd * (h + 1)] * (hd ** -0.5)).astype(BF16)
        s = lax.dot_general(qh, mk_ref[0, :, hd * h:hd * (h + 1)], _NT, preferred_element_type=F32)
        p = jnp.exp(s - jnp.max(s, axis=1, keepdims=True))
        o = _dot(p.astype(BF16), mv_ref[0, :, hd * h:hd * (h + 1)]) / jnp.sum(p, axis=1, keepdims=True)
        heads.append(o.astype(BF16))
    x2 = x1 + _dot(jnp.concatenate(heads, axis=1), wmo_ref[...])

    xn = _rms(x2, gffn_ref[...])
    xnb = xn.astype(BF16)
    hidden = _silu(_dot(xnb, ws1_ref[...])) * _dot(xnb, ws3_ref[...])
    xs_ref[0] = x2 + _dot(hidden.astype(BF16), ws2_ref[...])
    xn_ref[0] = xnb

    wr = wrt_ref[...]
    wr_hi = wr.astype(BF16)
    wr_lo = (wr - wr_hi.astype(F32)).astype(BF16)
    xn_lo = (xn - xnb.astype(F32)).astype(BF16)
    nt = lambda a, c: lax.dot_general(a, c, _NT, preferred_element_type=F32)
    logits_t = nt(wr_hi, xnb) + (nt(wr_hi, xn_lo) + nt(wr_lo, xnb))
    gates = _route_gates(logits_t, ebt_ref[...], n_experts)
    per_group = n_experts // N_EXPERT_GROUPS
    for g in range(N_EXPERT_GROUPS):
        gt_ref[0, per_group * g:per_group * (g + 1), :] = gates[g]
        cnt_ref[0, 0, per_group * g:per_group * (g + 1), :] = jnp.sum(
            jnp.where(gates[g] != 0.0, 1.0, 0.0), axis=1, keepdims=True)


def _mid(x, ys, yf, gfox, wout, gmq, wmq, mkb, mvb, wmo, gffn, wrt, ebt, ws1, ws3, ws2, tm):
    bsz, L, d = x.shape
    n_fox = yf.shape[2]
    n_ssm = ys.shape[1] // bsz
    n_mem = mkb.shape[1]
    n_experts = wrt.shape[0]
    fs = ws1.shape[1]
    row = lambda n: pl.BlockSpec((1, tm, n), lambda b, i: (b, i, 0))
    memspec = pl.BlockSpec((1, n_mem, d), lambda b, i: (b, 0, 0))
    return pl.pallas_call(
        functools.partial(_mid_kernel, n_experts=n_experts),
        grid=(bsz, L // tm),
        in_specs=[row(d), pl.BlockSpec((tm, n_ssm), lambda b, i: (i, b)), row(n_fox), _const_spec((1, n_fox)),
                  _const_spec((d, d)), _const_spec((1, d)),
                  _const_spec((d, d)), memspec, memspec, _const_spec((d, d)), _const_spec((1, d)),
                  _const_spec((n_experts, d)), _const_spec((n_experts, 1)), _const_spec((d, fs)), _const_spec((d, fs)),
                  _const_spec((fs, d))],
        out_specs=[row(d), row(d), pl.BlockSpec((1, n_experts, tm), lambda b, i: (b, 0, i)),
                   pl.BlockSpec((1, 1, n_experts, 1), lambda b, i: (b, i, 0, 0))],
        out_shape=[jax.ShapeDtypeStruct((bsz, L, d), F32), jax.ShapeDtypeStruct((bsz, L, d), BF16),
                   jax.ShapeDtypeStruct((bsz, n_experts, L), F32),
                   jax.ShapeDtypeStruct((bsz, L // tm, n_experts, 1), F32)],
        compiler_params=_params("parallel", "parallel"),
        name="mid_block",
    )(x, ys, yf, gfox, wout, gmq, wmq, mkb, mvb, wmo, gffn, wrt, ebt, ws1, ws3, ws2)


SEG_ROWS = 16
MOE_TOKENS = 512
COPY_ROWS = 4 * SEG_ROWS
GATHER_ROWS = 1024
_TN = (((0,), (0,)), ((), ()))
_NO_RANK = -(1 << 20)


def _sorted_rows(tb, n_experts):
    rows = TOP_K * tb + n_experts * (SEG_ROWS - 1)
    return -(-rows // GATHER_ROWS) * GATHER_ROWS


def _expert_tile(m):
    return 1024 if m >= 4096 else 128


def _dispatch_plan(cnt, ns, MOE_TILE):
    nblk, n_experts = cnt.shape

    def before(a, axis):
        n = a.shape[axis]
        earlier = jnp.arange(n)[:, None] < jnp.arange(n)[None, :]
        if axis == 0:
            return jnp.sum(jnp.where(earlier[:, :, None], a[:, None, :], 0), axis=0)
        return jnp.sum(jnp.where(earlier[None, :, :], a[:, :, None], 0), axis=1)

    cp = (cnt + SEG_ROWS - 1) // SEG_ROWS * SEG_ROWS
    o_loc = before(cp, 1)
    used = jnp.sum(cp, axis=1)
    tot_e = jnp.sum(cp, axis=0)
    reg_e = (tot_e + MOE_TILE - 1) // MOE_TILE * MOE_TILE
    base_e = before(reg_e[None, :], 1)[0]
    reg_end = base_e + reg_e
    dst = base_e[None, :] + before(cp, 0)
    rows_max = nblk * ns + n_experts * MOE_TILE
    n_tiles_max = -(-rows_max // MOE_TILE)
    n_tiles = reg_end[-1] // MOE_TILE
    tile_idx = jnp.clip(jnp.arange(n_tiles_max, dtype=jnp.int32), 0, jnp.maximum(n_tiles - 1, 0))
    tile_expert = jnp.sum((reg_end[None, :] <= (tile_idx * MOE_TILE)[:, None]).astype(jnp.int32), axis=1)
    tile_expert = jnp.minimum(tile_expert, n_experts - 1)
    n_pc = ns // SEG_ROWS
    piece_row = jnp.arange(n_pc, dtype=jnp.int32) * SEG_ROWS
    seg_end = o_loc + cp
    piece_e = jnp.sum((seg_end[:, None, :] <= piece_row[None, :, None]).astype(jnp.int32), axis=2)
    live = piece_row[None, :] < used[:, None]
    piece_e = jnp.minimum(piece_e, n_experts - 1)
    own = piece_e[:, :, None] == jnp.arange(n_experts, dtype=jnp.int32)[None, None, :]
    pick = lambda table: jnp.sum(jnp.where(own, table[:, None, :], 0), axis=2)
    rank0 = piece_row[None, :] - pick(o_loc)
    piece_rank = jnp.where(live, rank0, _NO_RANK)
    i32 = lambda a: a.astype(jnp.int32).reshape(-1)
    per_block = lambda a: a.astype(jnp.int32).reshape(nblk, 1, -1)

    def copy_list(n_e, first_rank, rows_each, n_max):
        end = before(n_e, 1) + n_e
        j = jnp.arange(n_max, dtype=jnp.int32)
        e_of = jnp.minimum(jnp.sum((end[:, None, :] <= j[None, :, None]).astype(jnp.int32), axis=2), n_experts - 1)
        mine = e_of[:, :, None] == jnp.arange(n_experts, dtype=jnp.int32)[None, None, :]
        take = lambda table: jnp.sum(jnp.where(mine, table[:, None, :], 0), axis=2)
        rank = take(first_rank) + (j[None, :] - take(end - n_e)) * rows_each
        ok = j[None, :] < jnp.sum(n_e, axis=1)[:, None]
        return (per_block(jnp.where(ok, take(o_loc) + rank, 0)), per_block(jnp.where(ok, take(dst) + rank, 0)),
                i32(jnp.sum(n_e, axis=1)))

    per_copy = COPY_ROWS // SEG_ROWS
    n_wide_e = (cp // SEG_ROWS) // per_copy
    n_seg_e = cp // SEG_ROWS - n_wide_e * per_copy
    wide_src, wide_dst, n_wide = copy_list(n_wide_e, jnp.zeros_like(cp), COPY_ROWS, ns // COPY_ROWS)
    seg_src, seg_dst, n_seg = copy_list(n_seg_e, n_wide_e * COPY_ROWS, SEG_ROWS, n_experts * (per_copy - 1))
    return dict(n_piece=i32(used // SEG_ROWS), piece_e=per_block(piece_e), piece_rank=per_block(piece_rank),
                wide_src=wide_src, wide_dst=wide_dst, n_wide=n_wide, seg_src=seg_src, seg_dst=seg_dst, n_seg=n_seg,
                tail_start=i32(base_e + tot_e),
                tail_chunks=i32((reg_e - tot_e) // SEG_ROWS), tile_idx=i32(tile_idx), tile_expert=i32(tile_expert),
                n_tiles=i32(n_tiles), n_tiles_max=n_tiles_max)


def _slot_ranks(gates, tb):
    sel = gates != 0.0
    r = lax.broadcasted_iota(jnp.int32, (tb, tb), 0)
    c = lax.broadcasted_iota(jnp.int32, (tb, tb), 1)
    earlier = jnp.where(r < c, 1.0, 0.0).astype(BF16)
    rank = _dot(jnp.where(sel, 1.0, 0.0).astype(BF16), earlier)
    return jnp.where(sel, rank, -1.0)


def _build_one_hot(out_ref, rm_ref, pe_ref, pr_ref, first, count, tb, value_row):
    rows = lax.broadcasted_iota(jnp.int32, (SEG_ROWS, tb), 0).astype(F32)
    for i in range(first, first + count):
        e = pe_ref[0, 0, i]
        hit = rows == (rm_ref[pl.ds(e, 1), :] - pr_ref[0, 0, i].astype(F32))
        out_ref[SEG_ROWS * i:SEG_ROWS * (i + 1), :] = jnp.where(hit, value_row(e), 0.0).astype(BF16)


def _wait_rows(n_rows, make_copy):
    wide = 16 * SEG_ROWS
    n_wide = n_rows // wide

    def wide_step(i, c):
        make_copy(wide).wait()
        return c

    def seg_step(i, c):
        make_copy(SEG_ROWS).wait()
        return c

    lax.fori_loop(0, n_wide, wide_step, 0)
    lax.fori_loop(0, (n_rows - n_wide * wide) // SEG_ROWS, seg_step, 0)


def _issue_copies(n_wide, n_seg, tables, make_copy):
    wide_src, wide_dst, seg_src, seg_dst = tables

    def wide(i, c):
        make_copy(pl.multiple_of(wide_src[0, 0, i], SEG_ROWS), pl.multiple_of(wide_dst[0, 0, i], SEG_ROWS),
                  COPY_ROWS).start()
        return c

    def seg(i, c):
        make_copy(pl.multiple_of(seg_src[0, 0, i], SEG_ROWS), pl.multiple_of(seg_dst[0, 0, i], SEG_ROWS),
                  SEG_ROWS).start()
        return c

    lax.fori_loop(0, n_wide, wide, 0)
    lax.fori_loop(0, n_seg, seg, 0)


def _dispatch_kernel(n_piece_ref, n_wide_ref, n_seg_ref, tail_start_ref, tail_chunks_ref, pe_ref, pr_ref,
                     ws_ref, wd_ref, ss_ref, sd_ref,
                     xn_ref, gt_ref, xg_ref, g_ref, xsb_ref, rm_ref, z_ref, sem, *, tb, ns, n_experts, nblk):
    b = pl.program_id(0)
    n_pc = ns // SEG_ROWS
    used = n_piece_ref[b] * SEG_ROWS
    rm_ref[...] = _slot_ranks(gt_ref[...], tb)
    per_slice = GATHER_ROWS // SEG_ROWS
    n_slices = ns // GATHER_ROWS
    build = lambda t: _build_one_hot(g_ref, rm_ref, pe_ref, pr_ref, per_slice * t, per_slice, tb, lambda e: 1.0)

    slot = b % 2
    build(0)
    for t in range(n_slices):
        if t + 1 < n_slices:
            build(t + 1)
        sl = slice(GATHER_ROWS * t, GATHER_ROWS * (t + 1))
        xsb_ref[slot, sl, :] = _dot(g_ref[sl, :], xn_ref[...]).astype(BF16)

    def rows_copy(buf, src_row, dst_row, size):
        return pltpu.make_async_copy(xsb_ref.at[buf, pl.ds(src_row, size)], xg_ref.at[pl.ds(dst_row, size)],
                                     sem.at[buf])

    _issue_copies(n_wide_ref[b], n_seg_ref[b], (ws_ref, wd_ref, ss_ref, sd_ref),
                  lambda block_row, global_row, size: rows_copy(slot, block_row, global_row, size))

    @pl.when(b > 0)
    def _():
        _wait_rows(n_piece_ref[b - 1] * SEG_ROWS, lambda size: rows_copy(1 - slot, 0, 0, size))

    @pl.when(b == nblk - 1)
    def _():
        _wait_rows(used, lambda size: rows_copy(slot, 0, 0, size))
        z_ref[...] = jnp.zeros(z_ref.shape, BF16)

        wide = z_ref.shape[0]
        per_wide = wide // SEG_ROWS

        def tail_copy(row, size):
            return pltpu.make_async_copy(z_ref.at[pl.ds(0, size)], xg_ref.at[pl.ds(pl.multiple_of(row, SEG_ROWS), size)],
                                         sem.at[slot])

        def each_tail(action):
            def per_expert(e, carry):
                n_wide = tail_chunks_ref[e] // per_wide

                def wide_piece(i, c2):
                    action(tail_copy(tail_start_ref[e] + i * wide, wide))
                    return c2

                def seg_piece(i, c2):
                    action(tail_copy(tail_start_ref[e] + n_wide * wide + i * SEG_ROWS, SEG_ROWS))
                    return c2

                lax.fori_loop(0, n_wide, wide_piece, 0)
                lax.fori_loop(0, tail_chunks_ref[e] - n_wide * per_wide, seg_piece, 0)
                return carry
            lax.fori_loop(0, n_experts, per_expert, 0)

        each_tail(lambda cp: cp.start())
        each_tail(lambda cp: cp.wait())


def _expert_kernel(tile_idx_ref, tile_expert_ref, n_tiles_ref, x_ref, w1_ref, w3_ref, w2_ref, o_ref):
    @pl.when(pl.program_id(0) < n_tiles_ref[0])
    def _():
        x = x_ref[...]
        h = _silu(_dot(x, w1_ref[0].astype(BF16))) * _dot(x, w3_ref[0].astype(BF16))
        o_ref[...] = _dot(h.astype(BF16), w2_ref[0].astype(BF16)).astype(BF16)


def _combine_kernel(n_piece_ref, n_wide_ref, n_seg_ref, pe_ref, pr_ref, ws_ref, wd_ref, ss_ref, sd_ref,
                    ws_next_ref, wd_next_ref, ss_next_ref, sd_next_ref,
                    og_ref, gt_ref, xs_ref, gfin_ref, y_ref, gw_ref, ob_ref, rm_ref, sem, *, tb, ns, n_experts, nblk):
    b = pl.program_id(0)
    slot = b % 2
    n_pc = ns // SEG_ROWS

    def rows_copy(buf, src_row, dst_row, size):
        return pltpu.make_async_copy(og_ref.at[pl.ds(src_row, size)], ob_ref.at[buf, pl.ds(dst_row, size)],
                                     sem.at[buf])

    def fetch_block(blk, tables, buf):
        _issue_copies(n_wide_ref[blk], n_seg_ref[blk], tables,
                      lambda block_row, global_row, size: rows_copy(buf, global_row, block_row, size))

    @pl.when(b == 0)
    def _():
        fetch_block(b, (ws_ref, wd_ref, ss_ref, sd_ref), slot)

    @pl.when(b + 1 < nblk)
    def _():
        fetch_block(jnp.minimum(b + 1, nblk - 1), (ws_next_ref, wd_next_ref, ss_next_ref, sd_next_ref), 1 - slot)

    rm_ref[...] = _slot_ranks(gt_ref[...], tb)
    _wait_rows(n_piece_ref[b] * SEG_ROWS, lambda size: rows_copy(slot, 0, 0, size))

    def clear(i, c):
        ob_ref[slot, pl.ds(pl.multiple_of((n_piece_ref[b] + i) * SEG_ROWS, SEG_ROWS), SEG_ROWS), :] = jnp.zeros(
            (SEG_ROWS, ob_ref.shape[2]), BF16)
        return c

    lax.fori_loop(0, n_pc - n_piece_ref[b], clear, 0)

    per_slice = GATHER_ROWS // SEG_ROWS
    n_slices = ns // GATHER_ROWS
    build = lambda t: _build_one_hot(gw_ref, rm_ref, pe_ref, pr_ref, per_slice * t, per_slice, tb,
                                     lambda e: gt_ref[pl.ds(e, 1), :])
    build(0)
    y = xs_ref[...]
    for t in range(n_slices):
        if t + 1 < n_slices:
            build(t + 1)
        sl = slice(GATHER_ROWS * t, GATHER_ROWS * (t + 1))
        y = y + lax.dot_general(gw_ref[sl, :], ob_ref[slot, sl, :], _TN, preferred_element_type=F32)
    y_ref[...] = _rms(y, gfin_ref[...])


def _moe(xn, gates_t, cnt, xs, w1, w3, w2, gfin):
    m, d = xn.shape
    n_experts, _, f = w1.shape
    tb = min(m, MOE_TOKENS)
    nblk = m // tb
    ns = _sorted_rows(tb, n_experts)
    MOE_TILE = _expert_tile(m)
    plan = _dispatch_plan(cnt, ns, MOE_TILE)
    n_tiles_max = plan["n_tiles_max"]
    rows_max = n_tiles_max * MOE_TILE
    this_block = lambda t: pl.BlockSpec((1, 1, t.shape[2]), lambda b, *_: (b, 0, 0), memory_space=pltpu.SMEM)
    next_block = lambda t: pl.BlockSpec((1, 1, t.shape[2]), lambda b, *_: (jnp.minimum(b + 1, nblk - 1), 0, 0),
                                        memory_space=pltpu.SMEM)
    build_tables = (plan["piece_e"], plan["piece_rank"])
    copy_tables = (plan["wide_src"], plan["wide_dst"], plan["seg_src"], plan["seg_dst"])
    counts = (plan["n_piece"], plan["n_wide"], plan["n_seg"])

    xg = pl.pallas_call(
        functools.partial(_dispatch_kernel, tb=tb, ns=ns, n_experts=n_experts, nblk=nblk),
        grid_spec=pltpu.PrefetchScalarGridSpec(
            num_scalar_prefetch=5, grid=(nblk,),
            in_specs=[this_block(t) for t in build_tables + copy_tables] + [
                      pl.BlockSpec((tb, d), lambda b, *_: (b, 0)), pl.BlockSpec((n_experts, tb), lambda b, *_: (0, b))],
            out_specs=pl.BlockSpec(memory_space=pl.ANY),
            scratch_shapes=[pltpu.VMEM((ns, tb), BF16), pltpu.VMEM((2, ns, d), BF16), pltpu.VMEM((n_experts, tb), F32),
                            pltpu.VMEM((16 * SEG_ROWS, d), BF16), pltpu.SemaphoreType.DMA((2,))]),
        out_shape=jax.ShapeDtypeStruct((rows_max, d), BF16),
        compiler_params=_params("arbitrary"),
        name="moe_dispatch",
    )(*counts, plan["tail_start"], plan["tail_chunks"], *build_tables, *copy_tables, xn, gates_t)

    tile = lambda i, idx, ex, n: (idx[i], 0)
    og = pl.pallas_call(
        _expert_kernel,
        grid_spec=pltpu.PrefetchScalarGridSpec(
            num_scalar_prefetch=3, grid=(n_tiles_max,),
            in_specs=[pl.BlockSpec((MOE_TILE, d), tile),
                      pl.BlockSpec((1, d, f), lambda i, idx, ex, n: (ex[i], 0, 0)),
                      pl.BlockSpec((1, d, f), lambda i, idx, ex, n: (ex[i], 0, 0)),
                      pl.BlockSpec((1, f, d), lambda i, idx, ex, n: (ex[i], 0, 0))],
            out_specs=pl.BlockSpec((MOE_TILE, d), tile)),
        out_shape=jax.ShapeDtypeStruct((rows_max, d), BF16),
        compiler_params=_params("arbitrary"),
        name="moe_experts",
    )(plan["tile_idx"], plan["tile_expert"], plan["n_tiles"], xg, w1, w3, w2)

    return pl.pallas_call(
        functools.partial(_combine_kernel, tb=tb, ns=ns, n_experts=n_experts, nblk=nblk),
        grid_spec=pltpu.PrefetchScalarGridSpec(
            num_scalar_prefetch=3, grid=(nblk,),
            in_specs=[this_block(t) for t in build_tables + copy_tables] + [next_block(t) for t in copy_tables] + [
                      pl.BlockSpec(memory_space=pl.ANY), pl.BlockSpec((n_experts, tb), lambda b, *_: (0, b)),
                      pl.BlockSpec((tb, d), lambda b, *_: (b, 0)), pl.BlockSpec((1, d), lambda b, *_: (0, 0))],
            out_specs=pl.BlockSpec((tb, d), lambda b, *_: (b, 0)),
            scratch_shapes=[pltpu.VMEM((ns, tb), BF16), pltpu.VMEM((2, ns, d), BF16), pltpu.VMEM((n_experts, tb), F32),
                            pltpu.SemaphoreType.DMA((2,))]),
        out_shape=jax.ShapeDtypeStruct((m, d), F32),
        compiler_params=_params("arbitrary"),
        name="moe_combine",
    )(*counts, *build_tables, *copy_tables, *copy_tables, og, gates_t, xs, gfin)


def _tiles(L):
    t_row = min(L, 512)
    t_key = 512
    t_scan = min(L, 64)
    return t_row, t_key, t_scan


def _group(x, h0_re, h0_im, past_k, past_v, past_lf, mkb, mvb, p, w):
    bsz, L, d = x.shape
    n_heads = p["b_f"].shape[0]
    n_fox = n_heads * FOX_HEAD_DIM
    n_ssm = p["d_skip"].shape[0]
    G, P = p["lam_re"].shape
    t_row, t_key, t_scan = _tiles(L)

    u, qb, k, v, kb, vb, lf = _proj_in(x, w["g_mix"], w["w_main"], w["w_f"], w["b_f"], t_row, n_ssm, n_fox, n_heads)

    u_tb = u.reshape(L * bsz, n_ssm)
    if h0_re is None:
        h0 = jnp.zeros((bsz, 2 * G * P), F32)
    else:
        h0 = _state_to_lanes(h0_re.astype(F32), h0_im.astype(F32))
    ys_tb, h_last = _s5(u_tb, h0, w["ar"], w["ai"], w["wb"], w["wc"], w["d_skip"], w["w_glu"], w["b_glu"],
                        w["g_ssm_out"], t_scan, bsz)
    ys = ys_tb.reshape(L, bsz * n_ssm)
    hl_re, hl_im = _lanes_to_state(h_last, G, P)

    if past_k is None:
        past = 0
        lf_all, k_all, v_all = lf, kb, vb
    else:
        past = past_k.shape[1]
        lf_all = jnp.concatenate([past_lf.astype(F32), lf], axis=1)
        k_all = jnp.concatenate([past_k.reshape(bsz, past, n_fox).astype(BF16), kb], axis=1)
        v_all = jnp.concatenate([past_v.reshape(bsz, past, n_fox).astype(BF16), vb], axis=1)
    lk = lf_all.shape[1]
    lk_pad = -(-lk // t_key) * t_key
    lf_t = jnp.pad(lf_all.transpose(0, 2, 1), ((0, 0), (0, 0), (0, lk_pad - lk)))
    d_pieces = _decay_cumsum(lf_t.reshape(1, bsz * n_heads, lk_pad), jnp.zeros((1, bsz * n_heads, 1), F32), t_key)
    d_pieces = d_pieces.reshape(3, bsz, n_heads, lk_pad).transpose(1, 0, 2, 3)
    k_all = jnp.pad(k_all, ((0, 0), (0, lk_pad - lk), (0, 0)))
    v_all = jnp.pad(v_all, ((0, 0), (0, lk_pad - lk), (0, 0)))
    t_att = t_key if L >= t_key else lk_pad
    yf = _fox(*_fox_operands(qb, k_all, v_all, d_pieces, past, L), t_row, t_att, past)

    xs, xnb, gates_t, cnt = _mid(x, ys, yf, w["g_fox_out"], w["w_out"], w["g_mem_q"], w["w_mq"], mkb, mvb, w["w_mo"],
                                 w["g_ffn"], w["w_router_t"], w["e_bias_t"], w["ws1"], w["ws3"], w["ws2"], t_row)
    m = bsz * L
    n_experts = gates_t.shape[1]
    tb = min(m, MOE_TOKENS)
    cnt = cnt.reshape(m // tb, tb // t_row, n_experts).sum(axis=1).astype(jnp.int32)
    y = _moe(xnb.reshape(m, d), gates_t.transpose(1, 0, 2).reshape(n_experts, m), cnt, xs.reshape(m, d),
             w["w1"], w["w3"], w["w2"], w["g_final"])
    return (y.reshape(bsz, L, d), hl_re, hl_im, k.reshape(bsz, L, n_heads, FOX_HEAD_DIM),
            v.reshape(bsz, L, n_heads, FOX_HEAD_DIM), lf)


def kernel(x_prompt, x_sample, state_ssm_re, state_ssm_im, cache_fox_k, cache_fox_v, cache_fox_logf, cache_mem_k, cache_mem_v, mem_prompt, g_mix, w_in, b_f, lam_re, lam_im, log_dt, b_re, b_im, c_re, c_im, d_skip, w_glu, b_glu, g_ssm_out, g_fox_out, w_out, g_mem_q, g_mem_kv, w_mq, w_mk, w_mv, w_mo, g_ffn, w_router, e_bias, w1, w3, w2, ws1, ws3, ws2, g_final):
    depth = w_in.shape[0]
    bsz = x_prompt.shape[0]
    hp, hs = x_prompt, x_sample
    outs_p, outs_s = [], []
    for l in range(depth):
        p = dict(b_f=b_f[l], lam_re=lam_re[l], d_skip=d_skip[l])
        n_heads = b_f.shape[1]
        n_ssm = d_skip.shape[1]
        n_main = w_in.shape[2] - n_heads
        row = lambda a: a.reshape(1, -1).astype(F32)
        ar, ai, wb, wc = _s5_tables(lam_re[l], lam_im[l], log_dt[l], b_re[l], b_im[l], c_re[l], c_im[l], bsz)
        w = dict(
            g_mix=row(g_mix[l]), w_main=w_in[l][:, :n_main].astype(BF16),
            w_f=jnp.pad(w_in[l][:, n_main:], ((0, 0), (0, LANES - n_heads))).astype(BF16),
            b_f=jnp.pad(b_f[l].astype(F32), (0, LANES - n_heads)).reshape(1, LANES),
            ar=ar, ai=ai, wb=wb, wc=wc, d_skip=row(d_skip[l]), w_glu=w_glu[l].astype(BF16), b_glu=row(b_glu[l]),
            g_ssm_out=row(g_ssm_out[l]), g_fox_out=row(g_fox_out[l]), w_out=w_out[l].astype(BF16),
            g_mem_q=row(g_mem_q[l]), w_mq=w_mq[l].astype(BF16), w_mo=w_mo[l].astype(BF16), g_ffn=row(g_ffn[l]),
            w_router_t=w_router[l].astype(F32).T, e_bias_t=e_bias[l].astype(F32).reshape(-1, 1),
            ws1=ws1[l].astype(BF16), ws3=ws3[l].astype(BF16), ws2=ws2[l].astype(BF16),
            w1=w1[l], w3=w3[l], w2=w2[l], g_final=row(g_final))
        assert depth == 1, "final norm fusion assumes a single layer"
        mk_p, mv_p, mkb_p, mvb_p = _mem_kv(mem_prompt, row(g_mem_kv[l]), w_mk[l].astype(BF16), w_mv[l].astype(BF16))
        nm, mh = mem_prompt.shape[1], MEM_HEADS
        hp, re_p, im_p, k_p, v_p, lf_p = _group(hp, None, None, None, None, None, mkb_p, mvb_p, p, w)
        cm_k = cache_mem_k[l].reshape(bsz, nm, -1).astype(BF16)
        cm_v = cache_mem_v[l].reshape(bsz, nm, -1).astype(BF16)
        hs, re_s, im_s, k_s, v_s, lf_s = _group(hs, state_ssm_re[l], state_ssm_im[l], cache_fox_k[l], cache_fox_v[l],
                                                cache_fox_logf[l], cm_k, cm_v, p, w)
        outs_p.append((re_p, im_p, k_p, v_p, lf_p, mk_p.reshape(bsz, nm, mh, -1), mv_p.reshape(bsz, nm, mh, -1)))
        outs_s.append((re_s, im_s, k_s, v_s, lf_s))
    stack = lambda outs, i: jnp.stack([o[i] for o in outs])
    return (hp, hs) + tuple(stack(outs_p, i) for i in range(7)) + tuple(stack(outs_s, i) for i in range(5))
```

```python
import functools
import math

import jax
import jax.numpy as jnp
from jax import lax
from jax.experimental import pallas as pl
from jax.experimental.pallas import tpu as pltpu

F32 = jnp.float32
BF16 = jnp.bfloat16

SSM_GROUP = 16
SSM_STATE = 64
FOX_HEAD_DIM = 64
MEM_HEADS = 4
TOP_K = 8
N_EXPERT_GROUPS = 8
TOPK_GROUPS = 4
ROUTED_SCALE = 2.5
RMS_EPS = 1e-6

LANES = 128
SUBLANES = 8
MXU_DIM = 256
VMEM_LIMIT_BYTES = 56 * 1024 * 1024

_HIGHEST = lax.Precision.HIGHEST
_NT = (((1,), (1,)), ((), ()))


def _params(*sem):
    return pltpu.CompilerParams(dimension_semantics=sem, vmem_limit_bytes=VMEM_LIMIT_BYTES)


def _rms(x, g):
    return x * lax.rsqrt(jnp.mean(x * x, axis=-1, keepdims=True) + RMS_EPS) * g


def _sigmoid(x):
    return 1.0 / (1.0 + jnp.exp(-x))


def _silu(x):
    return x * _sigmoid(x)


def _gelu_tanh(x):
    return x * (0.5 * (1.0 + jnp.tanh(math.sqrt(2.0 / math.pi) * (x + 0.044715 * (x * x * x)))))


def _log_sigmoid(x):
    return jnp.minimum(x, 0.0) - jnp.log1p(jnp.exp(-jnp.abs(x)))


def _dot(a, b):
    return jnp.dot(a, b, preferred_element_type=F32)


def _const_spec(shape):
    nd = len(shape)
    return pl.BlockSpec(shape, lambda *_: (0,) * nd)


def _memkv_kernel(m_ref, g_ref, wk_ref, wv_ref, k_ref, v_ref, kb_ref, vb_ref):
    mn = _rms(m_ref[0], g_ref[...]).astype(BF16)
    k = _dot(mn, wk_ref[...])
    v = _dot(mn, wv_ref[...])
    k_ref[0] = k
    v_ref[0] = v
    kb_ref[0] = k.astype(BF16)
    vb_ref[0] = v.astype(BF16)


def _mem_kv(mem, g, wk, wv):
    bsz, n, d = mem.shape
    blk = pl.BlockSpec((1, n, d), lambda b: (b, 0, 0))
    return pl.pallas_call(
        _memkv_kernel,
        grid=(bsz,),
        in_specs=[blk, _const_spec((1, d)), _const_spec((d, d)), _const_spec((d, d))],
        out_specs=[blk, blk, blk, blk],
        out_shape=[jax.ShapeDtypeStruct((bsz, n, d), F32)] * 2 + [jax.ShapeDtypeStruct((bsz, n, d), BF16)] * 2,
        compiler_params=_params("parallel"),
        name="mem_kv",
    )(mem, g, wk, wv)


def _proj_in_kernel(x_ref, g_ref, w_ref, wf_ref, bf_ref, u_ref, q_ref, k_ref, v_ref, kb_ref, vb_ref, lf_ref,
                    *, n_ssm, n_fox, n_heads):
    xb = _rms(x_ref[0], g_ref[...]).astype(BF16)
    z = _dot(xb, w_ref[...])
    u_ref[0] = z[:, :n_ssm]
    o = n_ssm
    q_ref[0] = (z[:, o:o + n_fox] * (FOX_HEAD_DIM ** -0.5 * math.log2(math.e))).astype(BF16)
    k = z[:, o + n_fox:o + 2 * n_fox]
    v = z[:, o + 2 * n_fox:o + 3 * n_fox]
    k_ref[0] = k
    v_ref[0] = v
    kb_ref[0] = k.astype(BF16)
    vb_ref[0] = v.astype(BF16)
    zf = _dot(xb, wf_ref[...])
    lf_ref[0] = _log_sigmoid(zf + bf_ref[...])[:, :n_heads]


def _proj_in(x, g, w_main, w_f, b_f, tm, n_ssm, n_fox, n_heads):
    bsz, L, d = x.shape
    nmain = w_main.shape[1]
    row = lambda n: pl.BlockSpec((1, tm, n), lambda b, i: (b, i, 0))
    outs = [jax.ShapeDtypeStruct((bsz, L, n_ssm), F32), jax.ShapeDtypeStruct((bsz, L, n_fox), BF16),
            jax.ShapeDtypeStruct((bsz, L, n_fox), F32), jax.ShapeDtypeStruct((bsz, L, n_fox), F32),
            jax.ShapeDtypeStruct((bsz, L, n_fox), BF16), jax.ShapeDtypeStruct((bsz, L, n_fox), BF16),
            jax.ShapeDtypeStruct((bsz, L, n_heads), F32)]
    return pl.pallas_call(
        functools.partial(_proj_in_kernel, n_ssm=n_ssm, n_fox=n_fox, n_heads=n_heads),
        grid=(bsz, L // tm),
        in_specs=[row(d), _const_spec((1, d)), _const_spec((d, nmain)), _const_spec((d, LANES)),
                  _const_spec((1, LANES))],
        out_specs=[row(n_ssm), row(n_fox), row(n_fox), row(n_fox), row(n_fox), row(n_fox), row(n_heads)],
        out_shape=outs,
        compiler_params=_params("parallel", "parallel"),
        name="proj_in",
    )(x, g, w_main, w_f, b_f)


def _decay_kernel(lft_ref, c0_ref, dt_ref, car_ref, *, tl):
    @pl.when(pl.program_id(1) == 0)
    def _():
        car_ref[...] = c0_ref[0]
    r = lax.broadcasted_iota(jnp.int32, (tl, tl), 0)
    c = lax.broadcasted_iota(jnp.int32, (tl, tl), 1)
    tri = (r <= c).astype(F32)
    d = jnp.dot(lft_ref[0], tri, preferred_element_type=F32, precision=_HIGHEST) + car_ref[...]
    car_ref[...] = d[:, tl - 1:tl]
    rest = d * math.log2(math.e)
    for piece in range(3):
        part = rest.astype(BF16).astype(F32)
        dt_ref[0, piece] = part
        rest = rest - part


def _decay_cumsum(lft, c0, tl):
    bsz, nh, L = lft.shape
    return pl.pallas_call(
        functools.partial(_decay_kernel, tl=tl),
        grid=(bsz, L // tl),
        in_specs=[pl.BlockSpec((1, nh, tl), lambda b, i: (b, 0, i)), pl.BlockSpec((1, nh, 1), lambda b, i: (b, 0, 0))],
        out_specs=pl.BlockSpec((1, 3, nh, tl), lambda b, i: (b, 0, 0, i)),
        out_shape=jax.ShapeDtypeStruct((bsz, 3, nh, L), F32),
        scratch_shapes=[pltpu.VMEM((nh, 1), F32)],
        compiler_params=_params("parallel", "arbitrary"),
        name="decay_cumsum",
    )(lft, c0)


def _s5_kernel(u_ref, h0_ref, ar_ref, ai_ref, wb_ref, wc_ref, dsk_ref, wglu_ref, bglu_ref, gout_ref,
               y_ref, hl_ref, hs_ref, hst_ref, *, t_chunk, bsz, n_slab):
    @pl.when(pl.program_id(0) == 0)
    def _():
        hst_ref[...] = h0_ref[...]

    u = u_ref[...]
    ub = u.astype(BF16)
    slab_per_k = MXU_DIM // (2 * SSM_GROUP)

    def project(j):
        kt = j // slab_per_k
        hs_ref[:, MXU_DIM * j:MXU_DIM * (j + 1)] = _dot(ub[:, MXU_DIM * kt:MXU_DIM * (kt + 1)], wb_ref[j])

    def scan(j):
        lo, mid, hi = MXU_DIM * j, MXU_DIM * j + LANES, MXU_DIM * (j + 1)
        ar = ar_ref[:, LANES * j:LANES * (j + 1)]
        ai = ai_ref[:, LANES * j:LANES * (j + 1)]
        re, im = hst_ref[:, lo:mid], hst_ref[:, mid:hi]
        for t in range(t_chunk):
            rows = slice(t * bsz, (t + 1) * bsz)
            re, im = (ar * re - ai * im + hs_ref[rows, lo:mid], ar * im + ai * re + hs_ref[rows, mid:hi])
            hs_ref[rows, lo:mid] = re
            hs_ref[rows, mid:hi] = im
        hst_ref[:, lo:mid] = re
        hst_ref[:, mid:hi] = im

    group = 4
    for j in range(group):
        project(j)
    for g0 in range(0, n_slab, group):
        for j in range(g0 + group, min(g0 + 2 * group, n_slab)):
            project(j)
        for j in range(g0, g0 + group):
            scan(j)
    hl_ref[...] = hst_ref[...]

    halves = []
    for hf in range(n_slab // slab_per_k):
        acc = None
        for jj in range(slab_per_k):
            j = hf * slab_per_k + jj
            d = _dot(hs_ref[:, MXU_DIM * j:MXU_DIM * (j + 1)].astype(BF16), wc_ref[j])
            acc = d if acc is None else acc + d
        halves.append(acc)
    y = jnp.concatenate(halves, axis=1) + dsk_ref[...] * u
    y = _gelu_tanh(y)
    y = y * _sigmoid(_dot(y.astype(BF16), wglu_ref[...]) + bglu_ref[...])
    y_ref[...] = _rms(y, gout_ref[...]).astype(BF16)


def _s5(u_tb, h0, ar, ai, wb, wc, dsk, wglu, bglu, gout, t_chunk, bsz):
    rows, n_ssm = u_tb.shape
    n_state = h0.shape[1]
    n_slab = n_state // MXU_DIM
    r = t_chunk * bsz
    return pl.pallas_call(
        functools.partial(_s5_kernel, t_chunk=t_chunk, bsz=bsz, n_slab=n_slab),
        grid=(rows // r,),
        in_specs=[pl.BlockSpec((r, n_ssm), lambda c: (c, 0)), _const_spec((bsz, n_state)),
                  _const_spec((bsz, n_state // 2)), _const_spec((bsz, n_state // 2)),
                  _const_spec((n_slab, MXU_DIM, MXU_DIM)), _const_spec((n_slab, MXU_DIM, MXU_DIM)),
                  _const_spec((1, n_ssm)), _const_spec((n_ssm, n_ssm)), _const_spec((1, n_ssm)),
                  _const_spec((1, n_ssm))],
        out_specs=[pl.BlockSpec((r, n_ssm), lambda c: (c, 0)), _const_spec((bsz, n_state))],
        out_shape=[jax.ShapeDtypeStruct((rows, n_ssm), BF16), jax.ShapeDtypeStruct((bsz, n_state), F32)],
        scratch_shapes=[pltpu.VMEM((r, n_state), F32), pltpu.VMEM((bsz, n_state), F32)],
        compiler_params=_params("arbitrary"),
        name="s5_mixer",
    )(u_tb, h0, ar, ai, wb, wc, dsk, wglu, bglu, gout)


def _s5_tables(lam_re, lam_im, log_dt, b_re, b_im, c_re, c_im, bsz):
    G, P = lam_re.shape
    H = b_re.shape[-1]
    lr, li = lam_re.astype(F32), lam_im.astype(F32)
    dt = jnp.exp(log_dt.astype(F32))[:, None]
    mag = jnp.exp(lr * dt)
    a_re, a_im = mag * jnp.cos(li * dt), mag * jnp.sin(li * dt)
    den = lr * lr + li * li
    c_r = ((a_re - 1.0) * lr + a_im * li) / den
    c_i = (a_im * lr - (a_re - 1.0) * li) / den
    br, bi = b_re.astype(F32), b_im.astype(F32)
    bbar_re = c_r[..., None] * br - c_i[..., None] * bi
    bbar_im = c_r[..., None] * bi + c_i[..., None] * br
    n_pair = G // 2
    per_k = MXU_DIM // (2 * H)
    eye2 = jnp.eye(2, dtype=F32)
    place = jax.nn.one_hot(jnp.arange(n_pair) % per_k, per_k, dtype=F32)

    ar = jnp.broadcast_to(a_re.reshape(1, -1), (bsz, G * P))
    ai = jnp.broadcast_to(a_im.reshape(1, -1), (bsz, G * P))

    bb = jnp.stack([bbar_re, bbar_im]).reshape(2, n_pair, 2, P, H)
    wpair = jnp.einsum("ajgph,gk->jghakp", bb, eye2).reshape(n_pair, 2 * H, MXU_DIM)
    wb = jnp.einsum("jrc,jk->jkrc", wpair, place).reshape(n_pair, MXU_DIM, MXU_DIM)

    cc = jnp.stack([c_re.astype(F32), -c_im.astype(F32)]).reshape(2, n_pair, 2, H, P)
    cpair = jnp.einsum("ajghp,gk->jagpkh", cc, eye2).reshape(n_pair, MXU_DIM, 2 * H)
    wc = jnp.einsum("jnc,jk->jnkc", cpair, place).reshape(n_pair, MXU_DIM, MXU_DIM)
    return ar, ai, wb.astype(BF16), wc.astype(BF16)


def _state_to_lanes(re, im):
    bsz, G, P = re.shape
    s = jnp.stack([re, im], axis=1).reshape(bsz, 2, G // 2, 2, P)
    return s.transpose(0, 2, 1, 3, 4).reshape(bsz, 2 * G * P)


def _lanes_to_state(h, G, P):
    bsz = h.shape[0]
    s = h.reshape(bsz, G // 2, 2, 2, P).transpose(0, 2, 1, 3, 4).reshape(bsz, 2, G, P)
    return s[:, 0], s[:, 1]


FOX_AUG_ROWS = 16


def _fox_kernel(qt_ref, qa_ref, k_ref, vt_ref, o_ref, m_ref, l_ref, acc_ref, *, tq, tk, past):
    qi = pl.program_id(2)
    qt = qt_ref[0]
    row = lax.broadcasted_iota(jnp.int32, (LANES, tq), 0)
    zero = jnp.zeros_like(qt)
    pad = jnp.zeros((LANES - FOX_AUG_ROWS, tq), BF16)
    qts = []
    for hh in range(2):
        own = (row < FOX_HEAD_DIM) if hh == 0 else (row >= FOX_HEAD_DIM)
        aug = qa_ref[0, 0, FOX_AUG_ROWS * hh:FOX_AUG_ROWS * (hh + 1), :]
        qts.append(jnp.concatenate([jnp.where(own, qt, zero), aug, pad], axis=0))
    m_ref[...] = jnp.full(m_ref.shape, -1e30, F32)
    l_ref[...] = jnp.zeros(l_ref.shape, F32)
    acc_ref[...] = jnp.zeros(acc_ref.shape, F32)
    q_start = past + qi * tq
    n_full = (q_start + 1) // tk
    n_all = (q_start + tq + tk - 1) // tk

    def block(j, masked, width):
        ks = pl.multiple_of(j * width, width)
        kb = k_ref[0, 0, pl.ds(ks, width), :]
        if masked:
            kpos = ks + lax.broadcasted_iota(jnp.int32, (width, tq), 0)
            qpos = q_start + lax.broadcasted_iota(jnp.int32, (width, tq), 1)
            visible = kpos <= qpos
        for hh in range(2):
            st = _dot(kb, qts[hh])
            if masked:
                st = jnp.where(visible, st, -jnp.inf)
            m_old = m_ref[hh]
            m_new = jnp.maximum(m_old, jnp.max(st, axis=0, keepdims=True))
            p = jnp.exp2(st - m_new)
            alpha = jnp.exp2(m_old - m_new)
            l_ref[hh] = alpha * l_ref[hh] + jnp.sum(p, axis=0, keepdims=True)
            vt = vt_ref[0, FOX_HEAD_DIM * hh:FOX_HEAD_DIM * (hh + 1), pl.ds(ks, width)]
            acc_ref[hh] = alpha * acc_ref[hh] + _dot(vt, p.astype(BF16))
            m_ref[hh] = m_new

    def steps(lo, hi, masked, width):
        def body(j, c):
            block(j, masked, width)
            return c
        lax.fori_loop(lo, hi, body, 0)

    n_wide = 0
    if 2 * tk <= k_ref.shape[2]:
        n_wide = q_start // (2 * tk)
        steps(0, n_wide, False, 2 * tk)
    steps(2 * n_wide, n_full, False, tk)
    steps(n_full, n_all, True, tk)
    out_t = jnp.concatenate([acc_ref[0] / l_ref[0], acc_ref[1] / l_ref[1]], axis=0)
    o_ref[0] = out_t.T


def _fox(q_t, q_aug, k_aug, v_t, tq, tk, past):
    bsz, n_fox, L = q_t.shape
    n_pair = n_fox // LANES
    lk = k_aug.shape[2]
    return pl.pallas_call(
        functools.partial(_fox_kernel, tq=tq, tk=tk, past=past),
        grid=(bsz, n_pair, L // tq),
        in_specs=[pl.BlockSpec((1, LANES, tq), lambda b, h, i: (b, h, i)),
                  pl.BlockSpec((1, 1, 2 * FOX_AUG_ROWS, tq), lambda b, h, i: (b, h, 0, i)),
                  pl.BlockSpec((1, 1, lk, 2 * LANES), lambda b, h, i: (b, h, 0, 0)),
                  pl.BlockSpec((1, LANES, lk), lambda b, h, i: (b, h, 0))],
        out_specs=pl.BlockSpec((1, tq, LANES), lambda b, h, i: (b, i, h)),
        out_shape=jax.ShapeDtypeStruct((bsz, L, n_fox), F32),
        scratch_shapes=[pltpu.VMEM((2, 1, tq), F32), pltpu.VMEM((2, 1, tq), F32),
                        pltpu.VMEM((2, FOX_HEAD_DIM, tq), F32)],
        compiler_params=_params("parallel", "parallel", "arbitrary"),
        name="fox_attention",
    )(q_t, q_aug, k_aug, v_t)


def _fox_operands(qb, k_all, v_all, d_pieces, past, L):
    bsz, lk_pad, n_fox = k_all.shape
    n_heads = n_fox // FOX_HEAD_DIM
    n_pair = n_heads // 2
    pieces = d_pieces.astype(BF16)
    dk = (-pieces).reshape(bsz, 3, n_pair, 2, lk_pad)
    dk = dk.transpose(0, 2, 4, 3, 1).reshape(bsz, n_pair, lk_pad, 6)
    ones_k = jnp.ones((bsz, n_pair, lk_pad, 3), BF16)
    zeros_k = jnp.zeros((bsz, n_pair, lk_pad, LANES - 9), BF16)
    k_pair = k_all.reshape(bsz, lk_pad, n_pair, LANES).transpose(0, 2, 1, 3)
    k_aug = jnp.concatenate([k_pair, dk, ones_k, zeros_k], axis=-1)

    dq = pieces[:, :, :, past:past + L].transpose(0, 2, 1, 3)
    sel = jax.nn.one_hot(jnp.arange(n_heads) % 2, 2, dtype=BF16)
    ones_q = jnp.broadcast_to(jnp.repeat(sel, 3, axis=1)[None, :, :, None], (bsz, n_heads, 6, L))
    zeros_q = jnp.zeros((bsz, n_heads, FOX_AUG_ROWS - 9, L), BF16)
    q_aug = jnp.concatenate([ones_q, dq, zeros_q], axis=2).reshape(bsz, n_pair, 2 * FOX_AUG_ROWS, L)
    return qb.transpose(0, 2, 1), q_aug, k_aug, v_all.transpose(0, 2, 1)


def _route_gates(logits_t, ebias_t, n_experts):
    per_group = n_experts // N_EXPERT_GROUPS
    tokens = logits_t.shape[1]
    score = _sigmoid(logits_t)
    sel = score + ebias_t
    row = lax.broadcasted_iota(jnp.int32, (per_group, tokens), 0).astype(F32)
    neg = jnp.float32(-jnp.inf)

    def first_argmax(tile, best):
        return jnp.min(jnp.where(tile == best, row, float(per_group)), axis=0, keepdims=True)

    score_g, sel_g, gsc = [], [], []
    for g in range(N_EXPERT_GROUPS):
        sg = sel[per_group * g:per_group * (g + 1), :]
        score_g.append(score[per_group * g:per_group * (g + 1), :])
        sel_g.append(sg)
        m1 = jnp.max(sg, axis=0, keepdims=True)
        rest = jnp.where(row == first_argmax(sg, m1), neg, sg)
        gsc.append(m1 + jnp.max(rest, axis=0, keepdims=True))
    cur = []
    for g in range(N_EXPERT_GROUPS):
        ahead = jnp.zeros((1, tokens), F32)
        for o in range(N_EXPERT_GROUPS):
            if o == g:
                continue
            beats = (gsc[o] >= gsc[g]) if o < g else (gsc[o] > gsc[g])
            ahead = ahead + jnp.where(beats, 1.0, 0.0)
        cur.append(jnp.where(ahead < TOPK_GROUPS, sel_g[g], neg))
    chosen = [jnp.zeros((per_group, tokens), jnp.bool_) for _ in range(N_EXPERT_GROUPS)]
    for _ in range(TOP_K):
        best = cur[0]
        for g in range(1, N_EXPERT_GROUPS):
            best = jnp.maximum(best, cur[g])
        best = jnp.max(best, axis=0, keepdims=True)
        idx = None
        for g in range(N_EXPERT_GROUPS):
            cand = jnp.min(jnp.where(cur[g] == best, row + float(per_group * g), float(n_experts)), axis=0,
                           keepdims=True)
            idx = cand if idx is None else jnp.minimum(idx, cand)
        for g in range(N_EXPERT_GROUPS):
            hit = (row + float(per_group * g)) == idx
            chosen[g] = chosen[g] | hit
            cur[g] = jnp.where(hit, neg, cur[g])
    w = [jnp.where(chosen[g], score_g[g], 0.0) for g in range(N_EXPERT_GROUPS)]
    total = w[0]
    for g in range(1, N_EXPERT_GROUPS):
        total = total + w[g]
    total = jnp.sum(total, axis=0, keepdims=True)
    return [w[g] / total * ROUTED_SCALE for g in range(N_EXPERT_GROUPS)]


def _mid_kernel(x_ref, ys_ref, yf_ref, gfox_ref, wout_ref, gmq_ref, wmq_ref, mk_ref, mv_ref, wmo_ref, gffn_ref,
                wrt_ref, ebt_ref, ws1_ref, ws3_ref, ws2_ref, xs_ref, xn_ref, gt_ref, cnt_ref, *, n_experts):
    x = x_ref[0]
    yfn = _rms(yf_ref[0], gfox_ref[...]).astype(BF16)
    mix = jnp.concatenate([ys_ref[0], yfn], axis=1)
    x1 = x + _dot(mix, wout_ref[...])

    qm = _dot(_rms(x1, gmq_ref[...]).astype(BF16), wmq_ref[...])
    hd = qm.shape[1] // MEM_HEADS
    heads = []
    for h in range(MEM_HEADS):
        qh = (qm[:, hd * h:hd * (h + 1)] * (hd ** -0.5)).astype(BF16)
        s = lax.dot_general(qh, mk_ref[0, :, hd * h:hd * (h + 1)], _NT, preferred_element_type=F32)
        p = jnp.exp(s - jnp.max(s, axis=1, keepdims=True))
        o = _dot(p.astype(BF16), mv_ref[0, :, hd * h:hd * (h + 1)]) / jnp.sum(p, axis=1, keepdims=True)
        heads.append(o.astype(BF16))
    x2 = x1 + _dot(jnp.concatenate(heads, axis=1), wmo_ref[...])

    xn = _rms(x2, gffn_ref[...])
    xnb = xn.astype(BF16)
    hidden = _silu(_dot(xnb, ws1_ref[...])) * _dot(xnb, ws3_ref[...])
    xs_ref[0] = x2 + _dot(hidden.astype(BF16), ws2_ref[...])
    xn_ref[0] = xnb

    wr = wrt_ref[...]
    wr_hi = wr.astype(BF16)
    wr_lo = (wr - wr_hi.astype(F32)).astype(BF16)
    xn_lo = (xn - xnb.astype(F32)).astype(BF16)
    nt = lambda a, c: lax.dot_general(a, c, _NT, preferred_element_type=F32)
    logits_t = nt(wr_hi, xnb) + (nt(wr_hi, xn_lo) + nt(wr_lo, xnb))
    gates = _route_gates(logits_t, ebt_ref[...], n_experts)
    per_group = n_experts // N_EXPERT_GROUPS
    for g in range(N_EXPERT_GROUPS):
        gt_ref[0, per_group * g:per_group * (g + 1), :] = gates[g]
        cnt_ref[0, 0, per_group * g:per_group * (g + 1), :] = jnp.sum(
            jnp.where(gates[g] != 0.0, 1.0, 0.0), axis=1, keepdims=True)


def _mid(x, ys, yf, gfox, wout, gmq, wmq, mkb, mvb, wmo, gffn, wrt, ebt, ws1, ws3, ws2, tm):
    bsz, L, d = x.shape
    n_ssm, n_fox = ys.shape[2], yf.shape[2]
    n_mem = mkb.shape[1]
    n_experts = wrt.shape[0]
    fs = ws1.shape[1]
    row = lambda n: pl.BlockSpec((1, tm, n), lambda b, i: (b, i, 0))
    memspec = pl.BlockSpec((1, n_mem, d), lambda b, i: (b, 0, 0))
    return pl.pallas_call(
        functools.partial(_mid_kernel, n_experts=n_experts),
        grid=(bsz, L // tm),
        in_specs=[row(d), row(n_ssm), row(n_fox), _const_spec((1, n_fox)),
                  _const_spec((d, d)), _const_spec((1, d)),
                  _const_spec((d, d)), memspec, memspec, _const_spec((d, d)), _const_spec((1, d)),
                  _const_spec((n_experts, d)), _const_spec((n_experts, 1)), _const_spec((d, fs)), _const_spec((d, fs)),
                  _const_spec((fs, d))],
        out_specs=[row(d), row(d), pl.BlockSpec((1, n_experts, tm), lambda b, i: (b, 0, i)),
                   pl.BlockSpec((1, 1, n_experts, 1), lambda b, i: (b, i, 0, 0))],
        out_shape=[jax.ShapeDtypeStruct((bsz, L, d), F32), jax.ShapeDtypeStruct((bsz, L, d), BF16),
                   jax.ShapeDtypeStruct((bsz, n_experts, L), F32),
                   jax.ShapeDtypeStruct((bsz, L // tm, n_experts, 1), F32)],
        compiler_params=_params("parallel", "parallel"),
        name="mid_block",
    )(x, ys, yf, gfox, wout, gmq, wmq, mkb, mvb, wmo, gffn, wrt, ebt, ws1, ws3, ws2)


SEG_ROWS = 16
MOE_TOKENS = 512
COPY_ROWS = 4 * SEG_ROWS
GATHER_ROWS = 1024
_TN = (((0,), (0,)), ((), ()))
_NO_RANK = -(1 << 20)


def _sorted_rows(tb, n_experts):
    rows = TOP_K * tb + n_experts * (SEG_ROWS - 1)
    return -(-rows // GATHER_ROWS) * GATHER_ROWS


def _expert_tile(m):
    return 1024 if m >= 4096 else 128


def _dispatch_plan(cnt, ns, MOE_TILE):
    nblk, n_experts = cnt.shape

    def before(a, axis):
        n = a.shape[axis]
        earlier = jnp.arange(n)[:, None] < jnp.arange(n)[None, :]
        if axis == 0:
            return jnp.sum(jnp.where(earlier[:, :, None], a[:, None, :], 0), axis=0)
        return jnp.sum(jnp.where(earlier[None, :, :], a[:, :, None], 0), axis=1)

    cp = (cnt + SEG_ROWS - 1) // SEG_ROWS * SEG_ROWS
    o_loc = before(cp, 1)
    used = jnp.sum(cp, axis=1)
    tot_e = jnp.sum(cp, axis=0)
    reg_e = (tot_e + MOE_TILE - 1) // MOE_TILE * MOE_TILE
    base_e = before(reg_e[None, :], 1)[0]
    reg_end = base_e + reg_e
    dst = base_e[None, :] + before(cp, 0)
    rows_max = nblk * ns + n_experts * MOE_TILE
    n_tiles_max = -(-rows_max // MOE_TILE)
    n_tiles = reg_end[-1] // MOE_TILE
    tile_idx = jnp.clip(jnp.arange(n_tiles_max, dtype=jnp.int32), 0, jnp.maximum(n_tiles - 1, 0))
    tile_expert = jnp.sum((reg_end[None, :] <= (tile_idx * MOE_TILE)[:, None]).astype(jnp.int32), axis=1)
    tile_expert = jnp.minimum(tile_expert, n_experts - 1)
    n_pc = ns // SEG_ROWS
    piece_row = jnp.arange(n_pc, dtype=jnp.int32) * SEG_ROWS
    seg_end = o_loc + cp
    piece_e = jnp.sum((seg_end[:, None, :] <= piece_row[None, :, None]).astype(jnp.int32), axis=2)
    live = piece_row[None, :] < used[:, None]
    piece_e = jnp.minimum(piece_e, n_experts - 1)
    own = piece_e[:, :, None] == jnp.arange(n_experts, dtype=jnp.int32)[None, None, :]
    pick = lambda table: jnp.sum(jnp.where(own, table[:, None, :], 0), axis=2)
    rank0 = piece_row[None, :] - pick(o_loc)
    piece_rank = jnp.where(live, rank0, _NO_RANK)
    i32 = lambda a: a.astype(jnp.int32).reshape(-1)
    per_block = lambda a: a.astype(jnp.int32).reshape(nblk, 1, -1)

    def copy_list(n_e, first_rank, rows_each, n_max):
        end = before(n_e, 1) + n_e
        j = jnp.arange(n_max, dtype=jnp.int32)
        e_of = jnp.minimum(jnp.sum((end[:, None, :] <= j[None, :, None]).astype(jnp.int32), axis=2), n_experts - 1)
        mine = e_of[:, :, None] == jnp.arange(n_experts, dtype=jnp.int32)[None, None, :]
        take = lambda table: jnp.sum(jnp.where(mine, table[:, None, :], 0), axis=2)
        rank = take(first_rank) + (j[None, :] - take(end - n_e)) * rows_each
        ok = j[None, :] < jnp.sum(n_e, axis=1)[:, None]
        return (per_block(jnp.where(ok, take(o_loc) + rank, 0)), per_block(jnp.where(ok, take(dst) + rank, 0)),
                i32(jnp.sum(n_e, axis=1)))

    per_copy = COPY_ROWS // SEG_ROWS
    n_wide_e = (cp // SEG_ROWS) // per_copy
    n_seg_e = cp // SEG_ROWS - n_wide_e * per_copy
    wide_src, wide_dst, n_wide = copy_list(n_wide_e, jnp.zeros_like(cp), COPY_ROWS, ns // COPY_ROWS)
    seg_src, seg_dst, n_seg = copy_list(n_seg_e, n_wide_e * COPY_ROWS, SEG_ROWS, n_experts * (per_copy - 1))
    return dict(n_piece=i32(used // SEG_ROWS), piece_e=per_block(piece_e), piece_rank=per_block(piece_rank),
                wide_src=wide_src, wide_dst=wide_dst, n_wide=n_wide, seg_src=seg_src, seg_dst=seg_dst, n_seg=n_seg,
                tail_start=i32(base_e + tot_e),
                tail_chunks=i32((reg_e - tot_e) // SEG_ROWS), tile_idx=i32(tile_idx), tile_expert=i32(tile_expert),
                n_tiles=i32(n_tiles), n_tiles_max=n_tiles_max)


def _slot_ranks(gates, tb):
    sel = gates != 0.0
    r = lax.broadcasted_iota(jnp.int32, (tb, tb), 0)
    c = lax.broadcasted_iota(jnp.int32, (tb, tb), 1)
    earlier = jnp.where(r < c, 1.0, 0.0).astype(BF16)
    rank = _dot(jnp.where(sel, 1.0, 0.0).astype(BF16), earlier)
    return jnp.where(sel, rank, -1.0)


def _build_one_hot(out_ref, rm_ref, pe_ref, pr_ref, first, count, tb, value_row):
    rows = lax.broadcasted_iota(jnp.int32, (SEG_ROWS, tb), 0).astype(F32)
    for i in range(first, first + count):
        e = pe_ref[0, 0, i]
        hit = rows == (rm_ref[pl.ds(e, 1), :] - pr_ref[0, 0, i].astype(F32))
        out_ref[SEG_ROWS * i:SEG_ROWS * (i + 1), :] = jnp.where(hit, value_row(e), 0.0).astype(BF16)


def _wait_rows(n_rows, make_copy):
    wide = 16 * SEG_ROWS
    n_wide = n_rows // wide

    def wide_step(i, c):
        make_copy(wide).wait()
        return c

    def seg_step(i, c):
        make_copy(SEG_ROWS).wait()
        return c

    lax.fori_loop(0, n_wide, wide_step, 0)
    lax.fori_loop(0, (n_rows - n_wide * wide) // SEG_ROWS, seg_step, 0)


def _issue_copies(n_wide, n_seg, tables, make_copy):
    wide_src, wide_dst, seg_src, seg_dst = tables

    def wide(i, c):
        make_copy(pl.multiple_of(wide_src[0, 0, i], SEG_ROWS), pl.multiple_of(wide_dst[0, 0, i], SEG_ROWS),
                  COPY_ROWS).start()
        return c

    def seg(i, c):
        make_copy(pl.multiple_of(seg_src[0, 0, i], SEG_ROWS), pl.multiple_of(seg_dst[0, 0, i], SEG_ROWS),
                  SEG_ROWS).start()
        return c

    lax.fori_loop(0, n_wide, wide, 0)
    lax.fori_loop(0, n_seg, seg, 0)


def _dispatch_kernel(n_piece_ref, n_wide_ref, n_seg_ref, tail_start_ref, tail_chunks_ref, pe_ref, pr_ref,
                     ws_ref, wd_ref, ss_ref, sd_ref,
                     xn_ref, gt_ref, xg_ref, g_ref, xsb_ref, rm_ref, z_ref, sem, *, tb, ns, n_experts, nblk):
    b = pl.program_id(0)
    n_pc = ns // SEG_ROWS
    used = n_piece_ref[b] * SEG_ROWS
    rm_ref[...] = _slot_ranks(gt_ref[...], tb)
    per_slice = GATHER_ROWS // SEG_ROWS
    n_slices = ns // GATHER_ROWS
    build = lambda t: _build_one_hot(g_ref, rm_ref, pe_ref, pr_ref, per_slice * t, per_slice, tb, lambda e: 1.0)

    slot = b % 2
    build(0)
    for t in range(n_slices):
        if t + 1 < n_slices:
            build(t + 1)
        sl = slice(GATHER_ROWS * t, GATHER_ROWS * (t + 1))
        xsb_ref[slot, sl, :] = _dot(g_ref[sl, :], xn_ref[...]).astype(BF16)

    def rows_copy(buf, src_row, dst_row, size):
        return pltpu.make_async_copy(xsb_ref.at[buf, pl.ds(src_row, size)], xg_ref.at[pl.ds(dst_row, size)],
                                     sem.at[buf])

    _issue_copies(n_wide_ref[b], n_seg_ref[b], (ws_ref, wd_ref, ss_ref, sd_ref),
                  lambda block_row, global_row, size: rows_copy(slot, block_row, global_row, size))

    @pl.when(b > 0)
    def _():
        _wait_rows(n_piece_ref[b - 1] * SEG_ROWS, lambda size: rows_copy(1 - slot, 0, 0, size))

    @pl.when(b == nblk - 1)
    def _():
        _wait_rows(used, lambda size: rows_copy(slot, 0, 0, size))
        z_ref[...] = jnp.zeros(z_ref.shape, BF16)

        wide = z_ref.shape[0]
        per_wide = wide // SEG_ROWS

        def tail_copy(row, size):
            return pltpu.make_async_copy(z_ref.at[pl.ds(0, size)], xg_ref.at[pl.ds(pl.multiple_of(row, SEG_ROWS), size)],
                                         sem.at[slot])

        def each_tail(action):
            def per_expert(e, carry):
                n_wide = tail_chunks_ref[e] // per_wide

                def wide_piece(i, c2):
                    action(tail_copy(tail_start_ref[e] + i * wide, wide))
                    return c2

                def seg_piece(i, c2):
                    action(tail_copy(tail_start_ref[e] + n_wide * wide + i * SEG_ROWS, SEG_ROWS))
                    return c2

                lax.fori_loop(0, n_wide, wide_piece, 0)
                lax.fori_loop(0, tail_chunks_ref[e] - n_wide * per_wide, seg_piece, 0)
                return carry
            lax.fori_loop(0, n_experts, per_expert, 0)

        each_tail(lambda cp: cp.start())
        each_tail(lambda cp: cp.wait())


def _expert_kernel(tile_idx_ref, tile_expert_ref, n_tiles_ref, x_ref, w1_ref, w3_ref, w2_ref, o_ref):
    @pl.when(pl.program_id(0) < n_tiles_ref[0])
    def _():
        x = x_ref[...]
        h = _silu(_dot(x, w1_ref[0].astype(BF16))) * _dot(x, w3_ref[0].astype(BF16))
        o_ref[...] = _dot(h.astype(BF16), w2_ref[0].astype(BF16)).astype(BF16)


def _combine_kernel(n_piece_ref, n_wide_ref, n_seg_ref, pe_ref, pr_ref, ws_ref, wd_ref, ss_ref, sd_ref,
                    ws_next_ref, wd_next_ref, ss_next_ref, sd_next_ref,
                    og_ref, gt_ref, xs_ref, gfin_ref, y_ref, gw_ref, ob_ref, rm_ref, sem, *, tb, ns, n_experts, nblk):
    b = pl.program_id(0)
    slot = b % 2
    n_pc = ns // SEG_ROWS

    def rows_copy(buf, src_row, dst_row, size):
        return pltpu.make_async_copy(og_ref.at[pl.ds(src_row, size)], ob_ref.at[buf, pl.ds(dst_row, size)],
                                     sem.at[buf])

    def fetch_block(blk, tables, buf):
        _issue_copies(n_wide_ref[blk], n_seg_ref[blk], tables,
                      lambda block_row, global_row, size: rows_copy(buf, global_row, block_row, size))

    @pl.when(b == 0)
    def _():
        fetch_block(b, (ws_ref, wd_ref, ss_ref, sd_ref), slot)

    @pl.when(b + 1 < nblk)
    def _():
        fetch_block(jnp.minimum(b + 1, nblk - 1), (ws_next_ref, wd_next_ref, ss_next_ref, sd_next_ref), 1 - slot)

    rm_ref[...] = _slot_ranks(gt_ref[...], tb)
    _wait_rows(n_piece_ref[b] * SEG_ROWS, lambda size: rows_copy(slot, 0, 0, size))

    def clear(i, c):
        ob_ref[slot, pl.ds(pl.multiple_of((n_piece_ref[b] + i) * SEG_ROWS, SEG_ROWS), SEG_ROWS), :] = jnp.zeros(
            (SEG_ROWS, ob_ref.shape[2]), BF16)
        return c

    lax.fori_loop(0, n_pc - n_piece_ref[b], clear, 0)

    per_slice = GATHER_ROWS // SEG_ROWS
    n_slices = ns // GATHER_ROWS
    build = lambda t: _build_one_hot(gw_ref, rm_ref, pe_ref, pr_ref, per_slice * t, per_slice, tb,
                                     lambda e: gt_ref[pl.ds(e, 1), :])
    build(0)
    y = xs_ref[...]
    for t in range(n_slices):
        if t + 1 < n_slices:
            build(t + 1)
        sl = slice(GATHER_ROWS * t, GATHER_ROWS * (t + 1))
        y = y + lax.dot_general(gw_ref[sl, :], ob_ref[slot, sl, :], _TN, preferred_element_type=F32)
    y_ref[...] = _rms(y, gfin_ref[...])


def _moe(xn, gates_t, cnt, xs, w1, w3, w2, gfin):
    m, d = xn.shape
    n_experts, _, f = w1.shape
    tb = min(m, MOE_TOKENS)
    nblk = m // tb
    ns = _sorted_rows(tb, n_experts)
    MOE_TILE = _expert_tile(m)
    plan = _dispatch_plan(cnt, ns, MOE_TILE)
    n_tiles_max = plan["n_tiles_max"]
    rows_max = n_tiles_max * MOE_TILE
    this_block = lambda t: pl.BlockSpec((1, 1, t.shape[2]), lambda b, *_: (b, 0, 0), memory_space=pltpu.SMEM)
    next_block = lambda t: pl.BlockSpec((1, 1, t.shape[2]), lambda b, *_: (jnp.minimum(b + 1, nblk - 1), 0, 0),
                                        memory_space=pltpu.SMEM)
    build_tables = (plan["piece_e"], plan["piece_rank"])
    copy_tables = (plan["wide_src"], plan["wide_dst"], plan["seg_src"], plan["seg_dst"])
    counts = (plan["n_piece"], plan["n_wide"], plan["n_seg"])

    xg = pl.pallas_call(
        functools.partial(_dispatch_kernel, tb=tb, ns=ns, n_experts=n_experts, nblk=nblk),
        grid_spec=pltpu.PrefetchScalarGridSpec(
            num_scalar_prefetch=5, grid=(nblk,),
            in_specs=[this_block(t) for t in build_tables + copy_tables] + [
                      pl.BlockSpec((tb, d), lambda b, *_: (b, 0)), pl.BlockSpec((n_experts, tb), lambda b, *_: (0, b))],
            out_specs=pl.BlockSpec(memory_space=pl.ANY),
            scratch_shapes=[pltpu.VMEM((ns, tb), BF16), pltpu.VMEM((2, ns, d), BF16), pltpu.VMEM((n_experts, tb), F32),
                            pltpu.VMEM((16 * SEG_ROWS, d), BF16), pltpu.SemaphoreType.DMA((2,))]),
        out_shape=jax.ShapeDtypeStruct((rows_max, d), BF16),
        compiler_params=_params("arbitrary"),
        name="moe_dispatch",
    )(*counts, plan["tail_start"], plan["tail_chunks"], *build_tables, *copy_tables, xn, gates_t)

    tile = lambda i, idx, ex, n: (idx[i], 0)
    og = pl.pallas_call(
        _expert_kernel,
        grid_spec=pltpu.PrefetchScalarGridSpec(
            num_scalar_prefetch=3, grid=(n_tiles_max,),
            in_specs=[pl.BlockSpec((MOE_TILE, d), tile),
                      pl.BlockSpec((1, d, f), lambda i, idx, ex, n: (ex[i], 0, 0)),
                      pl.BlockSpec((1, d, f), lambda i, idx, ex, n: (ex[i], 0, 0)),
                      pl.BlockSpec((1, f, d), lambda i, idx, ex, n: (ex[i], 0, 0))],
            out_specs=pl.BlockSpec((MOE_TILE, d), tile)),
        out_shape=jax.ShapeDtypeStruct((rows_max, d), BF16),
        compiler_params=_params("arbitrary"),
        name="moe_experts",
    )(plan["tile_idx"], plan["tile_expert"], plan["n_tiles"], xg, w1, w3, w2)

    return pl.pallas_call(
        functools.partial(_combine_kernel, tb=tb, ns=ns, n_experts=n_experts, nblk=nblk),
        grid_spec=pltpu.PrefetchScalarGridSpec(
            num_scalar_prefetch=3, grid=(nblk,),
            in_specs=[this_block(t) for t in build_tables + copy_tables] + [next_block(t) for t in copy_tables] + [
                      pl.BlockSpec(memory_space=pl.ANY), pl.BlockSpec((n_experts, tb), lambda b, *_: (0, b)),
                      pl.BlockSpec((tb, d), lambda b, *_: (b, 0)), pl.BlockSpec((1, d), lambda b, *_: (0, 0))],
            out_specs=pl.BlockSpec((tb, d), lambda b, *_: (b, 0)),
            scratch_shapes=[pltpu.VMEM((ns, tb), BF16), pltpu.VMEM((2, ns, d), BF16), pltpu.VMEM((n_experts, tb), F32),
                            pltpu.SemaphoreType.DMA((2,))]),
        out_shape=jax.ShapeDtypeStruct((m, d), F32),
        compiler_params=_params("arbitrary"),
        name="moe_combine",
    )(*counts, *build_tables, *copy_tables, *copy_tables, og, gates_t, xs, gfin)


def _tiles(L):
    t_row = min(L, 512)
    t_key = 512
    t_scan = min(L, 64)
    return t_row, t_key, t_scan


def _group(x, h0_re, h0_im, past_k, past_v, past_lf, mkb, mvb, p, w):
    bsz, L, d = x.shape
    n_heads = p["b_f"].shape[0]
    n_fox = n_heads * FOX_HEAD_DIM
    n_ssm = p["d_skip"].shape[0]
    G, P = p["lam_re"].shape
    t_row, t_key, t_scan = _tiles(L)

    u, qb, k, v, kb, vb, lf = _proj_in(x, w["g_mix"], w["w_main"], w["w_f"], w["b_f"], t_row, n_ssm, n_fox, n_heads)

    u_tb = u.transpose(1, 0, 2).reshape(L * bsz, n_ssm)
    if h0_re is None:
        h0 = jnp.zeros((bsz, 2 * G * P), F32)
    else:
        h0 = _state_to_lanes(h0_re.astype(F32), h0_im.astype(F32))
    ys_tb, h_last = _s5(u_tb, h0, w["ar"], w["ai"], w["wb"], w["wc"], w["d_skip"], w["w_glu"], w["b_glu"],
                        w["g_ssm_out"], t_scan, bsz)
    ys = ys_tb.reshape(L, bsz, n_ssm).transpose(1, 0, 2)
    hl_re, hl_im = _lanes_to_state(h_last, G, P)

    if past_k is None:
        past = 0
        lf_all, k_all, v_all = lf, kb, vb
    else:
        past = past_k.shape[1]
        lf_all = jnp.concatenate([past_lf.astype(F32), lf], axis=1)
        k_all = jnp.concatenate([past_k.reshape(bsz, past, n_fox).astype(BF16), kb], axis=1)
        v_all = jnp.concatenate([past_v.reshape(bsz, past, n_fox).astype(BF16), vb], axis=1)
    lk = lf_all.shape[1]
    lk_pad = -(-lk // t_key) * t_key
    lf_t = jnp.pad(lf_all.transpose(0, 2, 1), ((0, 0), (0, 0), (0, lk_pad - lk)))
    d_pieces = _decay_cumsum(lf_t.reshape(1, bsz * n_heads, lk_pad), jnp.zeros((1, bsz * n_heads, 1), F32), t_key)
    d_pieces = d_pieces.reshape(3, bsz, n_heads, lk_pad).transpose(1, 0, 2, 3)
    k_all = jnp.pad(k_all, ((0, 0), (0, lk_pad - lk), (0, 0)))
    v_all = jnp.pad(v_all, ((0, 0), (0, lk_pad - lk), (0, 0)))
    t_att = t_key if L >= t_key else lk_pad
    yf = _fox(*_fox_operands(qb, k_all, v_all, d_pieces, past, L), t_row, t_att, past)

    xs, xnb, gates_t, cnt = _mid(x, ys, yf, w["g_fox_out"], w["w_out"], w["g_mem_q"], w["w_mq"], mkb, mvb, w["w_mo"],
                                 w["g_ffn"], w["w_router_t"], w["e_bias_t"], w["ws1"], w["ws3"], w["ws2"], t_row)
    m = bsz * L
    n_experts = gates_t.shape[1]
    tb = min(m, MOE_TOKENS)
    cnt = cnt.reshape(m // tb, tb // t_row, n_experts).sum(axis=1).astype(jnp.int32)
    y = _moe(xnb.reshape(m, d), gates_t.transpose(1, 0, 2).reshape(n_experts, m), cnt, xs.reshape(m, d),
             w["w1"], w["w3"], w["w2"], w["g_final"])
    return (y.reshape(bsz, L, d), hl_re, hl_im, k.reshape(bsz, L, n_heads, FOX_HEAD_DIM),
            v.reshape(bsz, L, n_heads, FOX_HEAD_DIM), lf)


def kernel(x_prompt, x_sample, state_ssm_re, state_ssm_im, cache_fox_k, cache_fox_v, cache_fox_logf, cache_mem_k, cache_mem_v, mem_prompt, g_mix, w_in, b_f, lam_re, lam_im, log_dt, b_re, b_im, c_re, c_im, d_skip, w_glu, b_glu, g_ssm_out, g_fox_out, w_out, g_mem_q, g_mem_kv, w_mq, w_mk, w_mv, w_mo, g_ffn, w_router, e_bias, w1, w3, w2, ws1, ws3, ws2, g_final):
    depth = w_in.shape[0]
    bsz = x_prompt.shape[0]
    hp, hs = x_prompt, x_sample
    outs_p, outs_s = [], []
    for l in range(depth):
        p = dict(b_f=b_f[l], lam_re=lam_re[l], d_skip=d_skip[l])
        n_heads = b_f.shape[1]
        n_ssm = d_skip.shape[1]
        n_main = w_in.shape[2] - n_heads
        row = lambda a: a.reshape(1, -1).astype(F32)
        ar, ai, wb, wc = _s5_tables(lam_re[l], lam_im[l], log_dt[l], b_re[l], b_im[l], c_re[l], c_im[l], bsz)
        w = dict(
            g_mix=row(g_mix[l]), w_main=w_in[l][:, :n_main].astype(BF16),
            w_f=jnp.pad(w_in[l][:, n_main:], ((0, 0), (0, LANES - n_heads))).astype(BF16),
            b_f=jnp.pad(b_f[l].astype(F32), (0, LANES - n_heads)).reshape(1, LANES),
            ar=ar, ai=ai, wb=wb, wc=wc, d_skip=row(d_skip[l]), w_glu=w_glu[l].astype(BF16), b_glu=row(b_glu[l]),
            g_ssm_out=row(g_ssm_out[l]), g_fox_out=row(g_fox_out[l]), w_out=w_out[l].astype(BF16),
            g_mem_q=row(g_mem_q[l]), w_mq=w_mq[l].astype(BF16), w_mo=w_mo[l].astype(BF16), g_ffn=row(g_ffn[l]),
            w_router_t=w_router[l].astype(F32).T, e_bias_t=e_bias[l].astype(F32).reshape(-1, 1),
            ws1=ws1[l].astype(BF16), ws3=ws3[l].astype(BF16), ws2=ws2[l].astype(BF16),
            w1=w1[l], w3=w3[l], w2=w2[l], g_final=row(g_final))
        assert depth == 1, "final norm fusion assumes a single layer"
        mk_p, mv_p, mkb_p, mvb_p = _mem_kv(mem_prompt, row(g_mem_kv[l]), w_mk[l].astype(BF16), w_mv[l].astype(BF16))
        nm, mh = mem_prompt.shape[1], MEM_HEADS
        hp, re_p, im_p, k_p, v_p, lf_p = _group(hp, None, None, None, None, None, mkb_p, mvb_p, p, w)
        cm_k = cache_mem_k[l].reshape(bsz, nm, -1).astype(BF16)
        cm_v = cache_mem_v[l].reshape(bsz, nm, -1).astype(BF16)
        hs, re_s, im_s, k_s, v_s, lf_s = _group(hs, state_ssm_re[l], state_ssm_im[l], cache_fox_k[l], cache_fox_v[l],
                                                cache_fox_logf[l], cm_k, cm_v, p, w)
        outs_p.append((re_p, im_p, k_p, v_p, lf_p, mk_p.reshape(bsz, nm, mh, -1), mv_p.reshape(bsz, nm, mh, -1)))
        outs_s.append((re_s, im_s, k_s, v_s, lf_s))
    stack = lambda outs, i: jnp.stack([o[i] for o in outs])
    return (hp, hs) + tuple(stack(outs_p, i) for i in range(7)) + tuple(stack(outs_s, i) for i in range(5))
```

```python
import functools
import math

import jax
import jax.numpy as jnp
from jax import lax
from jax.experimental import pallas as pl
from jax.experimental.pallas import tpu as pltpu

F32 = jnp.float32
BF16 = jnp.bfloat16

SSM_GROUP = 16
SSM_STATE = 64
FOX_HEAD_DIM = 64
MEM_HEADS = 4
TOP_K = 8
N_EXPERT_GROUPS = 8
TOPK_GROUPS = 4
ROUTED_SCALE = 2.5
RMS_EPS = 1e-6

LANES = 128
SUBLANES = 8
MXU_DIM = 256
VMEM_LIMIT_BYTES = 56 * 1024 * 1024

_HIGHEST = lax.Precision.HIGHEST
_NT = (((1,), (1,)), ((), ()))


def _params(*sem):
    return pltpu.CompilerParams(dimension_semantics=sem, vmem_limit_bytes=VMEM_LIMIT_BYTES)


def _rms(x, g):
    return x * lax.rsqrt(jnp.mean(x * x, axis=-1, keepdims=True) + RMS_EPS) * g


def _sigmoid(x):
    return 1.0 / (1.0 + jnp.exp(-x))


def _silu(x):
    return x * _sigmoid(x)


def _gelu_tanh(x):
    return x * (0.5 * (1.0 + jnp.tanh(math.sqrt(2.0 / math.pi) * (x + 0.044715 * (x * x * x)))))


def _log_sigmoid(x):
    return jnp.minimum(x, 0.0) - jnp.log1p(jnp.exp(-jnp.abs(x)))


def _dot(a, b):
    return jnp.dot(a, b, preferred_element_type=F32)


def _const_spec(shape):
    nd = len(shape)
    return pl.BlockSpec(shape, lambda *_: (0,) * nd)


def _memkv_kernel(m_ref, g_ref, wk_ref, wv_ref, k_ref, v_ref, kb_ref, vb_ref):
    mn = _rms(m_ref[0], g_ref[...]).astype(BF16)
    k = _dot(mn, wk_ref[...])
    v = _dot(mn, wv_ref[...])
    k_ref[0] = k
    v_ref[0] = v
    kb_ref[0] = k.astype(BF16)
    vb_ref[0] = v.astype(BF16)


def _mem_kv(mem, g, wk, wv):
    bsz, n, d = mem.shape
    blk = pl.BlockSpec((1, n, d), lambda b: (b, 0, 0))
    return pl.pallas_call(
        _memkv_kernel,
        grid=(bsz,),
        in_specs=[blk, _const_spec((1, d)), _const_spec((d, d)), _const_spec((d, d))],
        out_specs=[blk, blk, blk, blk],
        out_shape=[jax.ShapeDtypeStruct((bsz, n, d), F32)] * 2 + [jax.ShapeDtypeStruct((bsz, n, d), BF16)] * 2,
        compiler_params=_params("parallel"),
        name="mem_kv",
    )(mem, g, wk, wv)


def _proj_in_kernel(x_ref, g_ref, w_ref, wf_ref, bf_ref, u_ref, q_ref, k_ref, v_ref, kb_ref, vb_ref, lf_ref,
                    *, n_ssm, n_fox, n_heads):
    xb = _rms(x_ref[0], g_ref[...]).astype(BF16)
    z = _dot(xb, w_ref[...])
    u_ref[0] = z[:, :n_ssm]
    o = n_ssm
    q_ref[0] = (z[:, o:o + n_fox] * (FOX_HEAD_DIM ** -0.5 * math.log2(math.e))).astype(BF16)
    k = z[:, o + n_fox:o + 2 * n_fox]
    v = z[:, o + 2 * n_fox:o + 3 * n_fox]
    k_ref[0] = k
    v_ref[0] = v
    kb_ref[0] = k.astype(BF16)
    vb_ref[0] = v.astype(BF16)
    zf = _dot(xb, wf_ref[...])
    lf_ref[0] = _log_sigmoid(zf + bf_ref[...])[:, :n_heads]


def _proj_in(x, g, w_main, w_f, b_f, tm, n_ssm, n_fox, n_heads):
    bsz, L, d = x.shape
    nmain = w_main.shape[1]
    row = lambda n: pl.BlockSpec((1, tm, n), lambda b, i: (b, i, 0))
    outs = [jax.ShapeDtypeStruct((bsz, L, n_ssm), F32), jax.ShapeDtypeStruct((bsz, L, n_fox), BF16),
            jax.ShapeDtypeStruct((bsz, L, n_fox), F32), jax.ShapeDtypeStruct((bsz, L, n_fox), F32),
            jax.ShapeDtypeStruct((bsz, L, n_fox), BF16), jax.ShapeDtypeStruct((bsz, L, n_fox), BF16),
            jax.ShapeDtypeStruct((bsz, L, n_heads), F32)]
    return pl.pallas_call(
        functools.partial(_proj_in_kernel, n_ssm=n_ssm, n_fox=n_fox, n_heads=n_heads),
        grid=(bsz, L // tm),
        in_specs=[row(d), _const_spec((1, d)), _const_spec((d, nmain)), _const_spec((d, LANES)),
                  _const_spec((1, LANES))],
        out_specs=[row(n_ssm), row(n_fox), row(n_fox), row(n_fox), row(n_fox), row(n_fox), row(n_heads)],
        out_shape=outs,
        compiler_params=_params("parallel", "parallel"),
        name="proj_in",
    )(x, g, w_main, w_f, b_f)


def _decay_kernel(lft_ref, c0_ref, dt_ref, car_ref, *, tl):
    @pl.when(pl.program_id(1) == 0)
    def _():
        car_ref[...] = c0_ref[0]
    r = lax.broadcasted_iota(jnp.int32, (tl, tl), 0)
    c = lax.broadcasted_iota(jnp.int32, (tl, tl), 1)
    tri = (r <= c).astype(F32)
    d = jnp.dot(lft_ref[0], tri, preferred_element_type=F32, precision=_HIGHEST) + car_ref[...]
    car_ref[...] = d[:, tl - 1:tl]
    rest = d * math.log2(math.e)
    for piece in range(3):
        part = rest.astype(BF16).astype(F32)
        dt_ref[0, piece] = part
        rest = rest - part


def _decay_cumsum(lft, c0, tl):
    bsz, nh, L = lft.shape
    return pl.pallas_call(
        functools.partial(_decay_kernel, tl=tl),
        grid=(bsz, L // tl),
        in_specs=[pl.BlockSpec((1, nh, tl), lambda b, i: (b, 0, i)), pl.BlockSpec((1, nh, 1), lambda b, i: (b, 0, 0))],
        out_specs=pl.BlockSpec((1, 3, nh, tl), lambda b, i: (b, 0, 0, i)),
        out_shape=jax.ShapeDtypeStruct((bsz, 3, nh, L), F32),
        scratch_shapes=[pltpu.VMEM((nh, 1), F32)],
        compiler_params=_params("parallel", "arbitrary"),
        name="decay_cumsum",
    )(lft, c0)


def _s5_kernel(u_ref, h0_ref, ar_ref, ai_ref, wb_ref, wc_ref, dsk_ref, wglu_ref, bglu_ref, gout_ref,
               y_ref, hl_ref, hs_ref, hst_ref, *, t_chunk, bsz, n_slab):
    @pl.when(pl.program_id(0) == 0)
    def _():
        hst_ref[...] = h0_ref[...]

    u = u_ref[...]
    ub = u.astype(BF16)
    slab_per_k = MXU_DIM // (2 * SSM_GROUP)

    def project(j):
        kt = j // slab_per_k
        hs_ref[:, MXU_DIM * j:MXU_DIM * (j + 1)] = _dot(ub[:, MXU_DIM * kt:MXU_DIM * (kt + 1)], wb_ref[j])

    def scan(j):
        lo, mid, hi = MXU_DIM * j, MXU_DIM * j + LANES, MXU_DIM * (j + 1)
        ar = ar_ref[:, LANES * j:LANES * (j + 1)]
        ai = ai_ref[:, LANES * j:LANES * (j + 1)]
        re, im = hst_ref[:, lo:mid], hst_ref[:, mid:hi]
        for t in range(t_chunk):
            rows = slice(t * bsz, (t + 1) * bsz)
            re, im = (ar * re - ai * im + hs_ref[rows, lo:mid], ar * im + ai * re + hs_ref[rows, mid:hi])
            hs_ref[rows, lo:mid] = re
            hs_ref[rows, mid:hi] = im
        hst_ref[:, lo:mid] = re
        hst_ref[:, mid:hi] = im

    group = 4
    for j in range(group):
        project(j)
    for g0 in range(0, n_slab, group):
        for j in range(g0 + group, min(g0 + 2 * group, n_slab)):
            project(j)
        for j in range(g0, g0 + group):
            scan(j)
    hl_ref[...] = hst_ref[...]

    halves = []
    for hf in range(n_slab // slab_per_k):
        acc = None
        for jj in range(slab_per_k):
            j = hf * slab_per_k + jj
            d = _dot(hs_ref[:, MXU_DIM * j:MXU_DIM * (j + 1)].astype(BF16), wc_ref[j])
            acc = d if acc is None else acc + d
        halves.append(acc)
    y = jnp.concatenate(halves, axis=1) + dsk_ref[...] * u
    y = _gelu_tanh(y)
    y = y * _sigmoid(_dot(y.astype(BF16), wglu_ref[...]) + bglu_ref[...])
    y_ref[...] = _rms(y, gout_ref[...]).astype(BF16)


def _s5(u_tb, h0, ar, ai, wb, wc, dsk, wglu, bglu, gout, t_chunk, bsz):
    rows, n_ssm = u_tb.shape
    n_state = h0.shape[1]
    n_slab = n_state // MXU_DIM
    r = t_chunk * bsz
    return pl.pallas_call(
        functools.partial(_s5_kernel, t_chunk=t_chunk, bsz=bsz, n_slab=n_slab),
        grid=(rows // r,),
        in_specs=[pl.BlockSpec((r, n_ssm), lambda c: (c, 0)), _const_spec((bsz, n_state)),
                  _const_spec((bsz, n_state // 2)), _const_spec((bsz, n_state // 2)),
                  _const_spec((n_slab, MXU_DIM, MXU_DIM)), _const_spec((n_slab, MXU_DIM, MXU_DIM)),
                  _const_spec((1, n_ssm)), _const_spec((n_ssm, n_ssm)), _const_spec((1, n_ssm)),
                  _const_spec((1, n_ssm))],
        out_specs=[pl.BlockSpec((r, n_ssm), lambda c: (c, 0)), _const_spec((bsz, n_state))],
        out_shape=[jax.ShapeDtypeStruct((rows, n_ssm), BF16), jax.ShapeDtypeStruct((bsz, n_state), F32)],
        scratch_shapes=[pltpu.VMEM((r, n_state), F32), pltpu.VMEM((bsz, n_state), F32)],
        compiler_params=_params("arbitrary"),
        name="s5_mixer",
    )(u_tb, h0, ar, ai, wb, wc, dsk, wglu, bglu, gout)


def _s5_tables(lam_re, lam_im, log_dt, b_re, b_im, c_re, c_im, bsz):
    G, P = lam_re.shape
    H = b_re.shape[-1]
    lr, li = lam_re.astype(F32), lam_im.astype(F32)
    dt = jnp.exp(log_dt.astype(F32))[:, None]
    mag = jnp.exp(lr * dt)
    a_re, a_im = mag * jnp.cos(li * dt), mag * jnp.sin(li * dt)
    den = lr * lr + li * li
    c_r = ((a_re - 1.0) * lr + a_im * li) / den
    c_i = (a_im * lr - (a_re - 1.0) * li) / den
    br, bi = b_re.astype(F32), b_im.astype(F32)
    bbar_re = c_r[..., None] * br - c_i[..., None] * bi
    bbar_im = c_r[..., None] * bi + c_i[..., None] * br
    n_pair = G // 2
    per_k = MXU_DIM // (2 * H)
    eye2 = jnp.eye(2, dtype=F32)
    place = jax.nn.one_hot(jnp.arange(n_pair) % per_k, per_k, dtype=F32)

    ar = jnp.broadcast_to(a_re.reshape(1, -1), (bsz, G * P))
    ai = jnp.broadcast_to(a_im.reshape(1, -1), (bsz, G * P))

    bb = jnp.stack([bbar_re, bbar_im]).reshape(2, n_pair, 2, P, H)
    wpair = jnp.einsum("ajgph,gk->jghakp", bb, eye2).reshape(n_pair, 2 * H, MXU_DIM)
    wb = jnp.einsum("jrc,jk->jkrc", wpair, place).reshape(n_pair, MXU_DIM, MXU_DIM)

    cc = jnp.stack([c_re.astype(F32), -c_im.astype(F32)]).reshape(2, n_pair, 2, H, P)
    cpair = jnp.einsum("ajghp,gk->jagpkh", cc, eye2).reshape(n_pair, MXU_DIM, 2 * H)
    wc = jnp.einsum("jnc,jk->jnkc", cpair, place).reshape(n_pair, MXU_DIM, MXU_DIM)
    return ar, ai, wb.astype(BF16), wc.astype(BF16)


def _state_to_lanes(re, im):
    bsz, G, P = re.shape
    s = jnp.stack([re, im], axis=1).reshape(bsz, 2, G // 2, 2, P)
    return s.transpose(0, 2, 1, 3, 4).reshape(bsz, 2 * G * P)


def _lanes_to_state(h, G, P):
    bsz = h.shape[0]
    s = h.reshape(bsz, G // 2, 2, 2, P).transpose(0, 2, 1, 3, 4).reshape(bsz, 2, G, P)
    return s[:, 0], s[:, 1]


FOX_AUG_ROWS = 16


def _fox_kernel(qt_ref, qa_ref, k_ref, vt_ref, o_ref, m_ref, l_ref, acc_ref, *, tq, tk, past):
    qi = pl.program_id(2)
    qt = qt_ref[0]
    row = lax.broadcasted_iota(jnp.int32, (LANES, tq), 0)
    zero = jnp.zeros_like(qt)
    pad = jnp.zeros((LANES - FOX_AUG_ROWS, tq), BF16)
    qts = []
    for hh in range(2):
        own = (row < FOX_HEAD_DIM) if hh == 0 else (row >= FOX_HEAD_DIM)
        aug = qa_ref[0, 0, FOX_AUG_ROWS * hh:FOX_AUG_ROWS * (hh + 1), :]
        qts.append(jnp.concatenate([jnp.where(own, qt, zero), aug, pad], axis=0))
    m_ref[...] = jnp.full(m_ref.shape, -1e30, F32)
    l_ref[...] = jnp.zeros(l_ref.shape, F32)
    acc_ref[...] = jnp.zeros(acc_ref.shape, F32)
    q_start = past + qi * tq
    n_full = (q_start + 1) // tk
    n_all = (q_start + tq + tk - 1) // tk

    def block(ks, masked, width, q_lo=0):
        nq = tq - q_lo
        kb = k_ref[0, 0, pl.ds(ks, width), :]
        if masked:
            kpos = ks + lax.broadcasted_iota(jnp.int32, (width, nq), 0)
            qpos = q_start + q_lo + lax.broadcasted_iota(jnp.int32, (width, nq), 1)
            visible = kpos <= qpos
        for hh in range(2):
            st = _dot(kb, qts[hh][:, q_lo:])
            if masked:
                st = jnp.where(visible, st, -jnp.inf)
            m_old = m_ref[hh, :, q_lo:]
            m_new = jnp.maximum(m_old, jnp.max(st, axis=0, keepdims=True))
            p = jnp.exp2(st - m_new)
            alpha = jnp.exp2(m_old - m_new)
            l_ref[hh, :, q_lo:] = alpha * l_ref[hh, :, q_lo:] + jnp.sum(p, axis=0, keepdims=True)
            vt = vt_ref[0, FOX_HEAD_DIM * hh:FOX_HEAD_DIM * (hh + 1), pl.ds(ks, width)]
            acc_ref[hh, :, q_lo:] = alpha * acc_ref[hh, :, q_lo:] + _dot(vt, p.astype(BF16))
            m_ref[hh, :, q_lo:] = m_new

    def steps(lo, hi, masked, width):
        def body(j, c):
            block(pl.multiple_of(j * width, width), masked, width)
            return c
        lax.fori_loop(lo, hi, body, 0)

    n_wide = 0
    if 2 * tk <= k_ref.shape[2]:
        n_wide = q_start // (2 * tk)
        steps(0, n_wide, False, 2 * tk)
    steps(2 * n_wide, n_full, False, tk)
    steps(n_full, n_all, True, tk)
    out_t = jnp.concatenate([acc_ref[0] / l_ref[0], acc_ref[1] / l_ref[1]], axis=0)
    o_ref[0] = out_t.T


def _fox(q_t, q_aug, k_aug, v_t, tq, tk, past):
    bsz, n_fox, L = q_t.shape
    n_pair = n_fox // LANES
    lk = k_aug.shape[2]
    return pl.pallas_call(
        functools.partial(_fox_kernel, tq=tq, tk=tk, past=past),
        grid=(bsz, n_pair, L // tq),
        in_specs=[pl.BlockSpec((1, LANES, tq), lambda b, h, i: (b, h, i)),
                  pl.BlockSpec((1, 1, 2 * FOX_AUG_ROWS, tq), lambda b, h, i: (b, h, 0, i)),
                  pl.BlockSpec((1, 1, lk, 2 * LANES), lambda b, h, i: (b, h, 0, 0)),
                  pl.BlockSpec((1, LANES, lk), lambda b, h, i: (b, h, 0))],
        out_specs=pl.BlockSpec((1, tq, LANES), lambda b, h, i: (b, i, h)),
        out_shape=jax.ShapeDtypeStruct((bsz, L, n_fox), F32),
        scratch_shapes=[pltpu.VMEM((2, 1, tq), F32), pltpu.VMEM((2, 1, tq), F32),
                        pltpu.VMEM((2, FOX_HEAD_DIM, tq), F32)],
        compiler_params=_params("parallel", "parallel", "arbitrary"),
        name="fox_attention",
    )(q_t, q_aug, k_aug, v_t)


def _fox_operands(qb, k_all, v_all, d_pieces, past, L):
    bsz, lk_pad, n_fox = k_all.shape
    n_heads = n_fox // FOX_HEAD_DIM
    n_pair = n_heads // 2
    pieces = d_pieces.astype(BF16)
    dk = (-pieces).reshape(bsz, 3, n_pair, 2, lk_pad)
    dk = dk.transpose(0, 2, 4, 3, 1).reshape(bsz, n_pair, lk_pad, 6)
    ones_k = jnp.ones((bsz, n_pair, lk_pad, 3), BF16)
    zeros_k = jnp.zeros((bsz, n_pair, lk_pad, LANES - 9), BF16)
    k_pair = k_all.reshape(bsz, lk_pad, n_pair, LANES).transpose(0, 2, 1, 3)
    k_aug = jnp.concatenate([k_pair, dk, ones_k, zeros_k], axis=-1)

    dq = pieces[:, :, :, past:past + L].transpose(0, 2, 1, 3)
    sel = jax.nn.one_hot(jnp.arange(n_heads) % 2, 2, dtype=BF16)
    ones_q = jnp.broadcast_to(jnp.repeat(sel, 3, axis=1)[None, :, :, None], (bsz, n_heads, 6, L))
    zeros_q = jnp.zeros((bsz, n_heads, FOX_AUG_ROWS - 9, L), BF16)
    q_aug = jnp.concatenate([ones_q, dq, zeros_q], axis=2).reshape(bsz, n_pair, 2 * FOX_AUG_ROWS, L)
    return qb.transpose(0, 2, 1), q_aug, k_aug, v_all.transpose(0, 2, 1)


def _route_gates(logits_t, ebias_t, n_experts):
    per_group = n_experts // N_EXPERT_GROUPS
    tokens = logits_t.shape[1]
    score = _sigmoid(logits_t)
    sel = score + ebias_t
    row = lax.broadcasted_iota(jnp.int32, (per_group, tokens), 0).astype(F32)
    neg = jnp.float32(-jnp.inf)

    def first_argmax(tile, best):
        return jnp.min(jnp.where(tile == best, row, float(per_group)), axis=0, keepdims=True)

    score_g, sel_g, gsc = [], [], []
    for g in range(N_EXPERT_GROUPS):
        sg = sel[per_group * g:per_group * (g + 1), :]
        score_g.append(score[per_group * g:per_group * (g + 1), :])
        sel_g.append(sg)
        m1 = jnp.max(sg, axis=0, keepdims=True)
        rest = jnp.where(row == first_argmax(sg, m1), neg, sg)
        gsc.append(m1 + jnp.max(rest, axis=0, keepdims=True))
    cur = []
    for g in range(N_EXPERT_GROUPS):
        ahead = jnp.zeros((1, tokens), F32)
        for o in range(N_EXPERT_GROUPS):
            if o == g:
                continue
            beats = (gsc[o] >= gsc[g]) if o < g else (gsc[o] > gsc[g])
            ahead = ahead + jnp.where(beats, 1.0, 0.0)
        cur.append(jnp.where(ahead < TOPK_GROUPS, sel_g[g], neg))
    chosen = [jnp.zeros((per_group, tokens), jnp.bool_) for _ in range(N_EXPERT_GROUPS)]
    for _ in range(TOP_K):
        best = cur[0]
        for g in range(1, N_EXPERT_GROUPS):
            best = jnp.maximum(best, cur[g])
        best = jnp.max(best, axis=0, keepdims=True)
        idx = None
        for g in range(N_EXPERT_GROUPS):
            cand = jnp.min(jnp.where(cur[g] == best, row + float(per_group * g), float(n_experts)), axis=0,
                           keepdims=True)
            idx = cand if idx is None else jnp.minimum(idx, cand)
        for g in range(N_EXPERT_GROUPS):
            hit = (row + float(per_group * g)) == idx
            chosen[g] = chosen[g] | hit
            cur[g] = jnp.where(hit, neg, cur[g])
    w = [jnp.where(chosen[g], score_g[g], 0.0) for g in range(N_EXPERT_GROUPS)]
    total = w[0]
    for g in range(1, N_EXPERT_GROUPS):
        total = total + w[g]
    total = jnp.sum(total, axis=0, keepdims=True)
    return [w[g] / total * ROUTED_SCALE for g in range(N_EXPERT_GROUPS)]


def _mid_kernel(x_ref, ys_ref, yf_ref, gfox_ref, wout_ref, gmq_ref, wmq_ref, mk_ref, mv_ref, wmo_ref, gffn_ref,
                wrt_ref, ebt_ref, ws1_ref, ws3_ref, ws2_ref, xs_ref, xn_ref, gt_ref, cnt_ref, *, n_experts):
    x = x_ref[0]
    yfn = _rms(yf_ref[0], gfox_ref[...]).astype(BF16)
    mix = jnp.concatenate([ys_ref[0], yfn], axis=1)
    x1 = x + _dot(mix, wout_ref[...])

    qm = _dot(_rms(x1, gmq_ref[...]).astype(BF16), wmq_ref[...])
    hd = qm.shape[1] // MEM_HEADS
    heads = []
    for h in range(MEM_HEADS):
        qh = (qm[:, hd * h:hd * (h + 1)] * (hd ** -0.5)).astype(BF16)
        s = lax.dot_general(qh, mk_ref[0, :, hd * h:hd * (h + 1)], _NT, preferred_element_type=F32)
        p = jnp.exp(s - jnp.max(s, axis=1, keepdims=True))
        o = _dot(p.astype(BF16), mv_ref[0, :, hd * h:hd * (h + 1)]) / jnp.sum(p, axis=1, keepdims=True)
        heads.append(o.astype(BF16))
    x2 = x1 + _dot(jnp.concatenate(heads, axis=1), wmo_ref[...])

    xn = _rms(x2, gffn_ref[...])
    xnb = xn.astype(BF16)
    hidden = _silu(_dot(xnb, ws1_ref[...])) * _dot(xnb, ws3_ref[...])
    xs_ref[0] = x2 + _dot(hidden.astype(BF16), ws2_ref[...])
    xn_ref[0] = xnb

    wr = wrt_ref[...]
    wr_hi = wr.astype(BF16)
    wr_lo = (wr - wr_hi.astype(F32)).astype(BF16)
    xn_lo = (xn - xnb.astype(F32)).astype(BF16)
    nt = lambda a, c: lax.dot_general(a, c, _NT, preferred_element_type=F32)
    logits_t = nt(wr_hi, xnb) + (nt(wr_hi, xn_lo) + nt(wr_lo, xnb))
    gates = _route_gates(logits_t, ebt_ref[...], n_experts)
    per_group = n_experts // N_EXPERT_GROUPS
    for g in range(N_EXPERT_GROUPS):
        gt_ref[0, per_group * g:per_group * (g + 1), :] = gates[g]
        cnt_ref[0, 0, per_group * g:per_group * (g + 1), :] = jnp.sum(
            jnp.where(gates[g] != 0.0, 1.0, 0.0), axis=1, keepdims=True)


def _mid(x, ys, yf, gfox, wout, gmq, wmq, mkb, mvb, wmo, gffn, wrt, ebt, ws1, ws3, ws2, tm):
    bsz, L, d = x.shape
    n_ssm, n_fox = ys.shape[2], yf.shape[2]
    n_mem = mkb.shape[1]
    n_experts = wrt.shape[0]
    fs = ws1.shape[1]
    row = lambda n: pl.BlockSpec((1, tm, n), lambda b, i: (b, i, 0))
    memspec = pl.BlockSpec((1, n_mem, d), lambda b, i: (b, 0, 0))
    return pl.pallas_call(
        functools.partial(_mid_kernel, n_experts=n_experts),
        grid=(bsz, L // tm),
        in_specs=[row(d), row(n_ssm), row(n_fox), _const_spec((1, n_fox)),
                  _const_spec((d, d)), _const_spec((1, d)),
                  _const_spec((d, d)), memspec, memspec, _const_spec((d, d)), _const_spec((1, d)),
                  _const_spec((n_experts, d)), _const_spec((n_experts, 1)), _const_spec((d, fs)), _const_spec((d, fs)),
                  _const_spec((fs, d))],
        out_specs=[row(d), row(d), pl.BlockSpec((1, n_experts, tm), lambda b, i: (b, 0, i)),
                   pl.BlockSpec((1, 1, n_experts, 1), lambda b, i: (b, i, 0, 0))],
        out_shape=[jax.ShapeDtypeStruct((bsz, L, d), F32), jax.ShapeDtypeStruct((bsz, L, d), BF16),
                   jax.ShapeDtypeStruct((bsz, n_experts, L), F32),
                   jax.ShapeDtypeStruct((bsz, L // tm, n_experts, 1), F32)],
        compiler_params=_params("parallel", "parallel"),
        name="mid_block",
    )(x, ys, yf, gfox, wout, gmq, wmq, mkb, mvb, wmo, gffn, wrt, ebt, ws1, ws3, ws2)


SEG_ROWS = 16
MOE_TOKENS = 512
COPY_ROWS = 4 * SEG_ROWS
GATHER_ROWS = 1024
_TN = (((0,), (0,)), ((), ()))
_NO_RANK = -(1 << 20)


def _sorted_rows(tb, n_experts):
    rows = TOP_K * tb + n_experts * (SEG_ROWS - 1)
    return -(-rows // GATHER_ROWS) * GATHER_ROWS


def _expert_tile(m):
    return 1024 if m >= 4096 else 128


def _dispatch_plan(cnt, ns, MOE_TILE):
    nblk, n_experts = cnt.shape

    def before(a, axis):
        n = a.shape[axis]
        earlier = jnp.arange(n)[:, None] < jnp.arange(n)[None, :]
        if axis == 0:
            return jnp.sum(jnp.where(earlier[:, :, None], a[:, None, :], 0), axis=0)
        return jnp.sum(jnp.where(earlier[None, :, :], a[:, :, None], 0), axis=1)

    cp = (cnt + SEG_ROWS - 1) // SEG_ROWS * SEG_ROWS
    o_loc = before(cp, 1)
    used = jnp.sum(cp, axis=1)
    tot_e = jnp.sum(cp, axis=0)
    reg_e = (tot_e + MOE_TILE - 1) // MOE_TILE * MOE_TILE
    base_e = before(reg_e[None, :], 1)[0]
    reg_end = base_e + reg_e
    dst = base_e[None, :] + before(cp, 0)
    rows_max = nblk * ns + n_experts * MOE_TILE
    n_tiles_max = -(-rows_max // MOE_TILE)
    n_tiles = reg_end[-1] // MOE_TILE
    tile_idx = jnp.clip(jnp.arange(n_tiles_max, dtype=jnp.int32), 0, jnp.maximum(n_tiles - 1, 0))
    tile_expert = jnp.sum((reg_end[None, :] <= (tile_idx * MOE_TILE)[:, None]).astype(jnp.int32), axis=1)
    tile_expert = jnp.minimum(tile_expert, n_experts - 1)
    n_pc = ns // SEG_ROWS
    piece_row = jnp.arange(n_pc, dtype=jnp.int32) * SEG_ROWS
    seg_end = o_loc + cp
    piece_e = jnp.sum((seg_end[:, None, :] <= piece_row[None, :, None]).astype(jnp.int32), axis=2)
    live = piece_row[None, :] < used[:, None]
    piece_e = jnp.minimum(piece_e, n_experts - 1)
    own = piece_e[:, :, None] == jnp.arange(n_experts, dtype=jnp.int32)[None, None, :]
    pick = lambda table: jnp.sum(jnp.where(own, table[:, None, :], 0), axis=2)
    rank0 = piece_row[None, :] - pick(o_loc)
    piece_rank = jnp.where(live, rank0, _NO_RANK)
    i32 = lambda a: a.astype(jnp.int32).reshape(-1)
    per_block = lambda a: a.astype(jnp.int32).reshape(nblk, 1, -1)

    def copy_list(n_e, first_rank, rows_each, n_max):
        end = before(n_e, 1) + n_e
        j = jnp.arange(n_max, dtype=jnp.int32)
        e_of = jnp.minimum(jnp.sum((end[:, None, :] <= j[None, :, None]).astype(jnp.int32), axis=2), n_experts - 1)
        mine = e_of[:, :, None] == jnp.arange(n_experts, dtype=jnp.int32)[None, None, :]
        take = lambda table: jnp.sum(jnp.where(mine, table[:, None, :], 0), axis=2)
        rank = take(first_rank) + (j[None, :] - take(end - n_e)) * rows_each
        ok = j[None, :] < jnp.sum(n_e, axis=1)[:, None]
        return (per_block(jnp.where(ok, take(o_loc) + rank, 0)), per_block(jnp.where(ok, take(dst) + rank, 0)),
                i32(jnp.sum(n_e, axis=1)))

    per_copy = COPY_ROWS // SEG_ROWS
    n_wide_e = (cp // SEG_ROWS) // per_copy
    n_seg_e = cp // SEG_ROWS - n_wide_e * per_copy
    wide_src, wide_dst, n_wide = copy_list(n_wide_e, jnp.zeros_like(cp), COPY_ROWS, ns // COPY_ROWS)
    seg_src, seg_dst, n_seg = copy_list(n_seg_e, n_wide_e * COPY_ROWS, SEG_ROWS, n_experts * (per_copy - 1))
    return dict(n_piece=i32(used // SEG_ROWS), piece_e=per_block(piece_e), piece_rank=per_block(piece_rank),
                wide_src=wide_src, wide_dst=wide_dst, n_wide=n_wide, seg_src=seg_src, seg_dst=seg_dst, n_seg=n_seg,
                tail_start=i32(base_e + tot_e),
                tail_chunks=i32((reg_e - tot_e) // SEG_ROWS), tile_idx=i32(tile_idx), tile_expert=i32(tile_expert),
                n_tiles=i32(n_tiles), n_tiles_max=n_tiles_max)


def _slot_ranks(gates, tb):
    sel = gates != 0.0
    r = lax.broadcasted_iota(jnp.int32, (tb, tb), 0)
    c = lax.broadcasted_iota(jnp.int32, (tb, tb), 1)
    earlier = jnp.where(r < c, 1.0, 0.0).astype(BF16)
    rank = _dot(jnp.where(sel, 1.0, 0.0).astype(BF16), earlier)
    return jnp.where(sel, rank, -1.0)


def _build_one_hot(out_ref, rm_ref, pe_ref, pr_ref, first, count, tb, value_row):
    rows = lax.broadcasted_iota(jnp.int32, (SEG_ROWS, tb), 0).astype(F32)
    for i in range(first, first + count):
        e = pe_ref[0, 0, i]
        hit = rows == (rm_ref[pl.ds(e, 1), :] - pr_ref[0, 0, i].astype(F32))
        out_ref[SEG_ROWS * i:SEG_ROWS * (i + 1), :] = jnp.where(hit, value_row(e), 0.0).astype(BF16)


def _wait_rows(n_rows, make_copy):
    wide = 16 * SEG_ROWS
    n_wide = n_rows // wide

    def wide_step(i, c):
        make_copy(wide).wait()
        return c

    def seg_step(i, c):
        make_copy(SEG_ROWS).wait()
        return c

    lax.fori_loop(0, n_wide, wide_step, 0)
    lax.fori_loop(0, (n_rows - n_wide * wide) // SEG_ROWS, seg_step, 0)


def _issue_copies(n_wide, n_seg, tables, make_copy):
    wide_src, wide_dst, seg_src, seg_dst = tables

    def wide(i, c):
        make_copy(pl.multiple_of(wide_src[0, 0, i], SEG_ROWS), pl.multiple_of(wide_dst[0, 0, i], SEG_ROWS),
                  COPY_ROWS).start()
        return c

    def seg(i, c):
        make_copy(pl.multiple_of(seg_src[0, 0, i], SEG_ROWS), pl.multiple_of(seg_dst[0, 0, i], SEG_ROWS),
                  SEG_ROWS).start()
        return c

    lax.fori_loop(0, n_wide, wide, 0)
    lax.fori_loop(0, n_seg, seg, 0)


def _dispatch_kernel(n_piece_ref, n_wide_ref, n_seg_ref, tail_start_ref, tail_chunks_ref, pe_ref, pr_ref,
                     ws_ref, wd_ref, ss_ref, sd_ref, xn_ref, gt_ref, *rest, tb, ns, n_experts, nblk, fill_tail):
    xg_ref, g_ref, xsb_ref, rm_ref, z_ref, sem = rest[-6:]
    b = pl.program_id(0)
    n_pc = ns // SEG_ROWS
    used = n_piece_ref[b] * SEG_ROWS
    rm_ref[...] = _slot_ranks(gt_ref[...], tb)
    per_slice = GATHER_ROWS // SEG_ROWS
    n_slices = ns // GATHER_ROWS
    build = lambda t: _build_one_hot(g_ref, rm_ref, pe_ref, pr_ref, per_slice * t, per_slice, tb, lambda e: 1.0)

    slot = b % 2
    build(0)
    for t in range(n_slices):
        if t + 1 < n_slices:
            build(t + 1)
        sl = slice(GATHER_ROWS * t, GATHER_ROWS * (t + 1))
        xsb_ref[slot, sl, :] = _dot(g_ref[sl, :], xn_ref[...]).astype(BF16)

    def rows_copy(buf, src_row, dst_row, size):
        return pltpu.make_async_copy(xsb_ref.at[buf, pl.ds(src_row, size)], xg_ref.at[pl.ds(dst_row, size)],
                                     sem.at[buf])

    _issue_copies(n_wide_ref[b], n_seg_ref[b], (ws_ref, wd_ref, ss_ref, sd_ref),
                  lambda block_row, global_row, size: rows_copy(slot, block_row, global_row, size))

    @pl.when(b > 0)
    def _():
        _wait_rows(n_piece_ref[b - 1] * SEG_ROWS, lambda size: rows_copy(1 - slot, 0, 0, size))

    @pl.when(b == nblk - 1)
    def _():
        _wait_rows(used, lambda size: rows_copy(slot, 0, 0, size))

    @pl.when(jnp.logical_and(b == nblk - 1, fill_tail))
    def _():
        z_ref[...] = jnp.zeros(z_ref.shape, BF16)

        wide = z_ref.shape[0]
        per_wide = wide // SEG_ROWS

        def tail_copy(row, size):
            return pltpu.make_async_copy(z_ref.at[pl.ds(0, size)], xg_ref.at[pl.ds(pl.multiple_of(row, SEG_ROWS), size)],
                                         sem.at[slot])

        def each_tail(action):
            def per_expert(e, carry):
                n_wide = tail_chunks_ref[e] // per_wide

                def wide_piece(i, c2):
                    action(tail_copy(tail_start_ref[e] + i * wide, wide))
                    return c2

                def seg_piece(i, c2):
                    action(tail_copy(tail_start_ref[e] + n_wide * wide + i * SEG_ROWS, SEG_ROWS))
                    return c2

                lax.fori_loop(0, n_wide, wide_piece, 0)
                lax.fori_loop(0, tail_chunks_ref[e] - n_wide * per_wide, seg_piece, 0)
                return carry
            lax.fori_loop(0, n_experts, per_expert, 0)

        each_tail(lambda cp: cp.start())
        each_tail(lambda cp: cp.wait())


def _expert_kernel(tile_idx_ref, tile_expert_ref, n_tiles_ref, x_ref, w1_ref, w3_ref, w2_ref, o_ref):
    @pl.when(pl.program_id(0) < n_tiles_ref[0])
    def _():
        x = x_ref[...]
        h = _silu(_dot(x, w1_ref[0].astype(BF16))) * _dot(x, w3_ref[0].astype(BF16))
        o_ref[...] = _dot(h.astype(BF16), w2_ref[0].astype(BF16)).astype(BF16)


def _combine_kernel(n_piece_ref, n_wide_ref, n_seg_ref, pe_ref, pr_ref, ws_ref, wd_ref, ss_ref, sd_ref,
                    ws_next_ref, wd_next_ref, ss_next_ref, sd_next_ref,
                    og_ref, gt_ref, xs_ref, gfin_ref, y_ref, gw_ref, ob_ref, rm_ref, sem, *, tb, ns, n_experts, nblk):
    b = pl.program_id(0)
    slot = b % 2
    n_pc = ns // SEG_ROWS

    def rows_copy(buf, src_row, dst_row, size):
        return pltpu.make_async_copy(og_ref.at[pl.ds(src_row, size)], ob_ref.at[buf, pl.ds(dst_row, size)],
                                     sem.at[buf])

    def fetch_block(blk, tables, buf):
        _issue_copies(n_wide_ref[blk], n_seg_ref[blk], tables,
                      lambda block_row, global_row, size: rows_copy(buf, global_row, block_row, size))

    @pl.when(b == 0)
    def _():
        fetch_block(b, (ws_ref, wd_ref, ss_ref, sd_ref), slot)

    @pl.when(b + 1 < nblk)
    def _():
        fetch_block(jnp.minimum(b + 1, nblk - 1), (ws_next_ref, wd_next_ref, ss_next_ref, sd_next_ref), 1 - slot)

    rm_ref[...] = _slot_ranks(gt_ref[...], tb)
    _wait_rows(n_piece_ref[b] * SEG_ROWS, lambda size: rows_copy(slot, 0, 0, size))

    def clear(i, c):
        ob_ref[slot, pl.ds(pl.multiple_of((n_piece_ref[b] + i) * SEG_ROWS, SEG_ROWS), SEG_ROWS), :] = jnp.zeros(
            (SEG_ROWS, ob_ref.shape[2]), BF16)
        return c

    lax.fori_loop(0, n_pc - n_piece_ref[b], clear, 0)

    per_slice = GATHER_ROWS // SEG_ROWS
    n_slices = ns // GATHER_ROWS
    build = lambda t: _build_one_hot(gw_ref, rm_ref, pe_ref, pr_ref, per_slice * t, per_slice, tb,
                                     lambda e: gt_ref[pl.ds(e, 1), :])
    build(0)
    y = xs_ref[...]
    for t in range(n_slices):
        if t + 1 < n_slices:
            build(t + 1)
        sl = slice(GATHER_ROWS * t, GATHER_ROWS * (t + 1))
        y = y + lax.dot_general(gw_ref[sl, :], ob_ref[slot, sl, :], _TN, preferred_element_type=F32)
    y_ref[...] = _rms(y, gfin_ref[...])


def _moe(groups, w1, w3, w2, gfin):
    d = groups[0][0].shape[1]
    n_experts, _, f = w1.shape
    tb = MOE_TOKENS
    blocks = [g[0].shape[0] // tb for g in groups]
    assert all(g[0].shape[0] == n * tb for g, n in zip(groups, blocks)), "token groups must be whole dispatch blocks"
    ns = _sorted_rows(tb, n_experts)
    MOE_TILE = _expert_tile(sum(blocks) * tb)
    plan = _dispatch_plan(jnp.concatenate([g[2] for g in groups], axis=0), ns, MOE_TILE)
    n_tiles_max = plan["n_tiles_max"]
    rows_max = n_tiles_max * MOE_TILE
    this_block = lambda t: pl.BlockSpec((1, 1, t.shape[2]), lambda b, *_: (b, 0, 0), memory_space=pltpu.SMEM)
    n_prefetch_dispatch = 5
    n_tables = 6

    def group_tables(first, nblk):
        rows = slice(first, first + nblk)
        build = (plan["piece_e"][rows], plan["piece_rank"][rows])
        copies = tuple(plan[k][rows] for k in ("wide_src", "wide_dst", "seg_src", "seg_dst"))
        counts = tuple(plan[k][rows] for k in ("n_piece", "n_wide", "n_seg"))
        return build, copies, counts

    xg = None
    first = 0
    for gi, ((xn, gates_t, _, _), nblk) in enumerate(zip(groups, blocks)):
        build_tables, copy_tables, counts = group_tables(first, nblk)
        first += nblk
        earlier = () if xg is None else (xg,)
        xg = pl.pallas_call(
            functools.partial(_dispatch_kernel, tb=tb, ns=ns, n_experts=n_experts, nblk=nblk,
                              fill_tail=gi == len(groups) - 1),
            grid_spec=pltpu.PrefetchScalarGridSpec(
                num_scalar_prefetch=n_prefetch_dispatch, grid=(nblk,),
                in_specs=[this_block(t) for t in build_tables + copy_tables] + [
                          pl.BlockSpec((tb, d), lambda b, *_: (b, 0)),
                          pl.BlockSpec((n_experts, tb), lambda b, *_: (0, b))] + [
                          pl.BlockSpec(memory_space=pl.ANY) for _ in earlier],
                out_specs=pl.BlockSpec(memory_space=pl.ANY),
                scratch_shapes=[pltpu.VMEM((ns, tb), BF16), pltpu.VMEM((2, ns, d), BF16),
                                pltpu.VMEM((n_experts, tb), F32), pltpu.VMEM((16 * SEG_ROWS, d), BF16),
                                pltpu.SemaphoreType.DMA((2,))]),
            out_shape=jax.ShapeDtypeStruct((rows_max, d), BF16),
            input_output_aliases={n_prefetch_dispatch + n_tables + 2: 0} if earlier else {},
            compiler_params=_params("arbitrary"),
            name="moe_dispatch",
        )(*counts, plan["tail_start"], plan["tail_chunks"], *build_tables, *copy_tables, xn, gates_t, *earlier)

    tile = lambda i, idx, ex, n: (idx[i], 0)
    og = pl.pallas_call(
        _expert_kernel,
        grid_spec=pltpu.PrefetchScalarGridSpec(
            num_scalar_prefetch=3, grid=(n_tiles_max,),
            in_specs=[pl.BlockSpec((MOE_TILE, d), tile),
                      pl.BlockSpec((1, d, f), lambda i, idx, ex, n: (ex[i], 0, 0)),
                      pl.BlockSpec((1, d, f), lambda i, idx, ex, n: (ex[i], 0, 0)),
                      pl.BlockSpec((1, f, d), lambda i, idx, ex, n: (ex[i], 0, 0))],
            out_specs=pl.BlockSpec((MOE_TILE, d), tile)),
        out_shape=jax.ShapeDtypeStruct((rows_max, d), BF16),
        compiler_params=_params("arbitrary"),
        name="moe_experts",
    )(plan["tile_idx"], plan["tile_expert"], plan["n_tiles"], xg, w1, w3, w2)

    results = []
    first = 0
    for (xn, gates_t, _, xs), nblk in zip(groups, blocks):
        build_tables, copy_tables, counts = group_tables(first, nblk)
        first += nblk
        next_block = lambda t, nblk=nblk: pl.BlockSpec(
            (1, 1, t.shape[2]), lambda b, *_: (jnp.minimum(b + 1, nblk - 1), 0, 0), memory_space=pltpu.SMEM)
        results.append(pl.pallas_call(
            functools.partial(_combine_kernel, tb=tb, ns=ns, n_experts=n_experts, nblk=nblk),
            grid_spec=pltpu.PrefetchScalarGridSpec(
                num_scalar_prefetch=3, grid=(nblk,),
                in_specs=[this_block(t) for t in build_tables + copy_tables] + [next_block(t) for t in copy_tables] + [
                          pl.BlockSpec(memory_space=pl.ANY), pl.BlockSpec((n_experts, tb), lambda b, *_: (0, b)),
                          pl.BlockSpec((tb, d), lambda b, *_: (b, 0)), pl.BlockSpec((1, d), lambda b, *_: (0, 0))],
                out_specs=pl.BlockSpec((tb, d), lambda b, *_: (b, 0)),
                scratch_shapes=[pltpu.VMEM((ns, tb), BF16), pltpu.VMEM((2, ns, d), BF16),
                                pltpu.VMEM((n_experts, tb), F32), pltpu.SemaphoreType.DMA((2,))]),
            out_shape=jax.ShapeDtypeStruct((nblk * tb, d), F32),
            compiler_params=_params("arbitrary"),
            name="moe_combine",
        )(*counts, *build_tables, *copy_tables, *copy_tables, og, gates_t, xs, gfin))
    return results


def _tiles(L):
    t_row = min(L, 512)
    t_key = 512
    t_scan = min(L, 64)
    return t_row, t_key, t_scan


def _group(x, h0_re, h0_im, past_k, past_v, past_lf, mkb, mvb, p, w):
    bsz, L, d = x.shape
    n_heads = p["b_f"].shape[0]
    n_fox = n_heads * FOX_HEAD_DIM
    n_ssm = p["d_skip"].shape[0]
    G, P = p["lam_re"].shape
    t_row, t_key, t_scan = _tiles(L)

    u, qb, k, v, kb, vb, lf = _proj_in(x, w["g_mix"], w["w_main"], w["w_f"], w["b_f"], t_row, n_ssm, n_fox, n_heads)

    u_tb = u.transpose(1, 0, 2).reshape(L * bsz, n_ssm)
    if h0_re is None:
        h0 = jnp.zeros((bsz, 2 * G * P), F32)
    else:
        h0 = _state_to_lanes(h0_re.astype(F32), h0_im.astype(F32))
    ys_tb, h_last = _s5(u_tb, h0, w["ar"], w["ai"], w["wb"], w["wc"], w["d_skip"], w["w_glu"], w["b_glu"],
                        w["g_ssm_out"], t_scan, bsz)
    ys = ys_tb.reshape(L, bsz, n_ssm).transpose(1, 0, 2)
    hl_re, hl_im = _lanes_to_state(h_last, G, P)

    if past_k is None:
        past = 0
        lf_all, k_all, v_all = lf, kb, vb
    else:
        past = past_k.shape[1]
        lf_all = jnp.concatenate([past_lf.astype(F32), lf], axis=1)
        k_all = jnp.concatenate([past_k.reshape(bsz, past, n_fox).astype(BF16), kb], axis=1)
        v_all = jnp.concatenate([past_v.reshape(bsz, past, n_fox).astype(BF16), vb], axis=1)
    lk = lf_all.shape[1]
    lk_pad = -(-lk // t_key) * t_key
    lf_t = jnp.pad(lf_all.transpose(0, 2, 1), ((0, 0), (0, 0), (0, lk_pad - lk)))
    d_pieces = _decay_cumsum(lf_t.reshape(1, bsz * n_heads, lk_pad), jnp.zeros((1, bsz * n_heads, 1), F32), t_key)
    d_pieces = d_pieces.reshape(3, bsz, n_heads, lk_pad).transpose(1, 0, 2, 3)
    k_all = jnp.pad(k_all, ((0, 0), (0, lk_pad - lk), (0, 0)))
    v_all = jnp.pad(v_all, ((0, 0), (0, lk_pad - lk), (0, 0)))
    t_att = t_key if L >= t_key else lk_pad
    yf = _fox(*_fox_operands(qb, k_all, v_all, d_pieces, past, L), t_row, t_att, past)

    xs, xnb, gates_t, cnt = _mid(x, ys, yf, w["g_fox_out"], w["w_out"], w["g_mem_q"], w["w_mq"], mkb, mvb, w["w_mo"],
                                 w["g_ffn"], w["w_router_t"], w["e_bias_t"], w["ws1"], w["ws3"], w["ws2"], t_row)
    m = bsz * L
    n_experts = gates_t.shape[1]
    cnt = cnt.reshape(m // MOE_TOKENS, MOE_TOKENS // t_row, n_experts).sum(axis=1).astype(jnp.int32)
    routed = (xnb.reshape(m, d), gates_t.transpose(1, 0, 2).reshape(n_experts, m), cnt, xs.reshape(m, d))
    return (routed, hl_re, hl_im, k.reshape(bsz, L, n_heads, FOX_HEAD_DIM),
            v.reshape(bsz, L, n_heads, FOX_HEAD_DIM), lf)


def kernel(x_prompt, x_sample, state_ssm_re, state_ssm_im, cache_fox_k, cache_fox_v, cache_fox_logf, cache_mem_k, cache_mem_v, mem_prompt, g_mix, w_in, b_f, lam_re, lam_im, log_dt, b_re, b_im, c_re, c_im, d_skip, w_glu, b_glu, g_ssm_out, g_fox_out, w_out, g_mem_q, g_mem_kv, w_mq, w_mk, w_mv, w_mo, g_ffn, w_router, e_bias, w1, w3, w2, ws1, ws3, ws2, g_final):
    depth = w_in.shape[0]
    bsz = x_prompt.shape[0]
    hp, hs = x_prompt, x_sample
    outs_p, outs_s = [], []
    for l in range(depth):
        p = dict(b_f=b_f[l], lam_re=lam_re[l], d_skip=d_skip[l])
        n_heads = b_f.shape[1]
        n_ssm = d_skip.shape[1]
        n_main = w_in.shape[2] - n_heads
        row = lambda a: a.reshape(1, -1).astype(F32)
        ar, ai, wb, wc = _s5_tables(lam_re[l], lam_im[l], log_dt[l], b_re[l], b_im[l], c_re[l], c_im[l], bsz)
        w = dict(
            g_mix=row(g_mix[l]), w_main=w_in[l][:, :n_main].astype(BF16),
            w_f=jnp.pad(w_in[l][:, n_main:], ((0, 0), (0, LANES - n_heads))).astype(BF16),
            b_f=jnp.pad(b_f[l].astype(F32), (0, LANES - n_heads)).reshape(1, LANES),
            ar=ar, ai=ai, wb=wb, wc=wc, d_skip=row(d_skip[l]), w_glu=w_glu[l].astype(BF16), b_glu=row(b_glu[l]),
            g_ssm_out=row(g_ssm_out[l]), g_fox_out=row(g_fox_out[l]), w_out=w_out[l].astype(BF16),
            g_mem_q=row(g_mem_q[l]), w_mq=w_mq[l].astype(BF16), w_mo=w_mo[l].astype(BF16), g_ffn=row(g_ffn[l]),
            w_router_t=w_router[l].astype(F32).T, e_bias_t=e_bias[l].astype(F32).reshape(-1, 1),
            ws1=ws1[l].astype(BF16), ws3=ws3[l].astype(BF16), ws2=ws2[l].astype(BF16),
            w1=w1[l], w3=w3[l], w2=w2[l], g_final=row(g_final))
        assert depth == 1, "final norm fusion assumes a single layer"
        mk_p, mv_p, mkb_p, mvb_p = _mem_kv(mem_prompt, row(g_mem_kv[l]), w_mk[l].astype(BF16), w_mv[l].astype(BF16))
        nm, mh = mem_prompt.shape[1], MEM_HEADS
        routed_p, re_p, im_p, k_p, v_p, lf_p = _group(hp, None, None, None, None, None, mkb_p, mvb_p, p, w)
        cm_k = cache_mem_k[l].reshape(bsz, nm, -1).astype(BF16)
        cm_v = cache_mem_v[l].reshape(bsz, nm, -1).astype(BF16)
        routed_s, re_s, im_s, k_s, v_s, lf_s = _group(hs, state_ssm_re[l], state_ssm_im[l], cache_fox_k[l],
                                                      cache_fox_v[l], cache_fox_logf[l], cm_k, cm_v, p, w)
        y_p, y_s = _moe([routed_p, routed_s], w["w1"], w["w3"], w["w2"], w["g_final"])
        hp, hs = y_p.reshape(hp.shape), y_s.reshape(hs.shape)
        outs_p.append((re_p, im_p, k_p, v_p, lf_p, mk_p.reshape(bsz, nm, mh, -1), mv_p.reshape(bsz, nm, mh, -1)))
        outs_s.append((re_s, im_s, k_s, v_s, lf_s))
    stack = lambda outs, i: jnp.stack([o[i] for o in outs])
    return (hp, hs) + tuple(stack(outs_p, i) for i in range(7)) + tuple(stack(outs_s, i) for i in range(5))
```

```python
import functools
import math

import jax
import jax.numpy as jnp
from jax import lax
from jax.experimental import pallas as pl
from jax.experimental.pallas import tpu as pltpu

F32 = jnp.float32
BF16 = jnp.bfloat16

SSM_GROUP = 16
SSM_STATE = 64
FOX_HEAD_DIM = 64
MEM_HEADS = 4
TOP_K = 8
N_EXPERT_GROUPS = 8
TOPK_GROUPS = 4
ROUTED_SCALE = 2.5
RMS_EPS = 1e-6

LANES = 128
SUBLANES = 8
MXU_DIM = 256
VMEM_LIMIT_BYTES = 56 * 1024 * 1024

_HIGHEST = lax.Precision.HIGHEST
_NT = (((1,), (1,)), ((), ()))


def _params(*sem):
    return pltpu.CompilerParams(dimension_semantics=sem, vmem_limit_bytes=VMEM_LIMIT_BYTES)


def _rms(x, g):
    return x * lax.rsqrt(jnp.mean(x * x, axis=-1, keepdims=True) + RMS_EPS) * g


def _sigmoid(x):
    return 1.0 / (1.0 + jnp.exp(-x))


def _silu(x):
    return x * _sigmoid(x)


def _gelu_tanh(x):
    return x * (0.5 * (1.0 + jnp.tanh(math.sqrt(2.0 / math.pi) * (x + 0.044715 * (x * x * x)))))


def _log_sigmoid(x):
    return jnp.minimum(x, 0.0) - jnp.log1p(jnp.exp(-jnp.abs(x)))


def _dot(a, b):
    return jnp.dot(a, b, preferred_element_type=F32)


def _const_spec(shape):
    nd = len(shape)
    return pl.BlockSpec(shape, lambda *_: (0,) * nd)


def _memkv_kernel(m_ref, g_ref, wk_ref, wv_ref, k_ref, v_ref, kb_ref, vb_ref):
    mn = _rms(m_ref[0], g_ref[...]).astype(BF16)
    k = _dot(mn, wk_ref[...])
    v = _dot(mn, wv_ref[...])
    k_ref[0] = k
    v_ref[0] = v
    kb_ref[0] = k.astype(BF16)
    vb_ref[0] = v.astype(BF16)


def _mem_kv(mem, g, wk, wv):
    bsz, n, d = mem.shape
    blk = pl.BlockSpec((1, n, d), lambda b: (b, 0, 0))
    return pl.pallas_call(
        _memkv_kernel,
        grid=(bsz,),
        in_specs=[blk, _const_spec((1, d)), _const_spec((d, d)), _const_spec((d, d))],
        out_specs=[blk, blk, blk, blk],
        out_shape=[jax.ShapeDtypeStruct((bsz, n, d), F32)] * 2 + [jax.ShapeDtypeStruct((bsz, n, d), BF16)] * 2,
        compiler_params=_params("parallel"),
        name="mem_kv",
    )(mem, g, wk, wv)


def _proj_in_kernel(x_ref, g_ref, w_ref, wf_ref, bf_ref, u_ref, q_ref, k_ref, v_ref, kb_ref, vb_ref, lf_ref,
                    *, n_ssm, n_fox, n_heads):
    xb = _rms(x_ref[0], g_ref[...]).astype(BF16)
    z = _dot(xb, w_ref[...])
    u_ref[0] = z[:, :n_ssm]
    o = n_ssm
    q_ref[0] = (z[:, o:o + n_fox] * (FOX_HEAD_DIM ** -0.5 * math.log2(math.e))).astype(BF16)
    k = z[:, o + n_fox:o + 2 * n_fox]
    v = z[:, o + 2 * n_fox:o + 3 * n_fox]
    k_ref[0] = k
    v_ref[0] = v
    kb_ref[0] = k.astype(BF16)
    vb_ref[0] = v.astype(BF16)
    zf = _dot(xb, wf_ref[...])
    lf_ref[0] = _log_sigmoid(zf + bf_ref[...])[:, :n_heads]


def _proj_in(x, g, w_main, w_f, b_f, tm, n_ssm, n_fox, n_heads):
    bsz, L, d = x.shape
    nmain = w_main.shape[1]
    row = lambda n: pl.BlockSpec((1, tm, n), lambda b, i: (b, i, 0))
    outs = [jax.ShapeDtypeStruct((bsz, L, n_ssm), F32), jax.ShapeDtypeStruct((bsz, L, n_fox), BF16),
            jax.ShapeDtypeStruct((bsz, L, n_fox), F32), jax.ShapeDtypeStruct((bsz, L, n_fox), F32),
            jax.ShapeDtypeStruct((bsz, L, n_fox), BF16), jax.ShapeDtypeStruct((bsz, L, n_fox), BF16),
            jax.ShapeDtypeStruct((bsz, L, n_heads), F32)]
    return pl.pallas_call(
        functools.partial(_proj_in_kernel, n_ssm=n_ssm, n_fox=n_fox, n_heads=n_heads),
        grid=(bsz, L // tm),
        in_specs=[row(d), _const_spec((1, d)), _const_spec((d, nmain)), _const_spec((d, LANES)),
                  _const_spec((1, LANES))],
        out_specs=[row(n_ssm), row(n_fox), row(n_fox), row(n_fox), row(n_fox), row(n_fox), row(n_heads)],
        out_shape=outs,
        compiler_params=_params("parallel", "parallel"),
        name="proj_in",
    )(x, g, w_main, w_f, b_f)


def _decay_kernel(lft_ref, c0_ref, dt_ref, car_ref, *, tl):
    @pl.when(pl.program_id(1) == 0)
    def _():
        car_ref[...] = c0_ref[0]
    r = lax.broadcasted_iota(jnp.int32, (tl, tl), 0)
    c = lax.broadcasted_iota(jnp.int32, (tl, tl), 1)
    tri = (r <= c).astype(F32)
    d = jnp.dot(lft_ref[0], tri, preferred_element_type=F32, precision=_HIGHEST) + car_ref[...]
    car_ref[...] = d[:, tl - 1:tl]
    rest = d * math.log2(math.e)
    for piece in range(3):
        part = rest.astype(BF16).astype(F32)
        dt_ref[0, piece] = part
        rest = rest - part


def _decay_cumsum(lft, c0, tl):
    bsz, nh, L = lft.shape
    return pl.pallas_call(
        functools.partial(_decay_kernel, tl=tl),
        grid=(bsz, L // tl),
        in_specs=[pl.BlockSpec((1, nh, tl), lambda b, i: (b, 0, i)), pl.BlockSpec((1, nh, 1), lambda b, i: (b, 0, 0))],
        out_specs=pl.BlockSpec((1, 3, nh, tl), lambda b, i: (b, 0, 0, i)),
        out_shape=jax.ShapeDtypeStruct((bsz, 3, nh, L), F32),
        scratch_shapes=[pltpu.VMEM((nh, 1), F32)],
        compiler_params=_params("parallel", "arbitrary"),
        name="decay_cumsum",
    )(lft, c0)


def _s5_kernel(u_ref, h0_ref, ar_ref, ai_ref, wb_ref, wc_ref, dsk_ref, wglu_ref, bglu_ref, gout_ref,
               y_ref, hl_ref, hs_ref, hst_ref, *, t_chunk, bsz, n_slab):
    @pl.when(pl.program_id(0) == 0)
    def _():
        hst_ref[...] = h0_ref[...]

    u = u_ref[...]
    ub = u.astype(BF16)
    slab_per_k = MXU_DIM // (2 * SSM_GROUP)

    def project(j):
        kt = j // slab_per_k
        hs_ref[:, MXU_DIM * j:MXU_DIM * (j + 1)] = _dot(ub[:, MXU_DIM * kt:MXU_DIM * (kt + 1)], wb_ref[j])

    def scan(j):
        lo, mid, hi = MXU_DIM * j, MXU_DIM * j + LANES, MXU_DIM * (j + 1)
        ar = ar_ref[:, LANES * j:LANES * (j + 1)]
        ai = ai_ref[:, LANES * j:LANES * (j + 1)]
        re, im = hst_ref[:, lo:mid], hst_ref[:, mid:hi]
        for t in range(t_chunk):
            rows = slice(t * bsz, (t + 1) * bsz)
            re, im = (ar * re - ai * im + hs_ref[rows, lo:mid], ar * im + ai * re + hs_ref[rows, mid:hi])
            hs_ref[rows, lo:mid] = re
            hs_ref[rows, mid:hi] = im
        hst_ref[:, lo:mid] = re
        hst_ref[:, mid:hi] = im

    group = 4
    for j in range(group):
        project(j)
    for g0 in range(0, n_slab, group):
        for j in range(g0 + group, min(g0 + 2 * group, n_slab)):
            project(j)
        for j in range(g0, g0 + group):
            scan(j)
    hl_ref[...] = hst_ref[...]

    halves = []
    for hf in range(n_slab // slab_per_k):
        acc = None
        for jj in range(slab_per_k):
            j = hf * slab_per_k + jj
            d = _dot(hs_ref[:, MXU_DIM * j:MXU_DIM * (j + 1)].astype(BF16), wc_ref[j])
            acc = d if acc is None else acc + d
        halves.append(acc)
    y = jnp.concatenate(halves, axis=1) + dsk_ref[...] * u
    y = _gelu_tanh(y)
    y = y * _sigmoid(_dot(y.astype(BF16), wglu_ref[...]) + bglu_ref[...])
    y_ref[...] = _rms(y, gout_ref[...]).astype(BF16)


def _s5(u_tb, h0, ar, ai, wb, wc, dsk, wglu, bglu, gout, t_chunk, bsz):
    rows, n_ssm = u_tb.shape
    n_state = h0.shape[1]
    n_slab = n_state // MXU_DIM
    r = t_chunk * bsz
    return pl.pallas_call(
        functools.partial(_s5_kernel, t_chunk=t_chunk, bsz=bsz, n_slab=n_slab),
        grid=(rows // r,),
        in_specs=[pl.BlockSpec((r, n_ssm), lambda c: (c, 0)), _const_spec((bsz, n_state)),
                  _const_spec((bsz, n_state // 2)), _const_spec((bsz, n_state // 2)),
                  _const_spec((n_slab, MXU_DIM, MXU_DIM)), _const_spec((n_slab, MXU_DIM, MXU_DIM)),
                  _const_spec((1, n_ssm)), _const_spec((n_ssm, n_ssm)), _const_spec((1, n_ssm)),
                  _const_spec((1, n_ssm))],
        out_specs=[pl.BlockSpec((r, n_ssm), lambda c: (c, 0)), _const_spec((bsz, n_state))],
        out_shape=[jax.ShapeDtypeStruct((rows, n_ssm), BF16), jax.ShapeDtypeStruct((bsz, n_state), F32)],
        scratch_shapes=[pltpu.VMEM((r, n_state), F32), pltpu.VMEM((bsz, n_state), F32)],
        compiler_params=_params("arbitrary"),
        name="s5_mixer",
    )(u_tb, h0, ar, ai, wb, wc, dsk, wglu, bglu, gout)


def _s5_tables(lam_re, lam_im, log_dt, b_re, b_im, c_re, c_im, bsz):
    G, P = lam_re.shape
    H = b_re.shape[-1]
    lr, li = lam_re.astype(F32), lam_im.astype(F32)
    dt = jnp.exp(log_dt.astype(F32))[:, None]
    mag = jnp.exp(lr * dt)
    a_re, a_im = mag * jnp.cos(li * dt), mag * jnp.sin(li * dt)
    den = lr * lr + li * li
    c_r = ((a_re - 1.0) * lr + a_im * li) / den
    c_i = (a_im * lr - (a_re - 1.0) * li) / den
    br, bi = b_re.astype(F32), b_im.astype(F32)
    bbar_re = c_r[..., None] * br - c_i[..., None] * bi
    bbar_im = c_r[..., None] * bi + c_i[..., None] * br
    n_pair = G // 2
    per_k = MXU_DIM // (2 * H)
    eye2 = jnp.eye(2, dtype=F32)
    place = jax.nn.one_hot(jnp.arange(n_pair) % per_k, per_k, dtype=F32)

    ar = jnp.broadcast_to(a_re.reshape(1, -1), (bsz, G * P))
    ai = jnp.broadcast_to(a_im.reshape(1, -1), (bsz, G * P))

    bb = jnp.stack([bbar_re, bbar_im]).reshape(2, n_pair, 2, P, H)
    wpair = jnp.einsum("ajgph,gk->jghakp", bb, eye2).reshape(n_pair, 2 * H, MXU_DIM)
    wb = jnp.einsum("jrc,jk->jkrc", wpair, place).reshape(n_pair, MXU_DIM, MXU_DIM)

    cc = jnp.stack([c_re.astype(F32), -c_im.astype(F32)]).reshape(2, n_pair, 2, H, P)
    cpair = jnp.einsum("ajghp,gk->jagpkh", cc, eye2).reshape(n_pair, MXU_DIM, 2 * H)
    wc = jnp.einsum("jnc,jk->jnkc", cpair, place).reshape(n_pair, MXU_DIM, MXU_DIM)
    return ar, ai, wb.astype(BF16), wc.astype(BF16)


def _state_to_lanes(re, im):
    bsz, G, P = re.shape
    s = jnp.stack([re, im], axis=1).reshape(bsz, 2, G // 2, 2, P)
    return s.transpose(0, 2, 1, 3, 4).reshape(bsz, 2 * G * P)


def _lanes_to_state(h, G, P):
    bsz = h.shape[0]
    s = h.reshape(bsz, G // 2, 2, 2, P).transpose(0, 2, 1, 3, 4).reshape(bsz, 2, G, P)
    return s[:, 0], s[:, 1]


FOX_AUG_ROWS = 16


def _fox_kernel(qt_ref, qa_ref, k_ref, vt_ref, o_ref, m_ref, l_ref, acc_ref, *, tq, tk, past):
    qi = pl.program_id(2)
    qt = qt_ref[0]
    row = lax.broadcasted_iota(jnp.int32, (LANES, tq), 0)
    zero = jnp.zeros_like(qt)
    pad = jnp.zeros((LANES - FOX_AUG_ROWS, tq), BF16)
    qts = []
    for hh in range(2):
        own = (row < FOX_HEAD_DIM) if hh == 0 else (row >= FOX_HEAD_DIM)
        aug = qa_ref[0, 0, FOX_AUG_ROWS * hh:FOX_AUG_ROWS * (hh + 1), :]
        qts.append(jnp.concatenate([jnp.where(own, qt, zero), aug, pad], axis=0))
    m_ref[...] = jnp.full(m_ref.shape, -1e30, F32)
    l_ref[...] = jnp.zeros(l_ref.shape, F32)
    acc_ref[...] = jnp.zeros(acc_ref.shape, F32)
    q_start = past + qi * tq
    n_full = (q_start + 1) // tk
    n_all = (q_start + tq + tk - 1) // tk

    def block(ks, masked, width):
        kb = k_ref[0, 0, pl.ds(ks, width), :]
        if masked:
            kpos = ks + lax.broadcasted_iota(jnp.int32, (width, tq), 0)
            qpos = q_start + lax.broadcasted_iota(jnp.int32, (width, tq), 1)
            visible = kpos <= qpos
        for hh in range(2):
            st = _dot(kb, qts[hh])
            if masked:
                st = jnp.where(visible, st, -jnp.inf)
            m_old = m_ref[hh]
            m_new = jnp.maximum(m_old, jnp.max(st, axis=0, keepdims=True))
            p = jnp.exp2(st - m_new)
            alpha = jnp.exp2(m_old - m_new)
            l_ref[hh] = alpha * l_ref[hh] + jnp.sum(p, axis=0, keepdims=True)
            vt = vt_ref[0, FOX_HEAD_DIM * hh:FOX_HEAD_DIM * (hh + 1), pl.ds(ks, width)]
            acc_ref[hh] = alpha * acc_ref[hh] + _dot(vt, p.astype(BF16))
            m_ref[hh] = m_new

    def steps(lo, hi, masked, width):
        def body(j, c):
            block(pl.multiple_of(j * width, width), masked, width)
            return c
        lax.fori_loop(lo, hi, body, 0)

    n_wide = 0
    if 2 * tk <= k_ref.shape[2]:
        n_wide = q_start // (2 * tk)
        steps(0, n_wide, False, 2 * tk)
    steps(2 * n_wide, n_full, False, tk)
    steps(n_full, n_all, True, tk)
    out_t = jnp.concatenate([acc_ref[0] / l_ref[0], acc_ref[1] / l_ref[1]], axis=0)
    o_ref[0] = out_t.T


def _fox(q_t, q_aug, k_aug, v_t, tq, tk, past):
    bsz, n_fox, L = q_t.shape
    n_pair = n_fox // LANES
    lk = k_aug.shape[2]
    return pl.pallas_call(
        functools.partial(_fox_kernel, tq=tq, tk=tk, past=past),
        grid=(bsz, n_pair, L // tq),
        in_specs=[pl.BlockSpec((1, LANES, tq), lambda b, h, i: (b, h, i)),
                  pl.BlockSpec((1, 1, 2 * FOX_AUG_ROWS, tq), lambda b, h, i: (b, h, 0, i)),
                  pl.BlockSpec((1, 1, lk, 2 * LANES), lambda b, h, i: (b, h, 0, 0)),
                  pl.BlockSpec((1, LANES, lk), lambda b, h, i: (b, h, 0))],
        out_specs=pl.BlockSpec((1, tq, LANES), lambda b, h, i: (b, i, h)),
        out_shape=jax.ShapeDtypeStruct((bsz, L, n_fox), F32),
        scratch_shapes=[pltpu.VMEM((2, 1, tq), F32), pltpu.VMEM((2, 1, tq), F32),
                        pltpu.VMEM((2, FOX_HEAD_DIM, tq), F32)],
        compiler_params=_params("parallel", "parallel", "arbitrary"),
        name="fox_attention",
    )(q_t, q_aug, k_aug, v_t)


def _fox_operands(qb, k_all, v_all, d_pieces, past, L):
    bsz, lk_pad, n_fox = k_all.shape
    n_heads = n_fox // FOX_HEAD_DIM
    n_pair = n_heads // 2
    pieces = d_pieces.astype(BF16)
    dk = (-pieces).reshape(bsz, 3, n_pair, 2, lk_pad)
    dk = dk.transpose(0, 2, 4, 3, 1).reshape(bsz, n_pair, lk_pad, 6)
    ones_k = jnp.ones((bsz, n_pair, lk_pad, 3), BF16)
    zeros_k = jnp.zeros((bsz, n_pair, lk_pad, LANES - 9), BF16)
    k_pair = k_all.reshape(bsz, lk_pad, n_pair, LANES).transpose(0, 2, 1, 3)
    k_aug = jnp.concatenate([k_pair, dk, ones_k, zeros_k], axis=-1)

    dq = pieces[:, :, :, past:past + L].transpose(0, 2, 1, 3)
    sel = jax.nn.one_hot(jnp.arange(n_heads) % 2, 2, dtype=BF16)
    ones_q = jnp.broadcast_to(jnp.repeat(sel, 3, axis=1)[None, :, :, None], (bsz, n_heads, 6, L))
    zeros_q = jnp.zeros((bsz, n_heads, FOX_AUG_ROWS - 9, L), BF16)
    q_aug = jnp.concatenate([ones_q, dq, zeros_q], axis=2).reshape(bsz, n_pair, 2 * FOX_AUG_ROWS, L)
    return qb.transpose(0, 2, 1), q_aug, k_aug, v_all.transpose(0, 2, 1)


def _route_gates(logits_t, ebias_t, n_experts):
    per_group = n_experts // N_EXPERT_GROUPS
    tokens = logits_t.shape[1]
    score = _sigmoid(logits_t)
    sel = score + ebias_t
    row = lax.broadcasted_iota(jnp.int32, (per_group, tokens), 0).astype(F32)
    neg = jnp.float32(-jnp.inf)

    def first_argmax(tile, best):
        return jnp.min(jnp.where(tile == best, row, float(per_group)), axis=0, keepdims=True)

    score_g, sel_g, gsc = [], [], []
    for g in range(N_EXPERT_GROUPS):
        sg = sel[per_group * g:per_group * (g + 1), :]
        score_g.append(score[per_group * g:per_group * (g + 1), :])
        sel_g.append(sg)
        m1 = jnp.max(sg, axis=0, keepdims=True)
        rest = jnp.where(row == first_argmax(sg, m1), neg, sg)
        gsc.append(m1 + jnp.max(rest, axis=0, keepdims=True))
    cur = []
    for g in range(N_EXPERT_GROUPS):
        ahead = jnp.zeros((1, tokens), F32)
        for o in range(N_EXPERT_GROUPS):
            if o == g:
                continue
            beats = (gsc[o] >= gsc[g]) if o < g else (gsc[o] > gsc[g])
            ahead = ahead + jnp.where(beats, 1.0, 0.0)
        cur.append(jnp.where(ahead < TOPK_GROUPS, sel_g[g], neg))
    chosen = [jnp.zeros((per_group, tokens), jnp.bool_) for _ in range(N_EXPERT_GROUPS)]
    for _ in range(TOP_K):
        best = cur[0]
        for g in range(1, N_EXPERT_GROUPS):
            best = jnp.maximum(best, cur[g])
        best = jnp.max(best, axis=0, keepdims=True)
        idx = None
        for g in range(N_EXPERT_GROUPS):
            cand = jnp.min(jnp.where(cur[g] == best, row + float(per_group * g), float(n_experts)), axis=0,
                           keepdims=True)
            idx = cand if idx is None else jnp.minimum(idx, cand)
        for g in range(N_EXPERT_GROUPS):
            hit = (row + float(per_group * g)) == idx
            chosen[g] = chosen[g] | hit
            cur[g] = jnp.where(hit, neg, cur[g])
    w = [jnp.where(chosen[g], score_g[g], 0.0) for g in range(N_EXPERT_GROUPS)]
    total = w[0]
    for g in range(1, N_EXPERT_GROUPS):
        total = total + w[g]
    total = jnp.sum(total, axis=0, keepdims=True)
    return [w[g] / total * ROUTED_SCALE for g in range(N_EXPERT_GROUPS)]


def _mid_kernel(x_ref, ys_ref, yf_ref, gfox_ref, wout_ref, gmq_ref, wmq_ref, mk_ref, mv_ref, wmo_ref, gffn_ref,
                wrt_ref, ebt_ref, ws1_ref, ws3_ref, ws2_ref, xs_ref, xn_ref, gt_ref, cnt_ref, *, n_experts):
    x = x_ref[0]
    yfn = _rms(yf_ref[0], gfox_ref[...]).astype(BF16)
    mix = jnp.concatenate([ys_ref[0], yfn], axis=1)
    x1 = x + _dot(mix, wout_ref[...])

    qm = _dot(_rms(x1, gmq_ref[...]).astype(BF16), wmq_ref[...])
    hd = qm.shape[1] // MEM_HEADS
    heads = []
    for h in range(MEM_HEADS):
        qh = (qm[:, hd * h:hd * (h + 1)] * (hd ** -0.5)).astype(BF16)
        s = lax.dot_general(qh, mk_ref[0, :, hd * h:hd * (h + 1)], _NT, preferred_element_type=F32)
        p = jnp.exp(s - jnp.max(s, axis=1, keepdims=True))
        o = _dot(p.astype(BF16), mv_ref[0, :, hd * h:hd * (h + 1)]) / jnp.sum(p, axis=1, keepdims=True)
        heads.append(o.astype(BF16))
    x2 = x1 + _dot(jnp.concatenate(heads, axis=1), wmo_ref[...])

    xn = _rms(x2, gffn_ref[...])
    xnb = xn.astype(BF16)
    hidden = _silu(_dot(xnb, ws1_ref[...])) * _dot(xnb, ws3_ref[...])
    xs_ref[0] = x2 + _dot(hidden.astype(BF16), ws2_ref[...])
    xn_ref[0] = xnb

    wr = wrt_ref[...]
    wr_hi = wr.astype(BF16)
    wr_lo = (wr - wr_hi.astype(F32)).astype(BF16)
    xn_lo = (xn - xnb.astype(F32)).astype(BF16)
    nt = lambda a, c: lax.dot_general(a, c, _NT, preferred_element_type=F32)
    logits_t = nt(wr_hi, xnb) + (nt(wr_hi, xn_lo) + nt(wr_lo, xnb))
    gates = _route_gates(logits_t, ebt_ref[...], n_experts)
    per_group = n_experts // N_EXPERT_GROUPS
    n_cnt = cnt_ref.shape[1]
    per_cnt = x.shape[0] // n_cnt
    for g in range(N_EXPERT_GROUPS):
        gt_ref[0, per_group * g:per_group * (g + 1), :] = gates[g]
        routed = jnp.where(gates[g] != 0.0, 1.0, 0.0)
        for c in range(n_cnt):
            cnt_ref[0, c, per_group * g:per_group * (g + 1), :] = jnp.sum(
                routed[:, per_cnt * c:per_cnt * (c + 1)], axis=1, keepdims=True)


def _mid(x, ys, yf, gfox, wout, gmq, wmq, mkb, mvb, wmo, gffn, wrt, ebt, ws1, ws3, ws2, tm):
    bsz, L, d = x.shape
    n_ssm, n_fox = ys.shape[2], yf.shape[2]
    n_mem = mkb.shape[1]
    n_experts = wrt.shape[0]
    fs = ws1.shape[1]
    row = lambda n: pl.BlockSpec((1, tm, n), lambda b, i: (b, i, 0))
    memspec = pl.BlockSpec((1, n_mem, d), lambda b, i: (b, 0, 0))
    weight = lambda shape: pl.BlockSpec(shape, lambda b, i: (0,) * len(shape), pipeline_mode=pl.Buffered(1))
    cb = min(tm, MOE_TOKENS)
    return pl.pallas_call(
        functools.partial(_mid_kernel, n_experts=n_experts),
        grid=(bsz, L // tm),
        in_specs=[row(d), row(n_ssm), row(n_fox), _const_spec((1, n_fox)),
                  weight((d, d)), _const_spec((1, d)),
                  weight((d, d)), memspec, memspec, weight((d, d)), _const_spec((1, d)),
                  _const_spec((n_experts, d)), _const_spec((n_experts, 1)), weight((d, fs)), weight((d, fs)),
                  weight((fs, d))],
        out_specs=[row(d), row(d), pl.BlockSpec((1, n_experts, tm), lambda b, i: (b, 0, i)),
                   pl.BlockSpec((1, tm // cb, n_experts, 1), lambda b, i: (b, i, 0, 0))],
        out_shape=[jax.ShapeDtypeStruct((bsz, L, d), F32), jax.ShapeDtypeStruct((bsz, L, d), BF16),
                   jax.ShapeDtypeStruct((bsz, n_experts, L), F32),
                   jax.ShapeDtypeStruct((bsz, L // cb, n_experts, 1), F32)],
        compiler_params=_params("parallel", "parallel"),
        name="mid_block",
    )(x, ys, yf, gfox, wout, gmq, wmq, mkb, mvb, wmo, gffn, wrt, ebt, ws1, ws3, ws2)


SEG_ROWS = 16
MOE_TOKENS = 512
COPY_ROWS = 4 * SEG_ROWS
GATHER_ROWS = 1024
_TN = (((0,), (0,)), ((), ()))
_NO_RANK = -(1 << 20)


def _sorted_rows(tb, n_experts):
    rows = TOP_K * tb + n_experts * (SEG_ROWS - 1)
    return -(-rows // GATHER_ROWS) * GATHER_ROWS


def _expert_tile(m):
    return 1024 if m >= 4096 else 128


def _dispatch_plan(cnt, ns, MOE_TILE):
    nblk, n_experts = cnt.shape

    def before(a, axis):
        n = a.shape[axis]
        earlier = jnp.arange(n)[:, None] < jnp.arange(n)[None, :]
        if axis == 0:
            return jnp.sum(jnp.where(earlier[:, :, None], a[:, None, :], 0), axis=0)
        return jnp.sum(jnp.where(earlier[None, :, :], a[:, :, None], 0), axis=1)

    cp = (cnt + SEG_ROWS - 1) // SEG_ROWS * SEG_ROWS
    o_loc = before(cp, 1)
    used = jnp.sum(cp, axis=1)
    tot_e = jnp.sum(cp, axis=0)
    reg_e = (tot_e + MOE_TILE - 1) // MOE_TILE * MOE_TILE
    base_e = before(reg_e[None, :], 1)[0]
    reg_end = base_e + reg_e
    dst = base_e[None, :] + before(cp, 0)
    rows_max = nblk * ns + n_experts * MOE_TILE
    n_tiles_max = -(-rows_max // MOE_TILE)
    n_tiles = reg_end[-1] // MOE_TILE
    tile_idx = jnp.clip(jnp.arange(n_tiles_max, dtype=jnp.int32), 0, jnp.maximum(n_tiles - 1, 0))
    tile_expert = jnp.sum((reg_end[None, :] <= (tile_idx * MOE_TILE)[:, None]).astype(jnp.int32), axis=1)
    tile_expert = jnp.minimum(tile_expert, n_experts - 1)
    n_pc = ns // SEG_ROWS
    piece_row = jnp.arange(n_pc, dtype=jnp.int32) * SEG_ROWS
    seg_end = o_loc + cp
    piece_e = jnp.sum((seg_end[:, None, :] <= piece_row[None, :, None]).astype(jnp.int32), axis=2)
    live = piece_row[None, :] < used[:, None]
    piece_e = jnp.minimum(piece_e, n_experts - 1)
    own = piece_e[:, :, None] == jnp.arange(n_experts, dtype=jnp.int32)[None, None, :]
    pick = lambda table: jnp.sum(jnp.where(own, table[:, None, :], 0), axis=2)
    rank0 = piece_row[None, :] - pick(o_loc)
    piece_rank = jnp.where(live, rank0, _NO_RANK)
    i32 = lambda a: a.astype(jnp.int32).reshape(-1)
    per_block = lambda a: a.astype(jnp.int32).reshape(nblk, 1, -1)

    def copy_list(n_e, first_rank, rows_each, n_max):
        end = before(n_e, 1) + n_e
        j = jnp.arange(n_max, dtype=jnp.int32)
        e_of = jnp.minimum(jnp.sum((end[:, None, :] <= j[None, :, None]).astype(jnp.int32), axis=2), n_experts - 1)
        mine = e_of[:, :, None] == jnp.arange(n_experts, dtype=jnp.int32)[None, None, :]
        take = lambda table: jnp.sum(jnp.where(mine, table[:, None, :], 0), axis=2)
        rank = take(first_rank) + (j[None, :] - take(end - n_e)) * rows_each
        ok = j[None, :] < jnp.sum(n_e, axis=1)[:, None]
        return (per_block(jnp.where(ok, take(o_loc) + rank, 0)), per_block(jnp.where(ok, take(dst) + rank, 0)),
                i32(jnp.sum(n_e, axis=1)))

    per_copy = COPY_ROWS // SEG_ROWS
    n_wide_e = (cp // SEG_ROWS) // per_copy
    n_seg_e = cp // SEG_ROWS - n_wide_e * per_copy
    wide_src, wide_dst, n_wide = copy_list(n_wide_e, jnp.zeros_like(cp), COPY_ROWS, ns // COPY_ROWS)
    seg_src, seg_dst, n_seg = copy_list(n_seg_e, n_wide_e * COPY_ROWS, SEG_ROWS, n_experts * (per_copy - 1))
    return dict(n_piece=i32(used // SEG_ROWS), piece_e=per_block(piece_e), piece_rank=per_block(piece_rank),
                wide_src=wide_src, wide_dst=wide_dst, n_wide=n_wide, seg_src=seg_src, seg_dst=seg_dst, n_seg=n_seg,
                tail_start=i32(base_e + tot_e),
                tail_chunks=i32((reg_e - tot_e) // SEG_ROWS), tile_idx=i32(tile_idx), tile_expert=i32(tile_expert),
                n_tiles=i32(n_tiles), n_tiles_max=n_tiles_max)


def _slot_ranks(gates, tb):
    sel = gates != 0.0
    r = lax.broadcasted_iota(jnp.int32, (tb, tb), 0)
    c = lax.broadcasted_iota(jnp.int32, (tb, tb), 1)
    earlier = jnp.where(r < c, 1.0, 0.0).astype(BF16)
    rank = _dot(jnp.where(sel, 1.0, 0.0).astype(BF16), earlier)
    return jnp.where(sel, rank, -1.0)


def _build_one_hot(out_ref, rm_ref, pe_ref, pr_ref, first, count, tb, value_row):
    rows = lax.broadcasted_iota(jnp.int32, (SEG_ROWS, tb), 0).astype(F32)
    for i in range(first, first + count):
        e = pe_ref[0, 0, i]
        hit = rows == (rm_ref[pl.ds(e, 1), :] - pr_ref[0, 0, i].astype(F32))
        out_ref[SEG_ROWS * i:SEG_ROWS * (i + 1), :] = jnp.where(hit, value_row(e), 0.0).astype(BF16)


def _wait_rows(n_rows, make_copy):
    wide = 16 * SEG_ROWS
    n_wide = n_rows // wide

    def wide_step(i, c):
        make_copy(wide).wait()
        return c

    def seg_step(i, c):
        make_copy(SEG_ROWS).wait()
        return c

    lax.fori_loop(0, n_wide, wide_step, 0)
    lax.fori_loop(0, (n_rows - n_wide * wide) // SEG_ROWS, seg_step, 0)


def _issue_copies(n_wide, n_seg, tables, make_copy):
    wide_src, wide_dst, seg_src, seg_dst = tables

    def wide(i, c):
        make_copy(pl.multiple_of(wide_src[0, 0, i], SEG_ROWS), pl.multiple_of(wide_dst[0, 0, i], SEG_ROWS),
                  COPY_ROWS).start()
        return c

    def seg(i, c):
        make_copy(pl.multiple_of(seg_src[0, 0, i], SEG_ROWS), pl.multiple_of(seg_dst[0, 0, i], SEG_ROWS),
                  SEG_ROWS).start()
        return c

    lax.fori_loop(0, n_wide, wide, 0)
    lax.fori_loop(0, n_seg, seg, 0)


def _dispatch_kernel(n_piece_ref, n_wide_ref, n_seg_ref, tail_start_ref, tail_chunks_ref, pe_ref, pr_ref,
                     ws_ref, wd_ref, ss_ref, sd_ref, xn_ref, gt_ref, *rest, tb, ns, n_experts, nblk, fill_tail):
    xg_ref, rm_ref, g_ref, xsb_ref, z_ref, sem = rest[-6:]
    b = pl.program_id(0)
    n_pc = ns // SEG_ROWS
    used = n_piece_ref[b] * SEG_ROWS
    rm_ref[...] = _slot_ranks(gt_ref[...], tb)
    per_slice = GATHER_ROWS // SEG_ROWS
    n_slices = ns // GATHER_ROWS
    build = lambda t: _build_one_hot(g_ref, rm_ref, pe_ref, pr_ref, per_slice * t, per_slice, tb, lambda e: 1.0)

    slot = b % 2
    build(0)
    for t in range(n_slices):
        if t + 1 < n_slices:
            build(t + 1)
        sl = slice(GATHER_ROWS * t, GATHER_ROWS * (t + 1))
        xsb_ref[slot, sl, :] = _dot(g_ref[sl, :], xn_ref[...]).astype(BF16)

    def rows_copy(buf, src_row, dst_row, size):
        return pltpu.make_async_copy(xsb_ref.at[buf, pl.ds(src_row, size)], xg_ref.at[pl.ds(dst_row, size)],
                                     sem.at[buf])

    _issue_copies(n_wide_ref[b], n_seg_ref[b], (ws_ref, wd_ref, ss_ref, sd_ref),
                  lambda block_row, global_row, size: rows_copy(slot, block_row, global_row, size))

    @pl.when(b > 0)
    def _():
        _wait_rows(n_piece_ref[b - 1] * SEG_ROWS, lambda size: rows_copy(1 - slot, 0, 0, size))

    @pl.when(b == nblk - 1)
    def _():
        _wait_rows(used, lambda size: rows_copy(slot, 0, 0, size))

    @pl.when(jnp.logical_and(b == nblk - 1, fill_tail))
    def _():
        z_ref[...] = jnp.zeros(z_ref.shape, BF16)

        wide = z_ref.shape[0]
        per_wide = wide // SEG_ROWS

        def tail_copy(row, size):
            return pltpu.make_async_copy(z_ref.at[pl.ds(0, size)], xg_ref.at[pl.ds(pl.multiple_of(row, SEG_ROWS), size)],
                                         sem.at[slot])

        def each_tail(action):
            def per_expert(e, carry):
                n_wide = tail_chunks_ref[e] // per_wide

                def wide_piece(i, c2):
                    action(tail_copy(tail_start_ref[e] + i * wide, wide))
                    return c2

                def seg_piece(i, c2):
                    action(tail_copy(tail_start_ref[e] + n_wide * wide + i * SEG_ROWS, SEG_ROWS))
                    return c2

                lax.fori_loop(0, n_wide, wide_piece, 0)
                lax.fori_loop(0, tail_chunks_ref[e] - n_wide * per_wide, seg_piece, 0)
                return carry
            lax.fori_loop(0, n_experts, per_expert, 0)

        each_tail(lambda cp: cp.start())
        each_tail(lambda cp: cp.wait())


def _expert_kernel(tile_idx_ref, tile_expert_ref, n_tiles_ref, x_ref, w1_ref, w3_ref, w2_ref, o_ref):
    @pl.when(pl.program_id(0) < n_tiles_ref[0])
    def _():
        x = x_ref[...]
        h = _silu(_dot(x, w1_ref[0].astype(BF16))) * _dot(x, w3_ref[0].astype(BF16))
        o_ref[...] = _dot(h.astype(BF16), w2_ref[0].astype(BF16)).astype(BF16)


def _combine_kernel(n_piece_ref, n_wide_ref, n_seg_ref, pe_ref, pr_ref, ws_ref, wd_ref, ss_ref, sd_ref,
                    ws_next_ref, wd_next_ref, ss_next_ref, sd_next_ref,
                    og_ref, gt_ref, rm_ref, xs_ref, gfin_ref, y_ref, gw_ref, ob_ref, sem, *, tb, ns, n_experts, nblk):
    b = pl.program_id(0)
    slot = b % 2
    n_pc = ns // SEG_ROWS

    def rows_copy(buf, src_row, dst_row, size):
        return pltpu.make_async_copy(og_ref.at[pl.ds(src_row, size)], ob_ref.at[buf, pl.ds(dst_row, size)],
                                     sem.at[buf])

    def fetch_block(blk, tables, buf):
        _issue_copies(n_wide_ref[blk], n_seg_ref[blk], tables,
                      lambda block_row, global_row, size: rows_copy(buf, global_row, block_row, size))

    @pl.when(b == 0)
    def _():
        fetch_block(b, (ws_ref, wd_ref, ss_ref, sd_ref), slot)

    @pl.when(b + 1 < nblk)
    def _():
        fetch_block(jnp.minimum(b + 1, nblk - 1), (ws_next_ref, wd_next_ref, ss_next_ref, sd_next_ref), 1 - slot)

    _wait_rows(n_piece_ref[b] * SEG_ROWS, lambda size: rows_copy(slot, 0, 0, size))

    def clear(i, c):
        ob_ref[slot, pl.ds(pl.multiple_of((n_piece_ref[b] + i) * SEG_ROWS, SEG_ROWS), SEG_ROWS), :] = jnp.zeros(
            (SEG_ROWS, ob_ref.shape[2]), BF16)
        return c

    lax.fori_loop(0, n_pc - n_piece_ref[b], clear, 0)

    per_slice = GATHER_ROWS // SEG_ROWS
    n_slices = ns // GATHER_ROWS
    build = lambda t: _build_one_hot(gw_ref, rm_ref, pe_ref, pr_ref, per_slice * t, per_slice, tb,
                                     lambda e: gt_ref[pl.ds(e, 1), :])
    build(0)
    y = xs_ref[...]
    for t in range(n_slices):
        if t + 1 < n_slices:
            build(t + 1)
        sl = slice(GATHER_ROWS * t, GATHER_ROWS * (t + 1))
        y = y + lax.dot_general(gw_ref[sl, :], ob_ref[slot, sl, :], _TN, preferred_element_type=F32)
    y_ref[...] = _rms(y, gfin_ref[...])


def _moe(groups, w1, w3, w2, gfin):
    d = groups[0][0].shape[1]
    n_experts, _, f = w1.shape
    tb = MOE_TOKENS
    blocks = [g[0].shape[0] // tb for g in groups]
    assert all(g[0].shape[0] == n * tb for g, n in zip(groups, blocks)), "token groups must be whole dispatch blocks"
    ns = _sorted_rows(tb, n_experts)
    MOE_TILE = _expert_tile(sum(blocks) * tb)
    plan = _dispatch_plan(jnp.concatenate([g[2] for g in groups], axis=0), ns, MOE_TILE)
    n_tiles_max = plan["n_tiles_max"]
    rows_max = n_tiles_max * MOE_TILE
    this_block = lambda t: pl.BlockSpec((1, 1, t.shape[2]), lambda b, *_: (b, 0, 0), memory_space=pltpu.SMEM)
    n_prefetch_dispatch = 5
    n_tables = 6

    def group_tables(first, nblk):
        rows = slice(first, first + nblk)
        build = (plan["piece_e"][rows], plan["piece_rank"][rows])
        copies = tuple(plan[k][rows] for k in ("wide_src", "wide_dst", "seg_src", "seg_dst"))
        counts = tuple(plan[k][rows] for k in ("n_piece", "n_wide", "n_seg"))
        return build, copies, counts

    xg = None
    slot_ranks = []
    first = 0
    for gi, ((xn, gates_t, _, _), nblk) in enumerate(zip(groups, blocks)):
        build_tables, copy_tables, counts = group_tables(first, nblk)
        first += nblk
        earlier = () if xg is None else (xg,)
        xg = pl.pallas_call(
            functools.partial(_dispatch_kernel, tb=tb, ns=ns, n_experts=n_experts, nblk=nblk,
                              fill_tail=gi == len(groups) - 1),
            grid_spec=pltpu.PrefetchScalarGridSpec(
                num_scalar_prefetch=n_prefetch_dispatch, grid=(nblk,),
                in_specs=[this_block(t) for t in build_tables + copy_tables] + [
                          pl.BlockSpec((tb, d), lambda b, *_: (b, 0)),
                          pl.BlockSpec((n_experts, tb), lambda b, *_: (0, b))] + [
                          pl.BlockSpec(memory_space=pl.ANY) for _ in earlier],
                out_specs=[pl.BlockSpec(memory_space=pl.ANY), pl.BlockSpec((n_experts, tb), lambda b, *_: (0, b))],
                scratch_shapes=[pltpu.VMEM((ns, tb), BF16), pltpu.VMEM((2, ns, d), BF16),
                                pltpu.VMEM((16 * SEG_ROWS, d), BF16), pltpu.SemaphoreType.DMA((2,))]),
            out_shape=[jax.ShapeDtypeStruct((rows_max, d), BF16), jax.ShapeDtypeStruct((n_experts, nblk * tb), F32)],
            input_output_aliases={n_prefetch_dispatch + n_tables + 2: 0} if earlier else {},
            compiler_params=_params("arbitrary"),
            name="moe_dispatch",
        )(*counts, plan["tail_start"], plan["tail_chunks"], *build_tables, *copy_tables, xn, gates_t, *earlier)
        xg, ranks = xg
        slot_ranks.append(ranks)

    tile = lambda i, idx, ex, n: (idx[i], 0)
    og = pl.pallas_call(
        _expert_kernel,
        grid_spec=pltpu.PrefetchScalarGridSpec(
            num_scalar_prefetch=3, grid=(n_tiles_max,),
            in_specs=[pl.BlockSpec((MOE_TILE, d), tile),
                      pl.BlockSpec((1, d, f), lambda i, idx, ex, n: (ex[i], 0, 0)),
                      pl.BlockSpec((1, d, f), lambda i, idx, ex, n: (ex[i], 0, 0)),
                      pl.BlockSpec((1, f, d), lambda i, idx, ex, n: (ex[i], 0, 0))],
            out_specs=pl.BlockSpec((MOE_TILE, d), tile)),
        out_shape=jax.ShapeDtypeStruct((rows_max, d), BF16),
        compiler_params=_params("arbitrary"),
        name="moe_experts",
    )(plan["tile_idx"], plan["tile_expert"], plan["n_tiles"], xg, w1, w3, w2)

    results = []
    first = 0
    for (xn, gates_t, _, xs), nblk, ranks in zip(groups, blocks, slot_ranks):
        build_tables, copy_tables, counts = group_tables(first, nblk)
        first += nblk
        next_block = lambda t, nblk=nblk: pl.BlockSpec(
            (1, 1, t.shape[2]), lambda b, *_: (jnp.minimum(b + 1, nblk - 1), 0, 0), memory_space=pltpu.SMEM)
        results.append(pl.pallas_call(
            functools.partial(_combine_kernel, tb=tb, ns=ns, n_experts=n_experts, nblk=nblk),
            grid_spec=pltpu.PrefetchScalarGridSpec(
                num_scalar_prefetch=3, grid=(nblk,),
                in_specs=[this_block(t) for t in build_tables + copy_tables] + [next_block(t) for t in copy_tables] + [
                          pl.BlockSpec(memory_space=pl.ANY), pl.BlockSpec((n_experts, tb), lambda b, *_: (0, b)),
                          pl.BlockSpec((n_experts, tb), lambda b, *_: (0, b)),
                          pl.BlockSpec((tb, d), lambda b, *_: (b, 0)), pl.BlockSpec((1, d), lambda b, *_: (0, 0))],
                out_specs=pl.BlockSpec((tb, d), lambda b, *_: (b, 0)),
                scratch_shapes=[pltpu.VMEM((ns, tb), BF16), pltpu.VMEM((2, ns, d), BF16),
                                pltpu.SemaphoreType.DMA((2,))]),
            out_shape=jax.ShapeDtypeStruct((nblk * tb, d), F32),
            compiler_params=_params("arbitrary"),
            name="moe_combine",
        )(*counts, *build_tables, *copy_tables, *copy_tables, og, gates_t, ranks, xs, gfin))
    return results


def _tiles(L):
    t_row = min(L, 512)
    t_mid = min(L, 1024)
    t_key = 512
    t_scan = min(L, 64)
    return t_row, t_mid, t_key, t_scan


def _group(x, h0_re, h0_im, past_k, past_v, past_lf, mkb, mvb, p, w):
    bsz, L, d = x.shape
    n_heads = p["b_f"].shape[0]
    n_fox = n_heads * FOX_HEAD_DIM
    n_ssm = p["d_skip"].shape[0]
    G, P = p["lam_re"].shape
    t_row, t_mid, t_key, t_scan = _tiles(L)

    u, qb, k, v, kb, vb, lf = _proj_in(x, w["g_mix"], w["w_main"], w["w_f"], w["b_f"], t_row, n_ssm, n_fox, n_heads)

    u_tb = u.transpose(1, 0, 2).reshape(L * bsz, n_ssm)
    if h0_re is None:
        h0 = jnp.zeros((bsz, 2 * G * P), F32)
    else:
        h0 = _state_to_lanes(h0_re.astype(F32), h0_im.astype(F32))
    ys_tb, h_last = _s5(u_tb, h0, w["ar"], w["ai"], w["wb"], w["wc"], w["d_skip"], w["w_glu"], w["b_glu"],
                        w["g_ssm_out"], t_scan, bsz)
    ys = ys_tb.reshape(L, bsz, n_ssm).transpose(1, 0, 2)
    hl_re, hl_im = _lanes_to_state(h_last, G, P)

    if past_k is None:
        past = 0
        lf_all, k_all, v_all = lf, kb, vb
    else:
        past = past_k.shape[1]
        lf_all = jnp.concatenate([past_lf.astype(F32), lf], axis=1)
        k_all = jnp.concatenate([past_k.reshape(bsz, past, n_fox).astype(BF16), kb], axis=1)
        v_all = jnp.concatenate([past_v.reshape(bsz, past, n_fox).astype(BF16), vb], axis=1)
    lk = lf_all.shape[1]
    lk_pad = -(-lk // t_key) * t_key
    lf_t = jnp.pad(lf_all.transpose(0, 2, 1), ((0, 0), (0, 0), (0, lk_pad - lk)))
    d_pieces = _decay_cumsum(lf_t.reshape(1, bsz * n_heads, lk_pad), jnp.zeros((1, bsz * n_heads, 1), F32), t_key)
    d_pieces = d_pieces.reshape(3, bsz, n_heads, lk_pad).transpose(1, 0, 2, 3)
    k_all = jnp.pad(k_all, ((0, 0), (0, lk_pad - lk), (0, 0)))
    v_all = jnp.pad(v_all, ((0, 0), (0, lk_pad - lk), (0, 0)))
    t_att = t_key if L >= t_key else lk_pad
    yf = _fox(*_fox_operands(qb, k_all, v_all, d_pieces, past, L), t_row, t_att, past)

    xs, xnb, gates_t, cnt = _mid(x, ys, yf, w["g_fox_out"], w["w_out"], w["g_mem_q"], w["w_mq"], mkb, mvb, w["w_mo"],
                                 w["g_ffn"], w["w_router_t"], w["e_bias_t"], w["ws1"], w["ws3"], w["ws2"], t_mid)
    m = bsz * L
    n_experts = gates_t.shape[1]
    cnt = cnt.reshape(m // MOE_TOKENS, -1, n_experts).sum(axis=1).astype(jnp.int32)
    routed = (xnb.reshape(m, d), gates_t.transpose(1, 0, 2).reshape(n_experts, m), cnt, xs.reshape(m, d))
    return (routed, hl_re, hl_im, k.reshape(bsz, L, n_heads, FOX_HEAD_DIM),
            v.reshape(bsz, L, n_heads, FOX_HEAD_DIM), lf)


def kernel(x_prompt, x_sample, state_ssm_re, state_ssm_im, cache_fox_k, cache_fox_v, cache_fox_logf, cache_mem_k, cache_mem_v, mem_prompt, g_mix, w_in, b_f, lam_re, lam_im, log_dt, b_re, b_im, c_re, c_im, d_skip, w_glu, b_glu, g_ssm_out, g_fox_out, w_out, g_mem_q, g_mem_kv, w_mq, w_mk, w_mv, w_mo, g_ffn, w_router, e_bias, w1, w3, w2, ws1, ws3, ws2, g_final):
    depth = w_in.shape[0]
    bsz = x_prompt.shape[0]
    hp, hs = x_prompt, x_sample
    outs_p, outs_s = [], []
    for l in range(depth):
        p = dict(b_f=b_f[l], lam_re=lam_re[l], d_skip=d_skip[l])
        n_heads = b_f.shape[1]
        n_ssm = d_skip.shape[1]
        n_main = w_in.shape[2] - n_heads
        row = lambda a: a.reshape(1, -1).astype(F32)
        ar, ai, wb, wc = _s5_tables(lam_re[l], lam_im[l], log_dt[l], b_re[l], b_im[l], c_re[l], c_im[l], bsz)
        w = dict(
            g_mix=row(g_mix[l]), w_main=w_in[l][:, :n_main].astype(BF16),
            w_f=jnp.pad(w_in[l][:, n_main:], ((0, 0), (0, LANES - n_heads))).astype(BF16),
            b_f=jnp.pad(b_f[l].astype(F32), (0, LANES - n_heads)).reshape(1, LANES),
            ar=ar, ai=ai, wb=wb, wc=wc, d_skip=row(d_skip[l]), w_glu=w_glu[l].astype(BF16), b_glu=row(b_glu[l]),
            g_ssm_out=row(g_ssm_out[l]), g_fox_out=row(g_fox_out[l]), w_out=w_out[l].astype(BF16),
            g_mem_q=row(g_mem_q[l]), w_mq=w_mq[l].astype(BF16), w_mo=w_mo[l].astype(BF16), g_ffn=row(g_ffn[l]),
            w_router_t=w_router[l].astype(F32).T, e_bias_t=e_bias[l].astype(F32).reshape(-1, 1),
            ws1=ws1[l].astype(BF16), ws3=ws3[l].astype(BF16), ws2=ws2[l].astype(BF16),
            w1=w1[l], w3=w3[l], w2=w2[l], g_final=row(g_final))
        assert depth == 1, "final norm fusion assumes a single layer"
        mk_p, mv_p, mkb_p, mvb_p = _mem_kv(mem_prompt, row(g_mem_kv[l]), w_mk[l].astype(BF16), w_mv[l].astype(BF16))
        nm, mh = mem_prompt.shape[1], MEM_HEADS
        routed_p, re_p, im_p, k_p, v_p, lf_p = _group(hp, None, None, None, None, None, mkb_p, mvb_p, p, w)
        cm_k = cache_mem_k[l].reshape(bsz, nm, -1).astype(BF16)
        cm_v = cache_mem_v[l].reshape(bsz, nm, -1).astype(BF16)
        routed_s, re_s, im_s, k_s, v_s, lf_s = _group(hs, state_ssm_re[l], state_ssm_im[l], cache_fox_k[l],
                                                      cache_fox_v[l], cache_fox_logf[l], cm_k, cm_v, p, w)
        y_p, y_s = _moe([routed_p, routed_s], w["w1"], w["w3"], w["w2"], w["g_final"])
        hp, hs = y_p.reshape(hp.shape), y_s.reshape(hs.shape)
        outs_p.append((re_p, im_p, k_p, v_p, lf_p, mk_p.reshape(bsz, nm, mh, -1), mv_p.reshape(bsz, nm, mh, -1)))
        outs_s.append((re_s, im_s, k_s, v_s, lf_s))
    stack = lambda outs, i: jnp.stack([o[i] for o in outs])
    return (hp, hs) + tuple(stack(outs_p, i) for i in range(7)) + tuple(stack(outs_s, i) for i in range(5))
```

```python
import functools
import math

import jax
import jax.numpy as jnp
from jax import lax
from jax.experimental import pallas as pl
from jax.experimental.pallas import tpu as pltpu

F32 = jnp.float32
BF16 = jnp.bfloat16

SSM_GROUP = 16
SSM_STATE = 64
FOX_HEAD_DIM = 64
MEM_HEADS = 4
TOP_K = 8
N_EXPERT_GROUPS = 8
TOPK_GROUPS = 4
ROUTED_SCALE = 2.5
RMS_EPS = 1e-6

LANES = 128
SUBLANES = 8
MXU_DIM = 256
VMEM_LIMIT_BYTES = 56 * 1024 * 1024

_HIGHEST = lax.Precision.HIGHEST
_NT = (((1,), (1,)), ((), ()))


def _params(*sem):
    return pltpu.CompilerParams(dimension_semantics=sem, vmem_limit_bytes=VMEM_LIMIT_BYTES)


def _rms(x, g):
    return x * lax.rsqrt(jnp.mean(x * x, axis=-1, keepdims=True) + RMS_EPS) * g


def _sigmoid(x):
    return 1.0 / (1.0 + jnp.exp(-x))


def _silu(x):
    return x * _sigmoid(x)


def _gelu_tanh(x):
    return x * (0.5 * (1.0 + jnp.tanh(math.sqrt(2.0 / math.pi) * (x + 0.044715 * (x * x * x)))))


def _log_sigmoid(x):
    return jnp.minimum(x, 0.0) - jnp.log1p(jnp.exp(-jnp.abs(x)))


def _dot(a, b):
    return jnp.dot(a, b, preferred_element_type=F32)


def _const_spec(shape):
    nd = len(shape)
    return pl.BlockSpec(shape, lambda *_: (0,) * nd)


def _memkv_kernel(m_ref, g_ref, wk_ref, wv_ref, k_ref, v_ref, kb_ref, vb_ref):
    mn = _rms(m_ref[0], g_ref[...]).astype(BF16)
    k = _dot(mn, wk_ref[...])
    v = _dot(mn, wv_ref[...])
    k_ref[0] = k
    v_ref[0] = v
    kb_ref[0] = k.astype(BF16)
    vb_ref[0] = v.astype(BF16)


def _mem_kv(mem, g, wk, wv):
    bsz, n, d = mem.shape
    blk = pl.BlockSpec((1, n, d), lambda b: (b, 0, 0))
    return pl.pallas_call(
        _memkv_kernel,
        grid=(bsz,),
        in_specs=[blk, _const_spec((1, d)), _const_spec((d, d)), _const_spec((d, d))],
        out_specs=[blk, blk, blk, blk],
        out_shape=[jax.ShapeDtypeStruct((bsz, n, d), F32)] * 2 + [jax.ShapeDtypeStruct((bsz, n, d), BF16)] * 2,
        compiler_params=_params("parallel"),
        name="mem_kv",
    )(mem, g, wk, wv)


def _proj_in_kernel(x_ref, g_ref, w_ref, wf_ref, bf_ref, u_ref, q_ref, k_ref, v_ref, kb_ref, vb_ref, lf_ref,
                    *, n_ssm, n_fox, n_heads):
    xb = _rms(x_ref[0], g_ref[...]).astype(BF16)
    z = _dot(xb, w_ref[...])
    u_ref[0] = z[:, :n_ssm]
    o = n_ssm
    q_ref[0] = (z[:, o:o + n_fox] * (FOX_HEAD_DIM ** -0.5 * math.log2(math.e))).astype(BF16)
    k = z[:, o + n_fox:o + 2 * n_fox]
    v = z[:, o + 2 * n_fox:o + 3 * n_fox]
    k_ref[0] = k
    v_ref[0] = v
    kb_ref[0] = k.astype(BF16)
    vb_ref[0] = v.astype(BF16)
    zf = _dot(xb, wf_ref[...])
    lf_ref[0] = _log_sigmoid(zf + bf_ref[...])[:, :n_heads]


def _proj_in(x, g, w_main, w_f, b_f, tm, n_ssm, n_fox, n_heads):
    bsz, L, d = x.shape
    nmain = w_main.shape[1]
    row = lambda n: pl.BlockSpec((1, tm, n), lambda b, i: (b, i, 0))
    outs = [jax.ShapeDtypeStruct((bsz, L, n_ssm), F32), jax.ShapeDtypeStruct((bsz, L, n_fox), BF16),
            jax.ShapeDtypeStruct((bsz, L, n_fox), F32), jax.ShapeDtypeStruct((bsz, L, n_fox), F32),
            jax.ShapeDtypeStruct((bsz, L, n_fox), BF16), jax.ShapeDtypeStruct((bsz, L, n_fox), BF16),
            jax.ShapeDtypeStruct((bsz, L, n_heads), F32)]
    return pl.pallas_call(
        functools.partial(_proj_in_kernel, n_ssm=n_ssm, n_fox=n_fox, n_heads=n_heads),
        grid=(bsz, L // tm),
        in_specs=[row(d), _const_spec((1, d)), _const_spec((d, nmain)), _const_spec((d, LANES)),
                  _const_spec((1, LANES))],
        out_specs=[row(n_ssm), row(n_fox), row(n_fox), row(n_fox), row(n_fox), row(n_fox), row(n_heads)],
        out_shape=outs,
        compiler_params=_params("parallel", "parallel"),
        name="proj_in",
    )(x, g, w_main, w_f, b_f)


def _decay_kernel(lft_ref, c0_ref, dt_ref, car_ref, *, tl):
    @pl.when(pl.program_id(1) == 0)
    def _():
        car_ref[...] = c0_ref[0]
    r = lax.broadcasted_iota(jnp.int32, (tl, tl), 0)
    c = lax.broadcasted_iota(jnp.int32, (tl, tl), 1)
    tri = (r <= c).astype(F32)
    d = jnp.dot(lft_ref[0], tri, preferred_element_type=F32, precision=_HIGHEST) + car_ref[...]
    car_ref[...] = d[:, tl - 1:tl]
    rest = d * math.log2(math.e)
    for piece in range(3):
        part = rest.astype(BF16).astype(F32)
        dt_ref[0, piece] = part
        rest = rest - part


def _decay_cumsum(lft, c0, tl):
    bsz, nh, L = lft.shape
    return pl.pallas_call(
        functools.partial(_decay_kernel, tl=tl),
        grid=(bsz, L // tl),
        in_specs=[pl.BlockSpec((1, nh, tl), lambda b, i: (b, 0, i)), pl.BlockSpec((1, nh, 1), lambda b, i: (b, 0, 0))],
        out_specs=pl.BlockSpec((1, 3, nh, tl), lambda b, i: (b, 0, 0, i)),
        out_shape=jax.ShapeDtypeStruct((bsz, 3, nh, L), F32),
        scratch_shapes=[pltpu.VMEM((nh, 1), F32)],
        compiler_params=_params("parallel", "arbitrary"),
        name="decay_cumsum",
    )(lft, c0)


def _s5_kernel(u_ref, h0_ref, ar_ref, ai_ref, wb_ref, wc_ref, dsk_ref, wglu_ref, bglu_ref, gout_ref,
               y_ref, hl_ref, hs_ref, hst_ref, *, t_chunk, bsz, n_slab):
    @pl.when(pl.program_id(0) == 0)
    def _():
        hst_ref[...] = h0_ref[...]

    u = u_ref[...]
    ub = u.astype(BF16)
    slab_per_k = MXU_DIM // (2 * SSM_GROUP)

    def project(j):
        kt = j // slab_per_k
        hs_ref[:, MXU_DIM * j:MXU_DIM * (j + 1)] = _dot(ub[:, MXU_DIM * kt:MXU_DIM * (kt + 1)], wb_ref[j])

    def scan(j):
        lo, mid, hi = MXU_DIM * j, MXU_DIM * j + LANES, MXU_DIM * (j + 1)
        ar = ar_ref[:, LANES * j:LANES * (j + 1)]
        ai = ai_ref[:, LANES * j:LANES * (j + 1)]
        re, im = hst_ref[:, lo:mid], hst_ref[:, mid:hi]
        for t in range(t_chunk):
            rows = slice(t * bsz, (t + 1) * bsz)
            re, im = (ar * re - ai * im + hs_ref[rows, lo:mid], ar * im + ai * re + hs_ref[rows, mid:hi])
            hs_ref[rows, lo:mid] = re
            hs_ref[rows, mid:hi] = im
        hst_ref[:, lo:mid] = re
        hst_ref[:, mid:hi] = im

    group = 4
    for j in range(group):
        project(j)
    for g0 in range(0, n_slab, group):
        for j in range(g0 + group, min(g0 + 2 * group, n_slab)):
            project(j)
        for j in range(g0, g0 + group):
            scan(j)
    hl_ref[...] = hst_ref[...]

    halves = []
    for hf in range(n_slab // slab_per_k):
        acc = None
        for jj in range(slab_per_k):
            j = hf * slab_per_k + jj
            d = _dot(hs_ref[:, MXU_DIM * j:MXU_DIM * (j + 1)].astype(BF16), wc_ref[j])
            acc = d if acc is None else acc + d
        halves.append(acc)
    y = jnp.concatenate(halves, axis=1) + dsk_ref[...] * u
    y = _gelu_tanh(y)
    y = y * _sigmoid(_dot(y.astype(BF16), wglu_ref[...]) + bglu_ref[...])
    y_ref[...] = _rms(y, gout_ref[...]).astype(BF16)


def _s5(u_tb, h0, ar, ai, wb, wc, dsk, wglu, bglu, gout, t_chunk, bsz):
    rows, n_ssm = u_tb.shape
    n_state = h0.shape[1]
    n_slab = n_state // MXU_DIM
    r = t_chunk * bsz
    return pl.pallas_call(
        functools.partial(_s5_kernel, t_chunk=t_chunk, bsz=bsz, n_slab=n_slab),
        grid=(rows // r,),
        in_specs=[pl.BlockSpec((r, n_ssm), lambda c: (c, 0)), _const_spec((bsz, n_state)),
                  _const_spec((bsz, n_state // 2)), _const_spec((bsz, n_state // 2)),
                  _const_spec((n_slab, MXU_DIM, MXU_DIM)), _const_spec((n_slab, MXU_DIM, MXU_DIM)),
                  _const_spec((1, n_ssm)), _const_spec((n_ssm, n_ssm)), _const_spec((1, n_ssm)),
                  _const_spec((1, n_ssm))],
        out_specs=[pl.BlockSpec((r, n_ssm), lambda c: (c, 0)), _const_spec((bsz, n_state))],
        out_shape=[jax.ShapeDtypeStruct((rows, n_ssm), BF16), jax.ShapeDtypeStruct((bsz, n_state), F32)],
        scratch_shapes=[pltpu.VMEM((r, n_state), F32), pltpu.VMEM((bsz, n_state), F32)],
        compiler_params=_params("arbitrary"),
        name="s5_mixer",
    )(u_tb, h0, ar, ai, wb, wc, dsk, wglu, bglu, gout)


def _s5_tables(lam_re, lam_im, log_dt, b_re, b_im, c_re, c_im, bsz):
    G, P = lam_re.shape
    H = b_re.shape[-1]
    lr, li = lam_re.astype(F32), lam_im.astype(F32)
    dt = jnp.exp(log_dt.astype(F32))[:, None]
    mag = jnp.exp(lr * dt)
    a_re, a_im = mag * jnp.cos(li * dt), mag * jnp.sin(li * dt)
    den = lr * lr + li * li
    c_r = ((a_re - 1.0) * lr + a_im * li) / den
    c_i = (a_im * lr - (a_re - 1.0) * li) / den
    br, bi = b_re.astype(F32), b_im.astype(F32)
    bbar_re = c_r[..., None] * br - c_i[..., None] * bi
    bbar_im = c_r[..., None] * bi + c_i[..., None] * br
    n_pair = G // 2
    per_k = MXU_DIM // (2 * H)
    eye2 = jnp.eye(2, dtype=F32)
    place = jax.nn.one_hot(jnp.arange(n_pair) % per_k, per_k, dtype=F32)

    ar = jnp.broadcast_to(a_re.reshape(1, -1), (bsz, G * P))
    ai = jnp.broadcast_to(a_im.reshape(1, -1), (bsz, G * P))

    bb = jnp.stack([bbar_re, bbar_im]).reshape(2, n_pair, 2, P, H)
    wpair = jnp.einsum("ajgph,gk->jghakp", bb, eye2).reshape(n_pair, 2 * H, MXU_DIM)
    wb = jnp.einsum("jrc,jk->jkrc", wpair, place).reshape(n_pair, MXU_DIM, MXU_DIM)

    cc = jnp.stack([c_re.astype(F32), -c_im.astype(F32)]).reshape(2, n_pair, 2, H, P)
    cpair = jnp.einsum("ajghp,gk->jagpkh", cc, eye2).reshape(n_pair, MXU_DIM, 2 * H)
    wc = jnp.einsum("jnc,jk->jnkc", cpair, place).reshape(n_pair, MXU_DIM, MXU_DIM)
    return ar, ai, wb.astype(BF16), wc.astype(BF16)


def _state_to_lanes(re, im):
    bsz, G, P = re.shape
    s = jnp.stack([re, im], axis=1).reshape(bsz, 2, G // 2, 2, P)
    return s.transpose(0, 2, 1, 3, 4).reshape(bsz, 2 * G * P)


def _lanes_to_state(h, G, P):
    bsz = h.shape[0]
    s = h.reshape(bsz, G // 2, 2, 2, P).transpose(0, 2, 1, 3, 4).reshape(bsz, 2, G, P)
    return s[:, 0], s[:, 1]


FOX_AUG_ROWS = 16


def _fox_kernel(qt_ref, qa_ref, k_ref, kx_ref, vt_ref, o_ref, m_ref, l_ref, acc_ref, *, tq, tk, past):
    qi = pl.program_id(2)
    qt = qt_ref[0]
    row = lax.broadcasted_iota(jnp.int32, (LANES, tq), 0)
    zero = jnp.zeros_like(qt)
    pad = jnp.zeros((LANES - FOX_AUG_ROWS, tq), BF16)
    qts = []
    for hh in range(2):
        own = (row < FOX_HEAD_DIM) if hh == 0 else (row >= FOX_HEAD_DIM)
        aug = qa_ref[0, 0, FOX_AUG_ROWS * hh:FOX_AUG_ROWS * (hh + 1), :]
        qts.append(jnp.concatenate([jnp.where(own, qt, zero), aug, pad], axis=0))
    m_ref[...] = jnp.full(m_ref.shape, -1e30, F32)
    l_ref[...] = jnp.zeros(l_ref.shape, F32)
    acc_ref[...] = jnp.zeros(acc_ref.shape, F32)
    q_start = past + qi * tq
    n_full = (q_start + 1) // tk
    n_all = (q_start + tq + tk - 1) // tk

    def block(ks, masked, width):
        kb = jnp.concatenate([k_ref[0, pl.ds(ks, width), :], kx_ref[0, 0, pl.ds(ks, width), :]], axis=1)
        if masked:
            kpos = ks + lax.broadcasted_iota(jnp.int32, (width, tq), 0)
            qpos = q_start + lax.broadcasted_iota(jnp.int32, (width, tq), 1)
            visible = kpos <= qpos
        for hh in range(2):
            st = _dot(kb, qts[hh])
            if masked:
                st = jnp.where(visible, st, -jnp.inf)
            m_old = m_ref[hh]
            m_new = jnp.maximum(m_old, jnp.max(st, axis=0, keepdims=True))
            p = jnp.exp2(st - m_new)
            alpha = jnp.exp2(m_old - m_new)
            l_ref[hh] = alpha * l_ref[hh] + jnp.sum(p, axis=0, keepdims=True)
            vt = vt_ref[0, FOX_HEAD_DIM * hh:FOX_HEAD_DIM * (hh + 1), pl.ds(ks, width)]
            acc_ref[hh] = alpha * acc_ref[hh] + _dot(vt, p.astype(BF16))
            m_ref[hh] = m_new

    def steps(lo, hi, masked, width):
        def body(j, c):
            block(pl.multiple_of(j * width, width), masked, width)
            return c
        lax.fori_loop(lo, hi, body, 0)

    n_wide = 0
    if 2 * tk <= k_ref.shape[1]:
        n_wide = q_start // (2 * tk)
        steps(0, n_wide, False, 2 * tk)
    steps(2 * n_wide, n_full, False, tk)
    steps(n_full, n_all, True, tk)
    out_t = jnp.concatenate([acc_ref[0] / l_ref[0], acc_ref[1] / l_ref[1]], axis=0)
    o_ref[0] = out_t.T


def _fox(q_t, q_aug, k_all, k_bias, v_t, tq, tk, past):
    bsz, n_fox, L = q_t.shape
    n_pair = n_fox // LANES
    lk = k_all.shape[1]
    return pl.pallas_call(
        functools.partial(_fox_kernel, tq=tq, tk=tk, past=past),
        grid=(bsz, n_pair, L // tq),
        in_specs=[pl.BlockSpec((1, LANES, tq), lambda b, h, i: (b, h, i)),
                  pl.BlockSpec((1, 1, 2 * FOX_AUG_ROWS, tq), lambda b, h, i: (b, h, 0, i)),
                  pl.BlockSpec((1, lk, LANES), lambda b, h, i: (b, 0, h)),
                  pl.BlockSpec((1, 1, lk, LANES), lambda b, h, i: (b, h, 0, 0)),
                  pl.BlockSpec((1, LANES, lk), lambda b, h, i: (b, h, 0))],
        out_specs=pl.BlockSpec((1, tq, LANES), lambda b, h, i: (b, i, h)),
        out_shape=jax.ShapeDtypeStruct((bsz, L, n_fox), F32),
        scratch_shapes=[pltpu.VMEM((2, 1, tq), F32), pltpu.VMEM((2, 1, tq), F32),
                        pltpu.VMEM((2, FOX_HEAD_DIM, tq), F32)],
        compiler_params=_params("parallel", "parallel", "arbitrary"),
        name="fox_attention",
    )(q_t, q_aug, k_all, k_bias, v_t)


def _fox_operands(qb, k_all, v_all, d_pieces, past, L):
    bsz, lk_pad, n_fox = k_all.shape
    n_heads = n_fox // FOX_HEAD_DIM
    n_pair = n_heads // 2
    pieces = d_pieces.astype(BF16)
    dk = (-pieces).reshape(bsz, 3, n_pair, 2, lk_pad)
    dk = dk.transpose(0, 2, 4, 3, 1).reshape(bsz, n_pair, lk_pad, 6)
    ones_k = jnp.ones((bsz, n_pair, lk_pad, 3), BF16)
    zeros_k = jnp.zeros((bsz, n_pair, lk_pad, LANES - 9), BF16)
    k_bias = jnp.concatenate([dk, ones_k, zeros_k], axis=-1)

    dq = pieces[:, :, :, past:past + L].transpose(0, 2, 1, 3)
    sel = jax.nn.one_hot(jnp.arange(n_heads) % 2, 2, dtype=BF16)
    ones_q = jnp.broadcast_to(jnp.repeat(sel, 3, axis=1)[None, :, :, None], (bsz, n_heads, 6, L))
    zeros_q = jnp.zeros((bsz, n_heads, FOX_AUG_ROWS - 9, L), BF16)
    q_aug = jnp.concatenate([ones_q, dq, zeros_q], axis=2).reshape(bsz, n_pair, 2 * FOX_AUG_ROWS, L)
    return qb.transpose(0, 2, 1), q_aug, k_all, k_bias, v_all.transpose(0, 2, 1)


def _route_gates(logits_t, ebias_t, n_experts):
    per_group = n_experts // N_EXPERT_GROUPS
    tokens = logits_t.shape[1]
    score = _sigmoid(logits_t)
    sel = score + ebias_t
    row = lax.broadcasted_iota(jnp.int32, (per_group, tokens), 0).astype(F32)
    neg = jnp.float32(-jnp.inf)

    def first_argmax(tile, best):
        return jnp.min(jnp.where(tile == best, row, float(per_group)), axis=0, keepdims=True)

    score_g, sel_g, gsc = [], [], []
    for g in range(N_EXPERT_GROUPS):
        sg = sel[per_group * g:per_group * (g + 1), :]
        score_g.append(score[per_group * g:per_group * (g + 1), :])
        sel_g.append(sg)
        m1 = jnp.max(sg, axis=0, keepdims=True)
        rest = jnp.where(row == first_argmax(sg, m1), neg, sg)
        gsc.append(m1 + jnp.max(rest, axis=0, keepdims=True))
    cur = []
    for g in range(N_EXPERT_GROUPS):
        ahead = jnp.zeros((1, tokens), F32)
        for o in range(N_EXPERT_GROUPS):
            if o == g:
                continue
            beats = (gsc[o] >= gsc[g]) if o < g else (gsc[o] > gsc[g])
            ahead = ahead + jnp.where(beats, 1.0, 0.0)
        cur.append(jnp.where(ahead < TOPK_GROUPS, sel_g[g], neg))
    chosen = [jnp.zeros((per_group, tokens), jnp.bool_) for _ in range(N_EXPERT_GROUPS)]
    for _ in range(TOP_K):
        best = cur[0]
        for g in range(1, N_EXPERT_GROUPS):
            best = jnp.maximum(best, cur[g])
        best = jnp.max(best, axis=0, keepdims=True)
        idx = None
        for g in range(N_EXPERT_GROUPS):
            cand = jnp.min(jnp.where(cur[g] == best, row + float(per_group * g), float(n_experts)), axis=0,
                           keepdims=True)
            idx = cand if idx is None else jnp.minimum(idx, cand)
        for g in range(N_EXPERT_GROUPS):
            hit = (row + float(per_group * g)) == idx
            chosen[g] = chosen[g] | hit
            cur[g] = jnp.where(hit, neg, cur[g])
    w = [jnp.where(chosen[g], score_g[g], 0.0) for g in range(N_EXPERT_GROUPS)]
    total = w[0]
    for g in range(1, N_EXPERT_GROUPS):
        total = total + w[g]
    total = jnp.sum(total, axis=0, keepdims=True)
    return [w[g] / total * ROUTED_SCALE for g in range(N_EXPERT_GROUPS)]


def _mid_kernel(x_ref, ys_ref, yf_ref, gfox_ref, wout_ref, gmq_ref, wmq_ref, mk_ref, mv_ref, wmo_ref, gffn_ref,
                wrt_ref, ebt_ref, ws1_ref, ws3_ref, ws2_ref, xs_ref, xn_ref, gt_ref, cnt_ref, *, n_experts):
    x = x_ref[0]
    yfn = _rms(yf_ref[0], gfox_ref[...]).astype(BF16)
    mix = jnp.concatenate([ys_ref[0], yfn], axis=1)
    x1 = x + _dot(mix, wout_ref[...])

    qm = _dot(_rms(x1, gmq_ref[...]).astype(BF16), wmq_ref[...])
    hd = qm.shape[1] // MEM_HEADS
    heads = []
    for h in range(MEM_HEADS):
        qh = (qm[:, hd * h:hd * (h + 1)] * (hd ** -0.5)).astype(BF16)
        s = lax.dot_general(qh, mk_ref[0, :, hd * h:hd * (h + 1)], _NT, preferred_element_type=F32)
        p = jnp.exp(s - jnp.max(s, axis=1, keepdims=True))
        o = _dot(p.astype(BF16), mv_ref[0, :, hd * h:hd * (h + 1)]) / jnp.sum(p, axis=1, keepdims=True)
        heads.append(o.astype(BF16))
    x2 = x1 + _dot(jnp.concatenate(heads, axis=1), wmo_ref[...])

    xn = _rms(x2, gffn_ref[...])
    xnb = xn.astype(BF16)
    hidden = _silu(_dot(xnb, ws1_ref[...])) * _dot(xnb, ws3_ref[...])
    xs_ref[0] = x2 + _dot(hidden.astype(BF16), ws2_ref[...])
    xn_ref[0] = xnb

    wr = wrt_ref[...]
    wr_hi = wr.astype(BF16)
    wr_lo = (wr - wr_hi.astype(F32)).astype(BF16)
    xn_lo = (xn - xnb.astype(F32)).astype(BF16)
    nt = lambda a, c: lax.dot_general(a, c, _NT, preferred_element_type=F32)
    logits_t = nt(wr_hi, xnb) + (nt(wr_hi, xn_lo) + nt(wr_lo, xnb))
    gates = _route_gates(logits_t, ebt_ref[...], n_experts)
    per_group = n_experts // N_EXPERT_GROUPS
    n_cnt = cnt_ref.shape[1]
    per_cnt = x.shape[0] // n_cnt
    for g in range(N_EXPERT_GROUPS):
        gt_ref[0, per_group * g:per_group * (g + 1), :] = gates[g]
        routed = jnp.where(gates[g] != 0.0, 1.0, 0.0)
        for c in range(n_cnt):
            cnt_ref[0, c, per_group * g:per_group * (g + 1), :] = jnp.sum(
                routed[:, per_cnt * c:per_cnt * (c + 1)], axis=1, keepdims=True)


def _mid(x, ys, yf, gfox, wout, gmq, wmq, mkb, mvb, wmo, gffn, wrt, ebt, ws1, ws3, ws2, tm):
    bsz, L, d = x.shape
    n_ssm, n_fox = ys.shape[2], yf.shape[2]
    n_mem = mkb.shape[1]
    n_experts = wrt.shape[0]
    fs = ws1.shape[1]
    row = lambda n: pl.BlockSpec((1, tm, n), lambda b, i: (b, i, 0))
    memspec = pl.BlockSpec((1, n_mem, d), lambda b, i: (b, 0, 0))
    weight = lambda shape: pl.BlockSpec(shape, lambda b, i: (0,) * len(shape), pipeline_mode=pl.Buffered(1))
    cb = min(tm, MOE_TOKENS)
    return pl.pallas_call(
        functools.partial(_mid_kernel, n_experts=n_experts),
        grid=(bsz, L // tm),
        in_specs=[row(d), row(n_ssm), row(n_fox), _const_spec((1, n_fox)),
                  weight((d, d)), _const_spec((1, d)),
                  weight((d, d)), memspec, memspec, weight((d, d)), _const_spec((1, d)),
                  _const_spec((n_experts, d)), _const_spec((n_experts, 1)), weight((d, fs)), weight((d, fs)),
                  weight((fs, d))],
        out_specs=[row(d), row(d), pl.BlockSpec((1, n_experts, tm), lambda b, i: (b, 0, i)),
                   pl.BlockSpec((1, tm // cb, n_experts, 1), lambda b, i: (b, i, 0, 0))],
        out_shape=[jax.ShapeDtypeStruct((bsz, L, d), F32), jax.ShapeDtypeStruct((bsz, L, d), BF16),
                   jax.ShapeDtypeStruct((bsz, n_experts, L), F32),
                   jax.ShapeDtypeStruct((bsz, L // cb, n_experts, 1), F32)],
        compiler_params=_params("parallel", "parallel"),
        name="mid_block",
    )(x, ys, yf, gfox, wout, gmq, wmq, mkb, mvb, wmo, gffn, wrt, ebt, ws1, ws3, ws2)


SEG_ROWS = 16
MOE_TOKENS = 512
COPY_ROWS = 4 * SEG_ROWS
GATHER_ROWS = 1024
_TN = (((0,), (0,)), ((), ()))
_NO_RANK = -(1 << 20)


def _sorted_rows(tb, n_experts):
    rows = TOP_K * tb + n_experts * (SEG_ROWS - 1)
    return -(-rows // GATHER_ROWS) * GATHER_ROWS


def _expert_tile(m):
    return 1024 if m >= 4096 else 128


def _dispatch_plan(cnt, ns, MOE_TILE):
    nblk, n_experts = cnt.shape

    def before(a, axis):
        n = a.shape[axis]
        earlier = jnp.arange(n)[:, None] < jnp.arange(n)[None, :]
        if axis == 0:
            return jnp.sum(jnp.where(earlier[:, :, None], a[:, None, :], 0), axis=0)
        return jnp.sum(jnp.where(earlier[None, :, :], a[:, :, None], 0), axis=1)

    cp = (cnt + SEG_ROWS - 1) // SEG_ROWS * SEG_ROWS
    o_loc = before(cp, 1)
    used = jnp.sum(cp, axis=1)
    tot_e = jnp.sum(cp, axis=0)
    reg_e = (tot_e + MOE_TILE - 1) // MOE_TILE * MOE_TILE
    base_e = before(reg_e[None, :], 1)[0]
    reg_end = base_e + reg_e
    dst = base_e[None, :] + before(cp, 0)
    rows_max = nblk * ns + n_experts * MOE_TILE
    n_tiles_max = -(-rows_max // MOE_TILE)
    n_tiles = reg_end[-1] // MOE_TILE
    tile_idx = jnp.clip(jnp.arange(n_tiles_max, dtype=jnp.int32), 0, jnp.maximum(n_tiles - 1, 0))
    tile_expert = jnp.sum((reg_end[None, :] <= (tile_idx * MOE_TILE)[:, None]).astype(jnp.int32), axis=1)
    tile_expert = jnp.minimum(tile_expert, n_experts - 1)
    n_pc = ns // SEG_ROWS
    piece_row = jnp.arange(n_pc, dtype=jnp.int32) * SEG_ROWS
    seg_end = o_loc + cp
    piece_e = jnp.sum((seg_end[:, None, :] <= piece_row[None, :, None]).astype(jnp.int32), axis=2)
    live = piece_row[None, :] < used[:, None]
    piece_e = jnp.minimum(piece_e, n_experts - 1)
    own = piece_e[:, :, None] == jnp.arange(n_experts, dtype=jnp.int32)[None, None, :]
    pick = lambda table: jnp.sum(jnp.where(own, table[:, None, :], 0), axis=2)
    rank0 = piece_row[None, :] - pick(o_loc)
    piece_rank = jnp.where(live, rank0, _NO_RANK)
    i32 = lambda a: a.astype(jnp.int32).reshape(-1)
    per_block = lambda a: a.astype(jnp.int32).reshape(nblk, 1, -1)

    def copy_list(n_e, first_rank, rows_each, n_max):
        end = before(n_e, 1) + n_e
        j = jnp.arange(n_max, dtype=jnp.int32)
        e_of = jnp.minimum(jnp.sum((end[:, None, :] <= j[None, :, None]).astype(jnp.int32), axis=2), n_experts - 1)
        mine = e_of[:, :, None] == jnp.arange(n_experts, dtype=jnp.int32)[None, None, :]
        take = lambda table: jnp.sum(jnp.where(mine, table[:, None, :], 0), axis=2)
        rank = take(first_rank) + (j[None, :] - take(end - n_e)) * rows_each
        ok = j[None, :] < jnp.sum(n_e, axis=1)[:, None]
        return (per_block(jnp.where(ok, take(o_loc) + rank, 0)), per_block(jnp.where(ok, take(dst) + rank, 0)),
                i32(jnp.sum(n_e, axis=1)))

    per_copy = COPY_ROWS // SEG_ROWS
    n_wide_e = (cp // SEG_ROWS) // per_copy
    n_seg_e = cp // SEG_ROWS - n_wide_e * per_copy
    wide_src, wide_dst, n_wide = copy_list(n_wide_e, jnp.zeros_like(cp), COPY_ROWS, ns // COPY_ROWS)
    seg_src, seg_dst, n_seg = copy_list(n_seg_e, n_wide_e * COPY_ROWS, SEG_ROWS, n_experts * (per_copy - 1))
    return dict(n_piece=i32(used // SEG_ROWS), piece_e=per_block(piece_e), piece_rank=per_block(piece_rank),
                wide_src=wide_src, wide_dst=wide_dst, n_wide=n_wide, seg_src=seg_src, seg_dst=seg_dst, n_seg=n_seg,
                tail_start=i32(base_e + tot_e),
                tail_chunks=i32((reg_e - tot_e) // SEG_ROWS), tile_idx=i32(tile_idx), tile_expert=i32(tile_expert),
                n_tiles=i32(n_tiles), n_tiles_max=n_tiles_max)


def _slot_ranks(gates, tb):
    sel = gates != 0.0
    r = lax.broadcasted_iota(jnp.int32, (tb, tb), 0)
    c = lax.broadcasted_iota(jnp.int32, (tb, tb), 1)
    earlier = jnp.where(r < c, 1.0, 0.0).astype(BF16)
    rank = _dot(jnp.where(sel, 1.0, 0.0).astype(BF16), earlier)
    return jnp.where(sel, rank, -1.0)


def _build_one_hot(out_ref, rm_ref, pe_ref, pr_ref, first, count, tb, value_row):
    rows = lax.broadcasted_iota(jnp.int32, (SEG_ROWS, tb), 0).astype(F32)
    for i in range(first, first + count):
        e = pe_ref[0, 0, i]
        hit = rows == (rm_ref[pl.ds(e, 1), :] - pr_ref[0, 0, i].astype(F32))
        out_ref[SEG_ROWS * i:SEG_ROWS * (i + 1), :] = jnp.where(hit, value_row(e), 0.0).astype(BF16)


def _wait_rows(n_rows, make_copy):
    wide = 16 * SEG_ROWS
    n_wide = n_rows // wide

    def wide_step(i, c):
        make_copy(wide).wait()
        return c

    def seg_step(i, c):
        make_copy(SEG_ROWS).wait()
        return c

    lax.fori_loop(0, n_wide, wide_step, 0)
    lax.fori_loop(0, (n_rows - n_wide * wide) // SEG_ROWS, seg_step, 0)


def _issue_copies(n_wide, n_seg, tables, make_copy):
    wide_src, wide_dst, seg_src, seg_dst = tables

    def wide(i, c):
        make_copy(pl.multiple_of(wide_src[0, 0, i], SEG_ROWS), pl.multiple_of(wide_dst[0, 0, i], SEG_ROWS),
                  COPY_ROWS).start()
        return c

    def seg(i, c):
        make_copy(pl.multiple_of(seg_src[0, 0, i], SEG_ROWS), pl.multiple_of(seg_dst[0, 0, i], SEG_ROWS),
                  SEG_ROWS).start()
        return c

    lax.fori_loop(0, n_wide, wide, 0)
    lax.fori_loop(0, n_seg, seg, 0)


def _dispatch_kernel(n_piece_ref, n_wide_ref, n_seg_ref, tail_start_ref, tail_chunks_ref, pe_ref, pr_ref,
                     ws_ref, wd_ref, ss_ref, sd_ref, xn_ref, gt_ref, *rest, tb, ns, n_experts, nblk, fill_tail):
    xg_ref, rm_ref, g_ref, xsb_ref, z_ref, sem = rest[-6:]
    b = pl.program_id(0)
    n_pc = ns // SEG_ROWS
    used = n_piece_ref[b] * SEG_ROWS
    rm_ref[...] = _slot_ranks(gt_ref[...], tb)
    per_slice = GATHER_ROWS // SEG_ROWS
    n_slices = ns // GATHER_ROWS
    build = lambda t: _build_one_hot(g_ref, rm_ref, pe_ref, pr_ref, per_slice * t, per_slice, tb, lambda e: 1.0)

    slot = b % 2
    build(0)
    for t in range(n_slices):
        if t + 1 < n_slices:
            build(t + 1)
        sl = slice(GATHER_ROWS * t, GATHER_ROWS * (t + 1))
        xsb_ref[slot, sl, :] = _dot(g_ref[sl, :], xn_ref[...]).astype(BF16)

    def rows_copy(buf, src_row, dst_row, size):
        return pltpu.make_async_copy(xsb_ref.at[buf, pl.ds(src_row, size)], xg_ref.at[pl.ds(dst_row, size)],
                                     sem.at[buf])

    _issue_copies(n_wide_ref[b], n_seg_ref[b], (ws_ref, wd_ref, ss_ref, sd_ref),
                  lambda block_row, global_row, size: rows_copy(slot, block_row, global_row, size))

    @pl.when(b > 0)
    def _():
        _wait_rows(n_piece_ref[b - 1] * SEG_ROWS, lambda size: rows_copy(1 - slot, 0, 0, size))

    @pl.when(b == nblk - 1)
    def _():
        _wait_rows(used, lambda size: rows_copy(slot, 0, 0, size))

    @pl.when(jnp.logical_and(b == nblk - 1, fill_tail))
    def _():
        z_ref[...] = jnp.zeros(z_ref.shape, BF16)

        wide = z_ref.shape[0]
        per_wide = wide // SEG_ROWS

        def tail_copy(row, size):
            return pltpu.make_async_copy(z_ref.at[pl.ds(0, size)], xg_ref.at[pl.ds(pl.multiple_of(row, SEG_ROWS), size)],
                                         sem.at[slot])

        def each_tail(action):
            def per_expert(e, carry):
                n_wide = tail_chunks_ref[e] // per_wide

                def wide_piece(i, c2):
                    action(tail_copy(tail_start_ref[e] + i * wide, wide))
                    return c2

                def seg_piece(i, c2):
                    action(tail_copy(tail_start_ref[e] + n_wide * wide + i * SEG_ROWS, SEG_ROWS))
                    return c2

                lax.fori_loop(0, n_wide, wide_piece, 0)
                lax.fori_loop(0, tail_chunks_ref[e] - n_wide * per_wide, seg_piece, 0)
                return carry
            lax.fori_loop(0, n_experts, per_expert, 0)

        each_tail(lambda cp: cp.start())
        each_tail(lambda cp: cp.wait())


def _expert_kernel(tile_idx_ref, tile_expert_ref, n_tiles_ref, x_ref, w1_ref, w3_ref, w2_ref, o_ref):
    @pl.when(pl.program_id(0) < n_tiles_ref[0])
    def _():
        x = x_ref[...]
        h = _silu(_dot(x, w1_ref[0].astype(BF16))) * _dot(x, w3_ref[0].astype(BF16))
        o_ref[...] = _dot(h.astype(BF16), w2_ref[0].astype(BF16)).astype(BF16)


def _combine_kernel(n_piece_ref, n_wide_ref, n_seg_ref, pe_ref, pr_ref, ws_ref, wd_ref, ss_ref, sd_ref,
                    ws_next_ref, wd_next_ref, ss_next_ref, sd_next_ref,
                    og_ref, gt_ref, rm_ref, xs_ref, gfin_ref, y_ref, gw_ref, ob_ref, sem, *, tb, ns, n_experts, nblk):
    b = pl.program_id(0)
    slot = b % 2
    n_pc = ns // SEG_ROWS

    def rows_copy(buf, src_row, dst_row, size):
        return pltpu.make_async_copy(og_ref.at[pl.ds(src_row, size)], ob_ref.at[buf, pl.ds(dst_row, size)],
                                     sem.at[buf])

    def fetch_block(blk, tables, buf):
        _issue_copies(n_wide_ref[blk], n_seg_ref[blk], tables,
                      lambda block_row, global_row, size: rows_copy(buf, global_row, block_row, size))

    @pl.when(b == 0)
    def _():
        fetch_block(b, (ws_ref, wd_ref, ss_ref, sd_ref), slot)

    @pl.when(b + 1 < nblk)
    def _():
        fetch_block(jnp.minimum(b + 1, nblk - 1), (ws_next_ref, wd_next_ref, ss_next_ref, sd_next_ref), 1 - slot)

    _wait_rows(n_piece_ref[b] * SEG_ROWS, lambda size: rows_copy(slot, 0, 0, size))

    def clear(i, c):
        ob_ref[slot, pl.ds(pl.multiple_of((n_piece_ref[b] + i) * SEG_ROWS, SEG_ROWS), SEG_ROWS), :] = jnp.zeros(
            (SEG_ROWS, ob_ref.shape[2]), BF16)
        return c

    lax.fori_loop(0, n_pc - n_piece_ref[b], clear, 0)

    per_slice = GATHER_ROWS // SEG_ROWS
    n_slices = ns // GATHER_ROWS
    build = lambda t: _build_one_hot(gw_ref, rm_ref, pe_ref, pr_ref, per_slice * t, per_slice, tb,
                                     lambda e: gt_ref[pl.ds(e, 1), :])
    build(0)
    y = xs_ref[...]
    for t in range(n_slices):
        if t + 1 < n_slices:
            build(t + 1)
        sl = slice(GATHER_ROWS * t, GATHER_ROWS * (t + 1))
        y = y + lax.dot_general(gw_ref[sl, :], ob_ref[slot, sl, :], _TN, preferred_element_type=F32)
    y_ref[...] = _rms(y, gfin_ref[...])


def _moe(groups, w1, w3, w2, gfin):
    d = groups[0][0].shape[1]
    n_experts, _, f = w1.shape
    tb = MOE_TOKENS
    blocks = [g[0].shape[0] // tb for g in groups]
    assert all(g[0].shape[0] == n * tb for g, n in zip(groups, blocks)), "token groups must be whole dispatch blocks"
    ns = _sorted_rows(tb, n_experts)
    MOE_TILE = _expert_tile(sum(blocks) * tb)
    plan = _dispatch_plan(jnp.concatenate([g[2] for g in groups], axis=0), ns, MOE_TILE)
    n_tiles_max = plan["n_tiles_max"]
    rows_max = n_tiles_max * MOE_TILE
    this_block = lambda t: pl.BlockSpec((1, 1, t.shape[2]), lambda b, *_: (b, 0, 0), memory_space=pltpu.SMEM)
    n_prefetch_dispatch = 5
    n_tables = 6

    def group_tables(first, nblk):
        rows = slice(first, first + nblk)
        build = (plan["piece_e"][rows], plan["piece_rank"][rows])
        copies = tuple(plan[k][rows] for k in ("wide_src", "wide_dst", "seg_src", "seg_dst"))
        counts = tuple(plan[k][rows] for k in ("n_piece", "n_wide", "n_seg"))
        return build, copies, counts

    xg = None
    slot_ranks = []
    first = 0
    for gi, ((xn, gates_t, _, _), nblk) in enumerate(zip(groups, blocks)):
        build_tables, copy_tables, counts = group_tables(first, nblk)
        first += nblk
        earlier = () if xg is None else (xg,)
        xg = pl.pallas_call(
            functools.partial(_dispatch_kernel, tb=tb, ns=ns, n_experts=n_experts, nblk=nblk,
                              fill_tail=gi == len(groups) - 1),
            grid_spec=pltpu.PrefetchScalarGridSpec(
                num_scalar_prefetch=n_prefetch_dispatch, grid=(nblk,),
                in_specs=[this_block(t) for t in build_tables + copy_tables] + [
                          pl.BlockSpec((tb, d), lambda b, *_: (b, 0)),
                          pl.BlockSpec((n_experts, tb), lambda b, *_: (0, b))] + [
                          pl.BlockSpec(memory_space=pl.ANY) for _ in earlier],
                out_specs=[pl.BlockSpec(memory_space=pl.ANY), pl.BlockSpec((n_experts, tb), lambda b, *_: (0, b))],
                scratch_shapes=[pltpu.VMEM((ns, tb), BF16), pltpu.VMEM((2, ns, d), BF16),
                                pltpu.VMEM((16 * SEG_ROWS, d), BF16), pltpu.SemaphoreType.DMA((2,))]),
            out_shape=[jax.ShapeDtypeStruct((rows_max, d), BF16), jax.ShapeDtypeStruct((n_experts, nblk * tb), F32)],
            input_output_aliases={n_prefetch_dispatch + n_tables + 2: 0} if earlier else {},
            compiler_params=_params("arbitrary"),
            name="moe_dispatch",
        )(*counts, plan["tail_start"], plan["tail_chunks"], *build_tables, *copy_tables, xn, gates_t, *earlier)
        xg, ranks = xg
        slot_ranks.append(ranks)

    tile = lambda i, idx, ex, n: (idx[i], 0)
    og = pl.pallas_call(
        _expert_kernel,
        grid_spec=pltpu.PrefetchScalarGridSpec(
            num_scalar_prefetch=3, grid=(n_tiles_max,),
            in_specs=[pl.BlockSpec((MOE_TILE, d), tile),
                      pl.BlockSpec((1, d, f), lambda i, idx, ex, n: (ex[i], 0, 0)),
                      pl.BlockSpec((1, d, f), lambda i, idx, ex, n: (ex[i], 0, 0)),
                      pl.BlockSpec((1, f, d), lambda i, idx, ex, n: (ex[i], 0, 0))],
            out_specs=pl.BlockSpec((MOE_TILE, d), tile)),
        out_shape=jax.ShapeDtypeStruct((rows_max, d), BF16),
        compiler_params=_params("arbitrary"),
        name="moe_experts",
    )(plan["tile_idx"], plan["tile_expert"], plan["n_tiles"], xg, w1, w3, w2)

    results = []
    first = 0
    for (xn, gates_t, _, xs), nblk, ranks in zip(groups, blocks, slot_ranks):
        build_tables, copy_tables, counts = group_tables(first, nblk)
        first += nblk
        next_block = lambda t, nblk=nblk: pl.BlockSpec(
            (1, 1, t.shape[2]), lambda b, *_: (jnp.minimum(b + 1, nblk - 1), 0, 0), memory_space=pltpu.SMEM)
        results.append(pl.pallas_call(
            functools.partial(_combine_kernel, tb=tb, ns=ns, n_experts=n_experts, nblk=nblk),
            grid_spec=pltpu.PrefetchScalarGridSpec(
                num_scalar_prefetch=3, grid=(nblk,),
                in_specs=[this_block(t) for t in build_tables + copy_tables] + [next_block(t) for t in copy_tables] + [
                          pl.BlockSpec(memory_space=pl.ANY), pl.BlockSpec((n_experts, tb), lambda b, *_: (0, b)),
                          pl.BlockSpec((n_experts, tb), lambda b, *_: (0, b)),
                          pl.BlockSpec((tb, d), lambda b, *_: (b, 0)), pl.BlockSpec((1, d), lambda b, *_: (0, 0))],
                out_specs=pl.BlockSpec((tb, d), lambda b, *_: (b, 0)),
                scratch_shapes=[pltpu.VMEM((ns, tb), BF16), pltpu.VMEM((2, ns, d), BF16),
                                pltpu.SemaphoreType.DMA((2,))]),
            out_shape=jax.ShapeDtypeStruct((nblk * tb, d), F32),
            compiler_params=_params("arbitrary"),
            name="moe_combine",
        )(*counts, *build_tables, *copy_tables, *copy_tables, og, gates_t, ranks, xs, gfin))
    return results


def _tiles(L):
    t_row = min(L, 512)
    t_mid = min(L, 1024)
    t_key = 512
    t_scan = min(L, 64)
    return t_row, t_mid, t_key, t_scan


def _group(x, h0_re, h0_im, past_k, past_v, past_lf, mkb, mvb, p, w):
    bsz, L, d = x.shape
    n_heads = p["b_f"].shape[0]
    n_fox = n_heads * FOX_HEAD_DIM
    n_ssm = p["d_skip"].shape[0]
    G, P = p["lam_re"].shape
    t_row, t_mid, t_key, t_scan = _tiles(L)

    u, qb, k, v, kb, vb, lf = _proj_in(x, w["g_mix"], w["w_main"], w["w_f"], w["b_f"], t_row, n_ssm, n_fox, n_heads)

    u_tb = u.transpose(1, 0, 2).reshape(L * bsz, n_ssm)
    if h0_re is None:
        h0 = jnp.zeros((bsz, 2 * G * P), F32)
    else:
        h0 = _state_to_lanes(h0_re.astype(F32), h0_im.astype(F32))
    ys_tb, h_last = _s5(u_tb, h0, w["ar"], w["ai"], w["wb"], w["wc"], w["d_skip"], w["w_glu"], w["b_glu"],
                        w["g_ssm_out"], t_scan, bsz)
    ys = ys_tb.reshape(L, bsz, n_ssm).transpose(1, 0, 2)
    hl_re, hl_im = _lanes_to_state(h_last, G, P)

    if past_k is None:
        past = 0
        lf_all, k_all, v_all = lf, kb, vb
    else:
        past = past_k.shape[1]
        lf_all = jnp.concatenate([past_lf.astype(F32), lf], axis=1)
        k_all = jnp.concatenate([past_k.reshape(bsz, past, n_fox).astype(BF16), kb], axis=1)
        v_all = jnp.concatenate([past_v.reshape(bsz, past, n_fox).astype(BF16), vb], axis=1)
    lk = lf_all.shape[1]
    lk_pad = -(-lk // t_key) * t_key
    lf_t = jnp.pad(lf_all.transpose(0, 2, 1), ((0, 0), (0, 0), (0, lk_pad - lk)))
    d_pieces = _decay_cumsum(lf_t.reshape(1, bsz * n_heads, lk_pad), jnp.zeros((1, bsz * n_heads, 1), F32), t_key)
    d_pieces = d_pieces.reshape(3, bsz, n_heads, lk_pad).transpose(1, 0, 2, 3)
    k_all = jnp.pad(k_all, ((0, 0), (0, lk_pad - lk), (0, 0)))
    v_all = jnp.pad(v_all, ((0, 0), (0, lk_pad - lk), (0, 0)))
    t_att = t_key if L >= t_key else lk_pad
    yf = _fox(*_fox_operands(qb, k_all, v_all, d_pieces, past, L), t_row, t_att, past)

    xs, xnb, gates_t, cnt = _mid(x, ys, yf, w["g_fox_out"], w["w_out"], w["g_mem_q"], w["w_mq"], mkb, mvb, w["w_mo"],
                                 w["g_ffn"], w["w_router_t"], w["e_bias_t"], w["ws1"], w["ws3"], w["ws2"], t_mid)
    m = bsz * L
    n_experts = gates_t.shape[1]
    cnt = cnt.reshape(m // MOE_TOKENS, -1, n_experts).sum(axis=1).astype(jnp.int32)
    routed = (xnb.reshape(m, d), gates_t.transpose(1, 0, 2).reshape(n_experts, m), cnt, xs.reshape(m, d))
    return (routed, hl_re, hl_im, k.reshape(bsz, L, n_heads, FOX_HEAD_DIM),
            v.reshape(bsz, L, n_heads, FOX_HEAD_DIM), lf)


def kernel(x_prompt, x_sample, state_ssm_re, state_ssm_im, cache_fox_k, cache_fox_v, cache_fox_logf, cache_mem_k, cache_mem_v, mem_prompt, g_mix, w_in, b_f, lam_re, lam_im, log_dt, b_re, b_im, c_re, c_im, d_skip, w_glu, b_glu, g_ssm_out, g_fox_out, w_out, g_mem_q, g_mem_kv, w_mq, w_mk, w_mv, w_mo, g_ffn, w_router, e_bias, w1, w3, w2, ws1, ws3, ws2, g_final):
    depth = w_in.shape[0]
    bsz = x_prompt.shape[0]
    hp, hs = x_prompt, x_sample
    outs_p, outs_s = [], []
    for l in range(depth):
        p = dict(b_f=b_f[l], lam_re=lam_re[l], d_skip=d_skip[l])
        n_heads = b_f.shape[1]
        n_ssm = d_skip.shape[1]
        n_main = w_in.shape[2] - n_heads
        row = lambda a: a.reshape(1, -1).astype(F32)
        ar, ai, wb, wc = _s5_tables(lam_re[l], lam_im[l], log_dt[l], b_re[l], b_im[l], c_re[l], c_im[l], bsz)
        w = dict(
            g_mix=row(g_mix[l]), w_main=w_in[l][:, :n_main].astype(BF16),
            w_f=jnp.pad(w_in[l][:, n_main:], ((0, 0), (0, LANES - n_heads))).astype(BF16),
            b_f=jnp.pad(b_f[l].astype(F32), (0, LANES - n_heads)).reshape(1, LANES),
            ar=ar, ai=ai, wb=wb, wc=wc, d_skip=row(d_skip[l]), w_glu=w_glu[l].astype(BF16), b_glu=row(b_glu[l]),
            g_ssm_out=row(g_ssm_out[l]), g_fox_out=row(g_fox_out[l]), w_out=w_out[l].astype(BF16),
            g_mem_q=row(g_mem_q[l]), w_mq=w_mq[l].astype(BF16), w_mo=w_mo[l].astype(BF16), g_ffn=row(g_ffn[l]),
            w_router_t=w_router[l].astype(F32).T, e_bias_t=e_bias[l].astype(F32).reshape(-1, 1),
            ws1=ws1[l].astype(BF16), ws3=ws3[l].astype(BF16), ws2=ws2[l].astype(BF16),
            w1=w1[l], w3=w3[l], w2=w2[l], g_final=row(g_final))
        assert depth == 1, "final norm fusion assumes a single layer"
        mk_p, mv_p, mkb_p, mvb_p = _mem_kv(mem_prompt, row(g_mem_kv[l]), w_mk[l].astype(BF16), w_mv[l].astype(BF16))
        nm, mh = mem_prompt.shape[1], MEM_HEADS
        routed_p, re_p, im_p, k_p, v_p, lf_p = _group(hp, None, None, None, None, None, mkb_p, mvb_p, p, w)
        cm_k = cache_mem_k[l].reshape(bsz, nm, -1).astype(BF16)
        cm_v = cache_mem_v[l].reshape(bsz, nm, -1).astype(BF16)
        routed_s, re_s, im_s, k_s, v_s, lf_s = _group(hs, state_ssm_re[l], state_ssm_im[l], cache_fox_k[l],
                                                      cache_fox_v[l], cache_fox_logf[l], cm_k, cm_v, p, w)
        y_p, y_s = _moe([routed_p, routed_s], w["w1"], w["w3"], w["w2"], w["g_final"])
        hp, hs = y_p.reshape(hp.shape), y_s.reshape(hs.shape)
        outs_p.append((re_p, im_p, k_p, v_p, lf_p, mk_p.reshape(bsz, nm, mh, -1), mv_p.reshape(bsz, nm, mh, -1)))
        outs_s.append((re_s, im_s, k_s, v_s, lf_s))
    stack = lambda outs, i: jnp.stack([o[i] for o in outs])
    return (hp, hs) + tuple(stack(outs_p, i) for i in range(7)) + tuple(stack(outs_s, i) for i in range(5))
```

```python
import functools
import math

import jax
import jax.numpy as jnp
from jax import lax
from jax.experimental import pallas as pl
from jax.experimental.pallas import tpu as pltpu

F32 = jnp.float32
BF16 = jnp.bfloat16

SSM_GROUP = 16
SSM_STATE = 64
FOX_HEAD_DIM = 64
MEM_HEADS = 4
TOP_K = 8
N_EXPERT_GROUPS = 8
TOPK_GROUPS = 4
ROUTED_SCALE = 2.5
RMS_EPS = 1e-6

LANES = 128
SUBLANES = 8
MXU_DIM = 256
VMEM_LIMIT_BYTES = 56 * 1024 * 1024

_HIGHEST = lax.Precision.HIGHEST
_NT = (((1,), (1,)), ((), ()))


def _params(*sem):
    return pltpu.CompilerParams(dimension_semantics=sem, vmem_limit_bytes=VMEM_LIMIT_BYTES)


def _rms(x, g):
    return x * lax.rsqrt(jnp.mean(x * x, axis=-1, keepdims=True) + RMS_EPS) * g


def _sigmoid(x):
    return 1.0 / (1.0 + jnp.exp(-x))


def _silu(x):
    return x * _sigmoid(x)


def _gelu_tanh(x):
    return x * (0.5 * (1.0 + jnp.tanh(math.sqrt(2.0 / math.pi) * (x + 0.044715 * (x * x * x)))))


def _log_sigmoid(x):
    return jnp.minimum(x, 0.0) - jnp.log1p(jnp.exp(-jnp.abs(x)))


def _dot(a, b):
    return jnp.dot(a, b, preferred_element_type=F32)


def _const_spec(shape):
    nd = len(shape)
    return pl.BlockSpec(shape, lambda *_: (0,) * nd)


def _memkv_kernel(m_ref, g_ref, wk_ref, wv_ref, k_ref, v_ref, kb_ref, vb_ref):
    mn = _rms(m_ref[0], g_ref[...]).astype(BF16)
    k = _dot(mn, wk_ref[...])
    v = _dot(mn, wv_ref[...])
    k_ref[0] = k
    v_ref[0] = v
    kb_ref[0] = k.astype(BF16)
    vb_ref[0] = v.astype(BF16)


def _mem_kv(mem, g, wk, wv):
    bsz, n, d = mem.shape
    blk = pl.BlockSpec((1, n, d), lambda b: (b, 0, 0))
    return pl.pallas_call(
        _memkv_kernel,
        grid=(bsz,),
        in_specs=[blk, _const_spec((1, d)), _const_spec((d, d)), _const_spec((d, d))],
        out_specs=[blk, blk, blk, blk],
        out_shape=[jax.ShapeDtypeStruct((bsz, n, d), F32)] * 2 + [jax.ShapeDtypeStruct((bsz, n, d), BF16)] * 2,
        compiler_params=_params("parallel"),
        name="mem_kv",
    )(mem, g, wk, wv)


def _proj_in_kernel(x_ref, g_ref, w_ref, wf_ref, bf_ref, u_ref, q_ref, k_ref, v_ref, kb_ref, vb_ref, lf_ref,
                    *, n_ssm, n_fox, n_heads):
    xb = _rms(x_ref[0], g_ref[...]).astype(BF16)
    z = _dot(xb, w_ref[...])
    u_ref[0] = z[:, :n_ssm]
    o = n_ssm
    q_ref[0] = (z[:, o:o + n_fox] * (FOX_HEAD_DIM ** -0.5 * math.log2(math.e))).astype(BF16)
    k = z[:, o + n_fox:o + 2 * n_fox]
    v = z[:, o + 2 * n_fox:o + 3 * n_fox]
    k_ref[0] = k
    v_ref[0] = v
    kb_ref[0] = k.astype(BF16)
    vb_ref[0] = v.astype(BF16)
    zf = _dot(xb, wf_ref[...])
    lf_ref[0] = _log_sigmoid(zf + bf_ref[...])[:, :n_heads]


def _proj_in(x, g, w_main, w_f, b_f, tm, n_ssm, n_fox, n_heads):
    bsz, L, d = x.shape
    nmain = w_main.shape[1]
    row = lambda n: pl.BlockSpec((1, tm, n), lambda b, i: (b, i, 0))
    outs = [jax.ShapeDtypeStruct((bsz, L, n_ssm), F32), jax.ShapeDtypeStruct((bsz, L, n_fox), BF16),
            jax.ShapeDtypeStruct((bsz, L, n_fox), F32), jax.ShapeDtypeStruct((bsz, L, n_fox), F32),
            jax.ShapeDtypeStruct((bsz, L, n_fox), BF16), jax.ShapeDtypeStruct((bsz, L, n_fox), BF16),
            jax.ShapeDtypeStruct((bsz, L, n_heads), F32)]
    return pl.pallas_call(
        functools.partial(_proj_in_kernel, n_ssm=n_ssm, n_fox=n_fox, n_heads=n_heads),
        grid=(bsz, L // tm),
        in_specs=[row(d), _const_spec((1, d)), _const_spec((d, nmain)), _const_spec((d, LANES)),
                  _const_spec((1, LANES))],
        out_specs=[row(n_ssm), row(n_fox), row(n_fox), row(n_fox), row(n_fox), row(n_fox), row(n_heads)],
        out_shape=outs,
        compiler_params=_params("parallel", "parallel"),
        name="proj_in",
    )(x, g, w_main, w_f, b_f)


def _decay_kernel(lft_ref, c0_ref, dt_ref, car_ref, *, tl):
    @pl.when(pl.program_id(1) == 0)
    def _():
        car_ref[...] = c0_ref[0]
    r = lax.broadcasted_iota(jnp.int32, (tl, tl), 0)
    c = lax.broadcasted_iota(jnp.int32, (tl, tl), 1)
    tri = (r <= c).astype(F32)
    d = jnp.dot(lft_ref[0], tri, preferred_element_type=F32, precision=_HIGHEST) + car_ref[...]
    car_ref[...] = d[:, tl - 1:tl]
    rest = d * math.log2(math.e)
    for piece in range(3):
        part = rest.astype(BF16).astype(F32)
        dt_ref[0, piece] = part
        rest = rest - part


def _decay_cumsum(lft, c0, tl):
    bsz, nh, L = lft.shape
    return pl.pallas_call(
        functools.partial(_decay_kernel, tl=tl),
        grid=(bsz, L // tl),
        in_specs=[pl.BlockSpec((1, nh, tl), lambda b, i: (b, 0, i)), pl.BlockSpec((1, nh, 1), lambda b, i: (b, 0, 0))],
        out_specs=pl.BlockSpec((1, 3, nh, tl), lambda b, i: (b, 0, 0, i)),
        out_shape=jax.ShapeDtypeStruct((bsz, 3, nh, L), F32),
        scratch_shapes=[pltpu.VMEM((nh, 1), F32)],
        compiler_params=_params("parallel", "arbitrary"),
        name="decay_cumsum",
    )(lft, c0)


def _s5_kernel(u_ref, h0_ref, ar_ref, ai_ref, wb_ref, wc_ref, dsk_ref, wglu_ref, bglu_ref, gout_ref,
               y_ref, hl_ref, hs_ref, hst_ref, *, t_chunk, bsz, n_slab):
    @pl.when(pl.program_id(0) == 0)
    def _():
        hst_ref[...] = h0_ref[...]

    u = u_ref[...]
    ub = u.astype(BF16)
    slab_per_k = MXU_DIM // (2 * SSM_GROUP)

    def project(j):
        kt = j // slab_per_k
        hs_ref[:, MXU_DIM * j:MXU_DIM * (j + 1)] = _dot(ub[:, MXU_DIM * kt:MXU_DIM * (kt + 1)], wb_ref[j])

    def scan(j):
        lo, mid, hi = MXU_DIM * j, MXU_DIM * j + LANES, MXU_DIM * (j + 1)
        ar = ar_ref[:, LANES * j:LANES * (j + 1)]
        ai = ai_ref[:, LANES * j:LANES * (j + 1)]
        re, im = hst_ref[:, lo:mid], hst_ref[:, mid:hi]
        for t in range(t_chunk):
            rows = slice(t * bsz, (t + 1) * bsz)
            re, im = (ar * re - ai * im + hs_ref[rows, lo:mid], ar * im + ai * re + hs_ref[rows, mid:hi])
            hs_ref[rows, lo:mid] = re
            hs_ref[rows, mid:hi] = im
        hst_ref[:, lo:mid] = re
        hst_ref[:, mid:hi] = im

    group = 4
    for j in range(group):
        project(j)
    for g0 in range(0, n_slab, group):
        for j in range(g0 + group, min(g0 + 2 * group, n_slab)):
            project(j)
        for j in range(g0, g0 + group):
            scan(j)
    hl_ref[...] = hst_ref[...]

    halves = []
    for hf in range(n_slab // slab_per_k):
        acc = None
        for jj in range(slab_per_k):
            j = hf * slab_per_k + jj
            d = _dot(hs_ref[:, MXU_DIM * j:MXU_DIM * (j + 1)].astype(BF16), wc_ref[j])
            acc = d if acc is None else acc + d
        halves.append(acc)
    y = jnp.concatenate(halves, axis=1) + dsk_ref[...] * u
    y = _gelu_tanh(y)
    y = y * _sigmoid(_dot(y.astype(BF16), wglu_ref[...]) + bglu_ref[...])
    y_ref[...] = _rms(y, gout_ref[...]).astype(BF16)


def _s5(u_tb, h0, ar, ai, wb, wc, dsk, wglu, bglu, gout, t_chunk, bsz):
    rows, n_ssm = u_tb.shape
    n_state = h0.shape[1]
    n_slab = n_state // MXU_DIM
    r = t_chunk * bsz
    return pl.pallas_call(
        functools.partial(_s5_kernel, t_chunk=t_chunk, bsz=bsz, n_slab=n_slab),
        grid=(rows // r,),
        in_specs=[pl.BlockSpec((r, n_ssm), lambda c: (c, 0)), _const_spec((bsz, n_state)),
                  _const_spec((bsz, n_state // 2)), _const_spec((bsz, n_state // 2)),
                  _const_spec((n_slab, MXU_DIM, MXU_DIM)), _const_spec((n_slab, MXU_DIM, MXU_DIM)),
                  _const_spec((1, n_ssm)), _const_spec((n_ssm, n_ssm)), _const_spec((1, n_ssm)),
                  _const_spec((1, n_ssm))],
        out_specs=[pl.BlockSpec((r, n_ssm), lambda c: (c, 0)), _const_spec((bsz, n_state))],
        out_shape=[jax.ShapeDtypeStruct((rows, n_ssm), BF16), jax.ShapeDtypeStruct((bsz, n_state), F32)],
        scratch_shapes=[pltpu.VMEM((r, n_state), F32), pltpu.VMEM((bsz, n_state), F32)],
        compiler_params=_params("arbitrary"),
        name="s5_mixer",
    )(u_tb, h0, ar, ai, wb, wc, dsk, wglu, bglu, gout)


def _s5_tables(lam_re, lam_im, log_dt, b_re, b_im, c_re, c_im, bsz):
    G, P = lam_re.shape
    H = b_re.shape[-1]
    lr, li = lam_re.astype(F32), lam_im.astype(F32)
    dt = jnp.exp(log_dt.astype(F32))[:, None]
    mag = jnp.exp(lr * dt)
    a_re, a_im = mag * jnp.cos(li * dt), mag * jnp.sin(li * dt)
    den = lr * lr + li * li
    c_r = ((a_re - 1.0) * lr + a_im * li) / den
    c_i = (a_im * lr - (a_re - 1.0) * li) / den
    br, bi = b_re.astype(F32), b_im.astype(F32)
    bbar_re = c_r[..., None] * br - c_i[..., None] * bi
    bbar_im = c_r[..., None] * bi + c_i[..., None] * br
    n_pair = G // 2
    per_k = MXU_DIM // (2 * H)
    eye2 = jnp.eye(2, dtype=F32)
    place = jax.nn.one_hot(jnp.arange(n_pair) % per_k, per_k, dtype=F32)

    ar = jnp.broadcast_to(a_re.reshape(1, -1), (bsz, G * P))
    ai = jnp.broadcast_to(a_im.reshape(1, -1), (bsz, G * P))

    bb = jnp.stack([bbar_re, bbar_im]).reshape(2, n_pair, 2, P, H)
    wpair = jnp.einsum("ajgph,gk->jghakp", bb, eye2).reshape(n_pair, 2 * H, MXU_DIM)
    wb = jnp.einsum("jrc,jk->jkrc", wpair, place).reshape(n_pair, MXU_DIM, MXU_DIM)

    cc = jnp.stack([c_re.astype(F32), -c_im.astype(F32)]).reshape(2, n_pair, 2, H, P)
    cpair = jnp.einsum("ajghp,gk->jagpkh", cc, eye2).reshape(n_pair, MXU_DIM, 2 * H)
    wc = jnp.einsum("jnc,jk->jnkc", cpair, place).reshape(n_pair, MXU_DIM, MXU_DIM)
    return ar, ai, wb.astype(BF16), wc.astype(BF16)


def _state_to_lanes(re, im):
    bsz, G, P = re.shape
    s = jnp.stack([re, im], axis=1).reshape(bsz, 2, G // 2, 2, P)
    return s.transpose(0, 2, 1, 3, 4).reshape(bsz, 2 * G * P)


def _lanes_to_state(h, G, P):
    bsz = h.shape[0]
    s = h.reshape(bsz, G // 2, 2, 2, P).transpose(0, 2, 1, 3, 4).reshape(bsz, 2, G, P)
    return s[:, 0], s[:, 1]


FOX_AUG_ROWS = 16


def _fox_kernel(qt_ref, qa_ref, k_ref, kx_ref, vt_ref, o_ref, m_ref, l_ref, acc_ref, *, tq, tk, past):
    qi = pl.program_id(2)
    qt = qt_ref[0]
    row = lax.broadcasted_iota(jnp.int32, (LANES, tq), 0)
    zero = jnp.zeros_like(qt)
    pad = jnp.zeros((LANES - FOX_AUG_ROWS, tq), BF16)
    qts = []
    for hh in range(2):
        own = (row < FOX_HEAD_DIM) if hh == 0 else (row >= FOX_HEAD_DIM)
        aug = qa_ref[0, 0, FOX_AUG_ROWS * hh:FOX_AUG_ROWS * (hh + 1), :]
        qts.append(jnp.concatenate([jnp.where(own, qt, zero), aug, pad], axis=0))
    m_ref[...] = jnp.full(m_ref.shape, -1e30, F32)
    l_ref[...] = jnp.zeros(l_ref.shape, F32)
    acc_ref[...] = jnp.zeros(acc_ref.shape, F32)
    q_start = past + qi * tq
    n_full = (q_start + 1) // tk
    n_all = (q_start + tq + tk - 1) // tk

    def block(ks, masked, width):
        kb = jnp.concatenate([k_ref[0, pl.ds(ks, width), :], kx_ref[0, 0, pl.ds(ks, width), :]], axis=1)
        if masked:
            kpos = ks + lax.broadcasted_iota(jnp.int32, (width, tq), 0)
            qpos = q_start + lax.broadcasted_iota(jnp.int32, (width, tq), 1)
            visible = kpos <= qpos
        for hh in range(2):
            st = _dot(kb, qts[hh])
            if masked:
                st = jnp.where(visible, st, -jnp.inf)
            m_old = m_ref[hh]
            m_new = jnp.maximum(m_old, jnp.max(st, axis=0, keepdims=True))
            p = jnp.exp2(st - m_new)
            alpha = jnp.exp2(m_old - m_new)
            l_ref[hh] = alpha * l_ref[hh] + jnp.sum(p, axis=0, keepdims=True)
            vt = vt_ref[0, FOX_HEAD_DIM * hh:FOX_HEAD_DIM * (hh + 1), pl.ds(ks, width)]
            acc_ref[hh] = alpha * acc_ref[hh] + _dot(vt, p.astype(BF16))
            m_ref[hh] = m_new

    def steps(lo, hi, masked, width):
        def body(j, c):
            block(pl.multiple_of(j * width, width), masked, width)
            return c
        lax.fori_loop(lo, hi, body, 0)

    n_wide = 0
    if 2 * tk <= k_ref.shape[1]:
        n_wide = q_start // (2 * tk)
        steps(0, n_wide, False, 2 * tk)
    steps(2 * n_wide, n_full, False, tk)
    steps(n_full, n_all, True, tk)
    out_t = jnp.concatenate([acc_ref[0] / l_ref[0], acc_ref[1] / l_ref[1]], axis=0)
    o_ref[0] = out_t.T


def _fox(q_t, q_aug, k_all, k_bias, v_t, tq, tk, past):
    bsz, n_fox, L = q_t.shape
    n_pair = n_fox // LANES
    lk = k_all.shape[1]
    return pl.pallas_call(
        functools.partial(_fox_kernel, tq=tq, tk=tk, past=past),
        grid=(bsz, n_pair, L // tq),
        in_specs=[pl.BlockSpec((1, LANES, tq), lambda b, h, i: (b, h, i)),
                  pl.BlockSpec((1, 1, 2 * FOX_AUG_ROWS, tq), lambda b, h, i: (b, h, 0, i)),
                  pl.BlockSpec((1, lk, LANES), lambda b, h, i: (b, 0, h)),
                  pl.BlockSpec((1, 1, lk, LANES), lambda b, h, i: (b, h, 0, 0)),
                  pl.BlockSpec((1, LANES, lk), lambda b, h, i: (b, h, 0))],
        out_specs=pl.BlockSpec((1, tq, LANES), lambda b, h, i: (b, i, h)),
        out_shape=jax.ShapeDtypeStruct((bsz, L, n_fox), F32),
        scratch_shapes=[pltpu.VMEM((2, 1, tq), F32), pltpu.VMEM((2, 1, tq), F32),
                        pltpu.VMEM((2, FOX_HEAD_DIM, tq), F32)],
        compiler_params=_params("parallel", "parallel", "arbitrary"),
        name="fox_attention",
    )(q_t, q_aug, k_all, k_bias, v_t)


def _fox_operands(qb, k_all, v_all, d_pieces, past, L):
    bsz, lk_pad, n_fox = k_all.shape
    n_heads = n_fox // FOX_HEAD_DIM
    n_pair = n_heads // 2
    pieces = d_pieces.astype(BF16)
    dk = (-pieces).reshape(bsz, 3, n_pair, 2, lk_pad)
    dk = dk.transpose(0, 2, 4, 3, 1).reshape(bsz, n_pair, lk_pad, 6)
    ones_k = jnp.ones((bsz, n_pair, lk_pad, 3), BF16)
    zeros_k = jnp.zeros((bsz, n_pair, lk_pad, LANES - 9), BF16)
    k_bias = jnp.concatenate([dk, ones_k, zeros_k], axis=-1)

    dq = pieces[:, :, :, past:past + L].transpose(0, 2, 1, 3)
    sel = jax.nn.one_hot(jnp.arange(n_heads) % 2, 2, dtype=BF16)
    ones_q = jnp.broadcast_to(jnp.repeat(sel, 3, axis=1)[None, :, :, None], (bsz, n_heads, 6, L))
    zeros_q = jnp.zeros((bsz, n_heads, FOX_AUG_ROWS - 9, L), BF16)
    q_aug = jnp.concatenate([ones_q, dq, zeros_q], axis=2).reshape(bsz, n_pair, 2 * FOX_AUG_ROWS, L)
    return qb.transpose(0, 2, 1), q_aug, k_all, k_bias, v_all.transpose(0, 2, 1)


def _route_gates(logits_t, ebias_t, n_experts):
    per_group = n_experts // N_EXPERT_GROUPS
    tokens = logits_t.shape[1]
    score = _sigmoid(logits_t)
    sel = score + ebias_t
    row = lax.broadcasted_iota(jnp.int32, (per_group, tokens), 0).astype(F32)
    neg = jnp.float32(-jnp.inf)

    def first_argmax(tile, best):
        return jnp.min(jnp.where(tile == best, row, float(per_group)), axis=0, keepdims=True)

    score_g, sel_g, gsc = [], [], []
    for g in range(N_EXPERT_GROUPS):
        sg = sel[per_group * g:per_group * (g + 1), :]
        score_g.append(score[per_group * g:per_group * (g + 1), :])
        sel_g.append(sg)
        m1 = jnp.max(sg, axis=0, keepdims=True)
        rest = jnp.where(row == first_argmax(sg, m1), neg, sg)
        gsc.append(m1 + jnp.max(rest, axis=0, keepdims=True))
    cur = []
    for g in range(N_EXPERT_GROUPS):
        ahead = jnp.zeros((1, tokens), F32)
        for o in range(N_EXPERT_GROUPS):
            if o == g:
                continue
            beats = (gsc[o] >= gsc[g]) if o < g else (gsc[o] > gsc[g])
            ahead = ahead + jnp.where(beats, 1.0, 0.0)
        cur.append(jnp.where(ahead < TOPK_GROUPS, sel_g[g], neg))
    chosen = [jnp.zeros((per_group, tokens), jnp.bool_) for _ in range(N_EXPERT_GROUPS)]
    for _ in range(TOP_K):
        best = cur[0]
        for g in range(1, N_EXPERT_GROUPS):
            best = jnp.maximum(best, cur[g])
        best = jnp.max(best, axis=0, keepdims=True)
        idx = None
        for g in range(N_EXPERT_GROUPS):
            cand = jnp.min(jnp.where(cur[g] == best, row + float(per_group * g), float(n_experts)), axis=0,
                           keepdims=True)
            idx = cand if idx is None else jnp.minimum(idx, cand)
        for g in range(N_EXPERT_GROUPS):
            hit = (row + float(per_group * g)) == idx
            chosen[g] = chosen[g] | hit
            cur[g] = jnp.where(hit, neg, cur[g])
    w = [jnp.where(chosen[g], score_g[g], 0.0) for g in range(N_EXPERT_GROUPS)]
    total = w[0]
    for g in range(1, N_EXPERT_GROUPS):
        total = total + w[g]
    total = jnp.sum(total, axis=0, keepdims=True)
    return [w[g] / total * ROUTED_SCALE for g in range(N_EXPERT_GROUPS)]


def _mid_kernel(x_ref, ys_ref, yf_ref, gfox_ref, wout_ref, gmq_ref, wmq_ref, mk_ref, mv_ref, wmo_ref, gffn_ref,
                wrt_ref, ebt_ref, ws1_ref, ws3_ref, ws2_ref, xs_ref, xn_ref, gt_ref, cnt_ref, *, n_experts):
    x = x_ref[0]
    yfn = _rms(yf_ref[0], gfox_ref[...]).astype(BF16)
    mix = jnp.concatenate([ys_ref[0], yfn], axis=1)
    x1 = x + _dot(mix, wout_ref[...])

    qm = _dot(_rms(x1, gmq_ref[...]).astype(BF16), wmq_ref[...])
    hd = qm.shape[1] // MEM_HEADS
    heads = []
    for h in range(MEM_HEADS):
        qh = (qm[:, hd * h:hd * (h + 1)] * (hd ** -0.5)).astype(BF16)
        s = lax.dot_general(qh, mk_ref[0, :, hd * h:hd * (h + 1)], _NT, preferred_element_type=F32)
        p = jnp.exp(s - jnp.max(s, axis=1, keepdims=True))
        o = _dot(p.astype(BF16), mv_ref[0, :, hd * h:hd * (h + 1)]) / jnp.sum(p, axis=1, keepdims=True)
        heads.append(o.astype(BF16))
    x2 = x1 + _dot(jnp.concatenate(heads, axis=1), wmo_ref[...])

    xn = _rms(x2, gffn_ref[...])
    xnb = xn.astype(BF16)
    hidden = _silu(_dot(xnb, ws1_ref[...])) * _dot(xnb, ws3_ref[...])
    xs_ref[0] = x2 + _dot(hidden.astype(BF16), ws2_ref[...])
    xn_ref[0] = xnb

    wr = wrt_ref[...]
    wr_hi = wr.astype(BF16)
    wr_lo = (wr - wr_hi.astype(F32)).astype(BF16)
    xn_lo = (xn - xnb.astype(F32)).astype(BF16)
    nt = lambda a, c: lax.dot_general(a, c, _NT, preferred_element_type=F32)
    logits_t = nt(wr_hi, xnb) + (nt(wr_hi, xn_lo) + nt(wr_lo, xnb))
    gates = _route_gates(logits_t, ebt_ref[...], n_experts)
    per_group = n_experts // N_EXPERT_GROUPS
    n_cnt = cnt_ref.shape[1]
    per_cnt = x.shape[0] // n_cnt
    for g in range(N_EXPERT_GROUPS):
        gt_ref[0, per_group * g:per_group * (g + 1), :] = gates[g]
        routed = jnp.where(gates[g] != 0.0, 1.0, 0.0)
        for c in range(n_cnt):
            cnt_ref[0, c, per_group * g:per_group * (g + 1), :] = jnp.sum(
                routed[:, per_cnt * c:per_cnt * (c + 1)], axis=1, keepdims=True)


def _mid(x, ys, yf, gfox, wout, gmq, wmq, mkb, mvb, wmo, gffn, wrt, ebt, ws1, ws3, ws2, tm):
    bsz, L, d = x.shape
    n_ssm, n_fox = ys.shape[2], yf.shape[2]
    n_mem = mkb.shape[1]
    n_experts = wrt.shape[0]
    fs = ws1.shape[1]
    row = lambda n: pl.BlockSpec((1, tm, n), lambda b, i: (b, i, 0))
    memspec = pl.BlockSpec((1, n_mem, d), lambda b, i: (b, 0, 0))
    weight = lambda shape: pl.BlockSpec(shape, lambda b, i: (0,) * len(shape), pipeline_mode=pl.Buffered(1))
    cb = min(tm, MOE_TOKENS)
    return pl.pallas_call(
        functools.partial(_mid_kernel, n_experts=n_experts),
        grid=(bsz, L // tm),
        in_specs=[row(d), row(n_ssm), row(n_fox), _const_spec((1, n_fox)),
                  weight((d, d)), _const_spec((1, d)),
                  weight((d, d)), memspec, memspec, weight((d, d)), _const_spec((1, d)),
                  _const_spec((n_experts, d)), _const_spec((n_experts, 1)), weight((d, fs)), weight((d, fs)),
                  weight((fs, d))],
        out_specs=[row(d), row(d), pl.BlockSpec((1, n_experts, tm), lambda b, i: (b, 0, i)),
                   pl.BlockSpec((1, tm // cb, n_experts, 1), lambda b, i: (b, i, 0, 0))],
        out_shape=[jax.ShapeDtypeStruct((bsz, L, d), F32), jax.ShapeDtypeStruct((bsz, L, d), BF16),
                   jax.ShapeDtypeStruct((bsz, n_experts, L), F32),
                   jax.ShapeDtypeStruct((bsz, L // cb, n_experts, 1), F32)],
        compiler_params=_params("parallel", "parallel"),
        name="mid_block",
    )(x, ys, yf, gfox, wout, gmq, wmq, mkb, mvb, wmo, gffn, wrt, ebt, ws1, ws3, ws2)


SEG_ROWS = 16
MOE_TOKENS = 512
COPY_ROWS = 4 * SEG_ROWS
GATHER_ROWS = 1024
_TN = (((0,), (0,)), ((), ()))
_NO_RANK = -(1 << 20)


def _sorted_rows(tb, n_experts):
    rows = TOP_K * tb + n_experts * (SEG_ROWS - 1)
    return -(-rows // GATHER_ROWS) * GATHER_ROWS


def _expert_tile(m):
    return 1024 if m >= 4096 else 128


def _dispatch_plan(cnt, ns, MOE_TILE):
    nblk, n_experts = cnt.shape

    def before(a, axis):
        n = a.shape[axis]
        earlier = jnp.arange(n)[:, None] < jnp.arange(n)[None, :]
        if axis == 0:
            return jnp.sum(jnp.where(earlier[:, :, None], a[:, None, :], 0), axis=0)
        return jnp.sum(jnp.where(earlier[None, :, :], a[:, :, None], 0), axis=1)

    cp = (cnt + SEG_ROWS - 1) // SEG_ROWS * SEG_ROWS
    o_loc = before(cp, 1)
    used = jnp.sum(cp, axis=1)
    tot_e = jnp.sum(cp, axis=0)
    reg_e = (tot_e + MOE_TILE - 1) // MOE_TILE * MOE_TILE
    base_e = before(reg_e[None, :], 1)[0]
    reg_end = base_e + reg_e
    dst = base_e[None, :] + before(cp, 0)
    rows_max = nblk * ns + n_experts * MOE_TILE
    n_tiles_max = -(-rows_max // MOE_TILE)
    n_tiles = reg_end[-1] // MOE_TILE
    tile_idx = jnp.clip(jnp.arange(n_tiles_max, dtype=jnp.int32), 0, jnp.maximum(n_tiles - 1, 0))
    tile_expert = jnp.sum((reg_end[None, :] <= (tile_idx * MOE_TILE)[:, None]).astype(jnp.int32), axis=1)
    tile_expert = jnp.minimum(tile_expert, n_experts - 1)
    n_pc = ns // SEG_ROWS
    piece_row = jnp.arange(n_pc, dtype=jnp.int32) * SEG_ROWS
    seg_end = o_loc + cp
    piece_e = jnp.sum((seg_end[:, None, :] <= piece_row[None, :, None]).astype(jnp.int32), axis=2)
    live = piece_row[None, :] < used[:, None]
    piece_e = jnp.minimum(piece_e, n_experts - 1)
    own = piece_e[:, :, None] == jnp.arange(n_experts, dtype=jnp.int32)[None, None, :]
    pick = lambda table: jnp.sum(jnp.where(own, table[:, None, :], 0), axis=2)
    rank0 = piece_row[None, :] - pick(o_loc)
    piece_rank = jnp.where(live, rank0, _NO_RANK)
    i32 = lambda a: a.astype(jnp.int32).reshape(-1)
    per_block = lambda a: a.astype(jnp.int32).reshape(nblk, 1, -1)

    def copy_list(n_e, first_rank, rows_each, n_max):
        end = before(n_e, 1) + n_e
        j = jnp.arange(n_max, dtype=jnp.int32)
        e_of = jnp.minimum(jnp.sum((end[:, None, :] <= j[None, :, None]).astype(jnp.int32), axis=2), n_experts - 1)
        mine = e_of[:, :, None] == jnp.arange(n_experts, dtype=jnp.int32)[None, None, :]
        take = lambda table: jnp.sum(jnp.where(mine, table[:, None, :], 0), axis=2)
        rank = take(first_rank) + (j[None, :] - take(end - n_e)) * rows_each
        ok = j[None, :] < jnp.sum(n_e, axis=1)[:, None]
        return (per_block(jnp.where(ok, take(o_loc) + rank, 0)), per_block(jnp.where(ok, take(dst) + rank, 0)),
                i32(jnp.sum(n_e, axis=1)))

    per_copy = COPY_ROWS // SEG_ROWS
    n_wide_e = (cp // SEG_ROWS) // per_copy
    n_seg_e = cp // SEG_ROWS - n_wide_e * per_copy
    wide_src, wide_dst, n_wide = copy_list(n_wide_e, jnp.zeros_like(cp), COPY_ROWS, ns // COPY_ROWS)
    seg_src, seg_dst, n_seg = copy_list(n_seg_e, n_wide_e * COPY_ROWS, SEG_ROWS, n_experts * (per_copy - 1))
    return dict(n_piece=i32(used // SEG_ROWS), piece_e=per_block(piece_e), piece_rank=per_block(piece_rank),
                wide_src=wide_src, wide_dst=wide_dst, n_wide=n_wide, seg_src=seg_src, seg_dst=seg_dst, n_seg=n_seg,
                tail_start=i32(base_e + tot_e),
                tail_chunks=i32((reg_e - tot_e) // SEG_ROWS), tile_idx=i32(tile_idx), tile_expert=i32(tile_expert),
                n_tiles=i32(n_tiles), n_tiles_max=n_tiles_max)


def _slot_ranks(gates, tb):
    sel = gates != 0.0
    r = lax.broadcasted_iota(jnp.int32, (tb, tb), 0)
    c = lax.broadcasted_iota(jnp.int32, (tb, tb), 1)
    earlier = jnp.where(r < c, 1.0, 0.0).astype(BF16)
    rank = _dot(jnp.where(sel, 1.0, 0.0).astype(BF16), earlier)
    return jnp.where(sel, rank, -1.0)


def _build_one_hot(out_ref, rm_ref, pe_ref, pr_ref, first, count, tb, value_row):
    rows = lax.broadcasted_iota(jnp.int32, (SEG_ROWS, tb), 0).astype(F32)
    for i in range(first, first + count):
        e = pe_ref[0, 0, i]
        hit = rows == (rm_ref[pl.ds(e, 1), :] - pr_ref[0, 0, i].astype(F32))
        out_ref[SEG_ROWS * i:SEG_ROWS * (i + 1), :] = jnp.where(hit, value_row(e), 0.0).astype(BF16)


def _wait_rows(n_rows, make_copy):
    wide = 16 * SEG_ROWS
    n_wide = n_rows // wide

    def wide_step(i, c):
        make_copy(wide).wait()
        return c

    def seg_step(i, c):
        make_copy(SEG_ROWS).wait()
        return c

    lax.fori_loop(0, n_wide, wide_step, 0)
    lax.fori_loop(0, (n_rows - n_wide * wide) // SEG_ROWS, seg_step, 0)


def _issue_copies(n_wide, n_seg, tables, make_copy):
    wide_src, wide_dst, seg_src, seg_dst = tables

    def burst(n, src, dst, size):
        def one(i, priority):
            make_copy(pl.multiple_of(src[0, 0, i], SEG_ROWS), pl.multiple_of(dst[0, 0, i], SEG_ROWS),
                      size).start(priority=priority)

        def pair(j, c):
            one(2 * j, 0)
            one(2 * j + 1, 1)
            return c

        lax.fori_loop(0, n // 2, pair, 0)

        @pl.when(n % 2 == 1)
        def _():
            one(n - 1, 0)

    burst(n_wide, wide_src, wide_dst, COPY_ROWS)
    burst(n_seg, seg_src, seg_dst, SEG_ROWS)


def _dispatch_kernel(n_piece_ref, n_wide_ref, n_seg_ref, tail_start_ref, tail_chunks_ref, pe_ref, pr_ref,
                     ws_ref, wd_ref, ss_ref, sd_ref, xn_ref, gt_ref, *rest, tb, ns, n_experts, nblk, fill_tail):
    xg_ref, rm_ref, g_ref, xsb_ref, z_ref, sem = rest[-6:]
    b = pl.program_id(0)
    n_pc = ns // SEG_ROWS
    used = n_piece_ref[b] * SEG_ROWS
    rm_ref[...] = _slot_ranks(gt_ref[...], tb)
    per_slice = GATHER_ROWS // SEG_ROWS
    n_slices = ns // GATHER_ROWS
    build = lambda t: _build_one_hot(g_ref, rm_ref, pe_ref, pr_ref, per_slice * t, per_slice, tb, lambda e: 1.0)

    slot = b % 2
    build(0)
    for t in range(n_slices):
        if t + 1 < n_slices:
            build(t + 1)
        sl = slice(GATHER_ROWS * t, GATHER_ROWS * (t + 1))
        xsb_ref[slot, sl, :] = _dot(g_ref[sl, :], xn_ref[...]).astype(BF16)

    def rows_copy(buf, src_row, dst_row, size):
        return pltpu.make_async_copy(xsb_ref.at[buf, pl.ds(src_row, size)], xg_ref.at[pl.ds(dst_row, size)],
                                     sem.at[buf])

    _issue_copies(n_wide_ref[b], n_seg_ref[b], (ws_ref, wd_ref, ss_ref, sd_ref),
                  lambda block_row, global_row, size: rows_copy(slot, block_row, global_row, size))

    @pl.when(b > 0)
    def _():
        _wait_rows(n_piece_ref[b - 1] * SEG_ROWS, lambda size: rows_copy(1 - slot, 0, 0, size))

    @pl.when(b == nblk - 1)
    def _():
        _wait_rows(used, lambda size: rows_copy(slot, 0, 0, size))

    @pl.when(jnp.logical_and(b == nblk - 1, fill_tail))
    def _():
        z_ref[...] = jnp.zeros(z_ref.shape, BF16)

        wide = z_ref.shape[0]
        per_wide = wide // SEG_ROWS

        def tail_copy(row, size):
            return pltpu.make_async_copy(z_ref.at[pl.ds(0, size)], xg_ref.at[pl.ds(pl.multiple_of(row, SEG_ROWS), size)],
                                         sem.at[slot])

        def each_tail(action):
            def per_expert(e, carry):
                n_wide = tail_chunks_ref[e] // per_wide

                def wide_piece(i, c2):
                    action(tail_copy(tail_start_ref[e] + i * wide, wide))
                    return c2

                def seg_piece(i, c2):
                    action(tail_copy(tail_start_ref[e] + n_wide * wide + i * SEG_ROWS, SEG_ROWS))
                    return c2

                lax.fori_loop(0, n_wide, wide_piece, 0)
                lax.fori_loop(0, tail_chunks_ref[e] - n_wide * per_wide, seg_piece, 0)
                return carry
            lax.fori_loop(0, n_experts, per_expert, 0)

        each_tail(lambda cp: cp.start())
        each_tail(lambda cp: cp.wait())


def _expert_kernel(tile_idx_ref, tile_expert_ref, n_tiles_ref, x_ref, w1_ref, w3_ref, w2_ref, o_ref):
    @pl.when(pl.program_id(0) < n_tiles_ref[0])
    def _():
        x = x_ref[...]
        h = _silu(_dot(x, w1_ref[0].astype(BF16))) * _dot(x, w3_ref[0].astype(BF16))
        o_ref[...] = _dot(h.astype(BF16), w2_ref[0].astype(BF16)).astype(BF16)


def _combine_kernel(n_piece_ref, n_wide_ref, n_seg_ref, pe_ref, pr_ref, ws_ref, wd_ref, ss_ref, sd_ref,
                    ws_next_ref, wd_next_ref, ss_next_ref, sd_next_ref,
                    og_ref, gt_ref, rm_ref, xs_ref, gfin_ref, y_ref, gw_ref, ob_ref, sem, *, tb, ns, n_experts, nblk):
    b = pl.program_id(0)
    slot = b % 2
    n_pc = ns // SEG_ROWS

    def rows_copy(buf, src_row, dst_row, size):
        return pltpu.make_async_copy(og_ref.at[pl.ds(src_row, size)], ob_ref.at[buf, pl.ds(dst_row, size)],
                                     sem.at[buf])

    def fetch_block(blk, tables, buf):
        _issue_copies(n_wide_ref[blk], n_seg_ref[blk], tables,
                      lambda block_row, global_row, size: rows_copy(buf, global_row, block_row, size))

    @pl.when(b == 0)
    def _():
        fetch_block(b, (ws_ref, wd_ref, ss_ref, sd_ref), slot)

    @pl.when(b + 1 < nblk)
    def _():
        fetch_block(jnp.minimum(b + 1, nblk - 1), (ws_next_ref, wd_next_ref, ss_next_ref, sd_next_ref), 1 - slot)

    _wait_rows(n_piece_ref[b] * SEG_ROWS, lambda size: rows_copy(slot, 0, 0, size))

    def clear(i, c):
        ob_ref[slot, pl.ds(pl.multiple_of((n_piece_ref[b] + i) * SEG_ROWS, SEG_ROWS), SEG_ROWS), :] = jnp.zeros(
            (SEG_ROWS, ob_ref.shape[2]), BF16)
        return c

    lax.fori_loop(0, n_pc - n_piece_ref[b], clear, 0)

    per_slice = GATHER_ROWS // SEG_ROWS
    n_slices = ns // GATHER_ROWS
    build = lambda t: _build_one_hot(gw_ref, rm_ref, pe_ref, pr_ref, per_slice * t, per_slice, tb,
                                     lambda e: gt_ref[pl.ds(e, 1), :])
    build(0)
    y = xs_ref[...]
    for t in range(n_slices):
        if t + 1 < n_slices:
            build(t + 1)
        sl = slice(GATHER_ROWS * t, GATHER_ROWS * (t + 1))
        y = y + lax.dot_general(gw_ref[sl, :], ob_ref[slot, sl, :], _TN, preferred_element_type=F32)
    y_ref[...] = _rms(y, gfin_ref[...])


def _moe(groups, w1, w3, w2, gfin):
    d = groups[0][0].shape[1]
    n_experts, _, f = w1.shape
    tb = MOE_TOKENS
    blocks = [g[0].shape[0] // tb for g in groups]
    assert all(g[0].shape[0] == n * tb for g, n in zip(groups, blocks)), "token groups must be whole dispatch blocks"
    ns = _sorted_rows(tb, n_experts)
    MOE_TILE = _expert_tile(sum(blocks) * tb)
    plan = _dispatch_plan(jnp.concatenate([g[2] for g in groups], axis=0), ns, MOE_TILE)
    n_tiles_max = plan["n_tiles_max"]
    rows_max = n_tiles_max * MOE_TILE
    this_block = lambda t: pl.BlockSpec((1, 1, t.shape[2]), lambda b, *_: (b, 0, 0), memory_space=pltpu.SMEM)
    n_prefetch_dispatch = 5
    n_tables = 6

    def group_tables(first, nblk):
        rows = slice(first, first + nblk)
        build = (plan["piece_e"][rows], plan["piece_rank"][rows])
        copies = tuple(plan[k][rows] for k in ("wide_src", "wide_dst", "seg_src", "seg_dst"))
        counts = tuple(plan[k][rows] for k in ("n_piece", "n_wide", "n_seg"))
        return build, copies, counts

    xg = None
    slot_ranks = []
    first = 0
    for gi, ((xn, gates_t, _, _), nblk) in enumerate(zip(groups, blocks)):
        build_tables, copy_tables, counts = group_tables(first, nblk)
        first += nblk
        earlier = () if xg is None else (xg,)
        xg = pl.pallas_call(
            functools.partial(_dispatch_kernel, tb=tb, ns=ns, n_experts=n_experts, nblk=nblk,
                              fill_tail=gi == len(groups) - 1),
            grid_spec=pltpu.PrefetchScalarGridSpec(
                num_scalar_prefetch=n_prefetch_dispatch, grid=(nblk,),
                in_specs=[this_block(t) for t in build_tables + copy_tables] + [
                          pl.BlockSpec((tb, d), lambda b, *_: (b, 0)),
                          pl.BlockSpec((n_experts, tb), lambda b, *_: (0, b))] + [
                          pl.BlockSpec(memory_space=pl.ANY) for _ in earlier],
                out_specs=[pl.BlockSpec(memory_space=pl.ANY), pl.BlockSpec((n_experts, tb), lambda b, *_: (0, b))],
                scratch_shapes=[pltpu.VMEM((ns, tb), BF16), pltpu.VMEM((2, ns, d), BF16),
                                pltpu.VMEM((16 * SEG_ROWS, d), BF16), pltpu.SemaphoreType.DMA((2,))]),
            out_shape=[jax.ShapeDtypeStruct((rows_max, d), BF16), jax.ShapeDtypeStruct((n_experts, nblk * tb), F32)],
            input_output_aliases={n_prefetch_dispatch + n_tables + 2: 0} if earlier else {},
            compiler_params=_params("arbitrary"),
            name="moe_dispatch",
        )(*counts, plan["tail_start"], plan["tail_chunks"], *build_tables, *copy_tables, xn, gates_t, *earlier)
        xg, ranks = xg
        slot_ranks.append(ranks)

    tile = lambda i, idx, ex, n: (idx[i], 0)
    og = pl.pallas_call(
        _expert_kernel,
        grid_spec=pltpu.PrefetchScalarGridSpec(
            num_scalar_prefetch=3, grid=(n_tiles_max,),
            in_specs=[pl.BlockSpec((MOE_TILE, d), tile),
                      pl.BlockSpec((1, d, f), lambda i, idx, ex, n: (ex[i], 0, 0)),
                      pl.BlockSpec((1, d, f), lambda i, idx, ex, n: (ex[i], 0, 0)),
                      pl.BlockSpec((1, f, d), lambda i, idx, ex, n: (ex[i], 0, 0))],
            out_specs=pl.BlockSpec((MOE_TILE, d), tile)),
        out_shape=jax.ShapeDtypeStruct((rows_max, d), BF16),
        compiler_params=_params("arbitrary"),
        name="moe_experts",
    )(plan["tile_idx"], plan["tile_expert"], plan["n_tiles"], xg, w1, w3, w2)

    results = []
    first = 0
    for (xn, gates_t, _, xs), nblk, ranks in zip(groups, blocks, slot_ranks):
        build_tables, copy_tables, counts = group_tables(first, nblk)
        first += nblk
        next_block = lambda t, nblk=nblk: pl.BlockSpec(
            (1, 1, t.shape[2]), lambda b, *_: (jnp.minimum(b + 1, nblk - 1), 0, 0), memory_space=pltpu.SMEM)
        results.append(pl.pallas_call(
            functools.partial(_combine_kernel, tb=tb, ns=ns, n_experts=n_experts, nblk=nblk),
            grid_spec=pltpu.PrefetchScalarGridSpec(
                num_scalar_prefetch=3, grid=(nblk,),
                in_specs=[this_block(t) for t in build_tables + copy_tables] + [next_block(t) for t in copy_tables] + [
                          pl.BlockSpec(memory_space=pl.ANY), pl.BlockSpec((n_experts, tb), lambda b, *_: (0, b)),
                          pl.BlockSpec((n_experts, tb), lambda b, *_: (0, b)),
                          pl.BlockSpec((tb, d), lambda b, *_: (b, 0)), pl.BlockSpec((1, d), lambda b, *_: (0, 0))],
                out_specs=pl.BlockSpec((tb, d), lambda b, *_: (b, 0)),
                scratch_shapes=[pltpu.VMEM((ns, tb), BF16), pltpu.VMEM((2, ns, d), BF16),
                                pltpu.SemaphoreType.DMA((2,))]),
            out_shape=jax.ShapeDtypeStruct((nblk * tb, d), F32),
            compiler_params=_params("arbitrary"),
            name="moe_combine",
        )(*counts, *build_tables, *copy_tables, *copy_tables, og, gates_t, ranks, xs, gfin))
    return results


def _tiles(L):
    t_row = min(L, 512)
    t_mid = min(L, 1024)
    t_key = 512
    t_scan = min(L, 64)
    return t_row, t_mid, t_key, t_scan


def _group(x, h0_re, h0_im, past_k, past_v, past_lf, mkb, mvb, p, w):
    bsz, L, d = x.shape
    n_heads = p["b_f"].shape[0]
    n_fox = n_heads * FOX_HEAD_DIM
    n_ssm = p["d_skip"].shape[0]
    G, P = p["lam_re"].shape
    t_row, t_mid, t_key, t_scan = _tiles(L)

    u, qb, k, v, kb, vb, lf = _proj_in(x, w["g_mix"], w["w_main"], w["w_f"], w["b_f"], t_row, n_ssm, n_fox, n_heads)

    u_tb = u.transpose(1, 0, 2).reshape(L * bsz, n_ssm)
    if h0_re is None:
        h0 = jnp.zeros((bsz, 2 * G * P), F32)
    else:
        h0 = _state_to_lanes(h0_re.astype(F32), h0_im.astype(F32))
    ys_tb, h_last = _s5(u_tb, h0, w["ar"], w["ai"], w["wb"], w["wc"], w["d_skip"], w["w_glu"], w["b_glu"],
                        w["g_ssm_out"], t_scan, bsz)
    ys = ys_tb.reshape(L, bsz, n_ssm).transpose(1, 0, 2)
    hl_re, hl_im = _lanes_to_state(h_last, G, P)

    if past_k is None:
        past = 0
        lf_all, k_all, v_all = lf, kb, vb
    else:
        past = past_k.shape[1]
        lf_all = jnp.concatenate([past_lf.astype(F32), lf], axis=1)
        k_all = jnp.concatenate([past_k.reshape(bsz, past, n_fox).astype(BF16), kb], axis=1)
        v_all = jnp.concatenate([past_v.reshape(bsz, past, n_fox).astype(BF16), vb], axis=1)
    lk = lf_all.shape[1]
    lk_pad = -(-lk // t_key) * t_key
    lf_t = jnp.pad(lf_all.transpose(0, 2, 1), ((0, 0), (0, 0), (0, lk_pad - lk)))
    d_pieces = _decay_cumsum(lf_t.reshape(1, bsz * n_heads, lk_pad), jnp.zeros((1, bsz * n_heads, 1), F32), t_key)
    d_pieces = d_pieces.reshape(3, bsz, n_heads, lk_pad).transpose(1, 0, 2, 3)
    k_all = jnp.pad(k_all, ((0, 0), (0, lk_pad - lk), (0, 0)))
    v_all = jnp.pad(v_all, ((0, 0), (0, lk_pad - lk), (0, 0)))
    t_att = t_key if L >= t_key else lk_pad
    yf = _fox(*_fox_operands(qb, k_all, v_all, d_pieces, past, L), t_row, t_att, past)

    xs, xnb, gates_t, cnt = _mid(x, ys, yf, w["g_fox_out"], w["w_out"], w["g_mem_q"], w["w_mq"], mkb, mvb, w["w_mo"],
                                 w["g_ffn"], w["w_router_t"], w["e_bias_t"], w["ws1"], w["ws3"], w["ws2"], t_mid)
    m = bsz * L
    n_experts = gates_t.shape[1]
    cnt = cnt.reshape(m // MOE_TOKENS, -1, n_experts).sum(axis=1).astype(jnp.int32)
    routed = (xnb.reshape(m, d), gates_t.transpose(1, 0, 2).reshape(n_experts, m), cnt, xs.reshape(m, d))
    return (routed, hl_re, hl_im, k.reshape(bsz, L, n_heads, FOX_HEAD_DIM),
            v.reshape(bsz, L, n_heads, FOX_HEAD_DIM), lf)


def kernel(x_prompt, x_sample, state_ssm_re, state_ssm_im, cache_fox_k, cache_fox_v, cache_fox_logf, cache_mem_k, cache_mem_v, mem_prompt, g_mix, w_in, b_f, lam_re, lam_im, log_dt, b_re, b_im, c_re, c_im, d_skip, w_glu, b_glu, g_ssm_out, g_fox_out, w_out, g_mem_q, g_mem_kv, w_mq, w_mk, w_mv, w_mo, g_ffn, w_router, e_bias, w1, w3, w2, ws1, ws3, ws2, g_final):
    depth = w_in.shape[0]
    bsz = x_prompt.shape[0]
    hp, hs = x_prompt, x_sample
    outs_p, outs_s = [], []
    for l in range(depth):
        p = dict(b_f=b_f[l], lam_re=lam_re[l], d_skip=d_skip[l])
        n_heads = b_f.shape[1]
        n_ssm = d_skip.shape[1]
        n_main = w_in.shape[2] - n_heads
        row = lambda a: a.reshape(1, -1).astype(F32)
        ar, ai, wb, wc = _s5_tables(lam_re[l], lam_im[l], log_dt[l], b_re[l], b_im[l], c_re[l], c_im[l], bsz)
        w = dict(
            g_mix=row(g_mix[l]), w_main=w_in[l][:, :n_main].astype(BF16),
            w_f=jnp.pad(w_in[l][:, n_main:], ((0, 0), (0, LANES - n_heads))).astype(BF16),
            b_f=jnp.pad(b_f[l].astype(F32), (0, LANES - n_heads)).reshape(1, LANES),
            ar=ar, ai=ai, wb=wb, wc=wc, d_skip=row(d_skip[l]), w_glu=w_glu[l].astype(BF16), b_glu=row(b_glu[l]),
            g_ssm_out=row(g_ssm_out[l]), g_fox_out=row(g_fox_out[l]), w_out=w_out[l].astype(BF16),
            g_mem_q=row(g_mem_q[l]), w_mq=w_mq[l].astype(BF16), w_mo=w_mo[l].astype(BF16), g_ffn=row(g_ffn[l]),
            w_router_t=w_router[l].astype(F32).T, e_bias_t=e_bias[l].astype(F32).reshape(-1, 1),
            ws1=ws1[l].astype(BF16), ws3=ws3[l].astype(BF16), ws2=ws2[l].astype(BF16),
            w1=w1[l], w3=w3[l], w2=w2[l], g_final=row(g_final))
        assert depth == 1, "final norm fusion assumes a single layer"
        mk_p, mv_p, mkb_p, mvb_p = _mem_kv(mem_prompt, row(g_mem_kv[l]), w_mk[l].astype(BF16), w_mv[l].astype(BF16))
        nm, mh = mem_prompt.shape[1], MEM_HEADS
        routed_p, re_p, im_p, k_p, v_p, lf_p = _group(hp, None, None, None, None, None, mkb_p, mvb_p, p, w)
        cm_k = cache_mem_k[l].reshape(bsz, nm, -1).astype(BF16)
        cm_v = cache_mem_v[l].reshape(bsz, nm, -1).astype(BF16)
        routed_s, re_s, im_s, k_s, v_s, lf_s = _group(hs, state_ssm_re[l], state_ssm_im[l], cache_fox_k[l],
                                                      cache_fox_v[l], cache_fox_logf[l], cm_k, cm_v, p, w)
        y_p, y_s = _moe([routed_p, routed_s], w["w1"], w["w3"], w["w2"], w["g_final"])
        hp, hs = y_p.reshape(hp.shape), y_s.reshape(hs.shape)
        outs_p.append((re_p, im_p, k_p, v_p, lf_p, mk_p.reshape(bsz, nm, mh, -1), mv_p.reshape(bsz, nm, mh, -1)))
        outs_s.append((re_s, im_s, k_s, v_s, lf_s))
    stack = lambda outs, i: jnp.stack([o[i] for o in outs])
    return (hp, hs) + tuple(stack(outs_p, i) for i in range(7)) + tuple(stack(outs_s, i) for i in range(5))
```
